```python
import math
import jax, jax.numpy as jnp
from jax import lax
import numpy as np

D_MODEL = 1024
BATCH = 2
SEQ = 8192
DEPTH = 1

N_META = 16
GROUP_SIZE = 16
SSM_WIDTH = D_MODEL
N_GROUPS = SSM_WIDTH // GROUP_SIZE
STATE = 64
DT_MIN = 0.001
DT_MAX = 0.1
HEAD_DIM = 64
N_Q_HEADS = D_MODEL // HEAD_DIM
N_KV_HEADS = 2
GQA = N_Q_HEADS // N_KV_HEADS
Q_WIDTH = N_Q_HEADS * HEAD_DIM
KV_WIDTH = N_KV_HEADS * HEAD_DIM
WINDOW = 128
ATTN_BLOCK = 128
ROT_DIM = HEAD_DIM // 4
ROPE_THETA = 500000.0
IN_WIDTH = SSM_WIDTH + Q_WIDTH + 2 * KV_WIDTH + 2 * D_MODEL
N_EXPERTS = 32
TOP_K = 4
D_FF = D_MODEL
SWIGLU_LIMIT = 7.0
SWIGLU_ALPHA = 1.702
EXPERT_BLOCK = 128
RMS_EPS = 1e-5
NEG_INF = -1e30

kernel_name = "hybrid_s5_swa_moe_block"


def rmsnorm(x, g):
    x32 = x.astype(jnp.float32)
    y = x32 * lax.rsqrt(jnp.mean(x32 * x32, axis=-1, keepdims=True) + RMS_EPS)
    return (y * g.astype(jnp.float32)).astype(x.dtype)


def partial_rotary(x, cos, sin):
    half = ROT_DIM // 2
    x1, x2, xp = x[..., :half], x[..., half:ROT_DIM], x[..., ROT_DIM:]
    c = cos[None, :, None, :].astype(x.dtype)
    s = sin[None, :, None, :].astype(x.dtype)
    return jnp.concatenate([x1 * c - x2 * s, x2 * c + x1 * s, xp], axis=-1)


def _ssm_combine(e1, e2):
    a1r, a1i, b1r, b1i = e1
    a2r, a2i, b2r, b2i = e2
    return (a2r * a1r - a2i * a1i,
            a2r * a1i + a2i * a1r,
            a2r * b1r - a2i * b1i + b2r,
            a2r * b1i + a2i * b1r + b2i)


def s5_ssm(u, lam_re, lam_im, log_dt, b_re, b_im, c_re, c_im, d):
    bsz, L, _ = u.shape
    ug = u.reshape(bsz, L, N_GROUPS, GROUP_SIZE).astype(jnp.float32)
    lr = lam_re.astype(jnp.float32)
    li = lam_im.astype(jnp.float32)
    dt = jnp.exp(log_dt.astype(jnp.float32))[:, None]
    mag = jnp.exp(dt * lr)
    a_re = mag * jnp.cos(dt * li)
    a_im = mag * jnp.sin(dt * li)
    den = lr * lr + li * li
    nr, ni = a_re - 1.0, a_im
    f_re = (nr * lr + ni * li) / den
    f_im = (ni * lr - nr * li) / den
    br, bi = b_re.astype(jnp.float32), b_im.astype(jnp.float32)
    bb_re = f_re[..., None] * br - f_im[..., None] * bi
    bb_im = f_re[..., None] * bi + f_im[..., None] * br
    bu_re = jnp.einsum('blgh,gph->blgp', ug, bb_re)
    bu_im = jnp.einsum('blgh,gph->blgp', ug, bb_im)
    a_re_t = jnp.broadcast_to(a_re[None, None], (1, L, N_GROUPS, STATE))
    a_im_t = jnp.broadcast_to(a_im[None, None], (1, L, N_GROUPS, STATE))
    _, _, x_re, x_im = lax.associative_scan(_ssm_combine, (a_re_t, a_im_t, bu_re, bu_im), axis=1)
    y = (jnp.einsum('blgp,ghp->blgh', x_re, c_re.astype(jnp.float32))
         - jnp.einsum('blgp,ghp->blgh', x_im, c_im.astype(jnp.float32))
         + d.astype(jnp.float32)[None, None] * ug)
    return y.reshape(bsz, L, SSM_WIDTH).astype(u.dtype)


def sliding_window_attention(q, k, v, sinks):
    bsz, L = q.shape[:2]
    pad = ATTN_BLOCK - N_META
    Lp = L + pad
    nb = Lp // ATTN_BLOCK
    padcfg = ((0, 0), (pad, 0), (0, 0), (0, 0))
    qb = jnp.pad(q, padcfg).reshape(bsz, nb, ATTN_BLOCK, N_KV_HEADS, GQA, HEAD_DIM)
    kb = jnp.pad(k, padcfg).reshape(bsz, nb, ATTN_BLOCK, N_KV_HEADS, HEAD_DIM)
    vb = jnp.pad(v, padcfg).reshape(bsz, nb, ATTN_BLOCK, N_KV_HEADS, HEAD_DIM)
    blkpad = ((0, 0), (1, 0), (0, 0), (0, 0), (0, 0))
    k_band = jnp.concatenate([jnp.pad(kb, blkpad)[:, :-1], kb], axis=2)
    v_band = jnp.concatenate([jnp.pad(vb, blkpad)[:, :-1], vb], axis=2)
    k_meta, v_meta = k[:, :N_META], v[:, :N_META]

    q_idx = jnp.arange(nb)[:, None] * ATTN_BLOCK + jnp.arange(ATTN_BLOCK)[None, :]
    k_idx = jnp.arange(nb)[:, None] * ATTN_BLOCK - ATTN_BLOCK + jnp.arange(2 * ATTN_BLOCK)[None, :]
    qi, ki = q_idx[:, :, None], k_idx[:, None, :]
    band_ok = (ki <= qi) & (qi - ki < WINDOW) & (ki >= pad + N_META)
    meta_ok = jnp.arange(N_META)[None, None, :] <= (q_idx - pad)[:, :, None]
    mask = jnp.concatenate([band_ok, meta_ok], axis=-1)

    scale = 1.0 / math.sqrt(HEAD_DIM)
    s = jnp.concatenate([jnp.einsum('bnqhgd,bnkhd->bnhgqk', qb, k_band),
                         jnp.einsum('bnqhgd,bmhd->bnhgqm', qb, k_meta)], axis=-1)
    s = jnp.where(mask[None, :, None, None], s.astype(jnp.float32) * scale, NEG_INF)
    sink = jnp.broadcast_to(sinks.astype(jnp.float32).reshape(N_KV_HEADS, GQA)[None, None, :, :, None, None],
                            s.shape[:-1] + (1,))
    p = jax.nn.softmax(jnp.concatenate([s, sink], axis=-1), axis=-1)[..., :-1].astype(v.dtype)
    nbk = 2 * ATTN_BLOCK
    o = (jnp.einsum('bnhgqk,bnkhd->bnqhgd', p[..., :nbk], v_band)
         + jnp.einsum('bnhgqm,bmhd->bnqhgd', p[..., nbk:], v_meta))
    return o.reshape(bsz, Lp, Q_WIDTH)[:, pad:]


def moe_ffn(h, w_router, b_router, w_gate_up, b_gate_up, w_down, b_down):
    bsz, L, dm = h.shape
    xt = h.reshape(-1, dm)
    n_tok = xt.shape[0]
    logits = (xt @ w_router + b_router).astype(jnp.float32)
    top_val, top_idx = lax.top_k(logits, TOP_K)
    gates = jax.nn.softmax(top_val, axis=-1)
    n_asg = n_tok * TOP_K
    e_flat = top_idx.reshape(-1).astype(jnp.int32)
    tok_flat = jnp.arange(n_asg, dtype=jnp.int32) // TOP_K
    g_flat = gates.reshape(-1)
    order = jnp.argsort(e_flat)
    e_sorted = e_flat[order]
    counts = jnp.bincount(e_flat, length=N_EXPERTS)
    starts = jnp.cumsum(counts) - counts
    padded = (counts + EXPERT_BLOCK - 1) // EXPERT_BLOCK * EXPERT_BLOCK
    pends = jnp.cumsum(padded)
    pstarts = pends - padded
    dest = pstarts[e_sorted] + (jnp.arange(n_asg, dtype=jnp.int32) - starts[e_sorted])
    n_blocks = -(-n_asg // EXPERT_BLOCK) + N_EXPERTS
    n_slots = n_blocks * EXPERT_BLOCK
    slot_tok = jnp.full((n_slots,), n_tok, jnp.int32).at[dest].set(tok_flat[order])
    slot_gate = jnp.zeros((n_slots,), jnp.float32).at[dest].set(g_flat[order])
    block_expert = jnp.clip(jnp.searchsorted(pends, jnp.arange(n_blocks) * EXPERT_BLOCK, side='right'),
                            0, N_EXPERTS - 1).astype(jnp.int32)
    x_ext = jnp.concatenate([xt, jnp.zeros((1, dm), xt.dtype)], axis=0)

    def run_block(args):
        toks, e = args
        xb = x_ext[toks]
        gu = xb @ w_gate_up[e] + b_gate_up[e]
        g, up = gu[:, :D_FF], gu[:, D_FF:]
        g = jnp.minimum(g, SWIGLU_LIMIT)
        up = jnp.clip(up, -SWIGLU_LIMIT, SWIGLU_LIMIT)
        hid = g * jax.nn.sigmoid(SWIGLU_ALPHA * g) * (up + 1.0)
        return hid @ w_down[e] + b_down[e]

    y_slots = lax.map(run_block, (slot_tok.reshape(n_blocks, EXPERT_BLOCK), block_expert))
    y_slots = y_slots.reshape(n_slots, dm)
    y = jnp.zeros((n_tok + 1, dm), h.dtype).at[slot_tok].add(
        y_slots * slot_gate[:, None].astype(y_slots.dtype))
    return y[:n_tok].reshape(bsz, L, dm)


def setup_inputs(seed: int = 0) -> dict:
    key = jax.random.key(seed)
    ks = jax.random.split(key, 26)
    f32 = jnp.float32
    nrm = lambda k, shape, s: jax.random.normal(k, shape, f32) * s
    lam_im = jnp.broadcast_to(jnp.pi * jnp.arange(STATE, dtype=f32), (DEPTH, N_GROUPS, STATE))
    return {
        "x": nrm(ks[0], (BATCH, SEQ, D_MODEL), 1.0),
        "meta_tokens": nrm(ks[1], (N_META, D_MODEL), 1.0),
        "norm_mix": 1.0 + nrm(ks[2], (DEPTH, D_MODEL), 0.02),
        "w_in": nrm(ks[3], (DEPTH, D_MODEL, IN_WIDTH), D_MODEL ** -0.5),
        "ssm_lam_re": -0.5 * jnp.exp(nrm(ks[4], (DEPTH, N_GROUPS, STATE), 0.01)),
        "ssm_lam_im": lam_im + nrm(ks[5], (DEPTH, N_GROUPS, STATE), 0.01),
        "ssm_log_dt": jax.random.uniform(ks[6], (DEPTH, N_GROUPS), f32, math.log(DT_MIN), math.log(DT_MAX)),
        "ssm_b_re": nrm(ks[7], (DEPTH, N_GROUPS, STATE, GROUP_SIZE), (2 * GROUP_SIZE) ** -0.5),
        "ssm_b_im": nrm(ks[8], (DEPTH, N_GROUPS, STATE, GROUP_SIZE), (2 * GROUP_SIZE) ** -0.5),
        "ssm_c_re": nrm(ks[9], (DEPTH, N_GROUPS, GROUP_SIZE, STATE), STATE ** -0.5),
        "ssm_c_im": nrm(ks[10], (DEPTH, N_GROUPS, GROUP_SIZE, STATE), STATE ** -0.5),
        "ssm_d": nrm(ks[11], (DEPTH, N_GROUPS, GROUP_SIZE), 1.0),
        "w_glu": nrm(ks[12], (DEPTH, SSM_WIDTH, SSM_WIDTH), SSM_WIDTH ** -0.5),
        "b_glu": nrm(ks[13], (DEPTH, SSM_WIDTH), 0.02),
        "attn_sinks": nrm(ks[14], (DEPTH, N_Q_HEADS), 0.5),
        "w_br_ssm": nrm(ks[15], (DEPTH, SSM_WIDTH, D_MODEL), SSM_WIDTH ** -0.5),
        "w_br_attn": nrm(ks[16], (DEPTH, Q_WIDTH, D_MODEL), Q_WIDTH ** -0.5),
        "w_out": nrm(ks[17], (DEPTH, D_MODEL, D_MODEL), D_MODEL ** -0.5),
        "norm_ffn": 1.0 + nrm(ks[18], (DEPTH, D_MODEL), 0.02),
        "w_router": nrm(ks[19], (DEPTH, D_MODEL, N_EXPERTS), D_MODEL ** -0.5),
        "b_router": nrm(ks[20], (DEPTH, N_EXPERTS), 0.01),
        "w_gate_up": nrm(ks[21], (DEPTH, N_EXPERTS, D_MODEL, 2 * D_FF), D_MODEL ** -0.5),
        "b_gate_up": nrm(ks[22], (DEPTH, N_EXPERTS, 2 * D_FF), 0.02),
        "w_down": nrm(ks[23], (DEPTH, N_EXPERTS, D_FF, D_MODEL), D_FF ** -0.5),
        "b_down": nrm(ks[24], (DEPTH, N_EXPERTS, D_MODEL), 0.02),
        "norm_final": 1.0 + nrm(ks[25], (D_MODEL,), 0.02),
    }


def reference(x, meta_tokens, norm_mix, w_in, ssm_lam_re, ssm_lam_im, ssm_log_dt, ssm_b_re, ssm_b_im,
              ssm_c_re, ssm_c_im, ssm_d, w_glu, b_glu, attn_sinks, w_br_ssm, w_br_attn, w_out,
              norm_ffn, w_router, b_router, w_gate_up, b_gate_up, w_down, b_down, norm_final):
    bsz = x.shape[0]
    meta = jnp.broadcast_to(meta_tokens[None].astype(x.dtype), (bsz, N_META, D_MODEL))
    h = jnp.concatenate([meta, x], axis=1)
    L = h.shape[1]
    pos = jnp.arange(L, dtype=jnp.float32)
    inv_freq = ROPE_THETA ** (-jnp.arange(0, ROT_DIM, 2, dtype=jnp.float32) / ROT_DIM)
    ang = pos[:, None] * inv_freq[None, :]
    cos, sin = jnp.cos(ang), jnp.sin(ang)
    splits = np.cumsum([SSM_WIDTH, Q_WIDTH, KV_WIDTH, KV_WIDTH, D_MODEL]).tolist()

    for layer in range(DEPTH):
        hn = rmsnorm(h, norm_mix[layer])
        proj = hn @ w_in[layer]
        u, q, k, v, g_ssm, g_attn = jnp.split(proj, splits, axis=-1)
        y = s5_ssm(u, ssm_lam_re[layer], ssm_lam_im[layer], ssm_log_dt[layer], ssm_b_re[layer],
                   ssm_b_im[layer], ssm_c_re[layer], ssm_c_im[layer], ssm_d[layer])
        z = jax.nn.gelu(y)
        ssm_out = z * jax.nn.sigmoid(z @ w_glu[layer] + b_glu[layer])
        q = partial_rotary(q.reshape(bsz, L, N_Q_HEADS, HEAD_DIM), cos, sin)
        k = partial_rotary(k.reshape(bsz, L, N_KV_HEADS, HEAD_DIM), cos, sin)
        v = v.reshape(bsz, L, N_KV_HEADS, HEAD_DIM)
        attn_out = sliding_window_attention(q, k, v, attn_sinks[layer])
        mix = (jax.nn.sigmoid(g_ssm) * (ssm_out @ w_br_ssm[layer])
               + jax.nn.sigmoid(g_attn) * (attn_out @ w_br_attn[layer]))
        h = h + mix @ w_out[layer]
        h = h + moe_ffn(rmsnorm(h, norm_ffn[layer]), w_router[layer], b_router[layer], w_gate_up[layer],
                        b_gate_up[layer], w_down[layer], b_down[layer])

    return rmsnorm(h, norm_final)[:, N_META:]
```

```python
import functools
import math

import jax
import jax.numpy as jnp
from jax import lax
from jax.experimental import pallas as pl
from jax.experimental.pallas import tpu as pltpu

F32 = jnp.float32
BF16 = jnp.bfloat16
I32 = jnp.int32
U32 = jnp.uint32

D_MODEL = 1024
N_META = 16
BLK = 128
PAD = BLK - N_META
N_GROUPS = 64
GROUP = 16
STATE = 64
N_Q_HEADS = 16
N_KV_HEADS = 2
HEAD_DIM = 64
ROT_HALF = 8
ROPE_THETA = 500000.0
N_EXPERTS = 32
TOP_K = 4
D_FF = 1024
SWIGLU_LIMIT = 7.0
SWIGLU_ALPHA = 1.702
RMS_EPS = 1e-5
NEG_INF = -1e30
SSM_STEP = 8
SSM_SEGS = 8
SSM_GBLK = 8
LANES = 128
VMEM_LIMIT = 56 * 1024 * 1024


def _cparams(sem):
    return pltpu.CompilerParams(dimension_semantics=sem, vmem_limit_bytes=VMEM_LIMIT)


def _dot(a, b):
    return jnp.dot(a, b, preferred_element_type=F32)


def _dot_nt(a, b):
    return lax.dot_general(a, b, (((1,), (1,)), ((), ())), preferred_element_type=F32)


def _pack_bf16_pairs(x):
    w = x.shape[1] // 2
    bits = pltpu.bitcast(x.astype(BF16).astype(F32), U32)
    return (bits[:, w:] & jnp.uint32(0xFFFF0000)) | (bits[:, :w] >> 16)


def _unpack_bf16_pairs(wd):
    lo = pltpu.bitcast(wd << 16, F32)
    hi = pltpu.bitcast(wd & jnp.uint32(0xFFFF0000), F32)
    return jnp.concatenate([lo, hi], axis=1)


def _inproj_kernel(hp_ref, nrm_ref, w_ref, rc_ref, rs1_ref, rs2_ref,
                   u_ref, q_ref, kx_ref, vx_ref, sgs_ref, sga_ref):
    x = hp_ref[...]
    ms = jnp.mean(x * x, axis=-1, keepdims=True)
    hn = (x * lax.rsqrt(ms + RMS_EPS) * nrm_ref[...]).astype(BF16)

    def proj(lo, hi):
        return _dot(hn, w_ref[:, lo:hi])

    u_ref[...] = proj(0, 1024)

    rc = rc_ref[...]
    rs1 = rs1_ref[...]
    rs2 = rs2_ref[...]

    def rope(blk):
        return blk * rc + pltpu.roll(blk, ROT_HALF, 1) * rs1 + pltpu.roll(blk, LANES - ROT_HALF, 1) * rs2

    qf = proj(1024, 2048)
    scale = 1.0 / math.sqrt(HEAD_DIM)
    for j in range(8):
        q_ref[:, LANES * j:LANES * (j + 1)] = (rope(qf[:, LANES * j:LANES * (j + 1)]) * scale).astype(BF16)

    kvf = proj(2048, 2304)
    k = rope(kvf[:, :LANES])
    v = kvf[:, LANES:]
    low = lax.broadcasted_iota(I32, k.shape, 1) < HEAD_DIM

    def expand(t, ref):
        tr = pltpu.roll(t, HEAD_DIM, 1)
        zero = jnp.zeros_like(t)
        ref[:, 0:128] = jnp.where(low, t, zero).astype(BF16)
        ref[:, 128:256] = jnp.where(low, zero, tr).astype(BF16)
        ref[:, 256:384] = jnp.where(low, tr, zero).astype(BF16)
        ref[:, 384:512] = jnp.where(low, zero, t).astype(BF16)

    expand(k, kx_ref)
    expand(v, vx_ref)
    sgs_ref[...] = jax.nn.sigmoid(proj(2304, 3328)).astype(BF16)
    sga_ref[...] = jax.nn.sigmoid(proj(3328, 4352)).astype(BF16)


def _inproj(hp, nrm, w_bf, rc, rs1, rs2, *, tm, lp):
    np_ = hp.shape[0]
    tiles_per_seq = lp // tm
    row = lambda i: (i, 0)
    fixed = lambda i: (0, 0)
    rope_idx = lambda i: (i % tiles_per_seq, 0)
    out_shape = (
        jax.ShapeDtypeStruct((np_, 1024), F32),
        jax.ShapeDtypeStruct((np_, 1024), BF16),
        jax.ShapeDtypeStruct((np_, 512), BF16),
        jax.ShapeDtypeStruct((np_, 512), BF16),
        jax.ShapeDtypeStruct((np_, 1024), BF16),
        jax.ShapeDtypeStruct((np_, 1024), BF16),
    )
    return pl.pallas_call(
        _inproj_kernel,
        out_shape=out_shape,
        grid=(np_ // tm,),
        in_specs=[
            pl.BlockSpec((tm, D_MODEL), row),
            pl.BlockSpec((1, D_MODEL), fixed),
            pl.BlockSpec(w_bf.shape, fixed),
            pl.BlockSpec((tm, LANES), rope_idx),
            pl.BlockSpec((tm, LANES), rope_idx),
            pl.BlockSpec((tm, LANES), rope_idx),
        ],
        out_specs=(
            pl.BlockSpec((tm, 1024), row),
            pl.BlockSpec((tm, 1024), row),
            pl.BlockSpec((tm, 512), row),
            pl.BlockSpec((tm, 512), row),
            pl.BlockSpec((tm, 1024), row),
            pl.BlockSpec((tm, 1024), row),
        ),
        compiler_params=_cparams(("arbitrary",)),
        name="inproj",
    )(hp, nrm, w_bf, rc, rs1, rs2)


def _ssm_kernel(u_ref, wb_ref, wcd_ref, tab_ref, z_ref, ust_ref, bu_ref, yn_ref, *, t8, seg, nchunk):
    rc = t8 // nchunk
    for i in range(SSM_STEP):
        ust_ref[:, LANES * i:LANES * (i + 1)] = u_ref[pl.ds(i, t8, stride=SSM_STEP), :].astype(BF16)

    for c in range(nchunk):
        r = _dot(ust_ref[c * rc:(c + 1) * rc, :], wb_ref[...])
        for m in range(8):
            bu_ref[m, c * rc:(c + 1) * rc, :] = r[:, LANES * m:LANES * (m + 1)]

    a_re = [jnp.broadcast_to(tab_ref[0:1, LANES * m:LANES * (m + 1)], (SSM_SEGS, LANES)) for m in range(4)]
    a_im = [jnp.broadcast_to(tab_ref[1:2, LANES * m:LANES * (m + 1)], (SSM_SEGS, LANES)) for m in range(4)]
    s_re = [jnp.broadcast_to(tab_ref[2:3, LANES * m:LANES * (m + 1)], (SSM_SEGS, LANES)) for m in range(4)]
    s_im = [jnp.broadcast_to(tab_ref[3:4, LANES * m:LANES * (m + 1)], (SSM_SEGS, LANES)) for m in range(4)]

    def cmul_add(mr, mi, xr, xi, br, bi):
        return mr * xr - mi * xi + br, mr * xi + mi * xr + bi

    def load(k):
        return [bu_ref[m, pl.ds(k, SSM_SEGS, stride=seg), :] for m in range(8)]

    def advance(st, b):
        nr, ni = [], []
        for m in range(4):
            r_, i_ = cmul_add(a_re[m], a_im[m], st[m], st[4 + m], b[m], b[4 + m])
            nr.append(r_)
            ni.append(i_)
        return tuple(nr + ni)

    zeros = tuple(jnp.zeros((SSM_SEGS, LANES), F32) for _ in range(8))
    fin = lax.fori_loop(0, seg, lambda k, st: advance(st, load(k)), zeros)

    first = lax.broadcasted_iota(I32, (SSM_SEGS, LANES), 0) == 0

    def shift_down(t):
        return jnp.where(first, 0.0, pltpu.roll(t, 1, 0))

    tot = fin
    for _ in range(SSM_SEGS - 1):
        nxt_r, nxt_i = [], []
        for m in range(4):
            r_, i_ = cmul_add(s_re[m], s_im[m], shift_down(tot[m]), shift_down(tot[4 + m]), fin[m], fin[4 + m])
            nxt_r.append(r_)
            nxt_i.append(i_)
        tot = tuple(nxt_r + nxt_i)
    init = tuple(shift_down(t) for t in tot)

    def pass2(k, st):
        b = load(k)
        for m in range(8):
            bu_ref[m, pl.ds(k, SSM_SEGS, stride=seg), :] = st[m]
        return advance(st, b)

    lax.fori_loop(0, seg, pass2, init)

    for c in range(nchunk):
        xp = jnp.concatenate([bu_ref[m, c * rc:(c + 1) * rc, :] for m in range(8)], axis=1).astype(BF16)
        lhs = jnp.concatenate([xp, ust_ref[c * rc:(c + 1) * rc, :]], axis=1)
        y = _dot(lhs, wcd_ref[...])
        for i in range(SSM_STEP):
            yn_ref[pl.ds(SSM_STEP * c * rc + i, rc, stride=SSM_STEP), :] = y[:, LANES * i:LANES * (i + 1)]

    y = yn_ref[...] + tab_ref[4:5, 0:LANES] * u_ref[...]
    z_ref[...] = jax.nn.gelu(y).astype(BF16)


def _ssm(u, wb, wcd, tab, *, bsz, lp, nchunk):
    np_ = u.shape[0]
    t8 = lp // SSM_STEP
    seg = t8 // SSM_SEGS
    nblk = D_MODEL // LANES
    kern = functools.partial(_ssm_kernel, t8=t8, seg=seg, nchunk=nchunk)
    return pl.pallas_call(
        kern,
        out_shape=jax.ShapeDtypeStruct((np_, D_MODEL), BF16),
        grid=(nblk, bsz),
        in_specs=[
            pl.BlockSpec((lp, LANES), lambda q, b: (b, q)),
            pl.BlockSpec((None, 1024, 1024), lambda q, b: (q, 0, 0)),
            pl.BlockSpec((None, 2048, 1024), lambda q, b: (q, 0, 0)),
            pl.BlockSpec((None, 8, 512), lambda q, b: (q, 0, 0)),
        ],
        out_specs=pl.BlockSpec((lp, LANES), lambda q, b: (b, q)),
        scratch_shapes=[
            pltpu.VMEM((t8, 1024), BF16),
            pltpu.VMEM((8, t8, LANES), F32),
            pltpu.VMEM((lp, LANES), F32),
        ],
        compiler_params=_cparams(("arbitrary", "arbitrary")),
        name="ssm",
    )(u, wb, wcd, tab)


def _ssm_tables(lam_re, lam_im, log_dt, b_re, b_im, c_re, c_im, d, seg):
    hi = lax.Precision.HIGHEST
    lr, li = lam_re.astype(F32), lam_im.astype(F32)
    dt = jnp.exp(log_dt.astype(F32))[:, None]
    mag = jnp.exp(dt * lr)
    ar, ai = mag * jnp.cos(dt * li), mag * jnp.sin(dt * li)
    den = lr * lr + li * li
    nr, ni = ar - 1.0, ai
    fr, fi = (nr * lr + ni * li) / den, (ni * lr - nr * li) / den
    br, bi = b_re.astype(F32), b_im.astype(F32)
    bbr = fr[..., None] * br - fi[..., None] * bi
    bbi = fr[..., None] * bi + fi[..., None] * br

    def cmul(xr, xi, yr, yi):
        return xr * yr - xi * yi, xr * yi + xi * yr

    pr, pi = [jnp.ones_like(ar)], [jnp.zeros_like(ar)]
    for _ in range(SSM_STEP):
        r_, i_ = cmul(pr[-1], pi[-1], ar, ai)
        pr.append(r_)
        pi.append(i_)
    pw_r, pw_i = jnp.stack(pr), jnp.stack(pi)
    a8r, a8i = pw_r[SSM_STEP], pw_i[SSM_STEP]
    sr, si = jnp.ones_like(ar), jnp.zeros_like(ar)
    qr, qi = a8r, a8i
    e = seg
    while e:
        if e & 1:
            sr, si = cmul(sr, si, qr, qi)
        qr, qi = cmul(qr, qi, qr, qi)
        e >>= 1

    cr, ci = c_re.astype(F32), c_im.astype(F32)
    eye = jnp.eye(SSM_GBLK, dtype=F32)
    nb = N_GROUPS // SSM_GBLK

    rev_r = jnp.stack(pr[SSM_STEP - 1::-1])
    rev_i = jnp.stack(pi[SSM_STEP - 1::-1])
    wbr, wbi = cmul(rev_r[..., None], rev_i[..., None], bbr[None], bbi[None])
    wbg = jnp.stack([wbr, wbi], axis=0)
    wbg = wbg.reshape(2, SSM_STEP, nb, SSM_GBLK, STATE, GROUP)
    wb = (jnp.transpose(wbg, (2, 1, 3, 5, 0, 4))[:, :, :, :, :, None, :]
          * eye[None, None, :, None, None, :, None])
    wb = wb.reshape(nb, SSM_STEP * SSM_GBLK * GROUP, 2 * SSM_GBLK * STATE)

    upr, upi = pw_r[1:], pw_i[1:]
    mr, mi = cmul(cr[None], ci[None], upr[:, :, None, :], upi[:, :, None, :])
    wcg = jnp.stack([mr, -mi], axis=0)
    wcg = wcg.reshape(2, SSM_STEP, nb, SSM_GBLK, GROUP, STATE)
    wc = (jnp.transpose(wcg, (2, 0, 3, 5, 1, 4))[:, :, :, :, :, None, :]
          * eye[None, None, :, None, None, :, None])
    wc = wc.reshape(nb, 2 * SSM_GBLK * STATE, SSM_STEP * SSM_GBLK * GROUP)

    tr, ti = cmul(pw_r[:SSM_STEP, :, :, None], pw_i[:SSM_STEP, :, :, None], bbr[None], bbi[None])
    kt = (jnp.einsum('ghp,tgpk->tghk', cr, tr, precision=hi)
          - jnp.einsum('ghp,tgpk->tghk', ci, ti, precision=hi))
    lag = jnp.arange(SSM_STEP)[None, :] - jnp.arange(SSM_STEP)[:, None]
    kd = jnp.where((lag >= 0)[:, :, None, None, None], kt[jnp.clip(lag, 0, SSM_STEP - 1)], 0.0)
    kd = kd.reshape(SSM_STEP, SSM_STEP, nb, SSM_GBLK, GROUP, GROUP)
    wd = (jnp.transpose(kd, (2, 0, 3, 5, 1, 4))[:, :, :, :, :, None, :]
          * eye[None, None, :, None, None, :, None])
    wd = wd.reshape(nb, SSM_STEP * SSM_GBLK * GROUP, SSM_STEP * SSM_GBLK * GROUP)

    wcd = jnp.concatenate([wc, wd], axis=1)

    def lanes(t):
        return t.reshape(nb, SSM_GBLK * STATE)

    dl = jnp.pad(d.astype(F32).reshape(nb, SSM_GBLK * GROUP), ((0, 0), (0, 512 - LANES)))
    zero = jnp.zeros((nb, 512), F32)
    tab = jnp.stack([lanes(a8r), lanes(a8i), lanes(sr), lanes(si), dl, zero, zero, zero], axis=1)
    return wb.astype(BF16), wcd.astype(BF16), tab


def _attn_kernel(q_ref, kc_ref, kp_ref, km_ref, vc_ref, vp_ref, vm_ref, sink_ref, o_ref):
    n = pl.program_id(1)
    nk = 3 * BLK
    row = lax.broadcasted_iota(I32, (BLK, nk), 0)
    col = lax.broadcasted_iota(I32, (BLK, nk), 1)
    band = (col > row) & (col <= row + BLK) & (col >= 2 * BLK - BLK * n)
    mrow = col - 2 * BLK
    meta = (mrow >= PAD) & (mrow <= BLK * n + row)
    valid = band | meta

    for g in range(N_KV_HEADS):
        qg = jnp.concatenate([q_ref[:, LANES * (4 * g + jj):LANES * (4 * g + jj + 1)] for jj in range(4)], axis=0)
        acc = None
        for par in range(2):
            cb = 2 * g + par
            sl = slice(LANES * cb, LANES * (cb + 1))
            kk = jnp.concatenate([kp_ref[:, sl], kc_ref[:, sl], km_ref[:, sl]], axis=0)
            vv = jnp.concatenate([vp_ref[:, sl], vc_ref[:, sl], vm_ref[:, sl]], axis=0)
            s_all = _dot_nt(qg, kk)
            probs = []
            for jj in range(4):
                sink = sink_ref[8 * g + 2 * jj + par]
                s = jnp.where(valid, s_all[BLK * jj:BLK * (jj + 1), :], NEG_INF)
                mx = jnp.maximum(jnp.max(s, axis=1, keepdims=True), sink)
                p = jnp.exp(s - mx)
                den = jnp.sum(p, axis=1, keepdims=True) + jnp.exp(sink - mx)
                probs.append((p / den).astype(BF16))
            o = _dot(jnp.concatenate(probs, axis=0), vv)
            acc = o if acc is None else acc + o
        for jj in range(4):
            o_ref[:, LANES * (4 * g + jj):LANES * (4 * g + jj + 1)] = acc[BLK * jj:BLK * (jj + 1), :].astype(BF16)


def _attention(q, kx, vx, sinks, *, bsz, nb):
    np_ = q.shape[0]
    cur = lambda b, n: (b * nb + n, 0)
    prev = lambda b, n: (b * nb + jnp.maximum(n - 1, 0), 0)
    first = lambda b, n: (b * nb, 0)
    return pl.pallas_call(
        _attn_kernel,
        out_shape=jax.ShapeDtypeStruct((np_, 1024), BF16),
        grid=(bsz, nb),
        in_specs=[
            pl.BlockSpec((BLK, 1024), cur),
            pl.BlockSpec((BLK, 512), cur),
            pl.BlockSpec((BLK, 512), prev),
            pl.BlockSpec((BLK, 512), first),
            pl.BlockSpec((BLK, 512), cur),
            pl.BlockSpec((BLK, 512), prev),
            pl.BlockSpec((BLK, 512), first),
            pl.BlockSpec(memory_space=pltpu.SMEM),
        ],
        out_specs=pl.BlockSpec((BLK, 1024), cur),
        compiler_params=_cparams(("arbitrary", "arbitrary")),
        name="attn",
    )(q, kx, kx, kx, vx, vx, vx, sinks)


def _post_kernel(z_ref, at_ref, sgs_ref, sga_ref, hp_ref, wglu_ref, bglu_ref, wbs_ref, wba_ref, wout_ref,
                 nffn_ref, wrh_ref, wrl_ref, br_ref,
                 h2_ref, hnp_ref, eidx_ref, rank_ref, gate_ref, cnt_ref, carry_ref):
    i = pl.program_id(0)

    @pl.when(i == 0)
    def _():
        carry_ref[...] = jnp.zeros_like(carry_ref)

    z = z_ref[...]
    t = _dot(z, wglu_ref[...]) + bglu_ref[...]
    so = (z.astype(F32) * jax.nn.sigmoid(t)).astype(BF16)
    mix = (sgs_ref[...].astype(F32) * _dot(so, wbs_ref[...])
           + sga_ref[...].astype(F32) * _dot(at_ref[...], wba_ref[...]))
    h2 = hp_ref[...] + _dot(mix.astype(BF16), wout_ref[...])
    h2_ref[...] = h2

    ms = jnp.mean(h2 * h2, axis=-1, keepdims=True)
    hn = h2 * lax.rsqrt(ms + RMS_EPS) * nffn_ref[...]
    hb = hn.astype(BF16)
    hnp_ref[...] = _pack_bf16_pairs(hn)

    lo = (hn - hb.astype(F32)).astype(BF16)
    logits = _dot(hb, wrh_ref[...]) + _dot(lo, wrh_ref[...]) + _dot(hb, wrl_ref[...]) + br_ref[...]

    tm = logits.shape[0]
    lane = lax.broadcasted_iota(I32, (tm, LANES), 1)
    lane_f = lane.astype(F32)
    work = logits
    vals, hots, idxs = [], [], []
    for _ in range(TOP_K):
        mx = jnp.max(work, axis=1, keepdims=True)
        idx = jnp.min(jnp.where(work == mx, lane_f, float(LANES)), axis=1, keepdims=True)
        hot = lane_f == idx
        vals.append(mx)
        idxs.append(idx)
        hots.append(hot)
        work = jnp.where(hot, -jnp.inf, work)

    ex = [jnp.exp(v - vals[0]) for v in vals]
    tot = ex[0] + ex[1] + ex[2] + ex[3]
    gates = [e / tot for e in ex]

    onehot = jnp.zeros((tm, LANES), F32)
    for hot in hots:
        onehot = onehot + jnp.where(hot, 1.0, 0.0)
    r_i = lax.broadcasted_iota(I32, (tm, tm), 0)
    c_i = lax.broadcasted_iota(I32, (tm, tm), 1)
    tri = jnp.where(r_i > c_i, 1.0, 0.0).astype(BF16)
    before = _dot(tri, onehot.astype(BF16)) + carry_ref[0:1, :]
    ranks = [jnp.sum(jnp.where(hot, before, 0.0), axis=1, keepdims=True).astype(I32) for hot in hots]
    new_carry = carry_ref[0:1, :] + jnp.sum(onehot, axis=0, keepdims=True)
    carry_ref[...] = jnp.broadcast_to(new_carry, carry_ref.shape)
    cnt_ref[...] = jnp.broadcast_to(new_carry, cnt_ref.shape)

    def spread(cols, dtype):
        out = jnp.zeros((tm, LANES), dtype)
        for k, cval in enumerate(cols):
            out = jnp.where(lane == k, cval, out)
        return out

    eidx_ref[...] = spread([ix.astype(I32) for ix in idxs], I32)
    rank_ref[...] = spread(ranks, I32)
    gate_ref[...] = spread(gates, F32)


def _post(z, attn, sgs, sga, hp, wglu, bglu, wbs, wba, wout, nffn, wrh, wrl, br, *, tm):
    np_ = z.shape[0]
    row = lambda i: (i, 0)
    fixed = lambda i: (0, 0)
    sq = pl.BlockSpec((1024, 1024), fixed)
    vec = pl.BlockSpec((1, 1024), fixed)
    out_shape = (
        jax.ShapeDtypeStruct((np_, 1024), F32),
        jax.ShapeDtypeStruct((np_, 512), U32),
        jax.ShapeDtypeStruct((np_, LANES), I32),
        jax.ShapeDtypeStruct((np_, LANES), I32),
        jax.ShapeDtypeStruct((np_, LANES), F32),
        jax.ShapeDtypeStruct((8, LANES), F32),
    )
    return pl.pallas_call(
        _post_kernel,
        out_shape=out_shape,
        grid=(np_ // tm,),
        in_specs=[
            pl.BlockSpec((tm, 1024), row), pl.BlockSpec((tm, 1024), row),
            pl.BlockSpec((tm, 1024), row), pl.BlockSpec((tm, 1024), row),
            pl.BlockSpec((tm, 1024), row),
            sq, vec, sq, sq, sq, vec,
            pl.BlockSpec((1024, LANES), fixed), pl.BlockSpec((1024, LANES), fixed),
            pl.BlockSpec((1, LANES), fixed),
        ],
        out_specs=(
            pl.BlockSpec((tm, 1024), row), pl.BlockSpec((tm, 512), row),
            pl.BlockSpec((tm, LANES), row), pl.BlockSpec((tm, LANES), row), pl.BlockSpec((tm, LANES), row),
            pl.BlockSpec((8, LANES), fixed),
        ),
        scratch_shapes=[pltpu.VMEM((8, LANES), F32)],
        compiler_params=_cparams(("arbitrary",)),
        name="post",
    )(z, attn, sgs, sga, hp, wglu, bglu, wbs, wba, wout, nffn, wrh, wrl, br)


def _dispatch_kernel(pos_ref, h_ref, xs_ref, sem, *, tt):
    def row_copy(t, p):
        return pltpu.make_async_copy(h_ref.at[pl.ds(t, 1)], xs_ref.at[pl.ds(p, 1)], sem)

    def issue(t, carry):
        for k in range(TOP_K):
            row_copy(t, pos_ref[0, TOP_K * t + k]).start()
        return carry

    lax.fori_loop(0, tt, issue, 0)

    def drain(t, carry):
        for k in range(TOP_K):
            row_copy(t, pos_ref[0, TOP_K * t + k]).wait()
        return carry

    lax.fori_loop(0, tt, drain, 0)


def _dispatch(pos3, hnp, *, tt):
    np_ = hnp.shape[0]
    ns = np_ * TOP_K
    return pl.pallas_call(
        functools.partial(_dispatch_kernel, tt=tt),
        out_shape=jax.ShapeDtypeStruct((ns, 512), U32),
        grid=(np_ // tt,),
        in_specs=[
            pl.BlockSpec((None, 1, tt * TOP_K), lambda i: (i, 0, 0), memory_space=pltpu.SMEM),
            pl.BlockSpec((tt, 512), lambda i: (i, 0)),
        ],
        out_specs=pl.BlockSpec(memory_space=pl.ANY),
        scratch_shapes=[pltpu.SemaphoreType.DMA],
        compiler_params=_cparams(("arbitrary",)),
        name="dispatch",
    )(pos3, hnp)


def _moe_kernel(blk_ref, exp_ref, lo_ref, hi_ref, x_ref, wgu_ref, bgu_ref, wd_ref, bd_ref, y_ref,
                wgu_bf, wd_bf):
    i = pl.program_id(0)
    prev = jnp.maximum(i - 1, 0)
    new_expert = (i == 0) | (exp_ref[i] != exp_ref[prev])
    new_block = (i == 0) | (blk_ref[i] != blk_ref[prev])
    lo = lo_ref[i]
    hi = hi_ref[i]

    @pl.when(new_expert)
    def _():
        wgu_bf[...] = wgu_ref[...].astype(BF16)
        wd_bf[...] = wd_ref[...].astype(BF16)

    @pl.when(new_block)
    def _():
        y_ref[...] = jnp.zeros_like(y_ref)

    @pl.when(hi > lo)
    def _():
        xb = _unpack_bf16_pairs(x_ref[...]).astype(BF16)
        gu = _dot(xb, wgu_bf[...]) + bgu_ref[...]
        g = jnp.minimum(gu[:, :D_FF], SWIGLU_LIMIT)
        up = jnp.clip(gu[:, D_FF:], -SWIGLU_LIMIT, SWIGLU_LIMIT)
        hid = g * jax.nn.sigmoid(SWIGLU_ALPHA * g) * (up + 1.0)
        y = _dot(hid.astype(BF16), wd_bf[...]) + bd_ref[...]
        rows = lax.broadcasted_iota(I32, (BLK, 512), 0)
        mine = (rows >= lo) & (rows < hi)
        y_ref[...] = jnp.where(mine, _pack_bf16_pairs(y), y_ref[...])


def _moe(blk, exp, lo, hi, xs, wgu, bgu, wd, bd):
    ns = xs.shape[0]
    nw = blk.shape[0]
    grid_spec = pltpu.PrefetchScalarGridSpec(
        num_scalar_prefetch=4,
        grid=(nw,),
        in_specs=[
            pl.BlockSpec((BLK, 512), lambda i, b, e, l, h: (b[i], 0)),
            pl.BlockSpec((None, D_MODEL, 2 * D_FF), lambda i, b, e, l, h: (e[i], 0, 0)),
            pl.BlockSpec((None, 1, 2 * D_FF), lambda i, b, e, l, h: (e[i], 0, 0)),
            pl.BlockSpec((None, D_FF, D_MODEL), lambda i, b, e, l, h: (e[i], 0, 0)),
            pl.BlockSpec((None, 1, D_MODEL), lambda i, b, e, l, h: (e[i], 0, 0)),
        ],
        out_specs=pl.BlockSpec((BLK, 512), lambda i, b, e, l, h: (b[i], 0)),
        scratch_shapes=[pltpu.VMEM((D_MODEL, 2 * D_FF), BF16), pltpu.VMEM((D_FF, D_MODEL), BF16)],
    )
    return pl.pallas_call(
        _moe_kernel,
        out_shape=jax.ShapeDtypeStruct((ns, 512), U32),
        grid_spec=grid_spec,
        compiler_params=_cparams(("arbitrary",)),
        name="moe",
    )(blk, exp, lo, hi, xs, wgu, bgu, wd, bd)


def _combine_kernel(pos_ref, ys_ref, h2_ref, gate_ref, nfin_ref, o_ref, buf, sem):
    def row_copy(t, k, p):
        return pltpu.make_async_copy(ys_ref.at[pl.ds(p, 1)], buf.at[k, pl.ds(t, 1)], sem)

    def issue(t, carry):
        for k in range(TOP_K):
            row_copy(t, k, pos_ref[0, TOP_K * t + k]).start()
        return carry

    lax.fori_loop(0, BLK, issue, 0)

    def drain(t, carry):
        for k in range(TOP_K):
            row_copy(t, k, pos_ref[0, TOP_K * t + k]).wait()
        return carry

    lax.fori_loop(0, BLK, drain, 0)

    acc = h2_ref[...]
    gates = gate_ref[...]
    for k in range(TOP_K):
        acc = acc + gates[:, k:k + 1] * _unpack_bf16_pairs(buf[k])
    ms = jnp.mean(acc * acc, axis=-1, keepdims=True)
    o_ref[...] = acc * lax.rsqrt(ms + RMS_EPS) * nfin_ref[...]


def _combine(pos3, ys, h2, gate, nfin, *, bsz, nb, seq):
    tok = lambda b, n: (b * nb + n + 1, 0)
    return pl.pallas_call(
        _combine_kernel,
        out_shape=jax.ShapeDtypeStruct((bsz, seq, D_MODEL), F32),
        grid=(bsz, nb - 1),
        in_specs=[
            pl.BlockSpec((None, 1, BLK * TOP_K), lambda b, n: (b * nb + n + 1, 0, 0), memory_space=pltpu.SMEM),
            pl.BlockSpec(memory_space=pl.ANY),
            pl.BlockSpec((BLK, D_MODEL), tok),
            pl.BlockSpec((BLK, LANES), tok),
            pl.BlockSpec((1, D_MODEL), lambda b, n: (0, 0)),
        ],
        out_specs=pl.BlockSpec((None, BLK, D_MODEL), lambda b, n: (b, n, 0)),
        scratch_shapes=[pltpu.VMEM((TOP_K, BLK, 512), U32), pltpu.SemaphoreType.DMA],
        compiler_params=_cparams(("arbitrary", "arbitrary")),
        name="combine",
    )(pos3, ys, h2, gate, nfin)


def _rope_tables(lp):
    pos = jnp.arange(lp, dtype=F32) - PAD
    inv_freq = ROPE_THETA ** (-jnp.arange(0, 2 * ROT_HALF, 2, dtype=F32) / (2 * ROT_HALF))
    ang = pos[:, None] * inv_freq[None, :]
    cos, sin = jnp.cos(ang), jnp.sin(ang)
    ones = jnp.ones((lp, HEAD_DIM - 2 * ROT_HALF), F32)
    zeros8 = jnp.zeros((lp, ROT_HALF), F32)
    zrest = jnp.zeros((lp, HEAD_DIM - 2 * ROT_HALF), F32)
    c = jnp.concatenate([cos, cos, ones], axis=1)
    s1 = jnp.concatenate([zeros8, sin, zrest], axis=1)
    s2 = jnp.concatenate([-sin, zeros8, zrest], axis=1)
    tile = lambda t: jnp.concatenate([t, t], axis=1)
    return tile(c), tile(s1), tile(s2)


def _segments(counts, n_slots):
    ends = jnp.cumsum(counts)
    starts = ends - counts
    nblocks = n_slots // BLK
    cuts = jnp.sort(jnp.concatenate([jnp.arange(nblocks, dtype=I32) * BLK, starts[1:].astype(I32)]))
    nxt = jnp.concatenate([cuts[1:], jnp.array([n_slots], I32)])
    blk = jnp.minimum(cuts // BLK, nblocks - 1)
    exp = jnp.clip(jnp.searchsorted(ends, cuts, side='right'), 0, N_EXPERTS - 1).astype(I32)
    lo = cuts - blk * BLK
    hi = lo + (nxt - cuts)
    return starts, blk.astype(I32), exp, lo.astype(I32), hi.astype(I32)


def kernel(x, meta_tokens, norm_mix, w_in, ssm_lam_re, ssm_lam_im, ssm_log_dt, ssm_b_re, ssm_b_im,
           ssm_c_re, ssm_c_im, ssm_d, w_glu, b_glu, attn_sinks, w_br_ssm, w_br_attn, w_out,
           norm_ffn, w_router, b_router, w_gate_up, b_gate_up, w_down, b_down, norm_final):
    bsz, seq, _ = x.shape
    lp = seq + BLK
    nb = lp // BLK
    np_ = bsz * lp
    tm = 640 if lp % 640 == 0 else BLK
    t8 = lp // SSM_STEP
    seg = t8 // SSM_SEGS
    nchunk = 5 if (t8 % 5 == 0 and (t8 // 5) % 16 == 0) else 1

    meta = jnp.broadcast_to(meta_tokens[None].astype(x.dtype), (bsz, N_META, D_MODEL))
    hp = jnp.concatenate([jnp.zeros((bsz, PAD, D_MODEL), x.dtype), meta, x], axis=1).reshape(np_, D_MODEL)

    rc, rs1, rs2 = _rope_tables(lp)
    u, q, kx, vx, sgs, sga = _inproj(hp, norm_mix[0][None], w_in[0].astype(BF16), rc, rs1, rs2, tm=tm, lp=lp)

    wb, wcd, tab = _ssm_tables(ssm_lam_re[0], ssm_lam_im[0], ssm_log_dt[0], ssm_b_re[0], ssm_b_im[0],
                               ssm_c_re[0], ssm_c_im[0], ssm_d[0], seg)
    z = _ssm(u, wb, wcd, tab, bsz=bsz, lp=lp, nchunk=nchunk)

    attn = _attention(q, kx, vx, attn_sinks[0].astype(F32), bsz=bsz, nb=nb)

    wr = jnp.pad(w_router[0].astype(F32), ((0, 0), (0, LANES - N_EXPERTS)))
    wrh = wr.astype(BF16)
    wrl = (wr - wrh.astype(F32)).astype(BF16)
    br = jnp.concatenate([b_router[0].astype(F32), jnp.full((LANES - N_EXPERTS,), NEG_INF, F32)])[None]
    h2, hnp, eidx, rank, gate, cnt = _post(
        z, attn, sgs, sga, hp, w_glu[0].astype(BF16), b_glu[0][None], w_br_ssm[0].astype(BF16),
        w_br_attn[0].astype(BF16), w_out[0].astype(BF16), norm_ffn[0][None], wrh, wrl, br, tm=tm)

    counts = cnt[0, :N_EXPERTS].astype(I32)
    n_slots = np_ * TOP_K
    starts, sblk, sexp, slo, shi = _segments(counts, n_slots)
    pos = (starts[eidx[:, :TOP_K]] + rank[:, :TOP_K]).astype(I32)

    tt = 256
    xs = _dispatch(pos.reshape(np_ // tt, 1, tt * TOP_K), hnp, tt=tt)
    ys = _moe(sblk, sexp, slo, shi, xs, w_gate_up[0], b_gate_up[0][:, None, :], w_down[0], b_down[0][:, None, :])
    return _combine(pos.reshape(np_ // BLK, 1, BLK * TOP_K), ys, h2, gate, norm_final[None],
                    bsz=bsz, nb=nb, seq=seq)
```

```python
import functools
import math

import jax
import jax.numpy as jnp
import numpy as np
from jax import lax
from jax.experimental import pallas as pl
from jax.experimental.pallas import tpu as pltpu

F32 = jnp.float32
BF16 = jnp.bfloat16
I32 = jnp.int32
U32 = jnp.uint32

D_MODEL = 1024
N_META = 16
BLK = 128
PAD = BLK - N_META
N_GROUPS = 64
GROUP = 16
STATE = 64
N_Q_HEADS = 16
N_KV_HEADS = 2
HEAD_DIM = 64
ROT_HALF = 8
ROPE_THETA = 500000.0
N_EXPERTS = 32
TOP_K = 4
D_FF = 1024
SWIGLU_LIMIT = 7.0
SWIGLU_ALPHA = 1.702
RMS_EPS = 1e-5
NEG_INF = -1e30
SSM_STEP = 8
SSM_SEGS = 8
SSM_GBLK = 8
LANES = 128
MOE_BLK = 256
VMEM_LIMIT = 56 * 1024 * 1024


def _cparams(sem):
    return pltpu.CompilerParams(dimension_semantics=sem, vmem_limit_bytes=VMEM_LIMIT)


def _dot(a, b):
    return jnp.dot(a, b, preferred_element_type=F32)


def _dot_nt(a, b):
    return lax.dot_general(a, b, (((1,), (1,)), ((), ())), preferred_element_type=F32)


def _pack_bf16_pairs(x):
    w = x.shape[1] // 2
    bits = pltpu.bitcast(x.astype(BF16).astype(F32), U32)
    return (bits[:, w:] & jnp.uint32(0xFFFF0000)) | (bits[:, :w] >> 16)


def _unpack_bf16_pairs(wd):
    lo = pltpu.bitcast(wd << 16, F32)
    hi = pltpu.bitcast(wd & jnp.uint32(0xFFFF0000), F32)
    return jnp.concatenate([lo, hi], axis=1)


def _padded_rows(x_refs, meta_ref, first_tile):
    head = jnp.concatenate([jnp.zeros((PAD, D_MODEL), F32), meta_ref[...]], axis=0)
    first = jnp.where(first_tile, head, x_refs[0][...])
    return jnp.concatenate([first] + [r[...] for r in x_refs[1:]], axis=0)


def _padded_row_specs(tm, tiles_per_seq):
    per = tm // BLK

    def spec(r):
        return pl.BlockSpec((None, BLK, D_MODEL),
                            lambda i: (i // tiles_per_seq, jnp.maximum((i % tiles_per_seq) * per + r - 1, 0), 0))

    return [spec(r) for r in range(per)]


def _inproj_kernel(*refs, tiles_per_seq, per):
    x_refs, (meta_ref, nrm_ref, w_ref, rc_ref, rs1_ref, rs2_ref,
             u_ref, q_ref, kx_ref, vx_ref, sgs_ref, sga_ref) = refs[:per], refs[per:]
    x = _padded_rows(x_refs, meta_ref, pl.program_id(0) % tiles_per_seq == 0)
    ms = jnp.mean(x * x, axis=-1, keepdims=True)
    hn = (x * lax.rsqrt(ms + RMS_EPS) * nrm_ref[...]).astype(BF16)

    def proj(lo, hi):
        return _dot(hn, w_ref[:, lo:hi])

    u_ref[...] = proj(0, 1024)

    rc = rc_ref[...]
    rs1 = rs1_ref[...]
    rs2 = rs2_ref[...]

    def rope(blk):
        return blk * rc + pltpu.roll(blk, ROT_HALF, 1) * rs1 + pltpu.roll(blk, LANES - ROT_HALF, 1) * rs2

    qf = proj(1024, 2048)
    scale = 1.0 / math.sqrt(HEAD_DIM)
    for j in range(8):
        q_ref[:, LANES * j:LANES * (j + 1)] = (rope(qf[:, LANES * j:LANES * (j + 1)]) * scale).astype(BF16)

    kvf = proj(2048, 2304)
    k = rope(kvf[:, :LANES])
    v = kvf[:, LANES:]
    low = lax.broadcasted_iota(I32, k.shape, 1) < HEAD_DIM

    def expand(t, ref):
        tr = pltpu.roll(t, HEAD_DIM, 1)
        zero = jnp.zeros_like(t)
        ref[:, 0:128] = jnp.where(low, t, zero).astype(BF16)
        ref[:, 128:256] = jnp.where(low, zero, tr).astype(BF16)
        ref[:, 256:384] = jnp.where(low, tr, zero).astype(BF16)
        ref[:, 384:512] = jnp.where(low, zero, t).astype(BF16)

    expand(k, kx_ref)
    expand(v, vx_ref)
    sgs_ref[...] = jax.nn.sigmoid(proj(2304, 3328)).astype(BF16)
    sga_ref[...] = jax.nn.sigmoid(proj(3328, 4352)).astype(BF16)


def _inproj(x, meta, nrm, w_bf, rc, rs1, rs2, *, tm, lp):
    np_ = x.shape[0] * lp
    tiles_per_seq = lp // tm
    per = tm // BLK
    row = lambda i: (i, 0)
    fixed = lambda i: (0, 0)
    rope_idx = lambda i: (i % tiles_per_seq, 0)
    out_shape = (
        jax.ShapeDtypeStruct((np_, 1024), F32),
        jax.ShapeDtypeStruct((np_, 1024), BF16),
        jax.ShapeDtypeStruct((np_, 512), BF16),
        jax.ShapeDtypeStruct((np_, 512), BF16),
        jax.ShapeDtypeStruct((np_, 1024), BF16),
        jax.ShapeDtypeStruct((np_, 1024), BF16),
    )
    return pl.pallas_call(
        functools.partial(_inproj_kernel, tiles_per_seq=tiles_per_seq, per=per),
        out_shape=out_shape,
        grid=(np_ // tm,),
        in_specs=_padded_row_specs(tm, tiles_per_seq) + [
            pl.BlockSpec((N_META, D_MODEL), fixed),
            pl.BlockSpec((1, D_MODEL), fixed),
            pl.BlockSpec(w_bf.shape, fixed),
            pl.BlockSpec((tm, LANES), rope_idx),
            pl.BlockSpec((tm, LANES), rope_idx),
            pl.BlockSpec((tm, LANES), rope_idx),
        ],
        out_specs=(
            pl.BlockSpec((tm, 1024), row),
            pl.BlockSpec((tm, 1024), row),
            pl.BlockSpec((tm, 512), row),
            pl.BlockSpec((tm, 512), row),
            pl.BlockSpec((tm, 1024), row),
            pl.BlockSpec((tm, 1024), row),
        ),
        compiler_params=_cparams(("arbitrary",)),
        name="inproj",
    )(*([x] * per), meta, nrm, w_bf, rc, rs1, rs2)


def _expand_block_diag(dst_ref, row0, t_ref, e_ref, row_shift, col_shift):
    n = t_ref.shape[0]
    step = 256
    for r0 in range(0, n, step):
        full = _dot(t_ref[r0:r0 + step, :], e_ref[...])
        rg = (lax.broadcasted_iota(I32, full.shape, 0) + r0) >> row_shift
        cg = lax.broadcasted_iota(I32, full.shape, 1) >> col_shift
        keep = ((rg ^ cg) & (SSM_GBLK - 1)) == 0
        dst_ref[row0 + r0:row0 + r0 + step, :] = jnp.where(keep, full, 0.0).astype(BF16)


def _ssm_kernel(u_ref, tb_ref, tc_ref, td_ref, eb_ref, ec_ref, tab_ref, z_ref,
                wb_ref, wcd_ref, ust_ref, bu_ref, yn_ref, *, t8, seg, nchunk):
    rc = t8 // nchunk

    @pl.when(pl.program_id(1) == 0)
    def _():
        _expand_block_diag(wb_ref, 0, tb_ref, eb_ref, 4, 6)
        _expand_block_diag(wcd_ref, 0, tc_ref, ec_ref, 6, 4)
        _expand_block_diag(wcd_ref, 1024, td_ref, ec_ref, 4, 4)

    for i in range(SSM_STEP):
        ust_ref[:, LANES * i:LANES * (i + 1)] = u_ref[pl.ds(i, t8, stride=SSM_STEP), :].astype(BF16)

    for c in range(nchunk):
        r = _dot(ust_ref[c * rc:(c + 1) * rc, :], wb_ref[...])
        for m in range(8):
            bu_ref[m, c * rc:(c + 1) * rc, :] = r[:, LANES * m:LANES * (m + 1)]

    a_re = [jnp.broadcast_to(tab_ref[0:1, LANES * m:LANES * (m + 1)], (SSM_SEGS, LANES)) for m in range(4)]
    a_im = [jnp.broadcast_to(tab_ref[1:2, LANES * m:LANES * (m + 1)], (SSM_SEGS, LANES)) for m in range(4)]
    s_re = [jnp.broadcast_to(tab_ref[2:3, LANES * m:LANES * (m + 1)], (SSM_SEGS, LANES)) for m in range(4)]
    s_im = [jnp.broadcast_to(tab_ref[3:4, LANES * m:LANES * (m + 1)], (SSM_SEGS, LANES)) for m in range(4)]

    def cmul_add(mr, mi, xr, xi, br, bi):
        return mr * xr - mi * xi + br, mr * xi + mi * xr + bi

    def load(k):
        return [bu_ref[m, pl.ds(k, SSM_SEGS, stride=seg), :] for m in range(8)]

    def advance(st, b):
        nr, ni = [], []
        for m in range(4):
            r_, i_ = cmul_add(a_re[m], a_im[m], st[m], st[4 + m], b[m], b[4 + m])
            nr.append(r_)
            ni.append(i_)
        return tuple(nr + ni)

    zeros = tuple(jnp.zeros((SSM_SEGS, LANES), F32) for _ in range(8))
    fin = lax.fori_loop(0, seg, lambda k, st: advance(st, load(k)), zeros)

    first = lax.broadcasted_iota(I32, (SSM_SEGS, LANES), 0) == 0

    def shift_down(t):
        return jnp.where(first, 0.0, pltpu.roll(t, 1, 0))

    tot = fin
    for _ in range(SSM_SEGS - 1):
        nxt_r, nxt_i = [], []
        for m in range(4):
            r_, i_ = cmul_add(s_re[m], s_im[m], shift_down(tot[m]), shift_down(tot[4 + m]), fin[m], fin[4 + m])
            nxt_r.append(r_)
            nxt_i.append(i_)
        tot = tuple(nxt_r + nxt_i)
    init = tuple(shift_down(t) for t in tot)

    def pass2(k, st):
        b = load(k)
        for m in range(8):
            bu_ref[m, pl.ds(k, SSM_SEGS, stride=seg), :] = st[m]
        return advance(st, b)

    lax.fori_loop(0, seg, pass2, init)

    for c in range(nchunk):
        xp = jnp.concatenate([bu_ref[m, c * rc:(c + 1) * rc, :] for m in range(8)], axis=1).astype(BF16)
        lhs = jnp.concatenate([xp, ust_ref[c * rc:(c + 1) * rc, :]], axis=1)
        y = _dot(lhs, wcd_ref[...])
        for i in range(SSM_STEP):
            yn_ref[pl.ds(SSM_STEP * c * rc + i, rc, stride=SSM_STEP), :] = y[:, LANES * i:LANES * (i + 1)]

    y = yn_ref[...] + tab_ref[4:5, 0:LANES] * u_ref[...]
    z_ref[...] = jax.nn.gelu(y).astype(BF16)


def _ssm(u, tb, tc, td, tab, *, bsz, lp, nchunk):
    np_ = u.shape[0]
    t8 = lp // SSM_STEP
    seg = t8 // SSM_SEGS
    nblk = D_MODEL // LANES
    col = jnp.arange(1024)
    src_b = (col >> 9) * STATE + (col & (STATE - 1))
    src_c = (col >> 7) * GROUP + (col & (GROUP - 1))
    eb = (jnp.arange(LANES)[:, None] == src_b[None, :]).astype(BF16)
    ec = (jnp.arange(LANES)[:, None] == src_c[None, :]).astype(BF16)
    kern = functools.partial(_ssm_kernel, t8=t8, seg=seg, nchunk=nchunk)
    per_q = lambda q, b: (q, 0, 0)
    fixed = lambda q, b: (0, 0)
    return pl.pallas_call(
        kern,
        out_shape=jax.ShapeDtypeStruct((np_, D_MODEL), BF16),
        grid=(nblk, bsz),
        in_specs=[
            pl.BlockSpec((lp, LANES), lambda q, b: (b, q)),
            pl.BlockSpec((None, 1024, LANES), per_q),
            pl.BlockSpec((None, 1024, LANES), per_q),
            pl.BlockSpec((None, 1024, LANES), per_q),
            pl.BlockSpec((LANES, 1024), fixed),
            pl.BlockSpec((LANES, 1024), fixed),
            pl.BlockSpec((None, 8, 512), per_q),
        ],
        out_specs=pl.BlockSpec((lp, LANES), lambda q, b: (b, q)),
        scratch_shapes=[
            pltpu.VMEM((1024, 1024), BF16),
            pltpu.VMEM((2048, 1024), BF16),
            pltpu.VMEM((t8, 1024), BF16),
            pltpu.VMEM((8, t8, LANES), F32),
            pltpu.VMEM((lp, LANES), F32),
        ],
        compiler_params=_cparams(("arbitrary", "arbitrary")),
        name="ssm",
    )(u, tb, tc, td, eb, ec, tab)


def _ssm_tables(lam_re, lam_im, log_dt, b_re, b_im, c_re, c_im, d, seg):
    hi = lax.Precision.HIGHEST
    lr, li = lam_re.astype(F32), lam_im.astype(F32)
    dt = jnp.exp(log_dt.astype(F32))[:, None]
    mag = jnp.exp(dt * lr)
    ar, ai = mag * jnp.cos(dt * li), mag * jnp.sin(dt * li)
    den = lr * lr + li * li
    nr, ni = ar - 1.0, ai
    fr, fi = (nr * lr + ni * li) / den, (ni * lr - nr * li) / den
    br, bi = b_re.astype(F32), b_im.astype(F32)
    bbr = fr[..., None] * br - fi[..., None] * bi
    bbi = fr[..., None] * bi + fi[..., None] * br

    def cmul(xr, xi, yr, yi):
        return xr * yr - xi * yi, xr * yi + xi * yr

    pr, pi = [jnp.ones_like(ar)], [jnp.zeros_like(ar)]
    for _ in range(SSM_STEP):
        r_, i_ = cmul(pr[-1], pi[-1], ar, ai)
        pr.append(r_)
        pi.append(i_)
    pw_r, pw_i = jnp.stack(pr), jnp.stack(pi)
    a8r, a8i = pw_r[SSM_STEP], pw_i[SSM_STEP]
    sr, si = jnp.ones_like(ar), jnp.zeros_like(ar)
    qr, qi = a8r, a8i
    e = seg
    while e:
        if e & 1:
            sr, si = cmul(sr, si, qr, qi)
        qr, qi = cmul(qr, qi, qr, qi)
        e >>= 1

    cr, ci = c_re.astype(F32), c_im.astype(F32)
    nb = N_GROUPS // SSM_GBLK

    rev_r = jnp.stack(pr[SSM_STEP - 1::-1])
    rev_i = jnp.stack(pi[SSM_STEP - 1::-1])
    wbr, wbi = cmul(rev_r[..., None], rev_i[..., None], bbr[None], bbi[None])
    wbg = jnp.stack([wbr, wbi], axis=0)
    wbg = wbg.reshape(2, SSM_STEP, nb, SSM_GBLK, STATE, GROUP)
    tb = jnp.transpose(wbg, (2, 1, 3, 5, 0, 4)).reshape(nb, 1024, LANES)

    upr, upi = pw_r[1:], pw_i[1:]
    mr, mi = cmul(cr[None], ci[None], upr[:, :, None, :], upi[:, :, None, :])
    wcg = jnp.stack([mr, -mi], axis=0)
    wcg = wcg.reshape(2, SSM_STEP, nb, SSM_GBLK, GROUP, STATE)
    tc = jnp.transpose(wcg, (2, 0, 3, 5, 1, 4)).reshape(nb, 1024, LANES)

    tr, ti = cmul(pw_r[:SSM_STEP, :, :, None], pw_i[:SSM_STEP, :, :, None], bbr[None], bbi[None])
    kt = (jnp.einsum('ghp,tgpk->tghk', cr, tr, precision=hi)
          - jnp.einsum('ghp,tgpk->tghk', ci, ti, precision=hi))
    zero_tap = jnp.zeros_like(kt[0])
    kd = jnp.stack([jnp.stack([kt[i - j] if i >= j else zero_tap for i in range(SSM_STEP)])
                    for j in range(SSM_STEP)])
    kd = kd.reshape(SSM_STEP, SSM_STEP, nb, SSM_GBLK, GROUP, GROUP)
    td = jnp.transpose(kd, (2, 0, 3, 5, 1, 4)).reshape(nb, 1024, LANES)

    def lanes(t):
        return t.reshape(nb, SSM_GBLK * STATE)

    dl = jnp.pad(d.astype(F32).reshape(nb, SSM_GBLK * GROUP), ((0, 0), (0, 512 - LANES)))
    zero = jnp.zeros((nb, 512), F32)
    tab = jnp.stack([lanes(a8r), lanes(a8i), lanes(sr), lanes(si), dl, zero, zero, zero], axis=1)
    return tb.astype(BF16), tc.astype(BF16), td.astype(BF16), tab


def _attn_kernel(q_ref, kc_ref, kp_ref, km_ref, vc_ref, vp_ref, vm_ref, sink_ref, o_ref):
    n = pl.program_id(1)
    nk = 3 * BLK
    row = lax.broadcasted_iota(I32, (BLK, nk), 0)
    col = lax.broadcasted_iota(I32, (BLK, nk), 1)
    band = (col > row) & (col <= row + BLK) & (col >= 2 * BLK - BLK * n)
    mrow = col - 2 * BLK
    meta = (mrow >= PAD) & (mrow <= BLK * n + row)
    valid = band | meta

    for g in range(N_KV_HEADS):
        qg = jnp.concatenate([q_ref[:, LANES * (4 * g + jj):LANES * (4 * g + jj + 1)] for jj in range(4)], axis=0)
        acc = None
        for par in range(2):
            cb = 2 * g + par
            sl = slice(LANES * cb, LANES * (cb + 1))
            kk = jnp.concatenate([kp_ref[:, sl], kc_ref[:, sl], km_ref[:, sl]], axis=0)
            vv = jnp.concatenate([vp_ref[:, sl], vc_ref[:, sl], vm_ref[:, sl]], axis=0)
            s_all = _dot_nt(qg, kk)
            probs = []
            for jj in range(4):
                sink = sink_ref[8 * g + 2 * jj + par]
                s = jnp.where(valid, s_all[BLK * jj:BLK * (jj + 1), :], NEG_INF)
                mx = jnp.maximum(jnp.max(s, axis=1, keepdims=True), sink)
                p = jnp.exp(s - mx)
                den = jnp.sum(p, axis=1, keepdims=True) + jnp.exp(sink - mx)
                probs.append((p / den).astype(BF16))
            o = _dot(jnp.concatenate(probs, axis=0), vv)
            acc = o if acc is None else acc + o
        for jj in range(4):
            o_ref[:, LANES * (4 * g + jj):LANES * (4 * g + jj + 1)] = acc[BLK * jj:BLK * (jj + 1), :].astype(BF16)


def _attention(q, kx, vx, sinks, *, bsz, nb):
    np_ = q.shape[0]
    cur = lambda b, n: (b * nb + n, 0)
    prev = lambda b, n: (b * nb + jnp.maximum(n - 1, 0), 0)
    first = lambda b, n: (b * nb, 0)
    return pl.pallas_call(
        _attn_kernel,
        out_shape=jax.ShapeDtypeStruct((np_, 1024), BF16),
        grid=(bsz, nb),
        in_specs=[
            pl.BlockSpec((BLK, 1024), cur),
            pl.BlockSpec((BLK, 512), cur),
            pl.BlockSpec((BLK, 512), prev),
            pl.BlockSpec((BLK, 512), first),
            pl.BlockSpec((BLK, 512), cur),
            pl.BlockSpec((BLK, 512), prev),
            pl.BlockSpec((BLK, 512), first),
            pl.BlockSpec(memory_space=pltpu.SMEM),
        ],
        out_specs=pl.BlockSpec((BLK, 1024), cur),
        compiler_params=_cparams(("arbitrary", "arbitrary")),
        name="attn",
    )(q, kx, kx, kx, vx, vx, vx, sinks)


def _post_kernel(*refs, tiles_per_seq, per):
    x_refs, (meta_ref, z_ref, at_ref, sgs_ref, sga_ref, wglu_ref, bglu_ref, wbs_ref, wba_ref, wout_ref,
             nffn_ref, wrh_ref, wrl_ref, br_ref,
             h2_ref, hnp_ref, eidx_ref, rank_ref, gate_ref, cnt_ref, carry_ref) = refs[:per], refs[per:]
    i = pl.program_id(0)

    @pl.when(i == 0)
    def _():
        carry_ref[...] = jnp.zeros_like(carry_ref)

    z = z_ref[...]
    t = _dot(z, wglu_ref[...]) + bglu_ref[...]
    so = (z.astype(F32) * jax.nn.sigmoid(t)).astype(BF16)
    mix = (sgs_ref[...].astype(F32) * _dot(so, wbs_ref[...])
           + sga_ref[...].astype(F32) * _dot(at_ref[...], wba_ref[...]))
    h2 = _padded_rows(x_refs, meta_ref, i % tiles_per_seq == 0) + _dot(mix.astype(BF16), wout_ref[...])
    h2_ref[...] = h2

    ms = jnp.mean(h2 * h2, axis=-1, keepdims=True)
    hn = h2 * lax.rsqrt(ms + RMS_EPS) * nffn_ref[...]
    hb = hn.astype(BF16)
    hnp_ref[...] = _pack_bf16_pairs(hn)

    lo = (hn - hb.astype(F32)).astype(BF16)
    logits = _dot(hb, wrh_ref[...]) + _dot(lo, wrh_ref[...]) + _dot(hb, wrl_ref[...]) + br_ref[...]

    tm = logits.shape[0]
    lane = lax.broadcasted_iota(I32, (tm, LANES), 1)
    lane_f = lane.astype(F32)
    work = logits
    vals, hots, idxs = [], [], []
    for _ in range(TOP_K):
        mx = jnp.max(work, axis=1, keepdims=True)
        idx = jnp.min(jnp.where(work == mx, lane_f, float(LANES)), axis=1, keepdims=True)
        hot = lane_f == idx
        vals.append(mx)
        idxs.append(idx)
        hots.append(hot)
        work = jnp.where(hot, -jnp.inf, work)

    ex = [jnp.exp(v - vals[0]) for v in vals]
    tot = ex[0] + ex[1] + ex[2] + ex[3]
    gates = [e / tot for e in ex]

    onehot = jnp.zeros((tm, LANES), F32)
    for hot in hots:
        onehot = onehot + jnp.where(hot, 1.0, 0.0)
    r_i = lax.broadcasted_iota(I32, (tm, tm), 0)
    c_i = lax.broadcasted_iota(I32, (tm, tm), 1)
    tri = jnp.where(r_i > c_i, 1.0, 0.0).astype(BF16)
    before = _dot(tri, onehot.astype(BF16)) + carry_ref[0:1, :]
    ranks = [jnp.sum(jnp.where(hot, before, 0.0), axis=1, keepdims=True).astype(I32) for hot in hots]
    new_carry = carry_ref[0:1, :] + jnp.sum(onehot, axis=0, keepdims=True)
    carry_ref[...] = jnp.broadcast_to(new_carry, carry_ref.shape)
    cnt_ref[...] = jnp.broadcast_to(new_carry, cnt_ref.shape)

    def spread(cols, dtype):
        out = jnp.zeros((tm, LANES), dtype)
        for k, cval in enumerate(cols):
            out = jnp.where(lane == k, cval, out)
        return out

    eidx_ref[...] = spread([ix.astype(I32) for ix in idxs], I32)
    rank_ref[...] = spread(ranks, I32)
    gate_ref[...] = spread(gates, F32)


def _post(x, meta, z, attn, sgs, sga, wglu, bglu, wbs, wba, wout, nffn, wrh, wrl, br, *, tm, lp):
    np_ = z.shape[0]
    tiles_per_seq = lp // tm
    per = tm // BLK
    row = lambda i: (i, 0)
    fixed = lambda i: (0, 0)
    sq = pl.BlockSpec((1024, 1024), fixed)
    vec = pl.BlockSpec((1, 1024), fixed)
    out_shape = (
        jax.ShapeDtypeStruct((np_, 1024), F32),
        jax.ShapeDtypeStruct((np_, 512), U32),
        jax.ShapeDtypeStruct((np_, LANES), I32),
        jax.ShapeDtypeStruct((np_, LANES), I32),
        jax.ShapeDtypeStruct((np_, LANES), F32),
        jax.ShapeDtypeStruct((8, LANES), F32),
    )
    return pl.pallas_call(
        functools.partial(_post_kernel, tiles_per_seq=tiles_per_seq, per=per),
        out_shape=out_shape,
        grid=(np_ // tm,),
        in_specs=_padded_row_specs(tm, tiles_per_seq) + [
            pl.BlockSpec((N_META, D_MODEL), fixed),
            pl.BlockSpec((tm, 1024), row), pl.BlockSpec((tm, 1024), row),
            pl.BlockSpec((tm, 1024), row), pl.BlockSpec((tm, 1024), row),
            sq, vec, sq, sq, sq, vec,
            pl.BlockSpec((1024, LANES), fixed), pl.BlockSpec((1024, LANES), fixed),
            pl.BlockSpec((1, LANES), fixed),
        ],
        out_specs=(
            pl.BlockSpec((tm, 1024), row), pl.BlockSpec((tm, 512), row),
            pl.BlockSpec((tm, LANES), row), pl.BlockSpec((tm, LANES), row), pl.BlockSpec((tm, LANES), row),
            pl.BlockSpec((8, LANES), fixed),
        ),
        scratch_shapes=[pltpu.VMEM((8, LANES), F32)],
        compiler_params=_cparams(("arbitrary",)),
        name="post",
    )(*([x] * per), meta, z, attn, sgs, sga, wglu, bglu, wbs, wba, wout, nffn, wrh, wrl, br)


def _dispatch_kernel(pos_ref, h_ref, xs_ref, sem, *, tt):
    def row_copy(t, p):
        return pltpu.make_async_copy(h_ref.at[pl.ds(t, 1)], xs_ref.at[pl.ds(p, 1)], sem)

    def issue(t, carry):
        for k in range(TOP_K):
            row_copy(t, pos_ref[0, TOP_K * t + k]).start()
        return carry

    lax.fori_loop(0, tt, issue, 0)

    def drain(t, carry):
        for k in range(TOP_K):
            row_copy(t, pos_ref[0, TOP_K * t + k]).wait()
        return carry

    lax.fori_loop(0, tt, drain, 0)


def _dispatch(pos3, hnp, *, tt):
    np_ = hnp.shape[0]
    ns = np_ * TOP_K
    return pl.pallas_call(
        functools.partial(_dispatch_kernel, tt=tt),
        out_shape=jax.ShapeDtypeStruct((ns, 512), U32),
        grid=(np_ // tt,),
        in_specs=[
            pl.BlockSpec((None, 1, tt * TOP_K), lambda i: (i, 0, 0), memory_space=pltpu.SMEM),
            pl.BlockSpec((tt, 512), lambda i: (i, 0)),
        ],
        out_specs=pl.BlockSpec(memory_space=pl.ANY),
        scratch_shapes=[pltpu.SemaphoreType.DMA],
        compiler_params=_cparams(("arbitrary",)),
        name="dispatch",
    )(pos3, hnp)


def _moe_kernel(blk_ref, exp_ref, lo_ref, hi_ref, x_ref, wgu_ref, bgu_ref, wd_ref, bd_ref, y_ref,
                wgu_bf, wd_bf):
    i = pl.program_id(0)
    prev = jnp.maximum(i - 1, 0)
    new_expert = (i == 0) | (exp_ref[i] != exp_ref[prev])
    new_block = (i == 0) | (blk_ref[i] != blk_ref[prev])
    lo = lo_ref[i]
    hi = hi_ref[i]

    @pl.when(new_expert)
    def _():
        wgu_bf[...] = wgu_ref[...].astype(BF16)
        wd_bf[...] = wd_ref[...].astype(BF16)

    @pl.when(new_block)
    def _():
        y_ref[...] = jnp.zeros_like(y_ref)

    @pl.when(hi > lo)
    def _():
        xb = _unpack_bf16_pairs(x_ref[...]).astype(BF16)
        gu = _dot(xb, wgu_bf[...]) + bgu_ref[...]
        g = jnp.minimum(gu[:, :D_FF], SWIGLU_LIMIT)
        up = jnp.clip(gu[:, D_FF:], -SWIGLU_LIMIT, SWIGLU_LIMIT)
        hid = g * jax.nn.sigmoid(SWIGLU_ALPHA * g) * (up + 1.0)
        y = _dot(hid.astype(BF16), wd_bf[...]) + bd_ref[...]
        rows = lax.broadcasted_iota(I32, (MOE_BLK, 512), 0)
        mine = (rows >= lo) & (rows < hi)
        y_ref[...] = jnp.where(mine, _pack_bf16_pairs(y), y_ref[...])


def _moe(blk, exp, lo, hi, xs, wgu, bgu, wd, bd):
    ns = xs.shape[0]
    nw = blk.shape[0]
    grid_spec = pltpu.PrefetchScalarGridSpec(
        num_scalar_prefetch=4,
        grid=(nw,),
        in_specs=[
            pl.BlockSpec((MOE_BLK, 512), lambda i, b, e, l, h: (b[i], 0)),
            pl.BlockSpec((None, D_MODEL, 2 * D_FF), lambda i, b, e, l, h: (e[i], 0, 0)),
            pl.BlockSpec((None, 1, 2 * D_FF), lambda i, b, e, l, h: (e[i], 0, 0)),
            pl.BlockSpec((None, D_FF, D_MODEL), lambda i, b, e, l, h: (e[i], 0, 0)),
            pl.BlockSpec((None, 1, D_MODEL), lambda i, b, e, l, h: (e[i], 0, 0)),
        ],
        out_specs=pl.BlockSpec((MOE_BLK, 512), lambda i, b, e, l, h: (b[i], 0)),
        scratch_shapes=[pltpu.VMEM((D_MODEL, 2 * D_FF), BF16), pltpu.VMEM((D_FF, D_MODEL), BF16)],
    )
    return pl.pallas_call(
        _moe_kernel,
        out_shape=jax.ShapeDtypeStruct((ns, 512), U32),
        grid_spec=grid_spec,
        compiler_params=_cparams(("arbitrary",)),
        name="moe",
    )(blk, exp, lo, hi, xs, wgu, bgu, wd, bd)


def _combine_kernel(pos_ref, ys_ref, h2_ref, gate_ref, nfin_ref, o_ref, buf, sem):
    def row_copy(t, k, p):
        return pltpu.make_async_copy(ys_ref.at[pl.ds(p, 1)], buf.at[k, pl.ds(t, 1)], sem)

    def issue(t, carry):
        for k in range(TOP_K):
            row_copy(t, k, pos_ref[0, TOP_K * t + k]).start()
        return carry

    lax.fori_loop(0, BLK, issue, 0)

    def drain(t, carry):
        for k in range(TOP_K):
            row_copy(t, k, pos_ref[0, TOP_K * t + k]).wait()
        return carry

    lax.fori_loop(0, BLK, drain, 0)

    acc = h2_ref[...]
    gates = gate_ref[...]
    for k in range(TOP_K):
        acc = acc + gates[:, k:k + 1] * _unpack_bf16_pairs(buf[k])
    ms = jnp.mean(acc * acc, axis=-1, keepdims=True)
    o_ref[...] = acc * lax.rsqrt(ms + RMS_EPS) * nfin_ref[...]


def _combine(pos3, ys, h2, gate, nfin, *, bsz, nb, seq):
    tok = lambda b, n: (b * nb + n + 1, 0)
    return pl.pallas_call(
        _combine_kernel,
        out_shape=jax.ShapeDtypeStruct((bsz, seq, D_MODEL), F32),
        grid=(bsz, nb - 1),
        in_specs=[
            pl.BlockSpec((None, 1, BLK * TOP_K), lambda b, n: (b * nb + n + 1, 0, 0), memory_space=pltpu.SMEM),
            pl.BlockSpec(memory_space=pl.ANY),
            pl.BlockSpec((BLK, D_MODEL), tok),
            pl.BlockSpec((BLK, LANES), tok),
            pl.BlockSpec((1, D_MODEL), lambda b, n: (0, 0)),
        ],
        out_specs=pl.BlockSpec((None, BLK, D_MODEL), lambda b, n: (b, n, 0)),
        scratch_shapes=[pltpu.VMEM((TOP_K, BLK, 512), U32), pltpu.SemaphoreType.DMA],
        compiler_params=_cparams(("arbitrary", "arbitrary")),
        name="combine",
    )(pos3, ys, h2, gate, nfin)


def _rope_tables(lp):
    pos = np.arange(lp, dtype=np.float32) - np.float32(PAD)
    inv_freq = np.float32(ROPE_THETA) ** (-np.arange(0, 2 * ROT_HALF, 2, dtype=np.float32) / np.float32(2 * ROT_HALF))
    ang = (pos[:, None] * inv_freq[None, :]).astype(np.float32)
    cos, sin = np.cos(ang), np.sin(ang)
    ones = np.ones((lp, HEAD_DIM - 2 * ROT_HALF), np.float32)
    zeros8 = np.zeros((lp, ROT_HALF), np.float32)
    zrest = np.zeros((lp, HEAD_DIM - 2 * ROT_HALF), np.float32)
    c = np.concatenate([cos, cos, ones], axis=1)
    s1 = np.concatenate([zeros8, sin, zrest], axis=1)
    s2 = np.concatenate([-sin, zeros8, zrest], axis=1)
    tile = lambda t: jnp.asarray(np.concatenate([t, t], axis=1), F32)
    return tile(c), tile(s1), tile(s2)


def _segments(counts, n_slots):
    ends = jnp.cumsum(counts)
    starts = ends - counts
    nblocks = n_slots // MOE_BLK
    vals = jnp.concatenate([jnp.arange(nblocks, dtype=I32) * MOE_BLK, starts[1:].astype(I32)])
    n = vals.shape[0]
    idx = jnp.arange(n, dtype=I32)
    before = (vals[None, :] < vals[:, None]) | ((vals[None, :] == vals[:, None]) & (idx[None, :] < idx[:, None]))
    place = jnp.sum(before.astype(I32), axis=1)
    cuts = jnp.sum(jnp.where(place[None, :] == idx[:, None], vals[None, :], 0), axis=1)
    nxt = jnp.concatenate([cuts[1:], jnp.array([n_slots], I32)])
    blk = jnp.minimum(cuts // MOE_BLK, nblocks - 1)
    exp = jnp.minimum(jnp.sum((ends[None, :] <= cuts[:, None]).astype(I32), axis=1), N_EXPERTS - 1)
    lo = cuts - blk * MOE_BLK
    hi = lo + (nxt - cuts)
    live = hi > lo
    first_live = jnp.min(jnp.where(live, exp, N_EXPERTS - 1))
    exp = jnp.maximum(lax.cummax(jnp.where(live, exp, -1), axis=0), first_live)
    return starts, blk.astype(I32), exp.astype(I32), lo.astype(I32), hi.astype(I32)


def kernel(x, meta_tokens, norm_mix, w_in, ssm_lam_re, ssm_lam_im, ssm_log_dt, ssm_b_re, ssm_b_im,
           ssm_c_re, ssm_c_im, ssm_d, w_glu, b_glu, attn_sinks, w_br_ssm, w_br_attn, w_out,
           norm_ffn, w_router, b_router, w_gate_up, b_gate_up, w_down, b_down, norm_final):
    bsz, seq, _ = x.shape
    lp = seq + BLK
    nb = lp // BLK
    np_ = bsz * lp
    tm = 640 if lp % 640 == 0 else BLK
    t8 = lp // SSM_STEP
    seg = t8 // SSM_SEGS
    nchunk = 5 if (t8 % 5 == 0 and (t8 // 5) % 16 == 0) else 1

    meta = meta_tokens.astype(F32)
    rc, rs1, rs2 = _rope_tables(lp)
    u, q, kx, vx, sgs, sga = _inproj(x, meta, norm_mix[0][None], w_in[0].astype(BF16), rc, rs1, rs2, tm=tm, lp=lp)

    tb, tc, td, tab = _ssm_tables(ssm_lam_re[0], ssm_lam_im[0], ssm_log_dt[0], ssm_b_re[0], ssm_b_im[0],
                                  ssm_c_re[0], ssm_c_im[0], ssm_d[0], seg)
    z = _ssm(u, tb, tc, td, tab, bsz=bsz, lp=lp, nchunk=nchunk)

    attn = _attention(q, kx, vx, attn_sinks[0].astype(F32), bsz=bsz, nb=nb)

    wr = jnp.pad(w_router[0].astype(F32), ((0, 0), (0, LANES - N_EXPERTS)))
    wrh = wr.astype(BF16)
    wrl = (wr - wrh.astype(F32)).astype(BF16)
    br = jnp.concatenate([b_router[0].astype(F32), jnp.full((LANES - N_EXPERTS,), NEG_INF, F32)])[None]
    h2, hnp, eidx, rank, gate, cnt = _post(
        x, meta, z, attn, sgs, sga, w_glu[0].astype(BF16), b_glu[0][None], w_br_ssm[0].astype(BF16),
        w_br_attn[0].astype(BF16), w_out[0].astype(BF16), norm_ffn[0][None], wrh, wrl, br, tm=tm, lp=lp)

    counts = cnt[0, :N_EXPERTS].astype(I32)
    n_slots = np_ * TOP_K
    starts, sblk, sexp, slo, shi = _segments(counts, n_slots)
    e4 = eidx[:, :TOP_K]
    own = e4[:, :, None] == jnp.arange(N_EXPERTS, dtype=I32)[None, None, :]
    pos = (jnp.sum(jnp.where(own, starts[None, None, :], 0), axis=2) + rank[:, :TOP_K]).astype(I32)

    tt = 256
    xs = _dispatch(pos.reshape(np_ // tt, 1, tt * TOP_K), hnp, tt=tt)
    ys = _moe(sblk, sexp, slo, shi, xs, w_gate_up[0], b_gate_up[0][:, None, :], w_down[0], b_down[0][:, None, :])
    return _combine(pos.reshape(np_ // BLK, 1, BLK * TOP_K), ys, h2, gate, norm_final[None],
                    bsz=bsz, nb=nb, seq=seq)
```

```python
import functools
import math

import jax
import jax.numpy as jnp
import numpy as np
from jax import lax
from jax.experimental import pallas as pl
from jax.experimental.pallas import tpu as pltpu

F32 = jnp.float32
BF16 = jnp.bfloat16
I32 = jnp.int32
U32 = jnp.uint32

D_MODEL = 1024
N_META = 16
BLK = 128
PAD = BLK - N_META
N_GROUPS = 64
GROUP = 16
STATE = 64
N_Q_HEADS = 16
N_KV_HEADS = 2
HEAD_DIM = 64
ROT_HALF = 8
ROPE_THETA = 500000.0
N_EXPERTS = 32
TOP_K = 4
D_FF = 1024
SWIGLU_LIMIT = 7.0
SWIGLU_ALPHA = 1.702
RMS_EPS = 1e-5
NEG_INF = -1e30
SSM_STEP = 8
SSM_SEGS = 8
SSM_GBLK = 8
LANES = 128
MOE_BLK = 256
GRAN = 8
MOE_GRANS = MOE_BLK // GRAN
SORT_CHUNK = 256
SORT_W = 512 + LANES
VMEM_LIMIT = 56 * 1024 * 1024


def _cparams(sem):
    return pltpu.CompilerParams(dimension_semantics=sem, vmem_limit_bytes=VMEM_LIMIT)


def _dot(a, b):
    return jnp.dot(a, b, preferred_element_type=F32)


def _dot_nt(a, b):
    return lax.dot_general(a, b, (((1,), (1,)), ((), ())), preferred_element_type=F32)


def _pack_bf16_pairs(x):
    w = x.shape[1] // 2
    bits = pltpu.bitcast(x.astype(BF16).astype(F32), U32)
    return (bits[:, w:] & jnp.uint32(0xFFFF0000)) | (bits[:, :w] >> 16)


def _unpack_bf16_pairs(wd):
    lo = pltpu.bitcast(wd << 16, F32)
    hi = pltpu.bitcast(wd & jnp.uint32(0xFFFF0000), F32)
    return jnp.concatenate([lo, hi], axis=1)


def _padded_rows(x_refs, meta_ref, first_tile):
    head = jnp.concatenate([jnp.zeros((PAD, D_MODEL), F32), meta_ref[...]], axis=0)
    first = jnp.where(first_tile, head, x_refs[0][...])
    return jnp.concatenate([first] + [r[...] for r in x_refs[1:]], axis=0)


def _padded_row_specs(tm, tiles_per_seq):
    per = tm // BLK

    def spec(r):
        return pl.BlockSpec((None, BLK, D_MODEL),
                            lambda i: (i // tiles_per_seq, jnp.maximum((i % tiles_per_seq) * per + r - 1, 0), 0))

    return [spec(r) for r in range(per)]


def _inproj_kernel(*refs, tiles_per_seq, per):
    x_refs, (meta_ref, nrm_ref, w_ref, rc_ref, rs1_ref, rs2_ref,
             u_ref, q_ref, kx_ref, vx_ref, sgs_ref, sga_ref) = refs[:per], refs[per:]
    x = _padded_rows(x_refs, meta_ref, pl.program_id(0) % tiles_per_seq == 0)
    ms = jnp.mean(x * x, axis=-1, keepdims=True)
    hn = (x * lax.rsqrt(ms + RMS_EPS) * nrm_ref[...]).astype(BF16)

    def proj(lo, hi):
        return _dot(hn, w_ref[:, lo:hi])

    u_ref[...] = proj(0, 1024)

    rc = rc_ref[...]
    rs1 = rs1_ref[...]
    rs2 = rs2_ref[...]

    def rope(blk):
        return blk * rc + pltpu.roll(blk, ROT_HALF, 1) * rs1 + pltpu.roll(blk, LANES - ROT_HALF, 1) * rs2

    qf = proj(1024, 2048)
    scale = 1.0 / math.sqrt(HEAD_DIM)
    for j in range(8):
        q_ref[:, LANES * j:LANES * (j + 1)] = (rope(qf[:, LANES * j:LANES * (j + 1)]) * scale).astype(BF16)

    kvf = proj(2048, 2304)
    k = rope(kvf[:, :LANES])
    v = kvf[:, LANES:]
    low = lax.broadcasted_iota(I32, k.shape, 1) < HEAD_DIM

    def expand(t, ref):
        tr = pltpu.roll(t, HEAD_DIM, 1)
        zero = jnp.zeros_like(t)
        ref[:, 0:128] = jnp.where(low, t, zero).astype(BF16)
        ref[:, 128:256] = jnp.where(low, zero, tr).astype(BF16)
        ref[:, 256:384] = jnp.where(low, tr, zero).astype(BF16)
        ref[:, 384:512] = jnp.where(low, zero, t).astype(BF16)

    expand(k, kx_ref)
    expand(v, vx_ref)
    sgs_ref[...] = jax.nn.sigmoid(proj(2304, 3328)).astype(BF16)
    sga_ref[...] = jax.nn.sigmoid(proj(3328, 4352)).astype(BF16)


def _inproj(x, meta, nrm, w_bf, rc, rs1, rs2, *, tm, lp):
    np_ = x.shape[0] * lp
    tiles_per_seq = lp // tm
    per = tm // BLK
    row = lambda i: (i, 0)
    fixed = lambda i: (0, 0)
    rope_idx = lambda i: (i % tiles_per_seq, 0)
    out_shape = (
        jax.ShapeDtypeStruct((np_, 1024), F32),
        jax.ShapeDtypeStruct((np_, 1024), BF16),
        jax.ShapeDtypeStruct((np_, 512), BF16),
        jax.ShapeDtypeStruct((np_, 512), BF16),
        jax.ShapeDtypeStruct((np_, 1024), BF16),
        jax.ShapeDtypeStruct((np_, 1024), BF16),
    )
    return pl.pallas_call(
        functools.partial(_inproj_kernel, tiles_per_seq=tiles_per_seq, per=per),
        out_shape=out_shape,
        grid=(np_ // tm,),
        in_specs=_padded_row_specs(tm, tiles_per_seq) + [
            pl.BlockSpec((N_META, D_MODEL), fixed),
            pl.BlockSpec((1, D_MODEL), fixed),
            pl.BlockSpec(w_bf.shape, fixed),
            pl.BlockSpec((tm, LANES), rope_idx),
            pl.BlockSpec((tm, LANES), rope_idx),
            pl.BlockSpec((tm, LANES), rope_idx),
        ],
        out_specs=(
            pl.BlockSpec((tm, 1024), row),
            pl.BlockSpec((tm, 1024), row),
            pl.BlockSpec((tm, 512), row),
            pl.BlockSpec((tm, 512), row),
            pl.BlockSpec((tm, 1024), row),
            pl.BlockSpec((tm, 1024), row),
        ),
        compiler_params=_cparams(("arbitrary",)),
        name="inproj",
    )(*([x] * per), meta, nrm, w_bf, rc, rs1, rs2)


def _expand_block_diag(dst_ref, row0, t_ref, e_ref, row_shift, col_shift):
    n = t_ref.shape[0]
    step = 256
    for r0 in range(0, n, step):
        full = _dot(t_ref[r0:r0 + step, :], e_ref[...])
        rg = (lax.broadcasted_iota(I32, full.shape, 0) + r0) >> row_shift
        cg = lax.broadcasted_iota(I32, full.shape, 1) >> col_shift
        keep = ((rg ^ cg) & (SSM_GBLK - 1)) == 0
        dst_ref[row0 + r0:row0 + r0 + step, :] = jnp.where(keep, full, 0.0).astype(BF16)


def _ssm_kernel(u_ref, tb_ref, tc_ref, td_ref, eb_ref, ec_ref, tab_ref, z_ref,
                wb_ref, wcd_ref, ust_ref, bu_ref, yn_ref, *, t8, seg, nchunk):
    rc = t8 // nchunk

    @pl.when(pl.program_id(1) == 0)
    def _():
        _expand_block_diag(wb_ref, 0, tb_ref, eb_ref, 4, 6)
        _expand_block_diag(wcd_ref, 0, tc_ref, ec_ref, 6, 4)
        _expand_block_diag(wcd_ref, 1024, td_ref, ec_ref, 4, 4)

    for i in range(SSM_STEP):
        ust_ref[:, LANES * i:LANES * (i + 1)] = u_ref[pl.ds(i, t8, stride=SSM_STEP), :].astype(BF16)

    for c in range(nchunk):
        r = _dot(ust_ref[c * rc:(c + 1) * rc, :], wb_ref[...])
        for m in range(8):
            bu_ref[m, c * rc:(c + 1) * rc, :] = r[:, LANES * m:LANES * (m + 1)]

    a_re = [jnp.broadcast_to(tab_ref[0:1, LANES * m:LANES * (m + 1)], (SSM_SEGS, LANES)) for m in range(4)]
    a_im = [jnp.broadcast_to(tab_ref[1:2, LANES * m:LANES * (m + 1)], (SSM_SEGS, LANES)) for m in range(4)]
    s_re = [jnp.broadcast_to(tab_ref[2:3, LANES * m:LANES * (m + 1)], (SSM_SEGS, LANES)) for m in range(4)]
    s_im = [jnp.broadcast_to(tab_ref[3:4, LANES * m:LANES * (m + 1)], (SSM_SEGS, LANES)) for m in range(4)]

    def cmul_add(mr, mi, xr, xi, br, bi):
        return mr * xr - mi * xi + br, mr * xi + mi * xr + bi

    def load(k):
        return [bu_ref[m, pl.ds(k, SSM_SEGS, stride=seg), :] for m in range(8)]

    def advance(st, b):
        nr, ni = [], []
        for m in range(4):
            r_, i_ = cmul_add(a_re[m], a_im[m], st[m], st[4 + m], b[m], b[4 + m])
            nr.append(r_)
            ni.append(i_)
        return tuple(nr + ni)

    zeros = tuple(jnp.zeros((SSM_SEGS, LANES), F32) for _ in range(8))
    fin = lax.fori_loop(0, seg, lambda k, st: advance(st, load(k)), zeros)

    first = lax.broadcasted_iota(I32, (SSM_SEGS, LANES), 0) == 0

    def shift_down(t):
        return jnp.where(first, 0.0, pltpu.roll(t, 1, 0))

    tot = fin
    for _ in range(SSM_SEGS - 1):
        nxt_r, nxt_i = [], []
        for m in range(4):
            r_, i_ = cmul_add(s_re[m], s_im[m], shift_down(tot[m]), shift_down(tot[4 + m]), fin[m], fin[4 + m])
            nxt_r.append(r_)
            nxt_i.append(i_)
        tot = tuple(nxt_r + nxt_i)
    init = tuple(shift_down(t) for t in tot)

    def pass2(k, st):
        b = load(k)
        for m in range(8):
            bu_ref[m, pl.ds(k, SSM_SEGS, stride=seg), :] = st[m]
        return advance(st, b)

    lax.fori_loop(0, seg, pass2, init)

    for c in range(nchunk):
        xp = jnp.concatenate([bu_ref[m, c * rc:(c + 1) * rc, :] for m in range(8)], axis=1).astype(BF16)
        lhs = jnp.concatenate([xp, ust_ref[c * rc:(c + 1) * rc, :]], axis=1)
        y = _dot(lhs, wcd_ref[...])
        for i in range(SSM_STEP):
            yn_ref[pl.ds(SSM_STEP * c * rc + i, rc, stride=SSM_STEP), :] = y[:, LANES * i:LANES * (i + 1)]

    y = yn_ref[...] + tab_ref[4:5, 0:LANES] * u_ref[...]
    z_ref[...] = jax.nn.gelu(y).astype(BF16)


def _ssm(u, tb, tc, td, tab, *, bsz, lp, nchunk):
    np_ = u.shape[0]
    t8 = lp // SSM_STEP
    seg = t8 // SSM_SEGS
    nblk = D_MODEL // LANES
    col = jnp.arange(1024)
    src_b = (col >> 9) * STATE + (col & (STATE - 1))
    src_c = (col >> 7) * GROUP + (col & (GROUP - 1))
    eb = (jnp.arange(LANES)[:, None] == src_b[None, :]).astype(BF16)
    ec = (jnp.arange(LANES)[:, None] == src_c[None, :]).astype(BF16)
    kern = functools.partial(_ssm_kernel, t8=t8, seg=seg, nchunk=nchunk)
    per_q = lambda q, b: (q, 0, 0)
    fixed = lambda q, b: (0, 0)
    return pl.pallas_call(
        kern,
        out_shape=jax.ShapeDtypeStruct((np_, D_MODEL), BF16),
        grid=(nblk, bsz),
        in_specs=[
            pl.BlockSpec((lp, LANES), lambda q, b: (b, q)),
            pl.BlockSpec((None, 1024, LANES), per_q),
            pl.BlockSpec((None, 1024, LANES), per_q),
            pl.BlockSpec((None, 1024, LANES), per_q),
            pl.BlockSpec((LANES, 1024), fixed),
            pl.BlockSpec((LANES, 1024), fixed),
            pl.BlockSpec((None, 8, 512), per_q),
        ],
        out_specs=pl.BlockSpec((lp, LANES), lambda q, b: (b, q)),
        scratch_shapes=[
            pltpu.VMEM((1024, 1024), BF16),
            pltpu.VMEM((2048, 1024), BF16),
            pltpu.VMEM((t8, 1024), BF16),
            pltpu.VMEM((8, t8, LANES), F32),
            pltpu.VMEM((lp, LANES), F32),
        ],
        compiler_params=_cparams(("arbitrary", "arbitrary")),
        name="ssm",
    )(u, tb, tc, td, eb, ec, tab)


def _ssm_tables(lam_re, lam_im, log_dt, b_re, b_im, c_re, c_im, d, seg):
    hi = lax.Precision.HIGHEST
    lr, li = lam_re.astype(F32), lam_im.astype(F32)
    dt = jnp.exp(log_dt.astype(F32))[:, None]
    mag = jnp.exp(dt * lr)
    ar, ai = mag * jnp.cos(dt * li), mag * jnp.sin(dt * li)
    den = lr * lr + li * li
    nr, ni = ar - 1.0, ai
    fr, fi = (nr * lr + ni * li) / den, (ni * lr - nr * li) / den
    br, bi = b_re.astype(F32), b_im.astype(F32)
    bbr = fr[..., None] * br - fi[..., None] * bi
    bbi = fr[..., None] * bi + fi[..., None] * br

    def cmul(xr, xi, yr, yi):
        return xr * yr - xi * yi, xr * yi + xi * yr

    pr, pi = [jnp.ones_like(ar)], [jnp.zeros_like(ar)]
    for _ in range(SSM_STEP):
        r_, i_ = cmul(pr[-1], pi[-1], ar, ai)
        pr.append(r_)
        pi.append(i_)
    pw_r, pw_i = jnp.stack(pr), jnp.stack(pi)
    a8r, a8i = pw_r[SSM_STEP], pw_i[SSM_STEP]
    sr, si = jnp.ones_like(ar), jnp.zeros_like(ar)
    qr, qi = a8r, a8i
    e = seg
    while e:
        if e & 1:
            sr, si = cmul(sr, si, qr, qi)
        qr, qi = cmul(qr, qi, qr, qi)
        e >>= 1

    cr, ci = c_re.astype(F32), c_im.astype(F32)
    nb = N_GROUPS // SSM_GBLK

    rev_r = jnp.stack(pr[SSM_STEP - 1::-1])
    rev_i = jnp.stack(pi[SSM_STEP - 1::-1])
    wbr, wbi = cmul(rev_r[..., None], rev_i[..., None], bbr[None], bbi[None])
    wbg = jnp.stack([wbr, wbi], axis=0)
    wbg = wbg.reshape(2, SSM_STEP, nb, SSM_GBLK, STATE, GROUP)
    tb = jnp.transpose(wbg, (2, 1, 3, 5, 0, 4)).reshape(nb, 1024, LANES)

    upr, upi = pw_r[1:], pw_i[1:]
    mr, mi = cmul(cr[None], ci[None], upr[:, :, None, :], upi[:, :, None, :])
    wcg = jnp.stack([mr, -mi], axis=0)
    wcg = wcg.reshape(2, SSM_STEP, nb, SSM_GBLK, GROUP, STATE)
    tc = jnp.transpose(wcg, (2, 0, 3, 5, 1, 4)).reshape(nb, 1024, LANES)

    tr, ti = cmul(pw_r[:SSM_STEP, :, :, None], pw_i[:SSM_STEP, :, :, None], bbr[None], bbi[None])
    kt = (jnp.einsum('ghp,tgpk->tghk', cr, tr, precision=hi)
          - jnp.einsum('ghp,tgpk->tghk', ci, ti, precision=hi))
    zero_tap = jnp.zeros_like(kt[0])
    kd = jnp.stack([jnp.stack([kt[i - j] if i >= j else zero_tap for i in range(SSM_STEP)])
                    for j in range(SSM_STEP)])
    kd = kd.reshape(SSM_STEP, SSM_STEP, nb, SSM_GBLK, GROUP, GROUP)
    td = jnp.transpose(kd, (2, 0, 3, 5, 1, 4)).reshape(nb, 1024, LANES)

    def lanes(t):
        return t.reshape(nb, SSM_GBLK * STATE)

    dl = jnp.pad(d.astype(F32).reshape(nb, SSM_GBLK * GROUP), ((0, 0), (0, 512 - LANES)))
    zero = jnp.zeros((nb, 512), F32)
    tab = jnp.stack([lanes(a8r), lanes(a8i), lanes(sr), lanes(si), dl, zero, zero, zero], axis=1)
    return tb.astype(BF16), tc.astype(BF16), td.astype(BF16), tab


def _attn_kernel(q_ref, kc_ref, kp_ref, km_ref, vc_ref, vp_ref, vm_ref, sink_ref, o_ref):
    n = pl.program_id(1)
    nk = 3 * BLK
    row = lax.broadcasted_iota(I32, (BLK, nk), 0)
    col = lax.broadcasted_iota(I32, (BLK, nk), 1)
    band = (col > row) & (col <= row + BLK) & (col >= 2 * BLK - BLK * n)
    mrow = col - 2 * BLK
    meta = (mrow >= PAD) & (mrow <= BLK * n + row)
    valid = band | meta

    for g in range(N_KV_HEADS):
        qg = jnp.concatenate([q_ref[:, LANES * (4 * g + jj):LANES * (4 * g + jj + 1)] for jj in range(4)], axis=0)
        acc = None
        for par in range(2):
            cb = 2 * g + par
            sl = slice(LANES * cb, LANES * (cb + 1))
            kk = jnp.concatenate([kp_ref[:, sl], kc_ref[:, sl], km_ref[:, sl]], axis=0)
            vv = jnp.concatenate([vp_ref[:, sl], vc_ref[:, sl], vm_ref[:, sl]], axis=0)
            s_all = _dot_nt(qg, kk)
            probs = []
            for jj in range(4):
                sink = sink_ref[8 * g + 2 * jj + par]
                s = jnp.where(valid, s_all[BLK * jj:BLK * (jj + 1), :], NEG_INF)
                mx = jnp.maximum(jnp.max(s, axis=1, keepdims=True), sink)
                p = jnp.exp(s - mx)
                den = jnp.sum(p, axis=1, keepdims=True) + jnp.exp(sink - mx)
                probs.append((p / den).astype(BF16))
            o = _dot(jnp.concatenate(probs, axis=0), vv)
            acc = o if acc is None else acc + o
        for jj in range(4):
            o_ref[:, LANES * (4 * g + jj):LANES * (4 * g + jj + 1)] = acc[BLK * jj:BLK * (jj + 1), :].astype(BF16)


def _attention(q, kx, vx, sinks, *, bsz, nb):
    np_ = q.shape[0]
    cur = lambda b, n: (b * nb + n, 0)
    prev = lambda b, n: (b * nb + jnp.maximum(n - 1, 0), 0)
    first = lambda b, n: (b * nb, 0)
    return pl.pallas_call(
        _attn_kernel,
        out_shape=jax.ShapeDtypeStruct((np_, 1024), BF16),
        grid=(bsz, nb),
        in_specs=[
            pl.BlockSpec((BLK, 1024), cur),
            pl.BlockSpec((BLK, 512), cur),
            pl.BlockSpec((BLK, 512), prev),
            pl.BlockSpec((BLK, 512), first),
            pl.BlockSpec((BLK, 512), cur),
            pl.BlockSpec((BLK, 512), prev),
            pl.BlockSpec((BLK, 512), first),
            pl.BlockSpec(memory_space=pltpu.SMEM),
        ],
        out_specs=pl.BlockSpec((BLK, 1024), cur),
        compiler_params=_cparams(("arbitrary", "arbitrary")),
        name="attn",
    )(q, kx, kx, kx, vx, vx, vx, sinks)


def _post_kernel(*refs, tiles_per_seq, per):
    x_refs, (meta_ref, z_ref, at_ref, sgs_ref, sga_ref, wglu_ref, bglu_ref, wbs_ref, wba_ref, wout_ref,
             nffn_ref, wrh_ref, wrl_ref, br_ref,
             h2_ref, hnp_ref, lslot_ref, ecol_ref, cnt_ref) = refs[:per], refs[per:]
    i = pl.program_id(0)

    z = z_ref[...]
    t = _dot(z, wglu_ref[...]) + bglu_ref[...]
    so = (z.astype(F32) * jax.nn.sigmoid(t)).astype(BF16)
    mix = (sgs_ref[...].astype(F32) * _dot(so, wbs_ref[...])
           + sga_ref[...].astype(F32) * _dot(at_ref[...], wba_ref[...]))
    h2 = _padded_rows(x_refs, meta_ref, i % tiles_per_seq == 0) + _dot(mix.astype(BF16), wout_ref[...])
    h2_ref[...] = h2

    ms = jnp.mean(h2 * h2, axis=-1, keepdims=True)
    hn = h2 * lax.rsqrt(ms + RMS_EPS) * nffn_ref[...]
    hb = hn.astype(BF16)
    hnp_ref[...] = _pack_bf16_pairs(hn)

    lo = (hn - hb.astype(F32)).astype(BF16)
    logits = _dot(hb, wrh_ref[...]) + _dot(lo, wrh_ref[...]) + _dot(hb, wrl_ref[...]) + br_ref[...]

    tm = logits.shape[0]
    lane = lax.broadcasted_iota(I32, (tm, LANES), 1)
    lane_f = lane.astype(F32)
    work = logits
    vals, hots, idxs = [], [], []
    for _ in range(TOP_K):
        mx = jnp.max(work, axis=1, keepdims=True)
        idx = jnp.min(jnp.where(work == mx, lane_f, float(LANES)), axis=1, keepdims=True)
        hot = lane_f == idx
        vals.append(mx)
        idxs.append(idx)
        hots.append(hot)
        work = jnp.where(hot, -jnp.inf, work)

    ex = [jnp.exp(v - vals[0]) for v in vals]
    tot = ex[0] + ex[1] + ex[2] + ex[3]
    gates = [e / tot for e in ex]

    onehot = jnp.zeros((tm, LANES), F32)
    for hot in hots:
        onehot = onehot + jnp.where(hot, 1.0, 0.0)
    r_i = lax.broadcasted_iota(I32, (tm, tm), 0)
    c_i = lax.broadcasted_iota(I32, (tm, tm), 1)
    tri = jnp.where(r_i > c_i, 1.0, 0.0).astype(BF16)
    before = _dot(tri, onehot.astype(BF16))
    count = jnp.sum(onehot, axis=0, keepdims=True)
    gran = jnp.floor((count + (GRAN - 1)) * (1.0 / GRAN))
    e_r = lax.broadcasted_iota(I32, (LANES, LANES), 0)
    e_c = lax.broadcasted_iota(I32, (LANES, LANES), 1)
    upper = jnp.where(e_r < e_c, 1.0, 0.0).astype(BF16)
    gstart = _dot(jnp.broadcast_to(gran, (8, LANES)).astype(BF16), upper)[0:1, :]
    base = before + gstart * float(GRAN)
    slots = [jnp.sum(jnp.where(hot, base, 0.0), axis=1, keepdims=True) for hot in hots]
    cnt_ref[...] = jnp.broadcast_to(gran, cnt_ref.shape)

    def spread(cols, dtype):
        out = jnp.zeros((tm, LANES), dtype)
        for k, cval in enumerate(cols):
            out = jnp.where(lane == k, cval, out)
        return out

    lslot_ref[...] = spread([s.astype(I32) for s in slots], I32)
    cols = []
    for g in gates:
        g1 = g.astype(BF16).astype(F32)
        g2 = (g - g1).astype(BF16).astype(F32)
        cols += [g1, g2, g - g1 - g2]
    for s in slots:
        hi = jnp.floor(s * (1.0 / 64.0))
        cols += [hi, s - 64.0 * hi]
    ecol_ref[...] = spread(cols, F32)


def _post(x, meta, z, attn, sgs, sga, wglu, bglu, wbs, wba, wout, nffn, wrh, wrl, br, *, tm, lp):
    np_ = z.shape[0]
    tiles_per_seq = lp // tm
    per = tm // BLK
    row = lambda i: (i, 0)
    fixed = lambda i: (0, 0)
    sq = pl.BlockSpec((1024, 1024), fixed)
    vec = pl.BlockSpec((1, 1024), fixed)
    out_shape = (
        jax.ShapeDtypeStruct((np_, 1024), F32),
        jax.ShapeDtypeStruct((np_, 512), U32),
        jax.ShapeDtypeStruct((np_, LANES), I32),
        jax.ShapeDtypeStruct((np_, LANES), F32),
        jax.ShapeDtypeStruct((np_ // tm * 8, LANES), F32),
    )
    return pl.pallas_call(
        functools.partial(_post_kernel, tiles_per_seq=tiles_per_seq, per=per),
        out_shape=out_shape,
        grid=(np_ // tm,),
        in_specs=_padded_row_specs(tm, tiles_per_seq) + [
            pl.BlockSpec((N_META, D_MODEL), fixed),
            pl.BlockSpec((tm, 1024), row), pl.BlockSpec((tm, 1024), row),
            pl.BlockSpec((tm, 1024), row), pl.BlockSpec((tm, 1024), row),
            sq, vec, sq, sq, sq, vec,
            pl.BlockSpec((1024, LANES), fixed), pl.BlockSpec((1024, LANES), fixed),
            pl.BlockSpec((1, LANES), fixed),
        ],
        out_specs=(
            pl.BlockSpec((tm, 1024), row), pl.BlockSpec((tm, 512), row),
            pl.BlockSpec((tm, LANES), row), pl.BlockSpec((tm, LANES), row),
            pl.BlockSpec((8, LANES), row),
        ),
        compiler_params=_cparams(("arbitrary",)),
        name="post",
    )(*([x] * per), meta, z, attn, sgs, sga, wglu, bglu, wbs, wba, wout, nffn, wrh, wrl, br)


def _sort_kernel(hnp_ref, ecol_ref, xloc_ref, *, tm, srows):
    ecol = ecol_ref[...].astype(BF16)
    pay = jnp.concatenate([_unpack_bf16_pairs(hnp_ref[...]).astype(BF16), ecol], axis=1)

    r8 = lax.broadcasted_iota(I32, (8, LANES), 0)
    l8 = lax.broadcasted_iota(I32, (8, LANES), 1)
    digit_w = jnp.where(l8 == 12 + 2 * r8, 64.0, jnp.where(l8 == 13 + 2 * r8, 1.0, 0.0)).astype(BF16)
    slot_rows = _dot_nt(digit_w, ecol)

    rr = lax.broadcasted_iota(I32, (LANES, LANES), 0)
    cc = lax.broadcasted_iota(I32, (LANES, LANES), 1)
    kk = jnp.where(cc < 3 * TOP_K, cc // 3, -LANES)
    spread_w = jnp.where(rr == 12 + 2 * kk, 64.0, jnp.where(rr == 13 + 2 * kk, 1.0, 0.0)).astype(BF16)
    gate_lane = lax.broadcasted_iota(I32, (SORT_CHUNK, LANES), 1) < 3 * TOP_K

    for c in range(srows // SORT_CHUNK):
        s_id = (lax.broadcasted_iota(I32, (SORT_CHUNK, tm), 0) + c * SORT_CHUNK).astype(F32)
        perm = jnp.zeros((SORT_CHUNK, tm), F32)
        for k in range(TOP_K):
            perm = perm + jnp.where(slot_rows[k:k + 1, :] == s_id, 1.0, 0.0)
        got = _dot(perm.astype(BF16), pay)
        rows = slice(c * SORT_CHUNK, (c + 1) * SORT_CHUNK)
        xloc_ref[rows, 0:512] = _pack_bf16_pairs(got[:, :D_MODEL])
        extra = got[:, D_MODEL:]
        owner = _dot(extra.astype(BF16), spread_w)
        me = (lax.broadcasted_iota(I32, (SORT_CHUNK, LANES), 0) + c * SORT_CHUNK).astype(F32)
        mine = jnp.where(gate_lane, owner, -1.0) == me
        gate = jnp.sum(jnp.where(mine, extra, 0.0), axis=1, keepdims=True)
        xloc_ref[rows, 512:SORT_W] = pltpu.bitcast(jnp.broadcast_to(gate, (SORT_CHUNK, LANES)), U32)


def _sort(hnp, ecol, *, tm, srows):
    np_ = hnp.shape[0]
    ntile = np_ // tm
    row = lambda i: (i, 0)
    return pl.pallas_call(
        functools.partial(_sort_kernel, tm=tm, srows=srows),
        out_shape=jax.ShapeDtypeStruct((ntile * srows, SORT_W), U32),
        grid=(ntile,),
        in_specs=[pl.BlockSpec((tm, 512), row), pl.BlockSpec((tm, LANES), row)],
        out_specs=pl.BlockSpec((srows, SORT_W), row),
        compiler_params=_cparams(("arbitrary",)),
        name="sort",
    )(hnp, ecol)


def _moe_kernel(blk_ref, exp_ref, lo_ref, hi_ref, kind_ref, gsrc_ref, xloc_ref, wgu_ref, bgu_ref, wd_ref, bd_ref,
                yloc_ref, xbuf, ybuf, sem_in, sem_out, wgu_bf, wd_bf, *, nseg, dump0):
    i = pl.program_id(0)
    prev = jnp.maximum(i - 1, 0)
    nxt = jnp.minimum(i + 1, nseg - 1)
    blk = blk_ref[i]
    new_expert = (i == 0) | (exp_ref[i] != exp_ref[prev])
    new_block = (i == 0) | (blk != blk_ref[prev])

    def gather(b, start):
        slot = b % 2
        for g in range(MOE_GRANS):
            src = pl.multiple_of(gsrc_ref[b * MOE_GRANS + g] * GRAN, GRAN)
            cp = pltpu.make_async_copy(xloc_ref.at[pl.ds(src, GRAN)], xbuf.at[slot, pl.ds(g * GRAN, GRAN)],
                                       sem_in.at[slot])
            if start:
                cp.start()
            else:
                cp.wait()

    def scatter(seg, start):
        slot = seg % 2
        b = blk_ref[seg]
        lo = lo_ref[seg]
        hi = hi_ref[seg]
        for g in range(MOE_GRANS):
            keep = (g >= lo) & (g < hi)
            dst = pl.multiple_of(jnp.where(keep, gsrc_ref[b * MOE_GRANS + g], dump0 + g) * GRAN, GRAN)
            cp = pltpu.make_async_copy(ybuf.at[slot, pl.ds(g * GRAN, GRAN)], yloc_ref.at[pl.ds(dst, GRAN)],
                                       sem_out.at[slot])
            if start:
                cp.start()
            else:
                cp.wait()

    @pl.when(i == 0)
    def _():
        gather(blk, True)

    @pl.when(new_block)
    def _():
        gather(blk, False)

    @pl.when((i + 1 < nseg) & (blk_ref[nxt] != blk))
    def _():
        gather(blk_ref[nxt], True)

    @pl.when(new_expert)
    def _():
        wgu_bf[...] = wgu_ref[...].astype(BF16)
        wd_bf[...] = wd_ref[...].astype(BF16)

    kind = kind_ref[i]
    live = kind > 0

    @pl.when(kind == 2)
    def _():
        ybuf[i % 2] = jnp.zeros((MOE_BLK, 512), U32)

    @pl.when(kind == 1)
    def _():
        xw = xbuf[blk % 2]
        xb = _unpack_bf16_pairs(xw[:, 0:512]).astype(BF16)
        gate = pltpu.bitcast(xw[:, 512:SORT_W], F32)
        gu = _dot(xb, wgu_bf[...]) + bgu_ref[...]
        g = jnp.minimum(gu[:, :D_FF], SWIGLU_LIMIT)
        up = jnp.clip(gu[:, D_FF:], -SWIGLU_LIMIT, SWIGLU_LIMIT)
        hid = g * jax.nn.sigmoid(SWIGLU_ALPHA * g) * (up + 1.0)
        y = _dot(hid.astype(BF16), wd_bf[...]) + bd_ref[...]
        y = y * jnp.concatenate([gate] * (D_MODEL // LANES), axis=1)
        ybuf[i % 2] = _pack_bf16_pairs(y)

    @pl.when((i > 0) & (kind_ref[prev] > 0))
    def _():
        scatter(prev, False)

    @pl.when(live)
    def _():
        scatter(i, True)

    @pl.when(live & (i == nseg - 1))
    def _():
        scatter(i, False)


def _moe(blk, exp, lo, hi, kind, gsrc, xloc, wgu, bgu, wd, bd, *, ncap):
    nseg = blk.shape[0]
    dump0 = ncap
    wmap = lambda i, b, e, l, h, k, s: (e[i], 0, 0)
    grid_spec = pltpu.PrefetchScalarGridSpec(
        num_scalar_prefetch=6,
        grid=(nseg,),
        in_specs=[
            pl.BlockSpec(memory_space=pl.ANY),
            pl.BlockSpec((None, D_MODEL, 2 * D_FF), wmap),
            pl.BlockSpec((None, 1, 2 * D_FF), wmap),
            pl.BlockSpec((None, D_FF, D_MODEL), wmap),
            pl.BlockSpec((None, 1, D_MODEL), wmap),
        ],
        out_specs=pl.BlockSpec(memory_space=pl.ANY),
        scratch_shapes=[
            pltpu.VMEM((2, MOE_BLK, SORT_W), U32),
            pltpu.VMEM((2, MOE_BLK, 512), U32),
            pltpu.SemaphoreType.DMA((2,)),
            pltpu.SemaphoreType.DMA((2,)),
            pltpu.VMEM((D_MODEL, 2 * D_FF), BF16),
            pltpu.VMEM((D_FF, D_MODEL), BF16),
        ],
    )
    return pl.pallas_call(
        functools.partial(_moe_kernel, nseg=nseg, dump0=dump0),
        out_shape=jax.ShapeDtypeStruct(((ncap + MOE_GRANS) * GRAN, 512), U32),
        grid_spec=grid_spec,
        compiler_params=_cparams(("arbitrary",)),
        name="moe",
    )(blk, exp, lo, hi, kind, gsrc, xloc, wgu, bgu, wd, bd)


def _combine_kernel(y_ref, lslot_ref, h2_ref, nfin_ref, o_ref, obuf, sem, *, tm, srows, tiles_per_seq):
    i = pl.program_id(0)
    b = i // tiles_per_seq
    j = i % tiles_per_seq
    per = tm // BLK
    slots = lslot_ref[...]
    acc = h2_ref[...]
    for c in range(srows // SORT_CHUNK):
        s_id = lax.broadcasted_iota(I32, (tm, SORT_CHUNK), 1) + c * SORT_CHUNK
        take = jnp.zeros((tm, SORT_CHUNK), F32)
        for k in range(TOP_K):
            take = take + jnp.where(slots[:, k:k + 1] == s_id, 1.0, 0.0)
        rows = _unpack_bf16_pairs(y_ref[c * SORT_CHUNK:(c + 1) * SORT_CHUNK, :]).astype(BF16)
        acc = acc + _dot(take.astype(BF16), rows)
    ms = jnp.mean(acc * acc, axis=-1, keepdims=True)
    obuf[...] = acc * lax.rsqrt(ms + RMS_EPS) * nfin_ref[...]

    def copies(start):
        for r in range(per):
            n = j * per + r
            cp = pltpu.make_async_copy(obuf.at[pl.ds(r * BLK, BLK)],
                                       o_ref.at[b, pl.ds(pl.multiple_of(jnp.maximum(n - 1, 0) * BLK, BLK), BLK)], sem)

            def go():
                if start:
                    cp.start()
                else:
                    cp.wait()

            if r == 0:
                pl.when(j > 0)(go)
            else:
                go()

    copies(True)
    copies(False)


def _combine(yloc, lslot, h2, nfin, *, bsz, seq, tm, lp, srows):
    np_ = h2.shape[0]
    tiles_per_seq = lp // tm
    row = lambda i: (i, 0)
    return pl.pallas_call(
        functools.partial(_combine_kernel, tm=tm, srows=srows, tiles_per_seq=tiles_per_seq),
        out_shape=jax.ShapeDtypeStruct((bsz, seq, D_MODEL), F32),
        grid=(np_ // tm,),
        in_specs=[
            pl.BlockSpec((srows, 512), row),
            pl.BlockSpec((tm, LANES), row),
            pl.BlockSpec((tm, D_MODEL), row),
            pl.BlockSpec((1, D_MODEL), lambda i: (0, 0)),
        ],
        out_specs=pl.BlockSpec(memory_space=pl.ANY),
        scratch_shapes=[pltpu.VMEM((tm, D_MODEL), F32), pltpu.SemaphoreType.DMA],
        compiler_params=_cparams(("arbitrary",)),
        name="combine",
    )(yloc, lslot, h2, nfin)


def _rope_tables(lp):
    pos = np.arange(lp, dtype=np.float32) - np.float32(PAD)
    inv_freq = np.float32(ROPE_THETA) ** (-np.arange(0, 2 * ROT_HALF, 2, dtype=np.float32) / np.float32(2 * ROT_HALF))
    ang = (pos[:, None] * inv_freq[None, :]).astype(np.float32)
    cos, sin = np.cos(ang), np.sin(ang)
    ones = np.ones((lp, HEAD_DIM - 2 * ROT_HALF), np.float32)
    zeros8 = np.zeros((lp, ROT_HALF), np.float32)
    zrest = np.zeros((lp, HEAD_DIM - 2 * ROT_HALF), np.float32)
    c = np.concatenate([cos, cos, ones], axis=1)
    s1 = np.concatenate([zeros8, sin, zrest], axis=1)
    s2 = np.concatenate([-sin, zeros8, zrest], axis=1)
    tile = lambda t: jnp.asarray(np.concatenate([t, t], axis=1), F32)
    return tile(c), tile(s1), tile(s2)


def _route_tables(grans, gt):
    ntile = grans.shape[0]
    ncap = ntile * gt
    used = jnp.sum(grans, axis=1)
    lstart = jnp.cumsum(grans, axis=1) - grans
    tile0 = jnp.arange(ntile, dtype=I32)[:, None] * gt
    run_len = jnp.concatenate([grans.T.reshape(-1), gt - used])
    run_src = jnp.concatenate([(tile0 + lstart).T.reshape(-1), tile0[:, 0] + used])
    run_end = jnp.cumsum(run_len)
    run_start = run_end - run_len
    off = run_src - run_start
    delta = off - jnp.concatenate([jnp.zeros((1,), I32), off[:-1]])
    pos = jnp.arange(ncap, dtype=I32)
    gsrc = pos + jnp.sum(jnp.where(run_start[None, :] <= pos[:, None], delta[None, :], 0), axis=1)

    estart = run_start[::ntile][:N_EXPERTS + 1]
    eend = jnp.concatenate([estart[1:], jnp.array([ncap], I32)])
    nblocks = ncap // MOE_GRANS
    vals = jnp.concatenate([jnp.arange(nblocks, dtype=I32) * MOE_GRANS, estart[1:]])
    n = vals.shape[0]
    idx = jnp.arange(n, dtype=I32)
    before = (vals[None, :] < vals[:, None]) | ((vals[None, :] == vals[:, None]) & (idx[None, :] < idx[:, None]))
    place = jnp.sum(before.astype(I32), axis=1)
    cuts = jnp.sum(jnp.where(place[None, :] == idx[:, None], vals[None, :], 0), axis=1)
    nxt = jnp.concatenate([cuts[1:], jnp.array([ncap], I32)])
    blk = jnp.minimum(cuts // MOE_GRANS, nblocks - 1)
    owner = jnp.sum((eend[None, :] <= cuts[:, None]).astype(I32), axis=1)
    lo = cuts - blk * MOE_GRANS
    hi = lo + (nxt - cuts)
    live = hi > lo
    kind = jnp.where(live, jnp.where(owner >= N_EXPERTS, 2, 1), 0)
    works = kind == 1
    first_work = jnp.min(jnp.where(works, owner, N_EXPERTS - 1))
    exp = jnp.maximum(lax.cummax(jnp.where(works, owner, -1), axis=0), first_work)
    return gsrc.astype(I32), blk.astype(I32), exp.astype(I32), lo.astype(I32), hi.astype(I32), kind.astype(I32)


def kernel(x, meta_tokens, norm_mix, w_in, ssm_lam_re, ssm_lam_im, ssm_log_dt, ssm_b_re, ssm_b_im,
           ssm_c_re, ssm_c_im, ssm_d, w_glu, b_glu, attn_sinks, w_br_ssm, w_br_attn, w_out,
           norm_ffn, w_router, b_router, w_gate_up, b_gate_up, w_down, b_down, norm_final):
    bsz, seq, _ = x.shape
    lp = seq + BLK
    nb = lp // BLK
    np_ = bsz * lp
    tm = 640 if lp % 640 == 0 else BLK
    t8 = lp // SSM_STEP
    seg = t8 // SSM_SEGS
    nchunk = 5 if (t8 % 5 == 0 and (t8 // 5) % 16 == 0) else 1

    meta = meta_tokens.astype(F32)
    rc, rs1, rs2 = _rope_tables(lp)
    u, q, kx, vx, sgs, sga = _inproj(x, meta, norm_mix[0][None], w_in[0].astype(BF16), rc, rs1, rs2, tm=tm, lp=lp)

    tb, tc, td, tab = _ssm_tables(ssm_lam_re[0], ssm_lam_im[0], ssm_log_dt[0], ssm_b_re[0], ssm_b_im[0],
                                  ssm_c_re[0], ssm_c_im[0], ssm_d[0], seg)
    z = _ssm(u, tb, tc, td, tab, bsz=bsz, lp=lp, nchunk=nchunk)

    attn = _attention(q, kx, vx, attn_sinks[0].astype(F32), bsz=bsz, nb=nb)

    wr = jnp.pad(w_router[0].astype(F32), ((0, 0), (0, LANES - N_EXPERTS)))
    wrh = wr.astype(BF16)
    wrl = (wr - wrh.astype(F32)).astype(BF16)
    br = jnp.concatenate([b_router[0].astype(F32), jnp.full((LANES - N_EXPERTS,), NEG_INF, F32)])[None]
    h2, hnp, lslot, ecol, cnt = _post(
        x, meta, z, attn, sgs, sga, w_glu[0].astype(BF16), b_glu[0][None], w_br_ssm[0].astype(BF16),
        w_br_attn[0].astype(BF16), w_out[0].astype(BF16), norm_ffn[0][None], wrh, wrl, br, tm=tm, lp=lp)

    srows = -(-(tm * TOP_K + N_EXPERTS * (GRAN - 1)) // SORT_CHUNK) * SORT_CHUNK
    gt = srows // GRAN
    ntile = np_ // tm
    grans = cnt.reshape(ntile, 8, LANES)[:, 0, :N_EXPERTS].astype(I32)
    gsrc, sblk, sexp, slo, shi, skind = _route_tables(grans, gt)

    xloc = _sort(hnp, ecol, tm=tm, srows=srows)
    yloc = _moe(sblk, sexp, slo, shi, skind, gsrc, xloc, w_gate_up[0], b_gate_up[0][:, None, :],
                w_down[0], b_down[0][:, None, :], ncap=ntile * gt)
    return _combine(yloc, lslot, h2, norm_final[None], bsz=bsz, seq=seq, tm=tm, lp=lp, srows=srows)
```

```python
import functools
import math

import jax
import jax.numpy as jnp
import numpy as np
from jax import lax
from jax.experimental import pallas as pl
from jax.experimental.pallas import tpu as pltpu

F32 = jnp.float32
BF16 = jnp.bfloat16
I32 = jnp.int32
U32 = jnp.uint32

D_MODEL = 1024
N_META = 16
BLK = 128
PAD = BLK - N_META
N_GROUPS = 64
GROUP = 16
STATE = 64
N_Q_HEADS = 16
N_KV_HEADS = 2
HEAD_DIM = 64
ROT_HALF = 8
ROPE_THETA = 500000.0
N_EXPERTS = 32
TOP_K = 4
D_FF = 1024
SWIGLU_LIMIT = 7.0
SWIGLU_ALPHA = 1.702
RMS_EPS = 1e-5
NEG_INF = -1e30
SSM_STEP = 8
SSM_SEGS = 8
SSM_GBLK = 8
LANES = 128
MOE_BLK = 256
GRAN = 8
MOE_GRANS = MOE_BLK // GRAN
SORT_CHUNK = 256
SORT_W = 512 + LANES
VMEM_LIMIT = 56 * 1024 * 1024


def _cparams(sem):
    return pltpu.CompilerParams(dimension_semantics=sem, vmem_limit_bytes=VMEM_LIMIT)


def _dot(a, b):
    return jnp.dot(a, b, preferred_element_type=F32)


def _dot_nt(a, b):
    return lax.dot_general(a, b, (((1,), (1,)), ((), ())), preferred_element_type=F32)


def _pack_bf16_pairs(x, exact=False):
    w = x.shape[1] // 2
    bits = pltpu.bitcast(x if exact else x.astype(BF16).astype(F32), U32)
    return (bits[:, w:] & jnp.uint32(0xFFFF0000)) | (bits[:, :w] >> 16)


def _unpack_bf16_pairs(wd):
    lo = pltpu.bitcast(wd << 16, F32)
    hi = pltpu.bitcast(wd & jnp.uint32(0xFFFF0000), F32)
    return jnp.concatenate([lo, hi], axis=1)


def _padded_rows(x_refs, meta_ref, first_tile):
    head = jnp.concatenate([jnp.zeros((PAD, D_MODEL), F32), meta_ref[...]], axis=0)
    first = jnp.where(first_tile, head, x_refs[0][...])
    return jnp.concatenate([first] + [r[...] for r in x_refs[1:]], axis=0)


def _padded_row_specs(tm, tiles_per_seq):
    per = tm // BLK

    def spec(r):
        return pl.BlockSpec((None, BLK, D_MODEL),
                            lambda i: (i // tiles_per_seq, jnp.maximum((i % tiles_per_seq) * per + r - 1, 0), 0))

    return [spec(r) for r in range(per)]


def _inproj_kernel(*refs, tiles_per_seq, per):
    x_refs, (meta_ref, nrm_ref, w_ref, rc_ref, rs1_ref, rs2_ref,
             u_ref, q_ref, kx_ref, vx_ref, sgs_ref, sga_ref) = refs[:per], refs[per:]
    x = _padded_rows(x_refs, meta_ref, pl.program_id(0) % tiles_per_seq == 0)
    ms = jnp.mean(x * x, axis=-1, keepdims=True)
    hn = (x * lax.rsqrt(ms + RMS_EPS) * nrm_ref[...]).astype(BF16)

    def proj(lo, hi):
        return _dot(hn, w_ref[:, lo:hi])

    u_ref[...] = proj(0, 1024)

    rc = rc_ref[...]
    rs1 = rs1_ref[...]
    rs2 = rs2_ref[...]

    def rope(blk):
        return blk * rc + pltpu.roll(blk, ROT_HALF, 1) * rs1 + pltpu.roll(blk, LANES - ROT_HALF, 1) * rs2

    qf = proj(1024, 2048)
    scale = 1.0 / math.sqrt(HEAD_DIM)
    for j in range(8):
        q_ref[:, LANES * j:LANES * (j + 1)] = (rope(qf[:, LANES * j:LANES * (j + 1)]) * scale).astype(BF16)

    kvf = proj(2048, 2304)
    k = rope(kvf[:, :LANES])
    v = kvf[:, LANES:]
    low = lax.broadcasted_iota(I32, k.shape, 1) < HEAD_DIM

    def expand(t, ref):
        tr = pltpu.roll(t, HEAD_DIM, 1)
        zero = jnp.zeros_like(t)
        ref[:, 0:128] = jnp.where(low, t, zero).astype(BF16)
        ref[:, 128:256] = jnp.where(low, zero, tr).astype(BF16)
        ref[:, 256:384] = jnp.where(low, tr, zero).astype(BF16)
        ref[:, 384:512] = jnp.where(low, zero, t).astype(BF16)

    expand(k, kx_ref)
    expand(v, vx_ref)
    sgs_ref[...] = jax.nn.sigmoid(proj(2304, 3328)).astype(BF16)
    sga_ref[...] = jax.nn.sigmoid(proj(3328, 4352)).astype(BF16)


def _inproj(x, meta, nrm, w_bf, rc, rs1, rs2, *, tm, lp):
    np_ = x.shape[0] * lp
    tiles_per_seq = lp // tm
    per = tm // BLK
    row = lambda i: (i, 0)
    fixed = lambda i: (0, 0)
    rope_idx = lambda i: (i % tiles_per_seq, 0)
    out_shape = (
        jax.ShapeDtypeStruct((np_, 1024), F32),
        jax.ShapeDtypeStruct((np_, 1024), BF16),
        jax.ShapeDtypeStruct((np_, 512), BF16),
        jax.ShapeDtypeStruct((np_, 512), BF16),
        jax.ShapeDtypeStruct((np_, 1024), BF16),
        jax.ShapeDtypeStruct((np_, 1024), BF16),
    )
    return pl.pallas_call(
        functools.partial(_inproj_kernel, tiles_per_seq=tiles_per_seq, per=per),
        out_shape=out_shape,
        grid=(np_ // tm,),
        in_specs=_padded_row_specs(tm, tiles_per_seq) + [
            pl.BlockSpec((N_META, D_MODEL), fixed),
            pl.BlockSpec((1, D_MODEL), fixed),
            pl.BlockSpec(w_bf.shape, fixed),
            pl.BlockSpec((tm, LANES), rope_idx),
            pl.BlockSpec((tm, LANES), rope_idx),
            pl.BlockSpec((tm, LANES), rope_idx),
        ],
        out_specs=(
            pl.BlockSpec((tm, 1024), row),
            pl.BlockSpec((tm, 1024), row),
            pl.BlockSpec((tm, 512), row),
            pl.BlockSpec((tm, 512), row),
            pl.BlockSpec((tm, 1024), row),
            pl.BlockSpec((tm, 1024), row),
        ),
        compiler_params=_cparams(("arbitrary",)),
        name="inproj",
    )(*([x] * per), meta, nrm, w_bf, rc, rs1, rs2)


def _expand_block_diag(dst_ref, row0, t_ref, e_ref, row_shift, col_shift):
    n = t_ref.shape[0]
    step = 256
    for r0 in range(0, n, step):
        full = _dot(t_ref[r0:r0 + step, :], e_ref[...])
        rg = (lax.broadcasted_iota(I32, full.shape, 0) + r0) >> row_shift
        cg = lax.broadcasted_iota(I32, full.shape, 1) >> col_shift
        keep = ((rg ^ cg) & (SSM_GBLK - 1)) == 0
        dst_ref[row0 + r0:row0 + r0 + step, :] = jnp.where(keep, full, 0.0).astype(BF16)


def _ssm_kernel(u_ref, tb_ref, tc_ref, td_ref, eb_ref, ec_ref, tab_ref, z_ref,
                wb_ref, wcd_ref, ust_ref, bu_ref, yn_ref, *, t8, seg, nchunk):
    rc = t8 // nchunk

    @pl.when(pl.program_id(1) == 0)
    def _():
        _expand_block_diag(wb_ref, 0, tb_ref, eb_ref, 4, 6)
        _expand_block_diag(wcd_ref, 0, tc_ref, ec_ref, 6, 4)
        _expand_block_diag(wcd_ref, 1024, td_ref, ec_ref, 4, 4)

    for i in range(SSM_STEP):
        ust_ref[:, LANES * i:LANES * (i + 1)] = u_ref[pl.ds(i, t8, stride=SSM_STEP), :].astype(BF16)

    for c in range(nchunk):
        r = _dot(ust_ref[c * rc:(c + 1) * rc, :], wb_ref[...])
        for m in range(8):
            bu_ref[m, c * rc:(c + 1) * rc, :] = r[:, LANES * m:LANES * (m + 1)]

    a_re = [jnp.broadcast_to(tab_ref[0:1, LANES * m:LANES * (m + 1)], (SSM_SEGS, LANES)) for m in range(4)]
    a_im = [jnp.broadcast_to(tab_ref[1:2, LANES * m:LANES * (m + 1)], (SSM_SEGS, LANES)) for m in range(4)]
    s_re = [jnp.broadcast_to(tab_ref[2:3, LANES * m:LANES * (m + 1)], (SSM_SEGS, LANES)) for m in range(4)]
    s_im = [jnp.broadcast_to(tab_ref[3:4, LANES * m:LANES * (m + 1)], (SSM_SEGS, LANES)) for m in range(4)]

    def cmul_add(mr, mi, xr, xi, br, bi):
        return mr * xr - mi * xi + br, mr * xi + mi * xr + bi

    def load(k):
        return [bu_ref[m, pl.ds(k, SSM_SEGS, stride=seg), :] for m in range(8)]

    def advance(st, b):
        nr, ni = [], []
        for m in range(4):
            r_, i_ = cmul_add(a_re[m], a_im[m], st[m], st[4 + m], b[m], b[4 + m])
            nr.append(r_)
            ni.append(i_)
        return tuple(nr + ni)

    zeros = tuple(jnp.zeros((SSM_SEGS, LANES), F32) for _ in range(8))
    fin = lax.fori_loop(0, seg, lambda k, st: advance(st, load(k)), zeros)

    first = lax.broadcasted_iota(I32, (SSM_SEGS, LANES), 0) == 0

    def shift_down(t):
        return jnp.where(first, 0.0, pltpu.roll(t, 1, 0))

    tot = fin
    for _ in range(SSM_SEGS - 1):
        nxt_r, nxt_i = [], []
        for m in range(4):
            r_, i_ = cmul_add(s_re[m], s_im[m], shift_down(tot[m]), shift_down(tot[4 + m]), fin[m], fin[4 + m])
            nxt_r.append(r_)
            nxt_i.append(i_)
        tot = tuple(nxt_r + nxt_i)
    init = tuple(shift_down(t) for t in tot)

    def pass2(k, st):
        b = load(k)
        for m in range(8):
            bu_ref[m, pl.ds(k, SSM_SEGS, stride=seg), :] = st[m]
        return advance(st, b)

    lax.fori_loop(0, seg, pass2, init)

    for c in range(nchunk):
        xp = jnp.concatenate([bu_ref[m, c * rc:(c + 1) * rc, :] for m in range(8)], axis=1).astype(BF16)
        lhs = jnp.concatenate([xp, ust_ref[c * rc:(c + 1) * rc, :]], axis=1)
        y = _dot(lhs, wcd_ref[...])
        for i in range(SSM_STEP):
            yn_ref[pl.ds(SSM_STEP * c * rc + i, rc, stride=SSM_STEP), :] = y[:, LANES * i:LANES * (i + 1)]

    y = yn_ref[...] + tab_ref[4:5, 0:LANES] * u_ref[...]
    z_ref[...] = jax.nn.gelu(y).astype(BF16)


def _ssm(u, tb, tc, td, tab, *, bsz, lp, nchunk):
    np_ = u.shape[0]
    t8 = lp // SSM_STEP
    seg = t8 // SSM_SEGS
    nblk = D_MODEL // LANES
    col = jnp.arange(1024)
    src_b = (col >> 9) * STATE + (col & (STATE - 1))
    src_c = (col >> 7) * GROUP + (col & (GROUP - 1))
    eb = (jnp.arange(LANES)[:, None] == src_b[None, :]).astype(BF16)
    ec = (jnp.arange(LANES)[:, None] == src_c[None, :]).astype(BF16)
    kern = functools.partial(_ssm_kernel, t8=t8, seg=seg, nchunk=nchunk)
    per_q = lambda q, b: (q, 0, 0)
    fixed = lambda q, b: (0, 0)
    return pl.pallas_call(
        kern,
        out_shape=jax.ShapeDtypeStruct((np_, D_MODEL), BF16),
        grid=(nblk, bsz),
        in_specs=[
            pl.BlockSpec((lp, LANES), lambda q, b: (b, q)),
            pl.BlockSpec((None, 1024, LANES), per_q),
            pl.BlockSpec((None, 1024, LANES), per_q),
            pl.BlockSpec((None, 1024, LANES), per_q),
            pl.BlockSpec((LANES, 1024), fixed),
            pl.BlockSpec((LANES, 1024), fixed),
            pl.BlockSpec((None, 8, 512), per_q),
        ],
        out_specs=pl.BlockSpec((lp, LANES), lambda q, b: (b, q)),
        scratch_shapes=[
            pltpu.VMEM((1024, 1024), BF16),
            pltpu.VMEM((2048, 1024), BF16),
            pltpu.VMEM((t8, 1024), BF16),
            pltpu.VMEM((8, t8, LANES), F32),
            pltpu.VMEM((lp, LANES), F32),
        ],
        compiler_params=_cparams(("arbitrary", "arbitrary")),
        name="ssm",
    )(u, tb, tc, td, eb, ec, tab)


def _ssm_tables(lam_re, lam_im, log_dt, b_re, b_im, c_re, c_im, d, seg):
    hi = lax.Precision.HIGHEST
    lr, li = lam_re.astype(F32), lam_im.astype(F32)
    dt = jnp.exp(log_dt.astype(F32))[:, None]
    mag = jnp.exp(dt * lr)
    ar, ai = mag * jnp.cos(dt * li), mag * jnp.sin(dt * li)
    den = lr * lr + li * li
    nr, ni = ar - 1.0, ai
    fr, fi = (nr * lr + ni * li) / den, (ni * lr - nr * li) / den
    br, bi = b_re.astype(F32), b_im.astype(F32)
    bbr = fr[..., None] * br - fi[..., None] * bi
    bbi = fr[..., None] * bi + fi[..., None] * br

    def cmul(xr, xi, yr, yi):
        return xr * yr - xi * yi, xr * yi + xi * yr

    pr, pi = [jnp.ones_like(ar)], [jnp.zeros_like(ar)]
    for _ in range(SSM_STEP):
        r_, i_ = cmul(pr[-1], pi[-1], ar, ai)
        pr.append(r_)
        pi.append(i_)
    pw_r, pw_i = jnp.stack(pr), jnp.stack(pi)
    a8r, a8i = pw_r[SSM_STEP], pw_i[SSM_STEP]
    sr, si = jnp.ones_like(ar), jnp.zeros_like(ar)
    qr, qi = a8r, a8i
    e = seg
    while e:
        if e & 1:
            sr, si = cmul(sr, si, qr, qi)
        qr, qi = cmul(qr, qi, qr, qi)
        e >>= 1

    cr, ci = c_re.astype(F32), c_im.astype(F32)
    nb = N_GROUPS // SSM_GBLK

    rev_r = jnp.stack(pr[SSM_STEP - 1::-1])
    rev_i = jnp.stack(pi[SSM_STEP - 1::-1])
    wbr, wbi = cmul(rev_r[..., None], rev_i[..., None], bbr[None], bbi[None])
    wbg = jnp.stack([wbr, wbi], axis=0)
    wbg = wbg.reshape(2, SSM_STEP, nb, SSM_GBLK, STATE, GROUP)
    tb = jnp.transpose(wbg, (2, 1, 3, 5, 0, 4)).reshape(nb, 1024, LANES)

    upr, upi = pw_r[1:], pw_i[1:]
    mr, mi = cmul(cr[None], ci[None], upr[:, :, None, :], upi[:, :, None, :])
    wcg = jnp.stack([mr, -mi], axis=0)
    wcg = wcg.reshape(2, SSM_STEP, nb, SSM_GBLK, GROUP, STATE)
    tc = jnp.transpose(wcg, (2, 0, 3, 5, 1, 4)).reshape(nb, 1024, LANES)

    tr, ti = cmul(pw_r[:SSM_STEP, :, :, None], pw_i[:SSM_STEP, :, :, None], bbr[None], bbi[None])
    kt = (jnp.einsum('ghp,tgpk->tghk', cr, tr, precision=hi)
          - jnp.einsum('ghp,tgpk->tghk', ci, ti, precision=hi))
    zero_tap = jnp.zeros_like(kt[0])
    kd = jnp.stack([jnp.stack([kt[i - j] if i >= j else zero_tap for i in range(SSM_STEP)])
                    for j in range(SSM_STEP)])
    kd = kd.reshape(SSM_STEP, SSM_STEP, nb, SSM_GBLK, GROUP, GROUP)
    td = jnp.transpose(kd, (2, 0, 3, 5, 1, 4)).reshape(nb, 1024, LANES)

    def lanes(t):
        return t.reshape(nb, SSM_GBLK * STATE)

    dl = jnp.pad(d.astype(F32).reshape(nb, SSM_GBLK * GROUP), ((0, 0), (0, 512 - LANES)))
    zero = jnp.zeros((nb, 512), F32)
    tab = jnp.stack([lanes(a8r), lanes(a8i), lanes(sr), lanes(si), dl, zero, zero, zero], axis=1)
    return tb.astype(BF16), tc.astype(BF16), td.astype(BF16), tab


def _attn_kernel(q_ref, kc_ref, kp_ref, km_ref, vc_ref, vp_ref, vm_ref, sink_ref, o_ref):
    n = pl.program_id(1)
    nk = 3 * BLK
    row = lax.broadcasted_iota(I32, (BLK, nk), 0)
    col = lax.broadcasted_iota(I32, (BLK, nk), 1)
    band = (col > row) & (col <= row + BLK) & (col >= 2 * BLK - BLK * n)
    mrow = col - 2 * BLK
    meta = (mrow >= PAD) & (mrow <= BLK * n + row)
    bias = jnp.where(band | meta, 0.0, NEG_INF)

    for g in range(N_KV_HEADS):
        qg = jnp.concatenate([q_ref[:, LANES * (4 * g + jj):LANES * (4 * g + jj + 1)] for jj in range(4)], axis=0)
        acc = None
        for par in range(2):
            sl = slice(LANES * (2 * g + par), LANES * (2 * g + par + 1))
            kk = jnp.concatenate([kp_ref[:, sl], kc_ref[:, sl], km_ref[:, sl]], axis=0)
            vv = jnp.concatenate([vp_ref[:, sl], vc_ref[:, sl], vm_ref[:, sl]], axis=0)
            s_all = _dot_nt(qg, kk)
            probs, rden = [], []
            for jj in range(4):
                sink = sink_ref[8 * g + 2 * jj + par]
                s = s_all[BLK * jj:BLK * (jj + 1), :] + bias
                mx = jnp.maximum(jnp.max(s, axis=1, keepdims=True), sink)
                p = jnp.exp(s - mx)
                den = jnp.sum(p, axis=1, keepdims=True) + jnp.exp(sink - mx)
                probs.append(p.astype(BF16))
                rden.append(jnp.broadcast_to(1.0 / den, (BLK, LANES)))
            o = _dot(jnp.concatenate(probs, axis=0), vv) * jnp.concatenate(rden, axis=0)
            acc = o if acc is None else acc + o
        for jj in range(4):
            o_ref[:, LANES * (4 * g + jj):LANES * (4 * g + jj + 1)] = acc[BLK * jj:BLK * (jj + 1), :].astype(BF16)


def _attention(q, kx, vx, sinks, *, bsz, nb):
    np_ = q.shape[0]
    cur = lambda b, n: (b * nb + n, 0)
    prev = lambda b, n: (b * nb + jnp.maximum(n - 1, 0), 0)
    first = lambda b, n: (b * nb, 0)
    return pl.pallas_call(
        _attn_kernel,
        out_shape=jax.ShapeDtypeStruct((np_, 1024), BF16),
        grid=(bsz, nb),
        in_specs=[
            pl.BlockSpec((BLK, 1024), cur),
            pl.BlockSpec((BLK, 512), cur),
            pl.BlockSpec((BLK, 512), prev),
            pl.BlockSpec((BLK, 512), first),
            pl.BlockSpec((BLK, 512), cur),
            pl.BlockSpec((BLK, 512), prev),
            pl.BlockSpec((BLK, 512), first),
            pl.BlockSpec(memory_space=pltpu.SMEM),
        ],
        out_specs=pl.BlockSpec((BLK, 1024), cur),
        compiler_params=_cparams(("arbitrary", "arbitrary")),
        name="attn",
    )(q, kx, kx, kx, vx, vx, vx, sinks)


def _post_kernel(*refs, tiles_per_seq, per):
    x_refs, (meta_ref, z_ref, at_ref, sgs_ref, sga_ref, wglu_ref, bglu_ref, wbs_ref, wba_ref, wout_ref,
             nffn_ref, wrh_ref, wrl_ref, br_ref,
             h2_ref, hnp_ref, lslot_ref, ecol_ref, cnt_ref) = refs[:per], refs[per:]
    i = pl.program_id(0)

    z = z_ref[...]
    t = _dot(z, wglu_ref[...]) + bglu_ref[...]
    so = (z.astype(F32) * jax.nn.sigmoid(t)).astype(BF16)
    mix = (sgs_ref[...].astype(F32) * _dot(so, wbs_ref[...])
           + sga_ref[...].astype(F32) * _dot(at_ref[...], wba_ref[...]))
    h2 = _padded_rows(x_refs, meta_ref, i % tiles_per_seq == 0) + _dot(mix.astype(BF16), wout_ref[...])
    h2_ref[...] = h2

    ms = jnp.mean(h2 * h2, axis=-1, keepdims=True)
    hn = h2 * lax.rsqrt(ms + RMS_EPS) * nffn_ref[...]
    hb = hn.astype(BF16)
    hnp_ref[...] = hb

    lo = (hn - hb.astype(F32)).astype(BF16)
    logits = _dot(hb, wrh_ref[...]) + _dot(lo, wrh_ref[...]) + _dot(hb, wrl_ref[...]) + br_ref[...]

    tm = logits.shape[0]
    lane = lax.broadcasted_iota(I32, (tm, LANES), 1)
    lane_f = lane.astype(F32)
    work = logits
    vals, hots, idxs = [], [], []
    for _ in range(TOP_K):
        mx = jnp.max(work, axis=1, keepdims=True)
        idx = jnp.min(jnp.where(work == mx, lane_f, float(LANES)), axis=1, keepdims=True)
        hot = lane_f == idx
        vals.append(mx)
        idxs.append(idx)
        hots.append(hot)
        work = jnp.where(hot, -jnp.inf, work)

    ex = [jnp.exp(v - vals[0]) for v in vals]
    tot = ex[0] + ex[1] + ex[2] + ex[3]
    gates = [e / tot for e in ex]

    onehot = jnp.zeros((tm, LANES), F32)
    for hot in hots:
        onehot = onehot + jnp.where(hot, 1.0, 0.0)
    r_i = lax.broadcasted_iota(I32, (tm, tm), 0)
    c_i = lax.broadcasted_iota(I32, (tm, tm), 1)
    tri = jnp.where(r_i > c_i, 1.0, 0.0).astype(BF16)
    before = _dot(tri, onehot.astype(BF16))
    count = jnp.sum(onehot, axis=0, keepdims=True)
    gran = jnp.floor((count + (GRAN - 1)) * (1.0 / GRAN))
    e_r = lax.broadcasted_iota(I32, (LANES, LANES), 0)
    e_c = lax.broadcasted_iota(I32, (LANES, LANES), 1)
    upper = jnp.where(e_r < e_c, 1.0, 0.0).astype(BF16)
    gstart = _dot(jnp.broadcast_to(gran, (8, LANES)).astype(BF16), upper)[0:1, :]
    base = before + gstart * float(GRAN)
    slots = [jnp.sum(jnp.where(hot, base, 0.0), axis=1, keepdims=True) for hot in hots]
    cnt_ref[...] = jnp.broadcast_to(gran, cnt_ref.shape)

    def spread(cols, dtype):
        out = jnp.zeros((tm, LANES), dtype)
        for k, cval in enumerate(cols):
            out = jnp.where(lane == k, cval, out)
        return out

    lslot_ref[...] = spread(slots, F32)
    cols = []
    for g in gates:
        g1 = g.astype(BF16).astype(F32)
        g2 = (g - g1).astype(BF16).astype(F32)
        cols += [g1, g2, g - g1 - g2]
    for s in slots:
        hi = jnp.floor(s * (1.0 / 64.0))
        cols += [hi, s - 64.0 * hi]
    ecol_ref[...] = spread(cols, F32)


def _post(x, meta, z, attn, sgs, sga, wglu, bglu, wbs, wba, wout, nffn, wrh, wrl, br, *, tm, lp):
    np_ = z.shape[0]
    tiles_per_seq = lp // tm
    per = tm // BLK
    row = lambda i: (i, 0)
    fixed = lambda i: (0, 0)
    sq = pl.BlockSpec((1024, 1024), fixed)
    vec = pl.BlockSpec((1, 1024), fixed)
    out_shape = (
        jax.ShapeDtypeStruct((np_, 1024), F32),
        jax.ShapeDtypeStruct((np_, 1024), BF16),
        jax.ShapeDtypeStruct((np_, LANES), F32),
        jax.ShapeDtypeStruct((np_, LANES), F32),
        jax.ShapeDtypeStruct((np_ // tm * 8, LANES), F32),
    )
    return pl.pallas_call(
        functools.partial(_post_kernel, tiles_per_seq=tiles_per_seq, per=per),
        out_shape=out_shape,
        grid=(np_ // tm,),
        in_specs=_padded_row_specs(tm, tiles_per_seq) + [
            pl.BlockSpec((N_META, D_MODEL), fixed),
            pl.BlockSpec((tm, 1024), row), pl.BlockSpec((tm, 1024), row),
            pl.BlockSpec((tm, 1024), row), pl.BlockSpec((tm, 1024), row),
            sq, vec, sq, sq, sq, vec,
            pl.BlockSpec((1024, LANES), fixed), pl.BlockSpec((1024, LANES), fixed),
            pl.BlockSpec((1, LANES), fixed),
        ],
        out_specs=(
            pl.BlockSpec((tm, 1024), row), pl.BlockSpec((tm, 1024), row),
            pl.BlockSpec((tm, LANES), row), pl.BlockSpec((tm, LANES), row),
            pl.BlockSpec((8, LANES), row),
        ),
        compiler_params=_cparams(("arbitrary",)),
        name="post",
    )(*([x] * per), meta, z, attn, sgs, sga, wglu, bglu, wbs, wba, wout, nffn, wrh, wrl, br)


def _sort_kernel(hnp_ref, ecol_ref, xloc_ref, *, tm, srows):
    ecol = ecol_ref[...].astype(BF16)
    pay = jnp.concatenate([hnp_ref[...], ecol], axis=1)

    r8 = lax.broadcasted_iota(I32, (8, LANES), 0)
    l8 = lax.broadcasted_iota(I32, (8, LANES), 1)
    digit_w = jnp.where(l8 == 12 + 2 * r8, 64.0, jnp.where(l8 == 13 + 2 * r8, 1.0, 0.0)).astype(BF16)
    slot_rows = _dot_nt(digit_w, ecol)

    rr = lax.broadcasted_iota(I32, (LANES, LANES), 0)
    cc = lax.broadcasted_iota(I32, (LANES, LANES), 1)
    kk = jnp.where(cc < 3 * TOP_K, cc // 3, -LANES)
    spread_w = jnp.where(rr == 12 + 2 * kk, 64.0, jnp.where(rr == 13 + 2 * kk, 1.0, 0.0)).astype(BF16)
    gate_lane = lax.broadcasted_iota(I32, (SORT_CHUNK, LANES), 1) < 3 * TOP_K

    for c in range(srows // SORT_CHUNK):
        s_id = (lax.broadcasted_iota(I32, (SORT_CHUNK, tm), 0) + c * SORT_CHUNK).astype(F32)
        perm = jnp.zeros((SORT_CHUNK, tm), F32)
        for k in range(TOP_K):
            perm = perm + jnp.where(slot_rows[k:k + 1, :] == s_id, 1.0, 0.0)
        got = _dot(perm.astype(BF16), pay)
        rows = slice(c * SORT_CHUNK, (c + 1) * SORT_CHUNK)
        xloc_ref[rows, 0:512] = _pack_bf16_pairs(got[:, :D_MODEL], exact=True)
        extra = got[:, D_MODEL:]
        owner = _dot(extra.astype(BF16), spread_w)
        me = (lax.broadcasted_iota(I32, (SORT_CHUNK, LANES), 0) + c * SORT_CHUNK).astype(F32)
        mine = jnp.where(gate_lane, owner, -1.0) == me
        gate = jnp.sum(jnp.where(mine, extra, 0.0), axis=1, keepdims=True)
        xloc_ref[rows, 512:SORT_W] = pltpu.bitcast(jnp.broadcast_to(gate, (SORT_CHUNK, LANES)), U32)


def _sort(hnp, ecol, *, tm, srows):
    np_ = hnp.shape[0]
    ntile = np_ // tm
    row = lambda i: (i, 0)
    return pl.pallas_call(
        functools.partial(_sort_kernel, tm=tm, srows=srows),
        out_shape=jax.ShapeDtypeStruct((ntile * srows, SORT_W), U32),
        grid=(ntile,),
        in_specs=[pl.BlockSpec((tm, D_MODEL), row), pl.BlockSpec((tm, LANES), row)],
        out_specs=pl.BlockSpec((srows, SORT_W), row),
        compiler_params=_cparams(("arbitrary",)),
        name="sort",
    )(hnp, ecol)


def _moe_kernel(blk_ref, exp_ref, lo_ref, hi_ref, kind_ref, wslot_ref, nexte_ref, gsrc_ref,
                xloc_ref, wgu_ref, bgu_ref, wd_ref, bd_ref, yloc_ref,
                xbuf, ybuf, sem_in, sem_out, wgu_f, wd_f, sem_w, wgu_bf, wd_bf, *, nseg, dump0):
    i = pl.program_id(0)
    prev = jnp.maximum(i - 1, 0)
    nxt = jnp.minimum(i + 1, nseg - 1)
    slot = i % 2
    new_expert = (i == 0) | (exp_ref[i] != exp_ref[prev])

    def gather(seg, dst_slot, start):
        b = blk_ref[seg]
        for g in range(MOE_GRANS):
            src = pl.multiple_of(gsrc_ref[b * MOE_GRANS + g] * GRAN, GRAN)
            cp = pltpu.make_async_copy(xloc_ref.at[pl.ds(src, GRAN)], xbuf.at[dst_slot, pl.ds(g * GRAN, GRAN)],
                                       sem_in.at[dst_slot])
            if start:
                cp.start()
            else:
                cp.wait()

    def scatter(seg, src_slot, start):
        b = blk_ref[seg]
        lo = lo_ref[seg]
        hi = hi_ref[seg]
        for g in range(MOE_GRANS):
            keep = (g >= lo) & (g < hi)
            spare = dump0 + src_slot * MOE_GRANS + g
            dst = pl.multiple_of(jnp.where(keep, gsrc_ref[b * MOE_GRANS + g], spare) * GRAN, GRAN)
            cp = pltpu.make_async_copy(ybuf.at[src_slot, pl.ds(g * GRAN, GRAN)], yloc_ref.at[pl.ds(dst, GRAN)],
                                       sem_out.at[src_slot])
            if start:
                cp.start()
            else:
                cp.wait()

    @pl.when(i == 0)
    def _():
        gather(i, slot, True)
        ybuf[1 - slot] = jnp.zeros((MOE_BLK, 512), U32)
        for g in range(MOE_GRANS):
            pltpu.make_async_copy(ybuf.at[1 - slot, pl.ds(g * GRAN, GRAN)],
                                  yloc_ref.at[pl.ds((dump0 + (1 - slot) * MOE_GRANS + g) * GRAN, GRAN)],
                                  sem_out.at[1 - slot]).start()

    gather(i, slot, False)

    def fetch(e, slot, start):
        for src, dst in ((wgu_ref, wgu_f), (wd_ref, wd_f)):
            cp = pltpu.make_async_copy(src.at[e], dst.at[slot], sem_w.at[slot])
            if start:
                cp.start()
            else:
                cp.wait()

    wslot = wslot_ref[i]

    @pl.when(i == 0)
    def _():
        fetch(exp_ref[i], wslot, True)

    @pl.when(new_expert)
    def _():
        fetch(exp_ref[i], wslot, False)
        wgu_bf[...] = wgu_f[wslot].astype(BF16)
        wd_bf[...] = wd_f[wslot].astype(BF16)

    @pl.when(new_expert & (nexte_ref[i] >= 0))
    def _():
        fetch(nexte_ref[i], 1 - wslot, True)

    kind = kind_ref[i]

    @pl.when(kind == 1)
    def _():
        gather(nxt, 1 - slot, True)
        xw = xbuf[slot]
        xb = _unpack_bf16_pairs(xw[:, 0:512]).astype(BF16)
        gate = pltpu.bitcast(xw[:, 512:SORT_W], F32)
        gu = _dot(xb, wgu_bf[...]) + bgu_ref[...]
        g = jnp.minimum(gu[:, :D_FF], SWIGLU_LIMIT)
        up = jnp.clip(gu[:, D_FF:], -SWIGLU_LIMIT, SWIGLU_LIMIT)
        hid = g * jax.nn.sigmoid(SWIGLU_ALPHA * g) * (up + 1.0)
        y = _dot(hid.astype(BF16), wd_bf[...]) + bd_ref[...]
        y = y * jnp.concatenate([gate] * (D_MODEL // LANES), axis=1)
        ybuf[slot] = _pack_bf16_pairs(y)
        scatter(i, slot, True)
        scatter(prev, 1 - slot, False)

    @pl.when(kind != 1)
    def _():
        gather(nxt, 1 - slot, True)
        ybuf[slot] = jnp.zeros((MOE_BLK, 512), U32)
        scatter(i, slot, True)
        scatter(prev, 1 - slot, False)

    @pl.when(i == nseg - 1)
    def _():
        scatter(i, slot, False)
        gather(nxt, 1 - slot, False)


def _moe(blk, exp, lo, hi, kind, gsrc, xloc, wgu, bgu, wd, bd, *, ncap):
    nseg = blk.shape[0]
    dump0 = ncap
    change = jnp.concatenate([jnp.zeros((1,), I32), (exp[1:] != exp[:-1]).astype(I32)])
    wslot = jnp.cumsum(change) % 2
    later = jnp.where(exp[None, :] > exp[:, None], exp[None, :], N_EXPERTS)
    nexte = jnp.min(later, axis=1)
    nexte = jnp.where(nexte >= N_EXPERTS, -1, nexte)
    wmap = lambda i, b, e, *_: (e[i], 0, 0)
    grid_spec = pltpu.PrefetchScalarGridSpec(
        num_scalar_prefetch=8,
        grid=(nseg,),
        in_specs=[
            pl.BlockSpec(memory_space=pl.ANY),
            pl.BlockSpec(memory_space=pl.ANY),
            pl.BlockSpec((None, 1, 2 * D_FF), wmap),
            pl.BlockSpec(memory_space=pl.ANY),
            pl.BlockSpec((None, 1, D_MODEL), wmap),
        ],
        out_specs=pl.BlockSpec(memory_space=pl.ANY),
        scratch_shapes=[
            pltpu.VMEM((2, MOE_BLK, SORT_W), U32),
            pltpu.VMEM((2, MOE_BLK, 512), U32),
            pltpu.SemaphoreType.DMA((2,)),
            pltpu.SemaphoreType.DMA((2,)),
            pltpu.VMEM((2, D_MODEL, 2 * D_FF), F32),
            pltpu.VMEM((2, D_FF, D_MODEL), F32),
            pltpu.SemaphoreType.DMA((2,)),
            pltpu.VMEM((D_MODEL, 2 * D_FF), BF16),
            pltpu.VMEM((D_FF, D_MODEL), BF16),
        ],
    )
    return pl.pallas_call(
        functools.partial(_moe_kernel, nseg=nseg, dump0=dump0),
        out_shape=jax.ShapeDtypeStruct(((ncap + 2 * MOE_GRANS) * GRAN, 512), U32),
        grid_spec=grid_spec,
        compiler_params=_cparams(("arbitrary",)),
        name="moe",
    )(blk, exp, lo, hi, kind, wslot.astype(I32), nexte.astype(I32), gsrc, xloc, wgu, bgu, wd, bd)


def _combine_kernel(y_ref, lslot_ref, h2_ref, nfin_ref, o_ref, obuf, sem, *, tm, srows, tiles_per_seq):
    i = pl.program_id(0)
    b = i // tiles_per_seq
    j = i % tiles_per_seq
    per = tm // BLK
    slots = lslot_ref[...]
    acc = h2_ref[...]
    for c in range(srows // SORT_CHUNK):
        s_id = (lax.broadcasted_iota(I32, (tm, SORT_CHUNK), 1) + c * SORT_CHUNK).astype(F32)
        take = jnp.zeros((tm, SORT_CHUNK), F32)
        for k in range(TOP_K):
            take = take + jnp.where(slots[:, k:k + 1] == s_id, 1.0, 0.0)
        rows = _unpack_bf16_pairs(y_ref[c * SORT_CHUNK:(c + 1) * SORT_CHUNK, :]).astype(BF16)
        acc = acc + _dot(take.astype(BF16), rows)
    ms = jnp.mean(acc * acc, axis=-1, keepdims=True)
    obuf[...] = acc * lax.rsqrt(ms + RMS_EPS) * nfin_ref[...]

    def copies(start):
        for r in range(per):
            n = j * per + r
            cp = pltpu.make_async_copy(obuf.at[pl.ds(r * BLK, BLK)],
                                       o_ref.at[b, pl.ds(pl.multiple_of(jnp.maximum(n - 1, 0) * BLK, BLK), BLK)], sem)

            def go():
                if start:
                    cp.start()
                else:
                    cp.wait()

            if r == 0:
                pl.when(j > 0)(go)
            else:
                go()

    copies(True)
    copies(False)


def _combine(yloc, lslot, h2, nfin, *, bsz, seq, tm, lp, srows):
    np_ = h2.shape[0]
    tiles_per_seq = lp // tm
    row = lambda i: (i, 0)
    return pl.pallas_call(
        functools.partial(_combine_kernel, tm=tm, srows=srows, tiles_per_seq=tiles_per_seq),
        out_shape=jax.ShapeDtypeStruct((bsz, seq, D_MODEL), F32),
        grid=(np_ // tm,),
        in_specs=[
            pl.BlockSpec((srows, 512), row),
            pl.BlockSpec((tm, LANES), row),
            pl.BlockSpec((tm, D_MODEL), row),
            pl.BlockSpec((1, D_MODEL), lambda i: (0, 0)),
        ],
        out_specs=pl.BlockSpec(memory_space=pl.ANY),
        scratch_shapes=[pltpu.VMEM((tm, D_MODEL), F32), pltpu.SemaphoreType.DMA],
        compiler_params=_cparams(("arbitrary",)),
        name="combine",
    )(yloc, lslot, h2, nfin)


def _rope_tables(lp):
    pos = np.arange(lp, dtype=np.float32) - np.float32(PAD)
    inv_freq = np.float32(ROPE_THETA) ** (-np.arange(0, 2 * ROT_HALF, 2, dtype=np.float32) / np.float32(2 * ROT_HALF))
    ang = (pos[:, None] * inv_freq[None, :]).astype(np.float32)
    cos, sin = np.cos(ang), np.sin(ang)
    ones = np.ones((lp, HEAD_DIM - 2 * ROT_HALF), np.float32)
    zeros8 = np.zeros((lp, ROT_HALF), np.float32)
    zrest = np.zeros((lp, HEAD_DIM - 2 * ROT_HALF), np.float32)
    c = np.concatenate([cos, cos, ones], axis=1)
    s1 = np.concatenate([zeros8, sin, zrest], axis=1)
    s2 = np.concatenate([-sin, zeros8, zrest], axis=1)
    tile = lambda t: jnp.asarray(np.concatenate([t, t], axis=1), F32)
    return tile(c), tile(s1), tile(s2)


def _route_tables(grans, gt):
    ntile = grans.shape[0]
    ncap = ntile * gt
    used = jnp.sum(grans, axis=1)
    lstart = jnp.cumsum(grans, axis=1) - grans
    tile0 = jnp.arange(ntile, dtype=I32)[:, None] * gt
    run_len = jnp.concatenate([grans.T.reshape(-1), gt - used])
    run_src = jnp.concatenate([(tile0 + lstart).T.reshape(-1), tile0[:, 0] + used])
    run_end = jnp.cumsum(run_len)
    run_start = run_end - run_len
    off = run_src - run_start
    delta = off - jnp.concatenate([jnp.zeros((1,), I32), off[:-1]])
    pos = jnp.arange(ncap, dtype=I32)
    gsrc = pos + jnp.sum(jnp.where(run_start[None, :] <= pos[:, None], delta[None, :], 0), axis=1)

    estart = run_start[::ntile][:N_EXPERTS + 1]
    eend = jnp.concatenate([estart[1:], jnp.array([ncap], I32)])
    nblocks = ncap // MOE_GRANS
    vals = jnp.concatenate([jnp.arange(nblocks, dtype=I32) * MOE_GRANS, estart[1:]])
    n = vals.shape[0]
    idx = jnp.arange(n, dtype=I32)
    before = (vals[None, :] < vals[:, None]) | ((vals[None, :] == vals[:, None]) & (idx[None, :] < idx[:, None]))
    place = jnp.sum(before.astype(I32), axis=1)
    cuts = jnp.sum(jnp.where(place[None, :] == idx[:, None], vals[None, :], 0), axis=1)
    nxt = jnp.concatenate([cuts[1:], jnp.array([ncap], I32)])
    blk = jnp.minimum(cuts // MOE_GRANS, nblocks - 1)
    owner = jnp.sum((eend[None, :] <= cuts[:, None]).astype(I32), axis=1)
    lo = cuts - blk * MOE_GRANS
    hi = lo + (nxt - cuts)
    live = hi > lo
    kind = jnp.where(live, jnp.where(owner >= N_EXPERTS, 2, 1), 0)
    works = kind == 1
    first_work = jnp.min(jnp.where(works, owner, N_EXPERTS - 1))
    exp = jnp.maximum(lax.cummax(jnp.where(works, owner, -1), axis=0), first_work)
    return gsrc.astype(I32), blk.astype(I32), exp.astype(I32), lo.astype(I32), hi.astype(I32), kind.astype(I32)


def kernel(x, meta_tokens, norm_mix, w_in, ssm_lam_re, ssm_lam_im, ssm_log_dt, ssm_b_re, ssm_b_im,
           ssm_c_re, ssm_c_im, ssm_d, w_glu, b_glu, attn_sinks, w_br_ssm, w_br_attn, w_out,
           norm_ffn, w_router, b_router, w_gate_up, b_gate_up, w_down, b_down, norm_final):
    bsz, seq, _ = x.shape
    lp = seq + BLK
    nb = lp // BLK
    np_ = bsz * lp
    tm = 640 if lp % 640 == 0 else BLK
    t8 = lp // SSM_STEP
    seg = t8 // SSM_SEGS
    nchunk = 5 if (t8 % 5 == 0 and (t8 // 5) % 16 == 0) else 1

    meta = meta_tokens.astype(F32)
    rc, rs1, rs2 = _rope_tables(lp)
    u, q, kx, vx, sgs, sga = _inproj(x, meta, norm_mix[0][None], w_in[0].astype(BF16), rc, rs1, rs2, tm=tm, lp=lp)

    tb, tc, td, tab = _ssm_tables(ssm_lam_re[0], ssm_lam_im[0], ssm_log_dt[0], ssm_b_re[0], ssm_b_im[0],
                                  ssm_c_re[0], ssm_c_im[0], ssm_d[0], seg)
    z = _ssm(u, tb, tc, td, tab, bsz=bsz, lp=lp, nchunk=nchunk)

    attn = _attention(q, kx, vx, attn_sinks[0].astype(F32), bsz=bsz, nb=nb)

    wr = jnp.pad(w_router[0].astype(F32), ((0, 0), (0, LANES - N_EXPERTS)))
    wrh = wr.astype(BF16)
    wrl = (wr - wrh.astype(F32)).astype(BF16)
    br = jnp.concatenate([b_router[0].astype(F32), jnp.full((LANES - N_EXPERTS,), NEG_INF, F32)])[None]
    h2, hnp, lslot, ecol, cnt = _post(
        x, meta, z, attn, sgs, sga, w_glu[0].astype(BF16), b_glu[0][None], w_br_ssm[0].astype(BF16),
        w_br_attn[0].astype(BF16), w_out[0].astype(BF16), norm_ffn[0][None], wrh, wrl, br, tm=tm, lp=lp)

    srows = -(-(tm * TOP_K + N_EXPERTS * (GRAN - 1)) // SORT_CHUNK) * SORT_CHUNK
    gt = srows // GRAN
    ntile = np_ // tm
    grans = cnt.reshape(ntile, 8, LANES)[:, 0, :N_EXPERTS].astype(I32)
    gsrc, sblk, sexp, slo, shi, skind = _route_tables(grans, gt)

    xloc = _sort(hnp, ecol, tm=tm, srows=srows)
    yloc = _moe(sblk, sexp, slo, shi, skind, gsrc, xloc, w_gate_up[0], b_gate_up[0][:, None, :],
                w_down[0], b_down[0][:, None, :], ncap=ntile * gt)
    return _combine(yloc, lslot, h2, norm_final[None], bsz=bsz, seq=seq, tm=tm, lp=lp, srows=srows)
```

```python
import functools
import math

import jax
import jax.numpy as jnp
import numpy as np
from jax import lax
from jax.experimental import pallas as pl
from jax.experimental.pallas import tpu as pltpu

F32 = jnp.float32
BF16 = jnp.bfloat16
I32 = jnp.int32
U32 = jnp.uint32

D_MODEL = 1024
N_META = 16
BLK = 128
PAD = BLK - N_META
N_GROUPS = 64
GROUP = 16
STATE = 64
N_Q_HEADS = 16
N_KV_HEADS = 2
HEAD_DIM = 64
ROT_HALF = 8
ROPE_THETA = 500000.0
N_EXPERTS = 32
TOP_K = 4
D_FF = 1024
SWIGLU_LIMIT = 7.0
SWIGLU_ALPHA = 1.702
RMS_EPS = 1e-5
NEG_INF = -1e30
SSM_STEP = 8
SSM_SEGS = 8
SSM_GBLK = 8
LANES = 128
MOE_BLK = 256
GRAN = 8
MOE_GRANS = MOE_BLK // GRAN
SORT_CHUNK = 256
SORT_W = 512 + LANES
VMEM_LIMIT = 56 * 1024 * 1024


def _cparams(sem):
    return pltpu.CompilerParams(dimension_semantics=sem, vmem_limit_bytes=VMEM_LIMIT)


def _dot(a, b):
    return jnp.dot(a, b, preferred_element_type=F32)


def _dot_nt(a, b):
    return lax.dot_general(a, b, (((1,), (1,)), ((), ())), preferred_element_type=F32)


def _pack_bf16_pairs(x, exact=False):
    w = x.shape[1] // 2
    bits = pltpu.bitcast(x if exact else x.astype(BF16).astype(F32), U32)
    return (bits[:, w:] & jnp.uint32(0xFFFF0000)) | (bits[:, :w] >> 16)


def _unpack_bf16_pairs(wd):
    lo = pltpu.bitcast(wd << 16, F32)
    hi = pltpu.bitcast(wd & jnp.uint32(0xFFFF0000), F32)
    return jnp.concatenate([lo, hi], axis=1)


def _padded_rows(x_refs, meta_ref, first_tile):
    head = jnp.concatenate([jnp.zeros((PAD, D_MODEL), F32), meta_ref[...]], axis=0)
    first = jnp.where(first_tile, head, x_refs[0][...])
    return [first] + [r[...] for r in x_refs[1:]]


def _padded_row_specs(tm, tiles_per_seq):
    per = tm // BLK

    def spec(r):
        return pl.BlockSpec((None, BLK, D_MODEL),
                            lambda i: (i // tiles_per_seq, jnp.maximum((i % tiles_per_seq) * per + r - 1, 0), 0))

    return [spec(r) for r in range(per)]


def _inproj_kernel(*refs, tiles_per_seq, per):
    x_refs, (meta_ref, nrm_ref, w_ref, rc_ref, rs1_ref, rs2_ref,
             u_ref, q_ref, kx_ref, vx_ref, sgs_ref, sga_ref) = refs[:per], refs[per:]
    normed = []
    for x in _padded_rows(x_refs, meta_ref, pl.program_id(0) % tiles_per_seq == 0):
        ms = jnp.mean(x * x, axis=-1, keepdims=True)
        normed.append((x * lax.rsqrt(ms + RMS_EPS) * nrm_ref[...]).astype(BF16))
    hn = jnp.concatenate(normed, axis=0)

    def proj(lo, hi):
        return _dot(hn, w_ref[:, lo:hi])

    u_ref[...] = proj(0, 1024)

    rc = rc_ref[...]
    rs1 = rs1_ref[...]
    rs2 = rs2_ref[...]

    def rope(blk):
        return blk * rc + pltpu.roll(blk, ROT_HALF, 1) * rs1 + pltpu.roll(blk, LANES - ROT_HALF, 1) * rs2

    qf = proj(1024, 2048)
    scale = 1.0 / math.sqrt(HEAD_DIM)
    for j in range(8):
        q_ref[:, LANES * j:LANES * (j + 1)] = (rope(qf[:, LANES * j:LANES * (j + 1)]) * scale).astype(BF16)

    kvf = proj(2048, 2304)
    k = rope(kvf[:, :LANES])
    v = kvf[:, LANES:]
    low = lax.broadcasted_iota(I32, k.shape, 1) < HEAD_DIM

    def expand(t, ref):
        tr = pltpu.roll(t, HEAD_DIM, 1)
        zero = jnp.zeros_like(t)
        ref[:, 0:128] = jnp.where(low, t, zero).astype(BF16)
        ref[:, 128:256] = jnp.where(low, zero, tr).astype(BF16)
        ref[:, 256:384] = jnp.where(low, tr, zero).astype(BF16)
        ref[:, 384:512] = jnp.where(low, zero, t).astype(BF16)

    expand(k, kx_ref)
    expand(v, vx_ref)
    sgs_ref[...] = jax.nn.sigmoid(proj(2304, 3328)).astype(BF16)
    sga_ref[...] = jax.nn.sigmoid(proj(3328, 4352)).astype(BF16)


def _inproj(x, meta, nrm, w_bf, rc, rs1, rs2, *, tm, lp):
    np_ = x.shape[0] * lp
    tiles_per_seq = lp // tm
    per = tm // BLK
    row = lambda i: (i, 0)
    fixed = lambda i: (0, 0)
    rope_idx = lambda i: (i % tiles_per_seq, 0)
    out_shape = (
        jax.ShapeDtypeStruct((np_, 1024), F32),
        jax.ShapeDtypeStruct((np_, 1024), BF16),
        jax.ShapeDtypeStruct((np_, 512), BF16),
        jax.ShapeDtypeStruct((np_, 512), BF16),
        jax.ShapeDtypeStruct((np_, 1024), BF16),
        jax.ShapeDtypeStruct((np_, 1024), BF16),
    )
    return pl.pallas_call(
        functools.partial(_inproj_kernel, tiles_per_seq=tiles_per_seq, per=per),
        out_shape=out_shape,
        grid=(np_ // tm,),
        in_specs=_padded_row_specs(tm, tiles_per_seq) + [
            pl.BlockSpec((N_META, D_MODEL), fixed),
            pl.BlockSpec((1, D_MODEL), fixed),
            pl.BlockSpec(w_bf.shape, fixed),
            pl.BlockSpec((tm, LANES), rope_idx),
            pl.BlockSpec((tm, LANES), rope_idx),
            pl.BlockSpec((tm, LANES), rope_idx),
        ],
        out_specs=(
            pl.BlockSpec((tm, 1024), row),
            pl.BlockSpec((tm, 1024), row),
            pl.BlockSpec((tm, 512), row),
            pl.BlockSpec((tm, 512), row),
            pl.BlockSpec((tm, 1024), row),
            pl.BlockSpec((tm, 1024), row),
        ),
        compiler_params=_cparams(("arbitrary",)),
        name="inproj",
    )(*([x] * per), meta, nrm, w_bf, rc, rs1, rs2)


def _expand_block_diag(dst_ref, row0, t, e_ref, row_shift, col_shift):
    n = t.shape[0]
    step = min(n, 256)
    for r0 in range(0, n, step):
        full = _dot(t[r0:r0 + step, :].astype(BF16), e_ref[...])
        rg = (lax.broadcasted_iota(I32, full.shape, 0) + r0) >> row_shift
        cg = lax.broadcasted_iota(I32, full.shape, 1) >> col_shift
        keep = ((rg ^ cg) & (SSM_GBLK - 1)) == 0
        dst_ref[row0 + r0:row0 + r0 + step, :] = jnp.where(keep, full, 0.0).astype(BF16)


def _ssm_kernel(u_ref, bc_ref, pb_ref, cc_ref, kc_ref, eb_ref, ec_ref, tab_ref, z_ref,
                wb_ref, wcd_ref, ust_ref, bu_ref, yn_ref, *, t8, seg, nchunk):
    rc = t8 // nchunk

    @pl.when(pl.program_id(1) == 0)
    def _():
        bc = bc_ref[...]
        bc_swapped = pltpu.roll(bc, STATE, 1)
        for i in range(SSM_STEP):
            blk = pb_ref[i, :, 0:LANES] * bc + pb_ref[i, :, LANES:2 * LANES] * bc_swapped
            _expand_block_diag(wb_ref, LANES * i, blk, eb_ref, 4, 6)
        c_r, c_i = cc_ref[:, 0:LANES], cc_ref[:, LANES:2 * LANES]
        p_r, p_i = cc_ref[:, 2 * LANES:3 * LANES], cc_ref[:, 3 * LANES:4 * LANES]
        _expand_block_diag(wcd_ref, 0, c_r * p_r - c_i * p_i, ec_ref, 6, 4)
        _expand_block_diag(wcd_ref, 512, -(c_r * p_i + c_i * p_r), ec_ref, 6, 4)
        kc = kc_ref[...]
        lane = lax.broadcasted_iota(I32, kc.shape, 1)
        for ip in range(SSM_STEP):
            blk = kc if ip == 0 else jnp.where(lane >= GROUP * ip, pltpu.roll(kc, GROUP * ip, 1), 0.0)
            _expand_block_diag(wcd_ref, 1024 + LANES * ip, blk, ec_ref, 4, 4)

    for i in range(SSM_STEP):
        ust_ref[:, LANES * i:LANES * (i + 1)] = u_ref[pl.ds(i, t8, stride=SSM_STEP), :].astype(BF16)

    for c in range(nchunk):
        r = _dot(ust_ref[c * rc:(c + 1) * rc, :], wb_ref[...])
        for m in range(8):
            bu_ref[m, c * rc:(c + 1) * rc, :] = r[:, LANES * m:LANES * (m + 1)]

    a_re = [jnp.broadcast_to(tab_ref[0:1, LANES * m:LANES * (m + 1)], (SSM_SEGS, LANES)) for m in range(4)]
    a_im = [jnp.broadcast_to(tab_ref[1:2, LANES * m:LANES * (m + 1)], (SSM_SEGS, LANES)) for m in range(4)]
    s_re = [jnp.broadcast_to(tab_ref[2:3, LANES * m:LANES * (m + 1)], (SSM_SEGS, LANES)) for m in range(4)]
    s_im = [jnp.broadcast_to(tab_ref[3:4, LANES * m:LANES * (m + 1)], (SSM_SEGS, LANES)) for m in range(4)]

    def cmul_add(mr, mi, xr, xi, br, bi):
        return mr * xr - mi * xi + br, mr * xi + mi * xr + bi

    def load(k):
        return [bu_ref[m, pl.ds(k, SSM_SEGS, stride=seg), :] for m in range(8)]

    def advance(st, b):
        nr, ni = [], []
        for m in range(4):
            r_, i_ = cmul_add(a_re[m], a_im[m], st[m], st[4 + m], b[m], b[4 + m])
            nr.append(r_)
            ni.append(i_)
        return tuple(nr + ni)

    zeros = tuple(jnp.zeros((SSM_SEGS, LANES), F32) for _ in range(8))
    fin = lax.fori_loop(0, seg, lambda k, st: advance(st, load(k)), zeros)

    first = lax.broadcasted_iota(I32, (SSM_SEGS, LANES), 0) == 0

    def shift_down(t):
        return jnp.where(first, 0.0, pltpu.roll(t, 1, 0))

    tot = fin
    for _ in range(SSM_SEGS - 1):
        nxt_r, nxt_i = [], []
        for m in range(4):
            r_, i_ = cmul_add(s_re[m], s_im[m], shift_down(tot[m]), shift_down(tot[4 + m]), fin[m], fin[4 + m])
            nxt_r.append(r_)
            nxt_i.append(i_)
        tot = tuple(nxt_r + nxt_i)
    init = tuple(shift_down(t) for t in tot)

    def pass2(k, st):
        b = load(k)
        for m in range(8):
            bu_ref[m, pl.ds(k, SSM_SEGS, stride=seg), :] = st[m]
        return advance(st, b)

    lax.fori_loop(0, seg, pass2, init)

    for c in range(nchunk):
        xp = jnp.concatenate([bu_ref[m, c * rc:(c + 1) * rc, :] for m in range(8)], axis=1).astype(BF16)
        lhs = jnp.concatenate([xp, ust_ref[c * rc:(c + 1) * rc, :]], axis=1)
        y = _dot(lhs, wcd_ref[...])
        for i in range(SSM_STEP):
            yn_ref[pl.ds(SSM_STEP * c * rc + i, rc, stride=SSM_STEP), :] = y[:, LANES * i:LANES * (i + 1)]

    y = yn_ref[...] + tab_ref[4:5, 0:LANES] * u_ref[...]
    z_ref[...] = jax.nn.gelu(y).astype(BF16)


def _ssm(u, bc, pb, cc, kc, tab, *, bsz, lp, nchunk):
    np_ = u.shape[0]
    t8 = lp // SSM_STEP
    seg = t8 // SSM_SEGS
    nblk = D_MODEL // LANES
    col = jnp.arange(1024)
    src_b = (col >> 9) * STATE + (col & (STATE - 1))
    src_c = (col >> 7) * GROUP + (col & (GROUP - 1))
    eb = (jnp.arange(LANES)[:, None] == src_b[None, :]).astype(BF16)
    ec = (jnp.arange(LANES)[:, None] == src_c[None, :]).astype(BF16)
    kern = functools.partial(_ssm_kernel, t8=t8, seg=seg, nchunk=nchunk)
    per_q = lambda q, b: (q, 0, 0)
    fixed = lambda q, b: (0, 0)
    return pl.pallas_call(
        kern,
        out_shape=jax.ShapeDtypeStruct((np_, D_MODEL), BF16),
        grid=(nblk, bsz),
        in_specs=[
            pl.BlockSpec((lp, LANES), lambda q, b: (b, q)),
            pl.BlockSpec((None, LANES, LANES), per_q),
            pl.BlockSpec((SSM_STEP, None, LANES, 2 * LANES), lambda q, b: (0, q, 0, 0)),
            pl.BlockSpec((None, SSM_GBLK * STATE, 4 * LANES), per_q),
            pl.BlockSpec((None, LANES, LANES), per_q),
            pl.BlockSpec((LANES, 1024), fixed),
            pl.BlockSpec((LANES, 1024), fixed),
            pl.BlockSpec((None, 8, 512), per_q),
        ],
        out_specs=pl.BlockSpec((lp, LANES), lambda q, b: (b, q)),
        scratch_shapes=[
            pltpu.VMEM((1024, 1024), BF16),
            pltpu.VMEM((2048, 1024), BF16),
            pltpu.VMEM((t8, 1024), BF16),
            pltpu.VMEM((8, t8, LANES), F32),
            pltpu.VMEM((lp, LANES), F32),
        ],
        compiler_params=_cparams(("arbitrary", "arbitrary")),
        name="ssm",
    )(u, bc, pb, cc, kc, eb, ec, tab)


def _ssm_tables(lam_re, lam_im, log_dt, b_re, b_im, c_re, c_im, d, seg):
    hi = lax.Precision.HIGHEST
    lr, li = lam_re.astype(F32), lam_im.astype(F32)
    dt = jnp.exp(log_dt.astype(F32))[:, None]
    mag = jnp.exp(dt * lr)
    ar, ai = mag * jnp.cos(dt * li), mag * jnp.sin(dt * li)
    den = lr * lr + li * li
    nr, ni = ar - 1.0, ai
    fr, fi = (nr * lr + ni * li) / den, (ni * lr - nr * li) / den
    br, bi = b_re.astype(F32), b_im.astype(F32)
    bbr = fr[..., None] * br - fi[..., None] * bi
    bbi = fr[..., None] * bi + fi[..., None] * br

    def cmul(xr, xi, yr, yi):
        return xr * yr - xi * yi, xr * yi + xi * yr

    pr, pi = [jnp.ones_like(ar)], [jnp.zeros_like(ar)]
    for _ in range(SSM_STEP):
        r_, i_ = cmul(pr[-1], pi[-1], ar, ai)
        pr.append(r_)
        pi.append(i_)
    pw_r, pw_i = jnp.stack(pr), jnp.stack(pi)
    a8r, a8i = pw_r[SSM_STEP], pw_i[SSM_STEP]
    sr, si = jnp.ones_like(ar), jnp.zeros_like(ar)
    qr, qi = a8r, a8i
    e = seg
    while e:
        if e & 1:
            sr, si = cmul(sr, si, qr, qi)
        qr, qi = cmul(qr, qi, qr, qi)
        e >>= 1

    cr, ci = c_re.astype(F32), c_im.astype(F32)
    nb = N_GROUPS // SSM_GBLK

    bc = jnp.concatenate([jnp.swapaxes(bbr, 1, 2), jnp.swapaxes(bbi, 1, 2)], axis=2).reshape(nb, LANES, LANES)
    pb = jnp.stack([jnp.concatenate([pr[SSM_STEP - 1 - i]] * 2 + [-pi[SSM_STEP - 1 - i], pi[SSM_STEP - 1 - i]], axis=1)
                    for i in range(SSM_STEP)])
    pb = jnp.broadcast_to(pb[:, :, None, :], (SSM_STEP, N_GROUPS, GROUP, 2 * LANES))
    pb = pb.reshape(SSM_STEP, nb, LANES, 2 * LANES)
    crt = jnp.tile(jnp.swapaxes(cr, 1, 2), (1, 1, SSM_STEP))
    cit = jnp.tile(jnp.swapaxes(ci, 1, 2), (1, 1, SSM_STEP))
    prc = jnp.repeat(jnp.stack(pr[1:], axis=2), GROUP, axis=2)
    pic = jnp.repeat(jnp.stack(pi[1:], axis=2), GROUP, axis=2)
    cc = jnp.concatenate([crt, cit, prc, pic], axis=2).reshape(nb, SSM_GBLK * STATE, 4 * LANES)
    tr, ti = cmul(pw_r[:SSM_STEP, :, :, None], pw_i[:SSM_STEP, :, :, None], bbr[None], bbi[None])
    kt = (jnp.einsum('ghp,tgpk->gkth', cr, tr, precision=hi)
          - jnp.einsum('ghp,tgpk->gkth', ci, ti, precision=hi))
    kc = kt.reshape(nb, LANES, LANES)

    def lanes(t):
        return t.reshape(nb, SSM_GBLK * STATE)

    dl = jnp.pad(d.astype(F32).reshape(nb, SSM_GBLK * GROUP), ((0, 0), (0, 512 - LANES)))
    zero = jnp.zeros((nb, 512), F32)
    tab = jnp.stack([lanes(a8r), lanes(a8i), lanes(sr), lanes(si), dl, zero, zero, zero], axis=1)
    return bc, pb, cc, kc, tab


def _attn_kernel(q_ref, kc_ref, kp_ref, km_ref, vc_ref, vp_ref, vm_ref, sink_ref, o_ref):
    n = pl.program_id(1)
    nk = 3 * BLK
    row = lax.broadcasted_iota(I32, (BLK, nk), 0)
    col = lax.broadcasted_iota(I32, (BLK, nk), 1)
    band = (col > row) & (col <= row + BLK) & (col >= 2 * BLK - BLK * n)
    mrow = col - 2 * BLK
    meta = (mrow >= PAD) & (mrow <= BLK * n + row)
    bias = jnp.where(band | meta, 0.0, NEG_INF)

    for g in range(N_KV_HEADS):
        qg = jnp.concatenate([q_ref[:, LANES * (4 * g + jj):LANES * (4 * g + jj + 1)] for jj in range(4)], axis=0)
        acc = None
        for par in range(2):
            sl = slice(LANES * (2 * g + par), LANES * (2 * g + par + 1))
            kk = jnp.concatenate([kp_ref[:, sl], kc_ref[:, sl], km_ref[:, sl]], axis=0)
            vv = jnp.concatenate([vp_ref[:, sl], vc_ref[:, sl], vm_ref[:, sl]], axis=0)
            s_all = _dot_nt(qg, kk)
            probs, rden = [], []
            for jj in range(4):
                sink = sink_ref[8 * g + 2 * jj + par]
                s = s_all[BLK * jj:BLK * (jj + 1), :] + bias
                mx = jnp.maximum(jnp.max(s, axis=1, keepdims=True), sink)
                p = jnp.exp(s - mx)
                den = jnp.sum(p, axis=1, keepdims=True) + jnp.exp(sink - mx)
                probs.append(p.astype(BF16))
                rden.append(jnp.broadcast_to(1.0 / den, (BLK, LANES)))
            o = _dot(jnp.concatenate(probs, axis=0), vv) * jnp.concatenate(rden, axis=0)
            acc = o if acc is None else acc + o
        for jj in range(4):
            o_ref[:, LANES * (4 * g + jj):LANES * (4 * g + jj + 1)] = acc[BLK * jj:BLK * (jj + 1), :].astype(BF16)


def _attention(q, kx, vx, sinks, *, bsz, nb):
    np_ = q.shape[0]
    cur = lambda b, n: (b * nb + n, 0)
    prev = lambda b, n: (b * nb + jnp.maximum(n - 1, 0), 0)
    first = lambda b, n: (b * nb, 0)
    return pl.pallas_call(
        _attn_kernel,
        out_shape=jax.ShapeDtypeStruct((np_, 1024), BF16),
        grid=(bsz, nb),
        in_specs=[
            pl.BlockSpec((BLK, 1024), cur),
            pl.BlockSpec((BLK, 512), cur),
            pl.BlockSpec((BLK, 512), prev),
            pl.BlockSpec((BLK, 512), first),
            pl.BlockSpec((BLK, 512), cur),
            pl.BlockSpec((BLK, 512), prev),
            pl.BlockSpec((BLK, 512), first),
            pl.BlockSpec(memory_space=pltpu.SMEM),
        ],
        out_specs=pl.BlockSpec((BLK, 1024), cur),
        compiler_params=_cparams(("arbitrary", "arbitrary")),
        name="attn",
    )(q, kx, kx, kx, vx, vx, vx, sinks)


def _post_kernel(*refs, tiles_per_seq, per):
    x_refs, (meta_ref, z_ref, at_ref, sgs_ref, sga_ref, wglu_ref, bglu_ref, wbs_ref, wba_ref, wout_ref,
             nffn_ref, wrh_ref, wrl_ref, br_ref,
             h2_ref, hnp_ref, lslot_ref, ecol_ref, cnt_ref) = refs[:per], refs[per:]
    i = pl.program_id(0)

    z = z_ref[...]
    t = _dot(z, wglu_ref[...]) + bglu_ref[...]
    so = (z.astype(F32) * jax.nn.sigmoid(t)).astype(BF16)
    mix = (sgs_ref[...].astype(F32) * _dot(so, wbs_ref[...])
           + sga_ref[...].astype(F32) * _dot(at_ref[...], wba_ref[...]))
    mixed = _dot(mix.astype(BF16), wout_ref[...])
    logit_blocks = []
    for r, xr in enumerate(_padded_rows(x_refs, meta_ref, i % tiles_per_seq == 0)):
        rows = slice(BLK * r, BLK * (r + 1))
        h2 = xr + mixed[rows, :]
        h2_ref[rows, :] = h2
        ms = jnp.mean(h2 * h2, axis=-1, keepdims=True)
        hn = h2 * lax.rsqrt(ms + RMS_EPS) * nffn_ref[...]
        hb = hn.astype(BF16)
        hnp_ref[rows, :] = hb
        lo = (hn - hb.astype(F32)).astype(BF16)
        logit_blocks.append(_dot(hb, wrh_ref[...]) + _dot(lo, wrh_ref[...]) + _dot(hb, wrl_ref[...]) + br_ref[...])
    logits = jnp.concatenate(logit_blocks, axis=0)

    tm = logits.shape[0]
    lane = lax.broadcasted_iota(I32, (tm, LANES), 1)
    lane_f = lane.astype(F32)
    work = logits
    vals, hots, idxs = [], [], []
    for _ in range(TOP_K):
        mx = jnp.max(work, axis=1, keepdims=True)
        idx = jnp.min(jnp.where(work == mx, lane_f, float(LANES)), axis=1, keepdims=True)
        hot = lane_f == idx
        vals.append(mx)
        idxs.append(idx)
        hots.append(hot)
        work = jnp.where(hot, -jnp.inf, work)

    ex = [jnp.exp(v - vals[0]) for v in vals]
    tot = ex[0] + ex[1] + ex[2] + ex[3]
    gates = [e / tot for e in ex]

    onehot = jnp.zeros((tm, LANES), F32)
    for hot in hots:
        onehot = onehot + jnp.where(hot, 1.0, 0.0)
    r_i = lax.broadcasted_iota(I32, (tm, tm), 0)
    c_i = lax.broadcasted_iota(I32, (tm, tm), 1)
    tri = jnp.where(r_i > c_i, 1.0, 0.0).astype(BF16)
    before = _dot(tri, onehot.astype(BF16))
    count = jnp.sum(onehot, axis=0, keepdims=True)
    gran = jnp.floor((count + (GRAN - 1)) * (1.0 / GRAN))
    e_r = lax.broadcasted_iota(I32, (LANES, LANES), 0)
    e_c = lax.broadcasted_iota(I32, (LANES, LANES), 1)
    upper = jnp.where(e_r < e_c, 1.0, 0.0).astype(BF16)
    gstart = _dot(jnp.broadcast_to(gran, (8, LANES)).astype(BF16), upper)[0:1, :]
    base = before + gstart * float(GRAN)
    slots = [jnp.sum(jnp.where(hot, base, 0.0), axis=1, keepdims=True) for hot in hots]
    cnt_ref[...] = jnp.broadcast_to(gran, cnt_ref.shape)

    def spread(cols, dtype):
        out = jnp.zeros((tm, LANES), dtype)
        for k, cval in enumerate(cols):
            out = jnp.where(lane == k, cval, out)
        return out

    lslot_ref[...] = spread(slots, F32)
    cols = []
    for g in gates:
        g1 = g.astype(BF16).astype(F32)
        g2 = (g - g1).astype(BF16).astype(F32)
        cols += [g1, g2, g - g1 - g2]
    for s in slots:
        hi = jnp.floor(s * (1.0 / 64.0))
        cols += [hi, s - 64.0 * hi]
    ecol_ref[...] = spread(cols, F32)


def _post(x, meta, z, attn, sgs, sga, wglu, bglu, wbs, wba, wout, nffn, wrh, wrl, br, *, tm, lp):
    np_ = z.shape[0]
    tiles_per_seq = lp // tm
    per = tm // BLK
    row = lambda i: (i, 0)
    fixed = lambda i: (0, 0)
    sq = pl.BlockSpec((1024, 1024), fixed)
    vec = pl.BlockSpec((1, 1024), fixed)
    out_shape = (
        jax.ShapeDtypeStruct((np_, 1024), F32),
        jax.ShapeDtypeStruct((np_, 1024), BF16),
        jax.ShapeDtypeStruct((np_, LANES), F32),
        jax.ShapeDtypeStruct((np_, LANES), F32),
        jax.ShapeDtypeStruct((np_ // tm * 8, LANES), F32),
    )
    return pl.pallas_call(
        functools.partial(_post_kernel, tiles_per_seq=tiles_per_seq, per=per),
        out_shape=out_shape,
        grid=(np_ // tm,),
        in_specs=_padded_row_specs(tm, tiles_per_seq) + [
            pl.BlockSpec((N_META, D_MODEL), fixed),
            pl.BlockSpec((tm, 1024), row), pl.BlockSpec((tm, 1024), row),
            pl.BlockSpec((tm, 1024), row), pl.BlockSpec((tm, 1024), row),
            sq, vec, sq, sq, sq, vec,
            pl.BlockSpec((1024, LANES), fixed), pl.BlockSpec((1024, LANES), fixed),
            pl.BlockSpec((1, LANES), fixed),
        ],
        out_specs=(
            pl.BlockSpec((tm, 1024), row), pl.BlockSpec((tm, 1024), row),
            pl.BlockSpec((tm, LANES), row), pl.BlockSpec((tm, LANES), row),
            pl.BlockSpec((8, LANES), row),
        ),
        compiler_params=_cparams(("arbitrary",)),
        name="post",
    )(*([x] * per), meta, z, attn, sgs, sga, wglu, bglu, wbs, wba, wout, nffn, wrh, wrl, br)


def _sort_kernel(hnp_ref, ecol_ref, xloc_ref, *, tm, srows):
    ecol = ecol_ref[...].astype(BF16)
    pay = jnp.concatenate([hnp_ref[...], ecol], axis=1)

    r8 = lax.broadcasted_iota(I32, (8, LANES), 0)
    l8 = lax.broadcasted_iota(I32, (8, LANES), 1)
    digit_w = jnp.where(l8 == 12 + 2 * r8, 64.0, jnp.where(l8 == 13 + 2 * r8, 1.0, 0.0)).astype(BF16)
    slot_rows = _dot_nt(digit_w, ecol)

    rr = lax.broadcasted_iota(I32, (LANES, LANES), 0)
    cc = lax.broadcasted_iota(I32, (LANES, LANES), 1)
    kk = jnp.where(cc < 3 * TOP_K, cc // 3, -LANES)
    spread_w = jnp.where(rr == 12 + 2 * kk, 64.0, jnp.where(rr == 13 + 2 * kk, 1.0, 0.0)).astype(BF16)
    gate_lane = lax.broadcasted_iota(I32, (SORT_CHUNK, LANES), 1) < 3 * TOP_K

    for c in range(srows // SORT_CHUNK):
        s_id = (lax.broadcasted_iota(I32, (SORT_CHUNK, tm), 0) + c * SORT_CHUNK).astype(F32)
        perm = jnp.zeros((SORT_CHUNK, tm), F32)
        for k in range(TOP_K):
            perm = perm + jnp.where(slot_rows[k:k + 1, :] == s_id, 1.0, 0.0)
        got = _dot(perm.astype(BF16), pay)
        rows = slice(c * SORT_CHUNK, (c + 1) * SORT_CHUNK)
        xloc_ref[rows, 0:512] = _pack_bf16_pairs(got[:, :D_MODEL], exact=True)
        extra = got[:, D_MODEL:]
        owner = _dot(extra.astype(BF16), spread_w)
        me = (lax.broadcasted_iota(I32, (SORT_CHUNK, LANES), 0) + c * SORT_CHUNK).astype(F32)
        mine = jnp.where(gate_lane, owner, -1.0) == me
        gate = jnp.sum(jnp.where(mine, extra, 0.0), axis=1, keepdims=True)
        xloc_ref[rows, 512:SORT_W] = pltpu.bitcast(jnp.broadcast_to(gate, (SORT_CHUNK, LANES)), U32)


def _sort(hnp, ecol, *, tm, srows):
    np_ = hnp.shape[0]
    ntile = np_ // tm
    row = lambda i: (i, 0)
    return pl.pallas_call(
        functools.partial(_sort_kernel, tm=tm, srows=srows),
        out_shape=jax.ShapeDtypeStruct((ntile * srows, SORT_W), U32),
        grid=(ntile,),
        in_specs=[pl.BlockSpec((tm, D_MODEL), row), pl.BlockSpec((tm, LANES), row)],
        out_specs=pl.BlockSpec((srows, SORT_W), row),
        compiler_params=_cparams(("arbitrary",)),
        name="sort",
    )(hnp, ecol)


def _moe_kernel(blk_ref, exp_ref, lo_ref, hi_ref, kind_ref, wslot_ref, nexte_ref, gsrc_ref,
                xloc_ref, wgu_ref, bgu_ref, wd_ref, bd_ref, yloc_ref,
                xbuf, ybuf, sem_in, sem_out, wgu_f, wd_f, sem_w, wgu_bf, wd_bf, *, nseg, dump0):
    i = pl.program_id(0)
    prev = jnp.maximum(i - 1, 0)
    nxt = jnp.minimum(i + 1, nseg - 1)
    slot = i % 2
    new_expert = (i == 0) | (exp_ref[i] != exp_ref[prev])

    def gather(seg, dst_slot, start):
        b = blk_ref[seg]
        for g in range(MOE_GRANS):
            src = pl.multiple_of(gsrc_ref[b * MOE_GRANS + g] * GRAN, GRAN)
            cp = pltpu.make_async_copy(xloc_ref.at[pl.ds(src, GRAN)], xbuf.at[dst_slot, pl.ds(g * GRAN, GRAN)],
                                       sem_in.at[dst_slot])
            if start:
                cp.start()
            else:
                cp.wait()

    def scatter(seg, src_slot, start):
        b = blk_ref[seg]
        lo = lo_ref[seg]
        hi = hi_ref[seg]
        for g in range(MOE_GRANS):
            keep = (g >= lo) & (g < hi)
            spare = dump0 + src_slot * MOE_GRANS + g
            dst = pl.multiple_of(jnp.where(keep, gsrc_ref[b * MOE_GRANS + g], spare) * GRAN, GRAN)
            cp = pltpu.make_async_copy(ybuf.at[src_slot, pl.ds(g * GRAN, GRAN)], yloc_ref.at[pl.ds(dst, GRAN)],
                                       sem_out.at[src_slot])
            if start:
                cp.start()
            else:
                cp.wait()

    @pl.when(i == 0)
    def _():
        gather(i, slot, True)
        ybuf[1 - slot] = jnp.zeros((MOE_BLK, 512), U32)
        for g in range(MOE_GRANS):
            pltpu.make_async_copy(ybuf.at[1 - slot, pl.ds(g * GRAN, GRAN)],
                                  yloc_ref.at[pl.ds((dump0 + (1 - slot) * MOE_GRANS + g) * GRAN, GRAN)],
                                  sem_out.at[1 - slot]).start()

    gather(i, slot, False)

    def fetch(e, slot, start):
        for src, dst in ((wgu_ref, wgu_f), (wd_ref, wd_f)):
            cp = pltpu.make_async_copy(src.at[e], dst.at[slot], sem_w.at[slot])
            if start:
                cp.start()
            else:
                cp.wait()

    wslot = wslot_ref[i]

    @pl.when(i == 0)
    def _():
        fetch(exp_ref[i], wslot, True)

    @pl.when(new_expert)
    def _():
        fetch(exp_ref[i], wslot, False)
        wgu_bf[...] = wgu_f[wslot].astype(BF16)
        wd_bf[...] = wd_f[wslot].astype(BF16)

    @pl.when(new_expert & (nexte_ref[i] >= 0))
    def _():
        fetch(nexte_ref[i], 1 - wslot, True)

    kind = kind_ref[i]

    @pl.when(kind == 1)
    def _():
        gather(nxt, 1 - slot, True)
        xw = xbuf[slot]
        xb = _unpack_bf16_pairs(xw[:, 0:512]).astype(BF16)
        gate = pltpu.bitcast(xw[:, 512:SORT_W], F32)
        gu = _dot(xb, wgu_bf[...]) + bgu_ref[...]
        g = jnp.minimum(gu[:, :D_FF], SWIGLU_LIMIT)
        up = jnp.clip(gu[:, D_FF:], -SWIGLU_LIMIT, SWIGLU_LIMIT)
        hid = g * jax.nn.sigmoid(SWIGLU_ALPHA * g) * (up + 1.0)
        y = _dot(hid.astype(BF16), wd_bf[...]) + bd_ref[...]
        y = y * jnp.concatenate([gate] * (D_MODEL // LANES), axis=1)
        ybuf[slot] = _pack_bf16_pairs(y)
        scatter(i, slot, True)
        scatter(prev, 1 - slot, False)

    @pl.when(kind != 1)
    def _():
        gather(nxt, 1 - slot, True)
        ybuf[slot] = jnp.zeros((MOE_BLK, 512), U32)
        scatter(i, slot, True)
        scatter(prev, 1 - slot, False)

    @pl.when(i == nseg - 1)
    def _():
        scatter(i, slot, False)
        gather(nxt, 1 - slot, False)


def _moe(blk, exp, lo, hi, kind, gsrc, xloc, wgu, bgu, wd, bd, *, ncap):
    nseg = blk.shape[0]
    dump0 = ncap
    change = jnp.concatenate([jnp.zeros((1,), I32), (exp[1:] != exp[:-1]).astype(I32)])
    wslot = jnp.cumsum(change) % 2
    later = jnp.where(exp[None, :] > exp[:, None], exp[None, :], N_EXPERTS)
    nexte = jnp.min(later, axis=1)
    nexte = jnp.where(nexte >= N_EXPERTS, -1, nexte)
    wmap = lambda i, b, e, *_: (e[i], 0, 0)
    grid_spec = pltpu.PrefetchScalarGridSpec(
        num_scalar_prefetch=8,
        grid=(nseg,),
        in_specs=[
            pl.BlockSpec(memory_space=pl.ANY),
            pl.BlockSpec(memory_space=pl.ANY),
            pl.BlockSpec((None, 1, 2 * D_FF), wmap),
            pl.BlockSpec(memory_space=pl.ANY),
            pl.BlockSpec((None, 1, D_MODEL), wmap),
        ],
        out_specs=pl.BlockSpec(memory_space=pl.ANY),
        scratch_shapes=[
            pltpu.VMEM((2, MOE_BLK, SORT_W), U32),
            pltpu.VMEM((2, MOE_BLK, 512), U32),
            pltpu.SemaphoreType.DMA((2,)),
            pltpu.SemaphoreType.DMA((2,)),
            pltpu.VMEM((2, D_MODEL, 2 * D_FF), F32),
            pltpu.VMEM((2, D_FF, D_MODEL), F32),
            pltpu.SemaphoreType.DMA((2,)),
            pltpu.VMEM((D_MODEL, 2 * D_FF), BF16),
            pltpu.VMEM((D_FF, D_MODEL), BF16),
        ],
    )
    return pl.pallas_call(
        functools.partial(_moe_kernel, nseg=nseg, dump0=dump0),
        out_shape=jax.ShapeDtypeStruct(((ncap + 2 * MOE_GRANS) * GRAN, 512), U32),
        grid_spec=grid_spec,
        compiler_params=_cparams(("arbitrary",)),
        name="moe",
    )(blk, exp, lo, hi, kind, wslot.astype(I32), nexte.astype(I32), gsrc, xloc, wgu, bgu, wd, bd)


def _combine_kernel(y_ref, lslot_ref, h2_ref, nfin_ref, o_ref, obuf, sem, *, tm, srows, tiles_per_seq):
    i = pl.program_id(0)
    b = i // tiles_per_seq
    j = i % tiles_per_seq
    per = tm // BLK
    slots = lslot_ref[...]
    acc = h2_ref[...]
    for c in range(srows // SORT_CHUNK):
        s_id = (lax.broadcasted_iota(I32, (tm, SORT_CHUNK), 1) + c * SORT_CHUNK).astype(F32)
        take = jnp.zeros((tm, SORT_CHUNK), F32)
        for k in range(TOP_K):
            take = take + jnp.where(slots[:, k:k + 1] == s_id, 1.0, 0.0)
        rows = _unpack_bf16_pairs(y_ref[c * SORT_CHUNK:(c + 1) * SORT_CHUNK, :]).astype(BF16)
        acc = acc + _dot(take.astype(BF16), rows)
    ms = jnp.mean(acc * acc, axis=-1, keepdims=True)
    obuf[...] = acc * lax.rsqrt(ms + RMS_EPS) * nfin_ref[...]

    def copies(start):
        for r in range(per):
            n = j * per + r
            cp = pltpu.make_async_copy(obuf.at[pl.ds(r * BLK, BLK)],
                                       o_ref.at[b, pl.ds(pl.multiple_of(jnp.maximum(n - 1, 0) * BLK, BLK), BLK)], sem)

            def go():
                if start:
                    cp.start()
                else:
                    cp.wait()

            if r == 0:
                pl.when(j > 0)(go)
            else:
                go()

    copies(True)
    copies(False)


def _combine(yloc, lslot, h2, nfin, *, bsz, seq, tm, lp, srows):
    np_ = h2.shape[0]
    tiles_per_seq = lp // tm
    row = lambda i: (i, 0)
    return pl.pallas_call(
        functools.partial(_combine_kernel, tm=tm, srows=srows, tiles_per_seq=tiles_per_seq),
        out_shape=jax.ShapeDtypeStruct((bsz, seq, D_MODEL), F32),
        grid=(np_ // tm,),
        in_specs=[
            pl.BlockSpec((srows, 512), row),
            pl.BlockSpec((tm, LANES), row),
            pl.BlockSpec((tm, D_MODEL), row),
            pl.BlockSpec((1, D_MODEL), lambda i: (0, 0)),
        ],
        out_specs=pl.BlockSpec(memory_space=pl.ANY),
        scratch_shapes=[pltpu.VMEM((tm, D_MODEL), F32), pltpu.SemaphoreType.DMA],
        compiler_params=_cparams(("arbitrary",)),
        name="combine",
    )(yloc, lslot, h2, nfin)


def _rope_tables(lp):
    pos = np.arange(lp, dtype=np.float32) - np.float32(PAD)
    inv_freq = np.float32(ROPE_THETA) ** (-np.arange(0, 2 * ROT_HALF, 2, dtype=np.float32) / np.float32(2 * ROT_HALF))
    ang = (pos[:, None] * inv_freq[None, :]).astype(np.float32)
    cos, sin = np.cos(ang), np.sin(ang)
    ones = np.ones((lp, HEAD_DIM - 2 * ROT_HALF), np.float32)
    zeros8 = np.zeros((lp, ROT_HALF), np.float32)
    zrest = np.zeros((lp, HEAD_DIM - 2 * ROT_HALF), np.float32)
    c = np.concatenate([cos, cos, ones], axis=1)
    s1 = np.concatenate([zeros8, sin, zrest], axis=1)
    s2 = np.concatenate([-sin, zeros8, zrest], axis=1)
    tile = lambda t: jnp.asarray(np.concatenate([t, t], axis=1), F32)
    return tile(c), tile(s1), tile(s2)


def _route_tables(grans, gt):
    ntile = grans.shape[0]
    ncap = ntile * gt
    used = jnp.sum(grans, axis=1)
    lstart = jnp.cumsum(grans, axis=1) - grans
    tile0 = jnp.arange(ntile, dtype=I32)[:, None] * gt
    run_len = jnp.concatenate([grans.T.reshape(-1), gt - used])
    run_src = jnp.concatenate([(tile0 + lstart).T.reshape(-1), tile0[:, 0] + used])
    run_end = jnp.cumsum(run_len)
    run_start = run_end - run_len
    off = run_src - run_start
    delta = off - jnp.concatenate([jnp.zeros((1,), I32), off[:-1]])
    pos = jnp.arange(ncap, dtype=I32)
    gsrc = pos + jnp.sum(jnp.where(run_start[None, :] <= pos[:, None], delta[None, :], 0), axis=1)

    estart = run_start[::ntile][:N_EXPERTS + 1]
    eend = jnp.concatenate([estart[1:], jnp.array([ncap], I32)])
    nblocks = ncap // MOE_GRANS
    vals = jnp.concatenate([jnp.arange(nblocks, dtype=I32) * MOE_GRANS, estart[1:]])
    n = vals.shape[0]
    idx = jnp.arange(n, dtype=I32)
    before = (vals[None, :] < vals[:, None]) | ((vals[None, :] == vals[:, None]) & (idx[None, :] < idx[:, None]))
    place = jnp.sum(before.astype(I32), axis=1)
    cuts = jnp.sum(jnp.where(place[None, :] == idx[:, None], vals[None, :], 0), axis=1)
    nxt = jnp.concatenate([cuts[1:], jnp.array([ncap], I32)])
    blk = jnp.minimum(cuts // MOE_GRANS, nblocks - 1)
    owner = jnp.sum((eend[None, :] <= cuts[:, None]).astype(I32), axis=1)
    lo = cuts - blk * MOE_GRANS
    hi = lo + (nxt - cuts)
    live = hi > lo
    kind = jnp.where(live, jnp.where(owner >= N_EXPERTS, 2, 1), 0)
    works = kind == 1
    first_work = jnp.min(jnp.where(works, owner, N_EXPERTS - 1))
    exp = jnp.maximum(lax.cummax(jnp.where(works, owner, -1), axis=0), first_work)
    return gsrc.astype(I32), blk.astype(I32), exp.astype(I32), lo.astype(I32), hi.astype(I32), kind.astype(I32)


def kernel(x, meta_tokens, norm_mix, w_in, ssm_lam_re, ssm_lam_im, ssm_log_dt, ssm_b_re, ssm_b_im,
           ssm_c_re, ssm_c_im, ssm_d, w_glu, b_glu, attn_sinks, w_br_ssm, w_br_attn, w_out,
           norm_ffn, w_router, b_router, w_gate_up, b_gate_up, w_down, b_down, norm_final):
    bsz, seq, _ = x.shape
    lp = seq + BLK
    nb = lp // BLK
    np_ = bsz * lp
    tm = 640 if lp % 640 == 0 else BLK
    t8 = lp // SSM_STEP
    seg = t8 // SSM_SEGS
    nchunk = 5 if (t8 % 5 == 0 and (t8 // 5) % 16 == 0) else 1

    meta = meta_tokens.astype(F32)
    rc, rs1, rs2 = _rope_tables(lp)
    u, q, kx, vx, sgs, sga = _inproj(x, meta, norm_mix[0][None], w_in[0].astype(BF16), rc, rs1, rs2, tm=tm, lp=lp)

    bc, pb, cc, kc, tab = _ssm_tables(ssm_lam_re[0], ssm_lam_im[0], ssm_log_dt[0], ssm_b_re[0], ssm_b_im[0],
                                      ssm_c_re[0], ssm_c_im[0], ssm_d[0], seg)
    z = _ssm(u, bc, pb, cc, kc, tab, bsz=bsz, lp=lp, nchunk=nchunk)

    attn = _attention(q, kx, vx, attn_sinks[0].astype(F32), bsz=bsz, nb=nb)

    wr = jnp.pad(w_router[0].astype(F32), ((0, 0), (0, LANES - N_EXPERTS)))
    wrh = wr.astype(BF16)
    wrl = (wr - wrh.astype(F32)).astype(BF16)
    br = jnp.concatenate([b_router[0].astype(F32), jnp.full((LANES - N_EXPERTS,), NEG_INF, F32)])[None]
    h2, hnp, lslot, ecol, cnt = _post(
        x, meta, z, attn, sgs, sga, w_glu[0].astype(BF16), b_glu[0][None], w_br_ssm[0].astype(BF16),
        w_br_attn[0].astype(BF16), w_out[0].astype(BF16), norm_ffn[0][None], wrh, wrl, br, tm=tm, lp=lp)

    srows = -(-(tm * TOP_K + N_EXPERTS * (GRAN - 1)) // SORT_CHUNK) * SORT_CHUNK
    gt = srows // GRAN
    ntile = np_ // tm
    grans = cnt.reshape(ntile, 8, LANES)[:, 0, :N_EXPERTS].astype(I32)
    gsrc, sblk, sexp, slo, shi, skind = _route_tables(grans, gt)

    xloc = _sort(hnp, ecol, tm=tm, srows=srows)
    yloc = _moe(sblk, sexp, slo, shi, skind, gsrc, xloc, w_gate_up[0], b_gate_up[0][:, None, :],
                w_down[0], b_down[0][:, None, :], ncap=ntile * gt)
    return _combine(yloc, lslot, h2, norm_final[None], bsz=bsz, seq=seq, tm=tm, lp=lp, srows=srows)
```

```python
import functools
import math

import jax
import jax.numpy as jnp
import numpy as np
from jax import lax
from jax.experimental import pallas as pl
from jax.experimental.pallas import tpu as pltpu

F32 = jnp.float32
BF16 = jnp.bfloat16
I32 = jnp.int32
U32 = jnp.uint32

D_MODEL = 1024
N_META = 16
BLK = 128
PAD = BLK - N_META
N_GROUPS = 64
GROUP = 16
STATE = 64
N_Q_HEADS = 16
N_KV_HEADS = 2
HEAD_DIM = 64
ROT_HALF = 8
ROPE_THETA = 500000.0
N_EXPERTS = 32
TOP_K = 4
D_FF = 1024
SWIGLU_LIMIT = 7.0
SWIGLU_ALPHA = 1.702
RMS_EPS = 1e-5
NEG_INF = -1e30
SSM_STEP = 8
SSM_SEGS = 8
SSM_GBLK = 8
LANES = 128
MOE_BLK = 256
GRAN = 8
MOE_GRANS = MOE_BLK // GRAN
SORT_CHUNK = 256
SORT_W = 512 + LANES
VMEM_LIMIT = 56 * 1024 * 1024


def _cparams(sem):
    return pltpu.CompilerParams(dimension_semantics=sem, vmem_limit_bytes=VMEM_LIMIT)


def _dot(a, b):
    return jnp.dot(a, b, preferred_element_type=F32)


def _dot_nt(a, b):
    return lax.dot_general(a, b, (((1,), (1,)), ((), ())), preferred_element_type=F32)


def _pack_bf16_pairs(x, exact=False):
    w = x.shape[1] // 2
    bits = pltpu.bitcast(x if exact else x.astype(BF16).astype(F32), U32)
    return (bits[:, w:] & jnp.uint32(0xFFFF0000)) | (bits[:, :w] >> 16)


def _unpack_bf16_pairs(wd):
    lo = pltpu.bitcast(wd << 16, F32)
    hi = pltpu.bitcast(wd & jnp.uint32(0xFFFF0000), F32)
    return jnp.concatenate([lo, hi], axis=1)


def _padded_rows(x_refs, meta_ref, first_tile):
    head = jnp.concatenate([jnp.zeros((PAD, D_MODEL), F32), meta_ref[...]], axis=0)
    first = jnp.where(first_tile, head, x_refs[0][...])
    return [first] + [r[...] for r in x_refs[1:]]


def _padded_row_specs(tm, tiles_per_seq):
    per = tm // BLK

    def spec(r):
        return pl.BlockSpec((None, BLK, D_MODEL),
                            lambda i: (i // tiles_per_seq, jnp.maximum((i % tiles_per_seq) * per + r - 1, 0), 0))

    return [spec(r) for r in range(per)]


def _inproj_kernel(*refs, tiles_per_seq, per):
    x_refs, (meta_ref, nrm_ref, w_ref, rc_ref, rs1_ref, rs2_ref,
             u_ref, q_ref, kx_ref, vx_ref, sgs_ref, sga_ref) = refs[:per], refs[per:]
    normed = []
    for x in _padded_rows(x_refs, meta_ref, pl.program_id(0) % tiles_per_seq == 0):
        ms = jnp.mean(x * x, axis=-1, keepdims=True)
        normed.append((x * lax.rsqrt(ms + RMS_EPS) * nrm_ref[...]).astype(BF16))
    hn = jnp.concatenate(normed, axis=0)

    def proj(lo, hi):
        return _dot(hn, w_ref[:, lo:hi])

    u_ref[...] = proj(0, 1024)

    rc = rc_ref[...]
    rs1 = rs1_ref[...]
    rs2 = rs2_ref[...]

    def rope(blk):
        return blk * rc + pltpu.roll(blk, ROT_HALF, 1) * rs1 + pltpu.roll(blk, LANES - ROT_HALF, 1) * rs2

    qf = proj(1024, 2048)
    scale = 1.0 / math.sqrt(HEAD_DIM)
    for j in range(8):
        q_ref[:, LANES * j:LANES * (j + 1)] = (rope(qf[:, LANES * j:LANES * (j + 1)]) * scale).astype(BF16)

    kvf = proj(2048, 2304)
    k = rope(kvf[:, :LANES])
    v = kvf[:, LANES:]
    low = lax.broadcasted_iota(I32, k.shape, 1) < HEAD_DIM

    def expand(t, ref):
        tr = pltpu.roll(t, HEAD_DIM, 1)
        zero = jnp.zeros_like(t)
        ref[:, 0:128] = jnp.where(low, t, zero).astype(BF16)
        ref[:, 128:256] = jnp.where(low, zero, tr).astype(BF16)
        ref[:, 256:384] = jnp.where(low, tr, zero).astype(BF16)
        ref[:, 384:512] = jnp.where(low, zero, t).astype(BF16)

    expand(k, kx_ref)
    expand(v, vx_ref)
    sgs_ref[...] = jax.nn.sigmoid(proj(2304, 3328)).astype(BF16)
    sga_ref[...] = jax.nn.sigmoid(proj(3328, 4352)).astype(BF16)


def _inproj(x, meta, nrm, w_bf, rc, rs1, rs2, *, tm, lp):
    np_ = x.shape[0] * lp
    tiles_per_seq = lp // tm
    per = tm // BLK
    row = lambda i: (i, 0)
    fixed = lambda i: (0, 0)
    rope_idx = lambda i: (i % tiles_per_seq, 0)
    out_shape = (
        jax.ShapeDtypeStruct((np_, 1024), F32),
        jax.ShapeDtypeStruct((np_, 1024), BF16),
        jax.ShapeDtypeStruct((np_, 512), BF16),
        jax.ShapeDtypeStruct((np_, 512), BF16),
        jax.ShapeDtypeStruct((np_, 1024), BF16),
        jax.ShapeDtypeStruct((np_, 1024), BF16),
    )
    return pl.pallas_call(
        functools.partial(_inproj_kernel, tiles_per_seq=tiles_per_seq, per=per),
        out_shape=out_shape,
        grid=(np_ // tm,),
        in_specs=_padded_row_specs(tm, tiles_per_seq) + [
            pl.BlockSpec((N_META, D_MODEL), fixed),
            pl.BlockSpec((1, D_MODEL), fixed),
            pl.BlockSpec(w_bf.shape, fixed),
            pl.BlockSpec((tm, LANES), rope_idx),
            pl.BlockSpec((tm, LANES), rope_idx),
            pl.BlockSpec((tm, LANES), rope_idx),
        ],
        out_specs=(
            pl.BlockSpec((tm, 1024), row),
            pl.BlockSpec((tm, 1024), row),
            pl.BlockSpec((tm, 512), row),
            pl.BlockSpec((tm, 512), row),
            pl.BlockSpec((tm, 1024), row),
            pl.BlockSpec((tm, 1024), row),
        ),
        compiler_params=_cparams(("arbitrary",)),
        name="inproj",
    )(*([x] * per), meta, nrm, w_bf, rc, rs1, rs2)


def _expand_block_diag(dst_ref, row0, t, e_ref, row_shift, col_shift):
    n = t.shape[0]
    step = min(n, 256)
    for r0 in range(0, n, step):
        full = _dot(t[r0:r0 + step, :].astype(BF16), e_ref[...])
        rg = (lax.broadcasted_iota(I32, full.shape, 0) + r0) >> row_shift
        cg = lax.broadcasted_iota(I32, full.shape, 1) >> col_shift
        keep = ((rg ^ cg) & (SSM_GBLK - 1)) == 0
        dst_ref[row0 + r0:row0 + r0 + step, :] = jnp.where(keep, full, 0.0).astype(BF16)


def _ssm_kernel(u_ref, bc_ref, pb_ref, cc_ref, kc_ref, eb_ref, ec_ref, tab_ref, z_ref,
                wb_ref, wcd_ref, ust_ref, bu_ref, yn_ref, *, t8, seg, nchunk):
    rc = t8 // nchunk

    @pl.when(pl.program_id(1) == 0)
    def _():
        bc = bc_ref[...]
        bc_swapped = pltpu.roll(bc, STATE, 1)
        for i in range(SSM_STEP):
            blk = pb_ref[i, :, 0:LANES] * bc + pb_ref[i, :, LANES:2 * LANES] * bc_swapped
            _expand_block_diag(wb_ref, LANES * i, blk, eb_ref, 4, 6)
        c_r, c_i = cc_ref[:, 0:LANES], cc_ref[:, LANES:2 * LANES]
        p_r, p_i = cc_ref[:, 2 * LANES:3 * LANES], cc_ref[:, 3 * LANES:4 * LANES]
        _expand_block_diag(wcd_ref, 0, c_r * p_r - c_i * p_i, ec_ref, 6, 4)
        _expand_block_diag(wcd_ref, 512, -(c_r * p_i + c_i * p_r), ec_ref, 6, 4)
        kc = kc_ref[...]
        lane = lax.broadcasted_iota(I32, kc.shape, 1)
        for ip in range(SSM_STEP):
            blk = kc if ip == 0 else jnp.where(lane >= GROUP * ip, pltpu.roll(kc, GROUP * ip, 1), 0.0)
            _expand_block_diag(wcd_ref, 1024 + LANES * ip, blk, ec_ref, 4, 4)

    for i in range(SSM_STEP):
        ust_ref[:, LANES * i:LANES * (i + 1)] = u_ref[pl.ds(i, t8, stride=SSM_STEP), :].astype(BF16)

    for c in range(nchunk):
        r = _dot(ust_ref[c * rc:(c + 1) * rc, :], wb_ref[...])
        for m in range(8):
            bu_ref[m, c * rc:(c + 1) * rc, :] = r[:, LANES * m:LANES * (m + 1)]

    a_re = [jnp.broadcast_to(tab_ref[0:1, LANES * m:LANES * (m + 1)], (SSM_SEGS, LANES)) for m in range(4)]
    a_im = [jnp.broadcast_to(tab_ref[1:2, LANES * m:LANES * (m + 1)], (SSM_SEGS, LANES)) for m in range(4)]
    s_re = [jnp.broadcast_to(tab_ref[2:3, LANES * m:LANES * (m + 1)], (SSM_SEGS, LANES)) for m in range(4)]
    s_im = [jnp.broadcast_to(tab_ref[3:4, LANES * m:LANES * (m + 1)], (SSM_SEGS, LANES)) for m in range(4)]

    def cmul_add(mr, mi, xr, xi, br, bi):
        return mr * xr - mi * xi + br, mr * xi + mi * xr + bi

    def load(k):
        return [bu_ref[m, pl.ds(k, SSM_SEGS, stride=seg), :] for m in range(8)]

    def advance(st, b):
        nr, ni = [], []
        for m in range(4):
            r_, i_ = cmul_add(a_re[m], a_im[m], st[m], st[4 + m], b[m], b[4 + m])
            nr.append(r_)
            ni.append(i_)
        return tuple(nr + ni)

    zeros = tuple(jnp.zeros((SSM_SEGS, LANES), F32) for _ in range(8))
    fin = lax.fori_loop(0, seg, lambda k, st: advance(st, load(k)), zeros)

    first = lax.broadcasted_iota(I32, (SSM_SEGS, LANES), 0) == 0

    def shift_down(t):
        return jnp.where(first, 0.0, pltpu.roll(t, 1, 0))

    tot = fin
    for _ in range(SSM_SEGS - 1):
        nxt_r, nxt_i = [], []
        for m in range(4):
            r_, i_ = cmul_add(s_re[m], s_im[m], shift_down(tot[m]), shift_down(tot[4 + m]), fin[m], fin[4 + m])
            nxt_r.append(r_)
            nxt_i.append(i_)
        tot = tuple(nxt_r + nxt_i)
    init = tuple(shift_down(t) for t in tot)

    def pass2(k, st):
        b = load(k)
        for m in range(8):
            bu_ref[m, pl.ds(k, SSM_SEGS, stride=seg), :] = st[m]
        return advance(st, b)

    lax.fori_loop(0, seg, pass2, init)

    for c in range(nchunk):
        xp = jnp.concatenate([bu_ref[m, c * rc:(c + 1) * rc, :] for m in range(8)], axis=1).astype(BF16)
        lhs = jnp.concatenate([xp, ust_ref[c * rc:(c + 1) * rc, :]], axis=1)
        y = _dot(lhs, wcd_ref[...])
        for i in range(SSM_STEP):
            yn_ref[pl.ds(SSM_STEP * c * rc + i, rc, stride=SSM_STEP), :] = y[:, LANES * i:LANES * (i + 1)]

    y = yn_ref[...] + tab_ref[4:5, 0:LANES] * u_ref[...]
    z_ref[...] = jax.nn.gelu(y).astype(BF16)


def _ssm(u, bc, pb, cc, kc, tab, *, bsz, lp, nchunk):
    np_ = u.shape[0]
    t8 = lp // SSM_STEP
    seg = t8 // SSM_SEGS
    nblk = D_MODEL // LANES
    col = jnp.arange(1024)
    src_b = (col >> 9) * STATE + (col & (STATE - 1))
    src_c = (col >> 7) * GROUP + (col & (GROUP - 1))
    eb = (jnp.arange(LANES)[:, None] == src_b[None, :]).astype(BF16)
    ec = (jnp.arange(LANES)[:, None] == src_c[None, :]).astype(BF16)
    kern = functools.partial(_ssm_kernel, t8=t8, seg=seg, nchunk=nchunk)
    per_q = lambda q, b: (q, 0, 0)
    fixed = lambda q, b: (0, 0)
    return pl.pallas_call(
        kern,
        out_shape=jax.ShapeDtypeStruct((np_, D_MODEL), BF16),
        grid=(nblk, bsz),
        in_specs=[
            pl.BlockSpec((lp, LANES), lambda q, b: (b, q)),
            pl.BlockSpec((None, LANES, LANES), per_q),
            pl.BlockSpec((SSM_STEP, None, LANES, 2 * LANES), lambda q, b: (0, q, 0, 0)),
            pl.BlockSpec((None, SSM_GBLK * STATE, 4 * LANES), per_q),
            pl.BlockSpec((None, LANES, LANES), per_q),
            pl.BlockSpec((LANES, 1024), fixed),
            pl.BlockSpec((LANES, 1024), fixed),
            pl.BlockSpec((None, 8, 512), per_q),
        ],
        out_specs=pl.BlockSpec((lp, LANES), lambda q, b: (b, q)),
        scratch_shapes=[
            pltpu.VMEM((1024, 1024), BF16),
            pltpu.VMEM((2048, 1024), BF16),
            pltpu.VMEM((t8, 1024), BF16),
            pltpu.VMEM((8, t8, LANES), F32),
            pltpu.VMEM((lp, LANES), F32),
        ],
        compiler_params=_cparams(("arbitrary", "arbitrary")),
        name="ssm",
    )(u, bc, pb, cc, kc, eb, ec, tab)


def _ssm_tables(lam_re, lam_im, log_dt, b_re, b_im, c_re, c_im, d, seg):
    hi = lax.Precision.HIGHEST
    lr, li = lam_re.astype(F32), lam_im.astype(F32)
    dt = jnp.exp(log_dt.astype(F32))[:, None]
    mag = jnp.exp(dt * lr)
    ar, ai = mag * jnp.cos(dt * li), mag * jnp.sin(dt * li)
    den = lr * lr + li * li
    nr, ni = ar - 1.0, ai
    fr, fi = (nr * lr + ni * li) / den, (ni * lr - nr * li) / den
    br, bi = b_re.astype(F32), b_im.astype(F32)
    bbr = fr[..., None] * br - fi[..., None] * bi
    bbi = fr[..., None] * bi + fi[..., None] * br

    def cmul(xr, xi, yr, yi):
        return xr * yr - xi * yi, xr * yi + xi * yr

    pr, pi = [jnp.ones_like(ar)], [jnp.zeros_like(ar)]
    for _ in range(SSM_STEP):
        r_, i_ = cmul(pr[-1], pi[-1], ar, ai)
        pr.append(r_)
        pi.append(i_)
    pw_r, pw_i = jnp.stack(pr), jnp.stack(pi)
    a8r, a8i = pw_r[SSM_STEP], pw_i[SSM_STEP]
    sr, si = jnp.ones_like(ar), jnp.zeros_like(ar)
    qr, qi = a8r, a8i
    e = seg
    while e:
        if e & 1:
            sr, si = cmul(sr, si, qr, qi)
        qr, qi = cmul(qr, qi, qr, qi)
        e >>= 1

    cr, ci = c_re.astype(F32), c_im.astype(F32)
    nb = N_GROUPS // SSM_GBLK

    bc = jnp.concatenate([jnp.swapaxes(bbr, 1, 2), jnp.swapaxes(bbi, 1, 2)], axis=2).reshape(nb, LANES, LANES)
    pb = jnp.stack([jnp.concatenate([pr[SSM_STEP - 1 - i]] * 2 + [-pi[SSM_STEP - 1 - i], pi[SSM_STEP - 1 - i]], axis=1)
                    for i in range(SSM_STEP)])
    pb = jnp.broadcast_to(pb[:, :, None, :], (SSM_STEP, N_GROUPS, GROUP, 2 * LANES))
    pb = pb.reshape(SSM_STEP, nb, LANES, 2 * LANES)
    crt = jnp.tile(jnp.swapaxes(cr, 1, 2), (1, 1, SSM_STEP))
    cit = jnp.tile(jnp.swapaxes(ci, 1, 2), (1, 1, SSM_STEP))
    prc = jnp.repeat(jnp.stack(pr[1:], axis=2), GROUP, axis=2)
    pic = jnp.repeat(jnp.stack(pi[1:], axis=2), GROUP, axis=2)
    cc = jnp.concatenate([crt, cit, prc, pic], axis=2).reshape(nb, SSM_GBLK * STATE, 4 * LANES)
    tr, ti = cmul(pw_r[:SSM_STEP, :, :, None], pw_i[:SSM_STEP, :, :, None], bbr[None], bbi[None])
    kt = (jnp.einsum('ghp,tgpk->gkth', cr, tr, precision=hi)
          - jnp.einsum('ghp,tgpk->gkth', ci, ti, precision=hi))
    kc = kt.reshape(nb, LANES, LANES)

    def lanes(t):
        return t.reshape(nb, SSM_GBLK * STATE)

    dl = jnp.pad(d.astype(F32).reshape(nb, SSM_GBLK * GROUP), ((0, 0), (0, 512 - LANES)))
    zero = jnp.zeros((nb, 512), F32)
    tab = jnp.stack([lanes(a8r), lanes(a8i), lanes(sr), lanes(si), dl, zero, zero, zero], axis=1)
    return bc, pb, cc, kc, tab


def _attn_kernel(q_ref, kc_ref, kp_ref, km_ref, vc_ref, vp_ref, vm_ref, sink_ref, o_ref):
    n = pl.program_id(1)
    nk = 3 * BLK
    row = lax.broadcasted_iota(I32, (BLK, nk), 0)
    col = lax.broadcasted_iota(I32, (BLK, nk), 1)
    band = (col > row) & (col <= row + BLK) & (col >= 2 * BLK - BLK * n)
    mrow = col - 2 * BLK
    meta = (mrow >= PAD) & (mrow <= BLK * n + row)
    bias = jnp.where(band | meta, 0.0, NEG_INF)

    for g in range(N_KV_HEADS):
        qg = jnp.concatenate([q_ref[:, LANES * (4 * g + jj):LANES * (4 * g + jj + 1)] for jj in range(4)], axis=0)
        acc = None
        for par in range(2):
            sl = slice(LANES * (2 * g + par), LANES * (2 * g + par + 1))
            kk = jnp.concatenate([kp_ref[:, sl], kc_ref[:, sl], km_ref[:, sl]], axis=0)
            vv = jnp.concatenate([vp_ref[:, sl], vc_ref[:, sl], vm_ref[:, sl]], axis=0)
            s_all = _dot_nt(qg, kk)
            probs, rden = [], []
            for jj in range(4):
                sink = sink_ref[8 * g + 2 * jj + par]
                s = s_all[BLK * jj:BLK * (jj + 1), :] + bias
                mx = jnp.maximum(jnp.max(s, axis=1, keepdims=True), sink)
                p = jnp.exp(s - mx)
                den = jnp.sum(p, axis=1, keepdims=True) + jnp.exp(sink - mx)
                probs.append(p.astype(BF16))
                rden.append(jnp.broadcast_to(1.0 / den, (BLK, LANES)))
            o = _dot(jnp.concatenate(probs, axis=0), vv) * jnp.concatenate(rden, axis=0)
            acc = o if acc is None else acc + o
        for jj in range(4):
            o_ref[:, LANES * (4 * g + jj):LANES * (4 * g + jj + 1)] = acc[BLK * jj:BLK * (jj + 1), :].astype(BF16)


def _attention(q, kx, vx, sinks, *, bsz, nb):
    np_ = q.shape[0]
    cur = lambda b, n: (b * nb + n, 0)
    prev = lambda b, n: (b * nb + jnp.maximum(n - 1, 0), 0)
    first = lambda b, n: (b * nb, 0)
    return pl.pallas_call(
        _attn_kernel,
        out_shape=jax.ShapeDtypeStruct((np_, 1024), BF16),
        grid=(bsz, nb),
        in_specs=[
            pl.BlockSpec((BLK, 1024), cur),
            pl.BlockSpec((BLK, 512), cur),
            pl.BlockSpec((BLK, 512), prev),
            pl.BlockSpec((BLK, 512), first),
            pl.BlockSpec((BLK, 512), cur),
            pl.BlockSpec((BLK, 512), prev),
            pl.BlockSpec((BLK, 512), first),
            pl.BlockSpec(memory_space=pltpu.SMEM),
        ],
        out_specs=pl.BlockSpec((BLK, 1024), cur),
        compiler_params=_cparams(("arbitrary", "arbitrary")),
        name="attn",
    )(q, kx, kx, kx, vx, vx, vx, sinks)


def _post_kernel(*refs, tiles_per_seq, per):
    x_refs, (meta_ref, z_ref, at_ref, sgs_ref, sga_ref, wglu_ref, bglu_ref, wbs_ref, wba_ref, wout_ref,
             nffn_ref, wrh_ref, wrl_ref, br_ref,
             h2_ref, hnp_ref, lslot_ref, ecol_ref, cnt_ref) = refs[:per], refs[per:]
    i = pl.program_id(0)

    z = z_ref[...]
    t = _dot(z, wglu_ref[...]) + bglu_ref[...]
    so = (z.astype(F32) * jax.nn.sigmoid(t)).astype(BF16)
    mix = (sgs_ref[...].astype(F32) * _dot(so, wbs_ref[...])
           + sga_ref[...].astype(F32) * _dot(at_ref[...], wba_ref[...]))
    mixed = _dot(mix.astype(BF16), wout_ref[...])
    logit_blocks = []
    for r, xr in enumerate(_padded_rows(x_refs, meta_ref, i % tiles_per_seq == 0)):
        rows = slice(BLK * r, BLK * (r + 1))
        h2 = xr + mixed[rows, :]
        h2_ref[rows, :] = h2
        ms = jnp.mean(h2 * h2, axis=-1, keepdims=True)
        hn = h2 * lax.rsqrt(ms + RMS_EPS) * nffn_ref[...]
        hb = hn.astype(BF16)
        hnp_ref[rows, :] = hb
        lo = (hn - hb.astype(F32)).astype(BF16)
        logit_blocks.append(_dot(hb, wrh_ref[...]) + _dot(lo, wrh_ref[...]) + _dot(hb, wrl_ref[...]) + br_ref[...])
    logits = jnp.concatenate(logit_blocks, axis=0)

    tm = logits.shape[0]
    lane = lax.broadcasted_iota(I32, (tm, LANES), 1)
    lane_f = lane.astype(F32)
    work = logits
    vals, hots, idxs = [], [], []
    for _ in range(TOP_K):
        mx = jnp.max(work, axis=1, keepdims=True)
        idx = jnp.min(jnp.where(work == mx, lane_f, float(LANES)), axis=1, keepdims=True)
        hot = lane_f == idx
        vals.append(mx)
        idxs.append(idx)
        hots.append(hot)
        work = jnp.where(hot, -jnp.inf, work)

    ex = [jnp.exp(v - vals[0]) for v in vals]
    tot = ex[0] + ex[1] + ex[2] + ex[3]
    gates = [e / tot for e in ex]

    onehot = jnp.zeros((tm, LANES), F32)
    for hot in hots:
        onehot = onehot + jnp.where(hot, 1.0, 0.0)
    r_i = lax.broadcasted_iota(I32, (tm, tm), 0)
    c_i = lax.broadcasted_iota(I32, (tm, tm), 1)
    tri = jnp.where(r_i > c_i, 1.0, 0.0).astype(BF16)
    before = _dot(tri, onehot.astype(BF16))
    count = jnp.sum(onehot, axis=0, keepdims=True)
    gran = jnp.floor((count + (GRAN - 1)) * (1.0 / GRAN))
    e_r = lax.broadcasted_iota(I32, (LANES, LANES), 0)
    e_c = lax.broadcasted_iota(I32, (LANES, LANES), 1)
    upper = jnp.where(e_r < e_c, 1.0, 0.0).astype(BF16)
    gstart = _dot(jnp.broadcast_to(gran, (8, LANES)).astype(BF16), upper)[0:1, :]
    base = before + gstart * float(GRAN)
    slots = [jnp.sum(jnp.where(hot, base, 0.0), axis=1, keepdims=True) for hot in hots]
    cnt_ref[...] = jnp.broadcast_to(gran, cnt_ref.shape)

    def spread(cols, dtype):
        out = jnp.zeros((tm, LANES), dtype)
        for k, cval in enumerate(cols):
            out = jnp.where(lane == k, cval, out)
        return out

    lslot_ref[...] = spread(slots, F32)
    cols = []
    for g in gates:
        g1 = g.astype(BF16).astype(F32)
        g2 = (g - g1).astype(BF16).astype(F32)
        cols += [g1, g2, g - g1 - g2]
    for s in slots:
        hi = jnp.floor(s * (1.0 / 64.0))
        cols += [hi, s - 64.0 * hi]
    ecol_ref[...] = spread(cols, F32)


def _post(x, meta, z, attn, sgs, sga, wglu, bglu, wbs, wba, wout, nffn, wrh, wrl, br, *, tm, lp):
    np_ = z.shape[0]
    tiles_per_seq = lp // tm
    per = tm // BLK
    row = lambda i: (i, 0)
    fixed = lambda i: (0, 0)
    sq = pl.BlockSpec((1024, 1024), fixed)
    vec = pl.BlockSpec((1, 1024), fixed)
    out_shape = (
        jax.ShapeDtypeStruct((np_, 1024), F32),
        jax.ShapeDtypeStruct((np_, 1024), BF16),
        jax.ShapeDtypeStruct((np_, LANES), F32),
        jax.ShapeDtypeStruct((np_, LANES), F32),
        jax.ShapeDtypeStruct((np_ // tm * 8, LANES), F32),
    )
    return pl.pallas_call(
        functools.partial(_post_kernel, tiles_per_seq=tiles_per_seq, per=per),
        out_shape=out_shape,
        grid=(np_ // tm,),
        in_specs=_padded_row_specs(tm, tiles_per_seq) + [
            pl.BlockSpec((N_META, D_MODEL), fixed),
            pl.BlockSpec((tm, 1024), row), pl.BlockSpec((tm, 1024), row),
            pl.BlockSpec((tm, 1024), row), pl.BlockSpec((tm, 1024), row),
            sq, vec, sq, sq, sq, vec,
            pl.BlockSpec((1024, LANES), fixed), pl.BlockSpec((1024, LANES), fixed),
            pl.BlockSpec((1, LANES), fixed),
        ],
        out_specs=(
            pl.BlockSpec((tm, 1024), row), pl.BlockSpec((tm, 1024), row),
            pl.BlockSpec((tm, LANES), row), pl.BlockSpec((tm, LANES), row),
            pl.BlockSpec((8, LANES), row),
        ),
        compiler_params=_cparams(("arbitrary",)),
        name="post",
    )(*([x] * per), meta, z, attn, sgs, sga, wglu, bglu, wbs, wba, wout, nffn, wrh, wrl, br)


def _sort_kernel(hnp_ref, ecol_ref, xloc_ref, *, tm, srows):
    ecol = ecol_ref[...].astype(BF16)
    pay = jnp.concatenate([hnp_ref[...], ecol], axis=1)

    r8 = lax.broadcasted_iota(I32, (8, LANES), 0)
    l8 = lax.broadcasted_iota(I32, (8, LANES), 1)
    digit_w = jnp.where(l8 == 12 + 2 * r8, 64.0, jnp.where(l8 == 13 + 2 * r8, 1.0, 0.0)).astype(BF16)
    slot_rows = _dot_nt(digit_w, ecol)

    rr = lax.broadcasted_iota(I32, (LANES, LANES), 0)
    cc = lax.broadcasted_iota(I32, (LANES, LANES), 1)
    kk = jnp.where(cc < 3 * TOP_K, cc // 3, -LANES)
    spread_w = jnp.where(rr == 12 + 2 * kk, 64.0, jnp.where(rr == 13 + 2 * kk, 1.0, 0.0)).astype(BF16)
    gate_lane = lax.broadcasted_iota(I32, (SORT_CHUNK, LANES), 1) < 3 * TOP_K

    for c in range(srows // SORT_CHUNK):
        s_id = (lax.broadcasted_iota(I32, (SORT_CHUNK, tm), 0) + c * SORT_CHUNK).astype(F32)
        perm = jnp.zeros((SORT_CHUNK, tm), F32)
        for k in range(TOP_K):
            perm = perm + jnp.where(slot_rows[k:k + 1, :] == s_id, 1.0, 0.0)
        got = _dot(perm.astype(BF16), pay)
        rows = slice(c * SORT_CHUNK, (c + 1) * SORT_CHUNK)
        xloc_ref[rows, 0:512] = _pack_bf16_pairs(got[:, :D_MODEL], exact=True)
        extra = got[:, D_MODEL:]
        owner = _dot(extra.astype(BF16), spread_w)
        me = (lax.broadcasted_iota(I32, (SORT_CHUNK, LANES), 0) + c * SORT_CHUNK).astype(F32)
        mine = jnp.where(gate_lane, owner, -1.0) == me
        gate = jnp.sum(jnp.where(mine, extra, 0.0), axis=1, keepdims=True)
        xloc_ref[rows, 512:SORT_W] = pltpu.bitcast(jnp.broadcast_to(gate, (SORT_CHUNK, LANES)), U32)


def _sort(hnp, ecol, *, tm, srows):
    np_ = hnp.shape[0]
    ntile = np_ // tm
    row = lambda i: (i, 0)
    return pl.pallas_call(
        functools.partial(_sort_kernel, tm=tm, srows=srows),
        out_shape=jax.ShapeDtypeStruct((ntile * srows, SORT_W), U32),
        grid=(ntile,),
        in_specs=[pl.BlockSpec((tm, D_MODEL), row), pl.BlockSpec((tm, LANES), row)],
        out_specs=pl.BlockSpec((srows, SORT_W), row),
        compiler_params=_cparams(("arbitrary",)),
        name="sort",
    )(hnp, ecol)


def _moe_kernel(blk_ref, exp_ref, lo_ref, hi_ref, kind_ref, wslot_ref, nexte_ref, gsrc_ref,
                xloc_ref, wgu_ref, bgu_ref, wd_ref, bd_ref, yloc_ref,
                xbuf, ybuf, sem_in, sem_out, wgu_f, wd_f, sem_w, wgu_bf, wd_bf, *, nseg, dump0):
    i = pl.program_id(0)
    prev = jnp.maximum(i - 1, 0)
    nxt = jnp.minimum(i + 1, nseg - 1)
    slot = i % 2
    new_expert = (i == 0) | (exp_ref[i] != exp_ref[prev])

    def gather(seg, dst_slot, start):
        b = blk_ref[seg]
        for g in range(MOE_GRANS):
            src = pl.multiple_of(gsrc_ref[b * MOE_GRANS + g] * GRAN, GRAN)
            cp = pltpu.make_async_copy(xloc_ref.at[pl.ds(src, GRAN)], xbuf.at[dst_slot, pl.ds(g * GRAN, GRAN)],
                                       sem_in.at[dst_slot])
            if start:
                cp.start()
            else:
                cp.wait()

    def scatter(seg, src_slot, start):
        b = blk_ref[seg]
        lo = lo_ref[seg]
        hi = hi_ref[seg]
        for g in range(MOE_GRANS):
            keep = (g >= lo) & (g < hi)
            spare = dump0 + src_slot * MOE_GRANS + g
            dst = pl.multiple_of(jnp.where(keep, gsrc_ref[b * MOE_GRANS + g], spare) * GRAN, GRAN)
            cp = pltpu.make_async_copy(ybuf.at[src_slot, pl.ds(g * GRAN, GRAN)], yloc_ref.at[pl.ds(dst, GRAN)],
                                       sem_out.at[src_slot])
            if start:
                cp.start()
            else:
                cp.wait()

    @pl.when(i == 0)
    def _():
        gather(i, slot, True)
        ybuf[1 - slot] = jnp.zeros((MOE_BLK, 512), U32)
        for g in range(MOE_GRANS):
            pltpu.make_async_copy(ybuf.at[1 - slot, pl.ds(g * GRAN, GRAN)],
                                  yloc_ref.at[pl.ds((dump0 + (1 - slot) * MOE_GRANS + g) * GRAN, GRAN)],
                                  sem_out.at[1 - slot]).start()

    gather(i, slot, False)

    def fetch(e, slot, start):
        for src, dst in ((wgu_ref, wgu_f), (wd_ref, wd_f)):
            cp = pltpu.make_async_copy(src.at[e], dst.at[slot], sem_w.at[slot])
            if start:
                cp.start(priority=1)
            else:
                cp.wait()

    wslot = wslot_ref[i]

    @pl.when(i == 0)
    def _():
        fetch(exp_ref[i], wslot, True)

    @pl.when(new_expert)
    def _():
        fetch(exp_ref[i], wslot, False)
        wgu_bf[...] = wgu_f[wslot].astype(BF16)
        wd_bf[...] = wd_f[wslot].astype(BF16)

    @pl.when(new_expert & (nexte_ref[i] >= 0))
    def _():
        fetch(nexte_ref[i], 1 - wslot, True)

    kind = kind_ref[i]

    @pl.when(kind == 1)
    def _():
        gather(nxt, 1 - slot, True)
        xw = xbuf[slot]
        xb = _unpack_bf16_pairs(xw[:, 0:512]).astype(BF16)
        gate = pltpu.bitcast(xw[:, 512:SORT_W], F32)
        gu = _dot(xb, wgu_bf[...]) + bgu_ref[...]
        g = jnp.minimum(gu[:, :D_FF], SWIGLU_LIMIT)
        up = jnp.clip(gu[:, D_FF:], -SWIGLU_LIMIT, SWIGLU_LIMIT)
        hid = g * jax.nn.sigmoid(SWIGLU_ALPHA * g) * (up + 1.0)
        y = _dot(hid.astype(BF16), wd_bf[...]) + bd_ref[...]
        y = y * jnp.concatenate([gate] * (D_MODEL // LANES), axis=1)
        ybuf[slot] = _pack_bf16_pairs(y)
        scatter(i, slot, True)
        scatter(prev, 1 - slot, False)

    @pl.when(kind != 1)
    def _():
        gather(nxt, 1 - slot, True)
        ybuf[slot] = jnp.zeros((MOE_BLK, 512), U32)
        scatter(i, slot, True)
        scatter(prev, 1 - slot, False)

    @pl.when(i == nseg - 1)
    def _():
        scatter(i, slot, False)
        gather(nxt, 1 - slot, False)


def _moe(blk, exp, lo, hi, kind, gsrc, xloc, wgu, bgu, wd, bd, *, ncap):
    nseg = blk.shape[0]
    dump0 = ncap
    change = jnp.concatenate([jnp.zeros((1,), I32), (exp[1:] != exp[:-1]).astype(I32)])
    wslot = jnp.cumsum(change) % 2
    later = jnp.where(exp[None, :] > exp[:, None], exp[None, :], N_EXPERTS)
    nexte = jnp.min(later, axis=1)
    nexte = jnp.where(nexte >= N_EXPERTS, -1, nexte)
    wmap = lambda i, b, e, *_: (e[i], 0, 0)
    grid_spec = pltpu.PrefetchScalarGridSpec(
        num_scalar_prefetch=8,
        grid=(nseg,),
        in_specs=[
            pl.BlockSpec(memory_space=pl.ANY),
            pl.BlockSpec(memory_space=pl.ANY),
            pl.BlockSpec((None, 1, 2 * D_FF), wmap),
            pl.BlockSpec(memory_space=pl.ANY),
            pl.BlockSpec((None, 1, D_MODEL), wmap),
        ],
        out_specs=pl.BlockSpec(memory_space=pl.ANY),
        scratch_shapes=[
            pltpu.VMEM((2, MOE_BLK, SORT_W), U32),
            pltpu.VMEM((2, MOE_BLK, 512), U32),
            pltpu.SemaphoreType.DMA((2,)),
            pltpu.SemaphoreType.DMA((2,)),
            pltpu.VMEM((2, D_MODEL, 2 * D_FF), F32),
            pltpu.VMEM((2, D_FF, D_MODEL), F32),
            pltpu.SemaphoreType.DMA((2,)),
            pltpu.VMEM((D_MODEL, 2 * D_FF), BF16),
            pltpu.VMEM((D_FF, D_MODEL), BF16),
        ],
    )
    return pl.pallas_call(
        functools.partial(_moe_kernel, nseg=nseg, dump0=dump0),
        out_shape=jax.ShapeDtypeStruct(((ncap + 2 * MOE_GRANS) * GRAN, 512), U32),
        grid_spec=grid_spec,
        compiler_params=_cparams(("arbitrary",)),
        name="moe",
    )(blk, exp, lo, hi, kind, wslot.astype(I32), nexte.astype(I32), gsrc, xloc, wgu, bgu, wd, bd)


def _combine_kernel(y_ref, lslot_ref, h2_ref, nfin_ref, o_ref, obuf, sem, *, tm, srows, tiles_per_seq):
    i = pl.program_id(0)
    b = i // tiles_per_seq
    j = i % tiles_per_seq
    per = tm // BLK
    slots = lslot_ref[...]
    acc = h2_ref[...]
    for c in range(srows // SORT_CHUNK):
        s_id = (lax.broadcasted_iota(I32, (tm, SORT_CHUNK), 1) + c * SORT_CHUNK).astype(F32)
        take = jnp.zeros((tm, SORT_CHUNK), F32)
        for k in range(TOP_K):
            take = take + jnp.where(slots[:, k:k + 1] == s_id, 1.0, 0.0)
        rows = _unpack_bf16_pairs(y_ref[c * SORT_CHUNK:(c + 1) * SORT_CHUNK, :]).astype(BF16)
        acc = acc + _dot(take.astype(BF16), rows)
    ms = jnp.mean(acc * acc, axis=-1, keepdims=True)
    obuf[...] = acc * lax.rsqrt(ms + RMS_EPS) * nfin_ref[...]

    def copies(start):
        for r in range(per):
            n = j * per + r
            cp = pltpu.make_async_copy(obuf.at[pl.ds(r * BLK, BLK)],
                                       o_ref.at[b, pl.ds(pl.multiple_of(jnp.maximum(n - 1, 0) * BLK, BLK), BLK)], sem)

            def go():
                if start:
                    cp.start()
                else:
                    cp.wait()

            if r == 0:
                pl.when(j > 0)(go)
            else:
                go()

    copies(True)
    copies(False)


def _combine(yloc, lslot, h2, nfin, *, bsz, seq, tm, lp, srows):
    np_ = h2.shape[0]
    tiles_per_seq = lp // tm
    row = lambda i: (i, 0)
    return pl.pallas_call(
        functools.partial(_combine_kernel, tm=tm, srows=srows, tiles_per_seq=tiles_per_seq),
        out_shape=jax.ShapeDtypeStruct((bsz, seq, D_MODEL), F32),
        grid=(np_ // tm,),
        in_specs=[
            pl.BlockSpec((srows, 512), row),
            pl.BlockSpec((tm, LANES), row),
            pl.BlockSpec((tm, D_MODEL), row),
            pl.BlockSpec((1, D_MODEL), lambda i: (0, 0)),
        ],
        out_specs=pl.BlockSpec(memory_space=pl.ANY),
        scratch_shapes=[pltpu.VMEM((tm, D_MODEL), F32), pltpu.SemaphoreType.DMA],
        compiler_params=_cparams(("arbitrary",)),
        name="combine",
    )(yloc, lslot, h2, nfin)


def _rope_tables(lp):
    pos = np.arange(lp, dtype=np.float32) - np.float32(PAD)
    inv_freq = np.float32(ROPE_THETA) ** (-np.arange(0, 2 * ROT_HALF, 2, dtype=np.float32) / np.float32(2 * ROT_HALF))
    ang = (pos[:, None] * inv_freq[None, :]).astype(np.float32)
    cos, sin = np.cos(ang), np.sin(ang)
    ones = np.ones((lp, HEAD_DIM - 2 * ROT_HALF), np.float32)
    zeros8 = np.zeros((lp, ROT_HALF), np.float32)
    zrest = np.zeros((lp, HEAD_DIM - 2 * ROT_HALF), np.float32)
    c = np.concatenate([cos, cos, ones], axis=1)
    s1 = np.concatenate([zeros8, sin, zrest], axis=1)
    s2 = np.concatenate([-sin, zeros8, zrest], axis=1)
    tile = lambda t: jnp.asarray(np.concatenate([t, t], axis=1), F32)
    return tile(c), tile(s1), tile(s2)


def _route_tables(grans, gt):
    ntile = grans.shape[0]
    ncap = ntile * gt
    nown = N_EXPERTS + 1
    npos = ncap + nown * MOE_GRANS
    nblocks = npos // MOE_GRANS
    used = jnp.sum(grans, axis=1)
    lstart = jnp.cumsum(grans, axis=1) - grans
    tile0 = jnp.arange(ntile, dtype=I32)[:, None] * gt
    lens = jnp.concatenate([grans.T, (gt - used)[None, :]], axis=0)
    srcs = jnp.concatenate([(tile0 + lstart).T, (tile0[:, 0] + used)[None, :]], axis=0)
    total = jnp.sum(lens, axis=1)
    room = (total + MOE_GRANS - 1) // MOE_GRANS * MOE_GRANS
    oend = jnp.cumsum(room)
    ostart = oend - room
    run_start = (ostart[:, None] + jnp.cumsum(lens, axis=1) - lens).reshape(-1)
    off = srcs.reshape(-1) - run_start
    delta = off - jnp.concatenate([jnp.zeros((1,), I32), off[:-1]])
    pos = jnp.arange(npos, dtype=I32)
    gsrc = pos + jnp.sum(jnp.where(run_start[None, :] <= pos[:, None], delta[None, :], 0), axis=1)

    first = jnp.arange(nblocks, dtype=I32) * MOE_GRANS
    owner = jnp.sum((oend[None, :] <= first[:, None]).astype(I32), axis=1)
    mine = owner[:, None] == jnp.arange(nown, dtype=I32)[None, :]
    valid_end = jnp.sum(jnp.where(mine, (ostart + total)[None, :], 0), axis=1)
    hi = jnp.clip(valid_end - first, 0, MOE_GRANS)
    lo = jnp.zeros_like(hi)
    gsrc = jnp.where((pos % MOE_GRANS) < jnp.repeat(hi, MOE_GRANS), gsrc, 0)
    kind = jnp.where(hi > 0, jnp.where(owner >= N_EXPERTS, 2, 1), 0)
    works = kind == 1
    first_work = jnp.min(jnp.where(works, owner, N_EXPERTS - 1))
    exp = jnp.maximum(lax.cummax(jnp.where(works, owner, -1), axis=0), first_work)
    blk = jnp.arange(nblocks, dtype=I32)
    return gsrc.astype(I32), blk, exp.astype(I32), lo.astype(I32), hi.astype(I32), kind.astype(I32)


def kernel(x, meta_tokens, norm_mix, w_in, ssm_lam_re, ssm_lam_im, ssm_log_dt, ssm_b_re, ssm_b_im,
           ssm_c_re, ssm_c_im, ssm_d, w_glu, b_glu, attn_sinks, w_br_ssm, w_br_attn, w_out,
           norm_ffn, w_router, b_router, w_gate_up, b_gate_up, w_down, b_down, norm_final):
    bsz, seq, _ = x.shape
    lp = seq + BLK
    nb = lp // BLK
    np_ = bsz * lp
    tm = 640 if lp % 640 == 0 else BLK
    t8 = lp // SSM_STEP
    seg = t8 // SSM_SEGS
    nchunk = 5 if (t8 % 5 == 0 and (t8 // 5) % 16 == 0) else 1

    meta = meta_tokens.astype(F32)
    rc, rs1, rs2 = _rope_tables(lp)
    u, q, kx, vx, sgs, sga = _inproj(x, meta, norm_mix[0][None], w_in[0].astype(BF16), rc, rs1, rs2, tm=tm, lp=lp)

    bc, pb, cc, kc, tab = _ssm_tables(ssm_lam_re[0], ssm_lam_im[0], ssm_log_dt[0], ssm_b_re[0], ssm_b_im[0],
                                      ssm_c_re[0], ssm_c_im[0], ssm_d[0], seg)
    z = _ssm(u, bc, pb, cc, kc, tab, bsz=bsz, lp=lp, nchunk=nchunk)

    attn = _attention(q, kx, vx, attn_sinks[0].astype(F32), bsz=bsz, nb=nb)

    wr = jnp.pad(w_router[0].astype(F32), ((0, 0), (0, LANES - N_EXPERTS)))
    wrh = wr.astype(BF16)
    wrl = (wr - wrh.astype(F32)).astype(BF16)
    br = jnp.concatenate([b_router[0].astype(F32), jnp.full((LANES - N_EXPERTS,), NEG_INF, F32)])[None]
    h2, hnp, lslot, ecol, cnt = _post(
        x, meta, z, attn, sgs, sga, w_glu[0].astype(BF16), b_glu[0][None], w_br_ssm[0].astype(BF16),
        w_br_attn[0].astype(BF16), w_out[0].astype(BF16), norm_ffn[0][None], wrh, wrl, br, tm=tm, lp=lp)

    srows = -(-(tm * TOP_K + N_EXPERTS * (GRAN - 1)) // SORT_CHUNK) * SORT_CHUNK
    gt = srows // GRAN
    ntile = np_ // tm
    grans = cnt.reshape(ntile, 8, LANES)[:, 0, :N_EXPERTS].astype(I32)
    gsrc, sblk, sexp, slo, shi, skind = _route_tables(grans, gt)

    xloc = _sort(hnp, ecol, tm=tm, srows=srows)
    yloc = _moe(sblk, sexp, slo, shi, skind, gsrc, xloc, w_gate_up[0], b_gate_up[0][:, None, :],
                w_down[0], b_down[0][:, None, :], ncap=ntile * gt)
    return _combine(yloc, lslot, h2, norm_final[None], bsz=bsz, seq=seq, tm=tm, lp=lp, srows=srows)
```

```python
import functools
import math

import jax
import jax.numpy as jnp
import numpy as np
from jax import lax
from jax.experimental import pallas as pl
from jax.experimental.pallas import tpu as pltpu

F32 = jnp.float32
BF16 = jnp.bfloat16
I32 = jnp.int32
U32 = jnp.uint32

D_MODEL = 1024
N_META = 16
BLK = 128
PAD = BLK - N_META
N_GROUPS = 64
GROUP = 16
STATE = 64
N_Q_HEADS = 16
N_KV_HEADS = 2
HEAD_DIM = 64
ROT_HALF = 8
ROPE_THETA = 500000.0
N_EXPERTS = 32
TOP_K = 4
D_FF = 1024
SWIGLU_LIMIT = 7.0
SWIGLU_ALPHA = 1.702
RMS_EPS = 1e-5
NEG_INF = -1e30
SSM_STEP = 8
SSM_SEGS = 16
SSM_GBLK = 8
LANES = 128
MOE_BLK = 256
GRAN = 8
MOE_GRANS = MOE_BLK // GRAN
SORT_CHUNK = 256
COMBINE_CHUNK = 256
SORT_W = 512 + LANES
VMEM_LIMIT = 56 * 1024 * 1024


def _cparams(sem):
    return pltpu.CompilerParams(dimension_semantics=sem, vmem_limit_bytes=VMEM_LIMIT)


def _dot(a, b):
    return jnp.dot(a, b, preferred_element_type=F32)


def _dot_nt(a, b):
    return lax.dot_general(a, b, (((1,), (1,)), ((), ())), preferred_element_type=F32)


def _pack_bf16_pairs(x, exact=False):
    w = x.shape[1] // 2
    bits = pltpu.bitcast(x if exact else x.astype(BF16).astype(F32), U32)
    return (bits[:, w:] & jnp.uint32(0xFFFF0000)) | (bits[:, :w] >> 16)


def _unpack_bf16_pairs(wd):
    lo = pltpu.bitcast(wd << 16, F32)
    hi = pltpu.bitcast(wd & jnp.uint32(0xFFFF0000), F32)
    return jnp.concatenate([lo, hi], axis=1)


def _padded_rows(x_refs, meta_ref, first_tile):
    head = jnp.concatenate([jnp.zeros((PAD, D_MODEL), F32), meta_ref[...]], axis=0)
    first = jnp.where(first_tile, head, x_refs[0][...])
    return [first] + [r[...] for r in x_refs[1:]]


def _padded_row_specs(tm, tiles_per_seq):
    per = tm // BLK

    def spec(r):
        return pl.BlockSpec((None, BLK, D_MODEL),
                            lambda i: (i // tiles_per_seq, jnp.maximum((i % tiles_per_seq) * per + r - 1, 0), 0))

    return [spec(r) for r in range(per)]


def _inproj_kernel(*refs, tiles_per_seq, per):
    x_refs, (meta_ref, nrm_ref, w_ref, rc_ref, rs1_ref, rs2_ref,
             u_ref, q_ref, kx_ref, vx_ref, sgs_ref, sga_ref) = refs[:per], refs[per:]
    normed = []
    for x in _padded_rows(x_refs, meta_ref, pl.program_id(0) % tiles_per_seq == 0):
        ms = jnp.mean(x * x, axis=-1, keepdims=True)
        normed.append((x * lax.rsqrt(ms + RMS_EPS) * nrm_ref[...]).astype(BF16))
    hn = jnp.concatenate(normed, axis=0)

    def proj(lo, hi):
        return _dot(hn, w_ref[:, lo:hi])

    u_ref[...] = proj(0, 1024)

    rc = rc_ref[...]
    rs1 = rs1_ref[...]
    rs2 = rs2_ref[...]

    def rope(blk):
        return blk * rc + pltpu.roll(blk, ROT_HALF, 1) * rs1 + pltpu.roll(blk, LANES - ROT_HALF, 1) * rs2

    qf = proj(1024, 2048)
    scale = 1.0 / math.sqrt(HEAD_DIM)
    for j in range(8):
        q_ref[:, LANES * j:LANES * (j + 1)] = (rope(qf[:, LANES * j:LANES * (j + 1)]) * scale).astype(BF16)

    kvf = proj(2048, 2304)
    k = rope(kvf[:, :LANES])
    v = kvf[:, LANES:]
    low = lax.broadcasted_iota(I32, k.shape, 1) < HEAD_DIM

    def expand(t, ref):
        tr = pltpu.roll(t, HEAD_DIM, 1)
        zero = jnp.zeros_like(t)
        ref[:, 0:128] = jnp.where(low, t, zero).astype(BF16)
        ref[:, 128:256] = jnp.where(low, zero, tr).astype(BF16)
        ref[:, 256:384] = jnp.where(low, tr, zero).astype(BF16)
        ref[:, 384:512] = jnp.where(low, zero, t).astype(BF16)

    expand(k, kx_ref)
    expand(v, vx_ref)
    sgs_ref[...] = jax.nn.sigmoid(proj(2304, 3328)).astype(BF16)
    sga_ref[...] = jax.nn.sigmoid(proj(3328, 4352)).astype(BF16)


def _inproj(x, meta, nrm, w_bf, rc, rs1, rs2, *, tm, lp):
    np_ = x.shape[0] * lp
    tiles_per_seq = lp // tm
    per = tm // BLK
    row = lambda i: (i, 0)
    fixed = lambda i: (0, 0)
    rope_idx = lambda i: (i % tiles_per_seq, 0)
    out_shape = (
        jax.ShapeDtypeStruct((np_, 1024), F32),
        jax.ShapeDtypeStruct((np_, 1024), BF16),
        jax.ShapeDtypeStruct((np_, 512), BF16),
        jax.ShapeDtypeStruct((np_, 512), BF16),
        jax.ShapeDtypeStruct((np_, 1024), BF16),
        jax.ShapeDtypeStruct((np_, 1024), BF16),
    )
    return pl.pallas_call(
        functools.partial(_inproj_kernel, tiles_per_seq=tiles_per_seq, per=per),
        out_shape=out_shape,
        grid=(np_ // tm,),
        in_specs=_padded_row_specs(tm, tiles_per_seq) + [
            pl.BlockSpec((N_META, D_MODEL), fixed),
            pl.BlockSpec((1, D_MODEL), fixed),
            pl.BlockSpec(w_bf.shape, fixed),
            pl.BlockSpec((tm, LANES), rope_idx),
            pl.BlockSpec((tm, LANES), rope_idx),
            pl.BlockSpec((tm, LANES), rope_idx),
        ],
        out_specs=(
            pl.BlockSpec((tm, 1024), row),
            pl.BlockSpec((tm, 1024), row),
            pl.BlockSpec((tm, 512), row),
            pl.BlockSpec((tm, 512), row),
            pl.BlockSpec((tm, 1024), row),
            pl.BlockSpec((tm, 1024), row),
        ),
        compiler_params=_cparams(("arbitrary",)),
        name="inproj",
    )(*([x] * per), meta, nrm, w_bf, rc, rs1, rs2)


def _expand_block_diag(dst_ref, row0, t, e_ref, row_shift, col_shift):
    n = t.shape[0]
    step = min(n, 256)
    for r0 in range(0, n, step):
        full = _dot(t[r0:r0 + step, :].astype(BF16), e_ref[...])
        rg = (lax.broadcasted_iota(I32, full.shape, 0) + r0) >> row_shift
        cg = lax.broadcasted_iota(I32, full.shape, 1) >> col_shift
        keep = ((rg ^ cg) & (SSM_GBLK - 1)) == 0
        dst_ref[row0 + r0:row0 + r0 + step, :] = jnp.where(keep, full, 0.0).astype(BF16)


def _ssm_kernel(u_ref, bc_ref, pb_ref, cc_ref, kc_ref, eb_ref, ec_ref, tab_ref, z_ref,
                wb_ref, wcd_ref, ust_ref, bu_ref, yn_ref, *, t8, seg, nchunk):
    rc = t8 // nchunk

    @pl.when(pl.program_id(1) == 0)
    def _():
        bc = bc_ref[...]
        bc_swapped = pltpu.roll(bc, STATE, 1)
        for i in range(SSM_STEP):
            blk = pb_ref[i, :, 0:LANES] * bc + pb_ref[i, :, LANES:2 * LANES] * bc_swapped
            _expand_block_diag(wb_ref, LANES * i, blk, eb_ref, 4, 6)
        c_r, c_i = cc_ref[:, 0:LANES], cc_ref[:, LANES:2 * LANES]
        p_r, p_i = cc_ref[:, 2 * LANES:3 * LANES], cc_ref[:, 3 * LANES:4 * LANES]
        _expand_block_diag(wcd_ref, 0, c_r * p_r - c_i * p_i, ec_ref, 6, 4)
        _expand_block_diag(wcd_ref, 512, -(c_r * p_i + c_i * p_r), ec_ref, 6, 4)
        kc = kc_ref[...]
        lane = lax.broadcasted_iota(I32, kc.shape, 1)
        for ip in range(SSM_STEP):
            blk = kc if ip == 0 else jnp.where(lane >= GROUP * ip, pltpu.roll(kc, GROUP * ip, 1), 0.0)
            _expand_block_diag(wcd_ref, 1024 + LANES * ip, blk, ec_ref, 4, 4)

    for i in range(SSM_STEP):
        ust_ref[:, LANES * i:LANES * (i + 1)] = u_ref[pl.ds(i, t8, stride=SSM_STEP), :].astype(BF16)

    for c in range(nchunk):
        r = _dot(ust_ref[c * rc:(c + 1) * rc, :], wb_ref[...])
        for m in range(8):
            bu_ref[m, c * rc:(c + 1) * rc, :] = r[:, LANES * m:LANES * (m + 1)]

    a_re = [jnp.broadcast_to(tab_ref[0:1, LANES * m:LANES * (m + 1)], (SSM_SEGS, LANES)) for m in range(4)]
    a_im = [jnp.broadcast_to(tab_ref[1:2, LANES * m:LANES * (m + 1)], (SSM_SEGS, LANES)) for m in range(4)]
    s_re = [jnp.broadcast_to(tab_ref[2:3, LANES * m:LANES * (m + 1)], (SSM_SEGS, LANES)) for m in range(4)]
    s_im = [jnp.broadcast_to(tab_ref[3:4, LANES * m:LANES * (m + 1)], (SSM_SEGS, LANES)) for m in range(4)]

    def cmul_add(mr, mi, xr, xi, br, bi):
        return mr * xr - mi * xi + br, mr * xi + mi * xr + bi

    def load(k):
        return [bu_ref[m, pl.ds(k, SSM_SEGS, stride=seg), :] for m in range(8)]

    def advance(st, b):
        nr, ni = [], []
        for m in range(4):
            r_, i_ = cmul_add(a_re[m], a_im[m], st[m], st[4 + m], b[m], b[4 + m])
            nr.append(r_)
            ni.append(i_)
        return tuple(nr + ni)

    zeros = tuple(jnp.zeros((SSM_SEGS, LANES), F32) for _ in range(8))
    fin = lax.fori_loop(0, seg, lambda k, st: advance(st, load(k)), zeros)

    first = lax.broadcasted_iota(I32, (SSM_SEGS, LANES), 0) == 0

    def shift_down(t):
        return jnp.where(first, 0.0, pltpu.roll(t, 1, 0))

    tot = fin
    for _ in range(SSM_SEGS - 1):
        nxt_r, nxt_i = [], []
        for m in range(4):
            r_, i_ = cmul_add(s_re[m], s_im[m], shift_down(tot[m]), shift_down(tot[4 + m]), fin[m], fin[4 + m])
            nxt_r.append(r_)
            nxt_i.append(i_)
        tot = tuple(nxt_r + nxt_i)
    init = tuple(shift_down(t) for t in tot)

    def pass2(k, st):
        b = load(k)
        for m in range(8):
            bu_ref[m, pl.ds(k, SSM_SEGS, stride=seg), :] = st[m]
        return advance(st, b)

    lax.fori_loop(0, seg, pass2, init)

    for c in range(nchunk):
        xp = jnp.concatenate([bu_ref[m, c * rc:(c + 1) * rc, :] for m in range(8)], axis=1).astype(BF16)
        lhs = jnp.concatenate([xp, ust_ref[c * rc:(c + 1) * rc, :]], axis=1)
        y = _dot(lhs, wcd_ref[...])
        for i in range(SSM_STEP):
            yn_ref[pl.ds(SSM_STEP * c * rc + i, rc, stride=SSM_STEP), :] = y[:, LANES * i:LANES * (i + 1)]

    y = yn_ref[...] + tab_ref[4:5, 0:LANES] * u_ref[...]
    z_ref[...] = jax.nn.gelu(y).astype(BF16)


def _ssm(u, bc, pb, cc, kc, tab, *, bsz, lp, nchunk):
    np_ = u.shape[0]
    t8 = lp // SSM_STEP
    seg = t8 // SSM_SEGS
    nblk = D_MODEL // LANES
    col = jnp.arange(1024)
    src_b = (col >> 9) * STATE + (col & (STATE - 1))
    src_c = (col >> 7) * GROUP + (col & (GROUP - 1))
    eb = (jnp.arange(LANES)[:, None] == src_b[None, :]).astype(BF16)
    ec = (jnp.arange(LANES)[:, None] == src_c[None, :]).astype(BF16)
    kern = functools.partial(_ssm_kernel, t8=t8, seg=seg, nchunk=nchunk)
    per_q = lambda q, b: (q, 0, 0)
    fixed = lambda q, b: (0, 0)
    return pl.pallas_call(
        kern,
        out_shape=jax.ShapeDtypeStruct((np_, D_MODEL), BF16),
        grid=(nblk, bsz),
        in_specs=[
            pl.BlockSpec((lp, LANES), lambda q, b: (b, q)),
            pl.BlockSpec((None, LANES, LANES), per_q),
            pl.BlockSpec((SSM_STEP, None, LANES, 2 * LANES), lambda q, b: (0, q, 0, 0)),
            pl.BlockSpec((None, SSM_GBLK * STATE, 4 * LANES), per_q),
            pl.BlockSpec((None, LANES, LANES), per_q),
            pl.BlockSpec((LANES, 1024), fixed),
            pl.BlockSpec((LANES, 1024), fixed),
            pl.BlockSpec((None, 8, 512), per_q),
        ],
        out_specs=pl.BlockSpec((lp, LANES), lambda q, b: (b, q)),
        scratch_shapes=[
            pltpu.VMEM((1024, 1024), BF16),
            pltpu.VMEM((2048, 1024), BF16),
            pltpu.VMEM((t8, 1024), BF16),
            pltpu.VMEM((8, t8, LANES), F32),
            pltpu.VMEM((lp, LANES), F32),
        ],
        compiler_params=_cparams(("arbitrary", "arbitrary")),
        name="ssm",
    )(u, bc, pb, cc, kc, eb, ec, tab)


def _ssm_tables(lam_re, lam_im, log_dt, b_re, b_im, c_re, c_im, d, seg):
    hi = lax.Precision.HIGHEST
    lr, li = lam_re.astype(F32), lam_im.astype(F32)
    dt = jnp.exp(log_dt.astype(F32))[:, None]
    mag = jnp.exp(dt * lr)
    ar, ai = mag * jnp.cos(dt * li), mag * jnp.sin(dt * li)
    den = lr * lr + li * li
    nr, ni = ar - 1.0, ai
    fr, fi = (nr * lr + ni * li) / den, (ni * lr - nr * li) / den
    br, bi = b_re.astype(F32), b_im.astype(F32)
    bbr = fr[..., None] * br - fi[..., None] * bi
    bbi = fr[..., None] * bi + fi[..., None] * br

    def cmul(xr, xi, yr, yi):
        return xr * yr - xi * yi, xr * yi + xi * yr

    pr, pi = [jnp.ones_like(ar)], [jnp.zeros_like(ar)]
    for _ in range(SSM_STEP):
        r_, i_ = cmul(pr[-1], pi[-1], ar, ai)
        pr.append(r_)
        pi.append(i_)
    pw_r, pw_i = jnp.stack(pr), jnp.stack(pi)
    a8r, a8i = pw_r[SSM_STEP], pw_i[SSM_STEP]
    sr, si = jnp.ones_like(ar), jnp.zeros_like(ar)
    qr, qi = a8r, a8i
    e = seg
    while e:
        if e & 1:
            sr, si = cmul(sr, si, qr, qi)
        qr, qi = cmul(qr, qi, qr, qi)
        e >>= 1

    cr, ci = c_re.astype(F32), c_im.astype(F32)
    nb = N_GROUPS // SSM_GBLK

    bc = jnp.concatenate([jnp.swapaxes(bbr, 1, 2), jnp.swapaxes(bbi, 1, 2)], axis=2).reshape(nb, LANES, LANES)
    pb = jnp.stack([jnp.concatenate([pr[SSM_STEP - 1 - i]] * 2 + [-pi[SSM_STEP - 1 - i], pi[SSM_STEP - 1 - i]], axis=1)
                    for i in range(SSM_STEP)])
    pb = jnp.broadcast_to(pb[:, :, None, :], (SSM_STEP, N_GROUPS, GROUP, 2 * LANES))
    pb = pb.reshape(SSM_STEP, nb, LANES, 2 * LANES)
    crt = jnp.tile(jnp.swapaxes(cr, 1, 2), (1, 1, SSM_STEP))
    cit = jnp.tile(jnp.swapaxes(ci, 1, 2), (1, 1, SSM_STEP))
    prc = jnp.repeat(jnp.stack(pr[1:], axis=2), GROUP, axis=2)
    pic = jnp.repeat(jnp.stack(pi[1:], axis=2), GROUP, axis=2)
    cc = jnp.concatenate([crt, cit, prc, pic], axis=2).reshape(nb, SSM_GBLK * STATE, 4 * LANES)
    tr, ti = cmul(pw_r[:SSM_STEP, :, :, None], pw_i[:SSM_STEP, :, :, None], bbr[None], bbi[None])
    kt = (jnp.einsum('ghp,tgpk->gkth', cr, tr, precision=hi)
          - jnp.einsum('ghp,tgpk->gkth', ci, ti, precision=hi))
    kc = kt.reshape(nb, LANES, LANES)

    def lanes(t):
        return t.reshape(nb, SSM_GBLK * STATE)

    dl = jnp.pad(d.astype(F32).reshape(nb, SSM_GBLK * GROUP), ((0, 0), (0, 512 - LANES)))
    zero = jnp.zeros((nb, 512), F32)
    tab = jnp.stack([lanes(a8r), lanes(a8i), lanes(sr), lanes(si), dl, zero, zero, zero], axis=1)
    return bc, pb, cc, kc, tab


def _attn_kernel(q_ref, kc_ref, kp_ref, km_ref, vc_ref, vp_ref, vm_ref, sink_ref, o_ref):
    n = pl.program_id(1)
    nk = 3 * BLK
    row = lax.broadcasted_iota(I32, (BLK, nk), 0)
    col = lax.broadcasted_iota(I32, (BLK, nk), 1)
    band = (col > row) & (col <= row + BLK) & (col >= 2 * BLK - BLK * n)
    mrow = col - 2 * BLK
    meta = (mrow >= PAD) & (mrow <= BLK * n + row)
    bias = jnp.where(band | meta, 0.0, NEG_INF)

    for g in range(N_KV_HEADS):
        qg = jnp.concatenate([q_ref[:, LANES * (4 * g + jj):LANES * (4 * g + jj + 1)] for jj in range(4)], axis=0)
        acc = None
        for par in range(2):
            sl = slice(LANES * (2 * g + par), LANES * (2 * g + par + 1))
            kk = jnp.concatenate([kp_ref[:, sl], kc_ref[:, sl], km_ref[:, sl]], axis=0)
            vv = jnp.concatenate([vp_ref[:, sl], vc_ref[:, sl], vm_ref[:, sl]], axis=0)
            s_all = _dot_nt(qg, kk)
            probs, rden = [], []
            for jj in range(4):
                sink = sink_ref[8 * g + 2 * jj + par]
                s = s_all[BLK * jj:BLK * (jj + 1), :] + bias
                mx = jnp.maximum(jnp.max(s, axis=1, keepdims=True), sink)
                p = jnp.exp(s - mx)
                den = jnp.sum(p, axis=1, keepdims=True) + jnp.exp(sink - mx)
                probs.append(p.astype(BF16))
                rden.append(jnp.broadcast_to(1.0 / den, (BLK, LANES)))
            o = _dot(jnp.concatenate(probs, axis=0), vv) * jnp.concatenate(rden, axis=0)
            acc = o if acc is None else acc + o
        for jj in range(4):
            o_ref[:, LANES * (4 * g + jj):LANES * (4 * g + jj + 1)] = acc[BLK * jj:BLK * (jj + 1), :].astype(BF16)


def _attention(q, kx, vx, sinks, *, bsz, nb):
    np_ = q.shape[0]
    cur = lambda b, n: (b * nb + n, 0)
    prev = lambda b, n: (b * nb + jnp.maximum(n - 1, 0), 0)
    first = lambda b, n: (b * nb, 0)
    return pl.pallas_call(
        _attn_kernel,
        out_shape=jax.ShapeDtypeStruct((np_, 1024), BF16),
        grid=(bsz, nb),
        in_specs=[
            pl.BlockSpec((BLK, 1024), cur),
            pl.BlockSpec((BLK, 512), cur),
            pl.BlockSpec((BLK, 512), prev),
            pl.BlockSpec((BLK, 512), first),
            pl.BlockSpec((BLK, 512), cur),
            pl.BlockSpec((BLK, 512), prev),
            pl.BlockSpec((BLK, 512), first),
            pl.BlockSpec(memory_space=pltpu.SMEM),
        ],
        out_specs=pl.BlockSpec((BLK, 1024), cur),
        compiler_params=_cparams(("arbitrary", "arbitrary")),
        name="attn",
    )(q, kx, kx, kx, vx, vx, vx, sinks)


def _post_kernel(*refs, tiles_per_seq, per):
    x_refs, (meta_ref, z_ref, at_ref, sgs_ref, sga_ref, wglu_ref, bglu_ref, wbs_ref, wba_ref, wout_ref,
             nffn_ref, wrh_ref, wrl_ref, br_ref,
             h2_ref, hnp_ref, lslot_ref, ecol_ref, cnt_ref) = refs[:per], refs[per:]
    i = pl.program_id(0)

    z = z_ref[...]
    t = _dot(z, wglu_ref[...]) + bglu_ref[...]
    so = (z.astype(F32) * jax.nn.sigmoid(t)).astype(BF16)
    mix = (sgs_ref[...].astype(F32) * _dot(so, wbs_ref[...])
           + sga_ref[...].astype(F32) * _dot(at_ref[...], wba_ref[...]))
    mixed = _dot(mix.astype(BF16), wout_ref[...])
    logit_blocks = []
    for r, xr in enumerate(_padded_rows(x_refs, meta_ref, i % tiles_per_seq == 0)):
        rows = slice(BLK * r, BLK * (r + 1))
        h2 = xr + mixed[rows, :]
        h2_ref[rows, :] = h2
        ms = jnp.mean(h2 * h2, axis=-1, keepdims=True)
        hn = h2 * lax.rsqrt(ms + RMS_EPS) * nffn_ref[...]
        hb = hn.astype(BF16)
        hnp_ref[rows, :] = hb
        lo = (hn - hb.astype(F32)).astype(BF16)
        logit_blocks.append(_dot(hb, wrh_ref[...]) + _dot(lo, wrh_ref[...]) + _dot(hb, wrl_ref[...]) + br_ref[...])
    logits = jnp.concatenate(logit_blocks, axis=0)

    tm = logits.shape[0]
    lane = lax.broadcasted_iota(I32, (tm, LANES), 1)
    lane_f = lane.astype(F32)
    work = logits
    vals, hots, idxs = [], [], []
    for _ in range(TOP_K):
        mx = jnp.max(work, axis=1, keepdims=True)
        idx = jnp.min(jnp.where(work == mx, lane_f, float(LANES)), axis=1, keepdims=True)
        hot = lane_f == idx
        vals.append(mx)
        idxs.append(idx)
        hots.append(hot)
        work = jnp.where(hot, -jnp.inf, work)

    ex = [jnp.exp(v - vals[0]) for v in vals]
    tot = ex[0] + ex[1] + ex[2] + ex[3]
    gates = [e / tot for e in ex]

    onehot = jnp.zeros((tm, LANES), F32)
    for hot in hots:
        onehot = onehot + jnp.where(hot, 1.0, 0.0)
    r_i = lax.broadcasted_iota(I32, (tm, tm), 0)
    c_i = lax.broadcasted_iota(I32, (tm, tm), 1)
    tri = jnp.where(r_i > c_i, 1.0, 0.0).astype(BF16)
    before = _dot(tri, onehot.astype(BF16))
    count = jnp.sum(onehot, axis=0, keepdims=True)
    gran = jnp.floor((count + (GRAN - 1)) * (1.0 / GRAN))
    e_r = lax.broadcasted_iota(I32, (LANES, LANES), 0)
    e_c = lax.broadcasted_iota(I32, (LANES, LANES), 1)
    upper = jnp.where(e_r < e_c, 1.0, 0.0).astype(BF16)
    gstart = _dot(jnp.broadcast_to(gran, (8, LANES)).astype(BF16), upper)[0:1, :]
    base = before + gstart * float(GRAN)
    slots = [jnp.sum(jnp.where(hot, base, 0.0), axis=1, keepdims=True) for hot in hots]
    cnt_ref[...] = jnp.broadcast_to(gran, cnt_ref.shape)

    def spread(cols, dtype):
        out = jnp.zeros((tm, LANES), dtype)
        for k, cval in enumerate(cols):
            out = jnp.where(lane == k, cval, out)
        return out

    lslot_ref[...] = spread(slots, F32)
    cols = []
    for g in gates:
        g1 = g.astype(BF16).astype(F32)
        g2 = (g - g1).astype(BF16).astype(F32)
        cols += [g1, g2, g - g1 - g2]
    for s in slots:
        hi = jnp.floor(s * (1.0 / 64.0))
        cols += [hi, s - 64.0 * hi]
    ecol_ref[...] = spread(cols, F32)


def _post(x, meta, z, attn, sgs, sga, wglu, bglu, wbs, wba, wout, nffn, wrh, wrl, br, *, tm, lp):
    np_ = z.shape[0]
    tiles_per_seq = lp // tm
    per = tm // BLK
    row = lambda i: (i, 0)
    fixed = lambda i: (0, 0)
    sq = pl.BlockSpec((1024, 1024), fixed)
    vec = pl.BlockSpec((1, 1024), fixed)
    out_shape = (
        jax.ShapeDtypeStruct((np_, 1024), F32),
        jax.ShapeDtypeStruct((np_, 1024), BF16),
        jax.ShapeDtypeStruct((np_, LANES), F32),
        jax.ShapeDtypeStruct((np_, LANES), F32),
        jax.ShapeDtypeStruct((np_ // tm * 8, LANES), F32),
    )
    return pl.pallas_call(
        functools.partial(_post_kernel, tiles_per_seq=tiles_per_seq, per=per),
        out_shape=out_shape,
        grid=(np_ // tm,),
        in_specs=_padded_row_specs(tm, tiles_per_seq) + [
            pl.BlockSpec((N_META, D_MODEL), fixed),
            pl.BlockSpec((tm, 1024), row), pl.BlockSpec((tm, 1024), row),
            pl.BlockSpec((tm, 1024), row), pl.BlockSpec((tm, 1024), row),
            sq, vec, sq, sq, sq, vec,
            pl.BlockSpec((1024, LANES), fixed), pl.BlockSpec((1024, LANES), fixed),
            pl.BlockSpec((1, LANES), fixed),
        ],
        out_specs=(
            pl.BlockSpec((tm, 1024), row), pl.BlockSpec((tm, 1024), row),
            pl.BlockSpec((tm, LANES), row), pl.BlockSpec((tm, LANES), row),
            pl.BlockSpec((8, LANES), row),
        ),
        compiler_params=_cparams(("arbitrary",)),
        name="post",
    )(*([x] * per), meta, z, attn, sgs, sga, wglu, bglu, wbs, wba, wout, nffn, wrh, wrl, br)


def _sort_kernel(hnp_ref, ecol_ref, xloc_ref, *, tm, srows):
    ecol = ecol_ref[...].astype(BF16)
    pay = jnp.concatenate([hnp_ref[...], ecol], axis=1)

    r8 = lax.broadcasted_iota(I32, (8, LANES), 0)
    l8 = lax.broadcasted_iota(I32, (8, LANES), 1)
    digit_w = jnp.where(l8 == 12 + 2 * r8, 64.0, jnp.where(l8 == 13 + 2 * r8, 1.0, 0.0)).astype(BF16)
    slot_rows = _dot_nt(digit_w, ecol)

    rr = lax.broadcasted_iota(I32, (LANES, LANES), 0)
    cc = lax.broadcasted_iota(I32, (LANES, LANES), 1)
    kk = jnp.where(cc < 3 * TOP_K, cc // 3, -LANES)
    spread_w = jnp.where(rr == 12 + 2 * kk, 64.0, jnp.where(rr == 13 + 2 * kk, 1.0, 0.0)).astype(BF16)
    gate_lane = lax.broadcasted_iota(I32, (SORT_CHUNK, LANES), 1) < 3 * TOP_K

    for c in range(srows // SORT_CHUNK):
        s_id = (lax.broadcasted_iota(I32, (SORT_CHUNK, tm), 0) + c * SORT_CHUNK).astype(F32)
        perm = jnp.zeros((SORT_CHUNK, tm), F32)
        for k in range(TOP_K):
            perm = perm + jnp.where(slot_rows[k:k + 1, :] == s_id, 1.0, 0.0)
        got = _dot(perm.astype(BF16), pay)
        rows = slice(c * SORT_CHUNK, (c + 1) * SORT_CHUNK)
        xloc_ref[rows, 0:512] = _pack_bf16_pairs(got[:, :D_MODEL], exact=True)
        extra = got[:, D_MODEL:]
        owner = _dot(extra.astype(BF16), spread_w)
        me = (lax.broadcasted_iota(I32, (SORT_CHUNK, LANES), 0) + c * SORT_CHUNK).astype(F32)
        mine = jnp.where(gate_lane, owner, -1.0) == me
        gate = jnp.sum(jnp.where(mine, extra, 0.0), axis=1, keepdims=True)
        xloc_ref[rows, 512:SORT_W] = pltpu.bitcast(jnp.broadcast_to(gate, (SORT_CHUNK, LANES)), U32)


def _sort(hnp, ecol, *, tm, srows):
    np_ = hnp.shape[0]
    ntile = np_ // tm
    row = lambda i: (i, 0)
    return pl.pallas_call(
        functools.partial(_sort_kernel, tm=tm, srows=srows),
        out_shape=jax.ShapeDtypeStruct((ntile * srows, SORT_W), U32),
        grid=(ntile,),
        in_specs=[pl.BlockSpec((tm, D_MODEL), row), pl.BlockSpec((tm, LANES), row)],
        out_specs=pl.BlockSpec((srows, SORT_W), row),
        compiler_params=_cparams(("arbitrary",)),
        name="sort",
    )(hnp, ecol)


def _moe_kernel(blk_ref, exp_ref, lo_ref, hi_ref, kind_ref, wslot_ref, nexte_ref, gsrc_ref,
                xloc_ref, wgu_ref, bgu_ref, wd_ref, bd_ref, yloc_ref,
                xbuf, ybuf, sem_in, sem_out, wgu_f, wd_f, sem_w, wgu_bf, wd_bf, *, nseg, dump0):
    i = pl.program_id(0)
    prev = jnp.maximum(i - 1, 0)
    nxt = jnp.minimum(i + 1, nseg - 1)
    slot = i % 2
    new_expert = (i == 0) | (exp_ref[i] != exp_ref[prev])

    def gather(seg, dst_slot, start):
        b = blk_ref[seg]
        for g in range(MOE_GRANS):
            src = pl.multiple_of(gsrc_ref[b * MOE_GRANS + g] * GRAN, GRAN)
            cp = pltpu.make_async_copy(xloc_ref.at[pl.ds(src, GRAN)], xbuf.at[dst_slot, pl.ds(g * GRAN, GRAN)],
                                       sem_in.at[dst_slot])
            if start:
                cp.start()
            else:
                cp.wait()

    def scatter(seg, src_slot, start):
        b = blk_ref[seg]
        lo = lo_ref[seg]
        hi = hi_ref[seg]
        for g in range(MOE_GRANS):
            keep = (g >= lo) & (g < hi)
            spare = dump0 + src_slot * MOE_GRANS + g
            dst = pl.multiple_of(jnp.where(keep, gsrc_ref[b * MOE_GRANS + g], spare) * GRAN, GRAN)
            cp = pltpu.make_async_copy(ybuf.at[src_slot, pl.ds(g * GRAN, GRAN)], yloc_ref.at[pl.ds(dst, GRAN)],
                                       sem_out.at[src_slot])
            if start:
                cp.start()
            else:
                cp.wait()

    @pl.when(i == 0)
    def _():
        gather(i, slot, True)
        ybuf[1 - slot] = jnp.zeros((MOE_BLK, 512), U32)
        for g in range(MOE_GRANS):
            pltpu.make_async_copy(ybuf.at[1 - slot, pl.ds(g * GRAN, GRAN)],
                                  yloc_ref.at[pl.ds((dump0 + (1 - slot) * MOE_GRANS + g) * GRAN, GRAN)],
                                  sem_out.at[1 - slot]).start()

    gather(i, slot, False)

    def fetch(e, slot, start):
        for src, dst in ((wgu_ref, wgu_f), (wd_ref, wd_f)):
            cp = pltpu.make_async_copy(src.at[e], dst.at[slot], sem_w.at[slot])
            if start:
                cp.start(priority=1)
            else:
                cp.wait()

    wslot = wslot_ref[i]

    @pl.when(i == 0)
    def _():
        fetch(exp_ref[i], wslot, True)

    @pl.when(new_expert)
    def _():
        fetch(exp_ref[i], wslot, False)
        wgu_bf[...] = wgu_f[wslot].astype(BF16)
        wd_bf[...] = wd_f[wslot].astype(BF16)

    @pl.when(new_expert & (nexte_ref[i] >= 0))
    def _():
        fetch(nexte_ref[i], 1 - wslot, True)

    kind = kind_ref[i]

    @pl.when(kind == 1)
    def _():
        gather(nxt, 1 - slot, True)
        xw = xbuf[slot]
        xb = _unpack_bf16_pairs(xw[:, 0:512]).astype(BF16)
        gate = pltpu.bitcast(xw[:, 512:SORT_W], F32)
        gu = _dot(xb, wgu_bf[...]) + bgu_ref[...]
        g = jnp.minimum(gu[:, :D_FF], SWIGLU_LIMIT)
        up = jnp.clip(gu[:, D_FF:], -SWIGLU_LIMIT, SWIGLU_LIMIT)
        hid = g * jax.nn.sigmoid(SWIGLU_ALPHA * g) * (up + 1.0)
        y = _dot(hid.astype(BF16), wd_bf[...]) + bd_ref[...]
        y = y * jnp.concatenate([gate] * (D_MODEL // LANES), axis=1)
        ybuf[slot] = _pack_bf16_pairs(y)
        scatter(i, slot, True)
        scatter(prev, 1 - slot, False)

    @pl.when(kind != 1)
    def _():
        gather(nxt, 1 - slot, True)
        ybuf[slot] = jnp.zeros((MOE_BLK, 512), U32)
        scatter(i, slot, True)
        scatter(prev, 1 - slot, False)

    @pl.when(i == nseg - 1)
    def _():
        scatter(i, slot, False)
        gather(nxt, 1 - slot, False)


def _moe(blk, exp, lo, hi, kind, gsrc, xloc, wgu, bgu, wd, bd, *, ncap):
    nseg = blk.shape[0]
    dump0 = ncap
    change = jnp.concatenate([jnp.zeros((1,), I32), (exp[1:] != exp[:-1]).astype(I32)])
    wslot = jnp.cumsum(change) % 2
    later = jnp.where(exp[None, :] > exp[:, None], exp[None, :], N_EXPERTS)
    nexte = jnp.min(later, axis=1)
    nexte = jnp.where(nexte >= N_EXPERTS, -1, nexte)
    wmap = lambda i, b, e, *_: (e[i], 0, 0)
    grid_spec = pltpu.PrefetchScalarGridSpec(
        num_scalar_prefetch=8,
        grid=(nseg,),
        in_specs=[
            pl.BlockSpec(memory_space=pl.ANY),
            pl.BlockSpec(memory_space=pl.ANY),
            pl.BlockSpec((None, 1, 2 * D_FF), wmap),
            pl.BlockSpec(memory_space=pl.ANY),
            pl.BlockSpec((None, 1, D_MODEL), wmap),
        ],
        out_specs=pl.BlockSpec(memory_space=pl.ANY),
        scratch_shapes=[
            pltpu.VMEM((2, MOE_BLK, SORT_W), U32),
            pltpu.VMEM((2, MOE_BLK, 512), U32),
            pltpu.SemaphoreType.DMA((2,)),
            pltpu.SemaphoreType.DMA((2,)),
            pltpu.VMEM((2, D_MODEL, 2 * D_FF), F32),
            pltpu.VMEM((2, D_FF, D_MODEL), F32),
            pltpu.SemaphoreType.DMA((2,)),
            pltpu.VMEM((D_MODEL, 2 * D_FF), BF16),
            pltpu.VMEM((D_FF, D_MODEL), BF16),
        ],
    )
    return pl.pallas_call(
        functools.partial(_moe_kernel, nseg=nseg, dump0=dump0),
        out_shape=jax.ShapeDtypeStruct(((ncap + 2 * MOE_GRANS) * GRAN, 512), U32),
        grid_spec=grid_spec,
        compiler_params=_cparams(("arbitrary",)),
        name="moe",
    )(blk, exp, lo, hi, kind, wslot.astype(I32), nexte.astype(I32), gsrc, xloc, wgu, bgu, wd, bd)


def _combine_kernel(y_ref, lslot_ref, h2_ref, nfin_ref, o_ref, obuf, sem, *, tm, srows, tiles_per_seq):
    i = pl.program_id(0)
    b = i // tiles_per_seq
    j = i % tiles_per_seq
    per = tm // BLK
    slots = lslot_ref[...]
    acc = h2_ref[...]
    for c in range(srows // COMBINE_CHUNK):
        s_id = (lax.broadcasted_iota(I32, (tm, COMBINE_CHUNK), 1) + c * COMBINE_CHUNK).astype(F32)
        take = jnp.zeros((tm, COMBINE_CHUNK), F32)
        for k in range(TOP_K):
            take = take + jnp.where(slots[:, k:k + 1] == s_id, 1.0, 0.0)
        rows = _unpack_bf16_pairs(y_ref[c * COMBINE_CHUNK:(c + 1) * COMBINE_CHUNK, :]).astype(BF16)
        acc = acc + _dot(take.astype(BF16), rows)
    ms = jnp.mean(acc * acc, axis=-1, keepdims=True)
    obuf[...] = acc * lax.rsqrt(ms + RMS_EPS) * nfin_ref[...]

    def copies(start):
        for r in range(per):
            n = j * per + r
            cp = pltpu.make_async_copy(obuf.at[pl.ds(r * BLK, BLK)],
                                       o_ref.at[b, pl.ds(pl.multiple_of(jnp.maximum(n - 1, 0) * BLK, BLK), BLK)], sem)

            def go():
                if start:
                    cp.start()
                else:
                    cp.wait()

            if r == 0:
                pl.when(j > 0)(go)
            else:
                go()

    copies(True)
    copies(False)


def _combine(yloc, lslot, h2, nfin, *, bsz, seq, tm, lp, srows):
    np_ = h2.shape[0]
    tiles_per_seq = lp // tm
    row = lambda i: (i, 0)
    return pl.pallas_call(
        functools.partial(_combine_kernel, tm=tm, srows=srows, tiles_per_seq=tiles_per_seq),
        out_shape=jax.ShapeDtypeStruct((bsz, seq, D_MODEL), F32),
        grid=(np_ // tm,),
        in_specs=[
            pl.BlockSpec((srows, 512), row),
            pl.BlockSpec((tm, LANES), row),
            pl.BlockSpec((tm, D_MODEL), row),
            pl.BlockSpec((1, D_MODEL), lambda i: (0, 0)),
        ],
        out_specs=pl.BlockSpec(memory_space=pl.ANY),
        scratch_shapes=[pltpu.VMEM((tm, D_MODEL), F32), pltpu.SemaphoreType.DMA],
        compiler_params=_cparams(("arbitrary",)),
        name="combine",
    )(yloc, lslot, h2, nfin)


def _rope_tables(lp):
    pos = np.arange(lp, dtype=np.float32) - np.float32(PAD)
    inv_freq = np.float32(ROPE_THETA) ** (-np.arange(0, 2 * ROT_HALF, 2, dtype=np.float32) / np.float32(2 * ROT_HALF))
    ang = (pos[:, None] * inv_freq[None, :]).astype(np.float32)
    cos, sin = np.cos(ang), np.sin(ang)
    ones = np.ones((lp, HEAD_DIM - 2 * ROT_HALF), np.float32)
    zeros8 = np.zeros((lp, ROT_HALF), np.float32)
    zrest = np.zeros((lp, HEAD_DIM - 2 * ROT_HALF), np.float32)
    c = np.concatenate([cos, cos, ones], axis=1)
    s1 = np.concatenate([zeros8, sin, zrest], axis=1)
    s2 = np.concatenate([-sin, zeros8, zrest], axis=1)
    tile = lambda t: jnp.asarray(np.concatenate([t, t], axis=1), F32)
    return tile(c), tile(s1), tile(s2)


def _route_tables(grans, gt):
    ntile = grans.shape[0]
    ncap = ntile * gt
    nown = N_EXPERTS + 1
    npos = ncap + nown * MOE_GRANS
    nblocks = npos // MOE_GRANS
    used = jnp.sum(grans, axis=1)
    lstart = jnp.cumsum(grans, axis=1) - grans
    tile0 = jnp.arange(ntile, dtype=I32)[:, None] * gt
    lens = jnp.concatenate([grans.T, (gt - used)[None, :]], axis=0)
    srcs = jnp.concatenate([(tile0 + lstart).T, (tile0[:, 0] + used)[None, :]], axis=0)
    total = jnp.sum(lens, axis=1)
    room = (total + MOE_GRANS - 1) // MOE_GRANS * MOE_GRANS
    oend = jnp.cumsum(room)
    ostart = oend - room
    run_start = (ostart[:, None] + jnp.cumsum(lens, axis=1) - lens).reshape(-1)
    off = srcs.reshape(-1) - run_start
    delta = off - jnp.concatenate([jnp.zeros((1,), I32), off[:-1]])
    pos = jnp.arange(npos, dtype=I32)
    gsrc = pos + jnp.sum(jnp.where(run_start[None, :] <= pos[:, None], delta[None, :], 0), axis=1)

    first = jnp.arange(nblocks, dtype=I32) * MOE_GRANS
    owner = jnp.sum((oend[None, :] <= first[:, None]).astype(I32), axis=1)
    mine = owner[:, None] == jnp.arange(nown, dtype=I32)[None, :]
    valid_end = jnp.sum(jnp.where(mine, (ostart + total)[None, :], 0), axis=1)
    hi = jnp.clip(valid_end - first, 0, MOE_GRANS)
    lo = jnp.zeros_like(hi)
    gsrc = jnp.where((pos % MOE_GRANS) < jnp.repeat(hi, MOE_GRANS), gsrc, 0)
    kind = jnp.where(hi > 0, jnp.where(owner >= N_EXPERTS, 2, 1), 0)
    works = kind == 1
    first_work = jnp.min(jnp.where(works, owner, N_EXPERTS - 1))
    exp = jnp.maximum(lax.cummax(jnp.where(works, owner, -1), axis=0), first_work)
    blk = jnp.arange(nblocks, dtype=I32)
    return gsrc.astype(I32), blk, exp.astype(I32), lo.astype(I32), hi.astype(I32), kind.astype(I32)


def kernel(x, meta_tokens, norm_mix, w_in, ssm_lam_re, ssm_lam_im, ssm_log_dt, ssm_b_re, ssm_b_im,
           ssm_c_re, ssm_c_im, ssm_d, w_glu, b_glu, attn_sinks, w_br_ssm, w_br_attn, w_out,
           norm_ffn, w_router, b_router, w_gate_up, b_gate_up, w_down, b_down, norm_final):
    bsz, seq, _ = x.shape
    lp = seq + BLK
    nb = lp // BLK
    np_ = bsz * lp
    tm = 640 if lp % 640 == 0 else BLK
    t8 = lp // SSM_STEP
    seg = t8 // SSM_SEGS
    nchunk = 5 if (t8 % 5 == 0 and (t8 // 5) % 16 == 0) else 1

    meta = meta_tokens.astype(F32)
    rc, rs1, rs2 = _rope_tables(lp)
    u, q, kx, vx, sgs, sga = _inproj(x, meta, norm_mix[0][None], w_in[0].astype(BF16), rc, rs1, rs2, tm=tm, lp=lp)

    bc, pb, cc, kc, tab = _ssm_tables(ssm_lam_re[0], ssm_lam_im[0], ssm_log_dt[0], ssm_b_re[0], ssm_b_im[0],
                                      ssm_c_re[0], ssm_c_im[0], ssm_d[0], seg)
    z = _ssm(u, bc, pb, cc, kc, tab, bsz=bsz, lp=lp, nchunk=nchunk)

    attn = _attention(q, kx, vx, attn_sinks[0].astype(F32), bsz=bsz, nb=nb)

    wr = jnp.pad(w_router[0].astype(F32), ((0, 0), (0, LANES - N_EXPERTS)))
    wrh = wr.astype(BF16)
    wrl = (wr - wrh.astype(F32)).astype(BF16)
    br = jnp.concatenate([b_router[0].astype(F32), jnp.full((LANES - N_EXPERTS,), NEG_INF, F32)])[None]
    h2, hnp, lslot, ecol, cnt = _post(
        x, meta, z, attn, sgs, sga, w_glu[0].astype(BF16), b_glu[0][None], w_br_ssm[0].astype(BF16),
        w_br_attn[0].astype(BF16), w_out[0].astype(BF16), norm_ffn[0][None], wrh, wrl, br, tm=tm, lp=lp)

    unit = math.lcm(SORT_CHUNK, COMBINE_CHUNK)
    srows = -(-(tm * TOP_K + N_EXPERTS * (GRAN - 1)) // unit) * unit
    gt = srows // GRAN
    ntile = np_ // tm
    grans = cnt.reshape(ntile, 8, LANES)[:, 0, :N_EXPERTS].astype(I32)
    gsrc, sblk, sexp, slo, shi, skind = _route_tables(grans, gt)

    xloc = _sort(hnp, ecol, tm=tm, srows=srows)
    yloc = _moe(sblk, sexp, slo, shi, skind, gsrc, xloc, w_gate_up[0], b_gate_up[0][:, None, :],
                w_down[0], b_down[0][:, None, :], ncap=ntile * gt)
    return _combine(yloc, lslot, h2, norm_final[None], bsz=bsz, seq=seq, tm=tm, lp=lp, srows=srows)
```

```python
import functools
import math

import jax
import jax.numpy as jnp
import numpy as np
from jax import lax
from jax.experimental import pallas as pl
from jax.experimental.pallas import tpu as pltpu

F32 = jnp.float32
BF16 = jnp.bfloat16
I32 = jnp.int32
U32 = jnp.uint32

D_MODEL = 1024
N_META = 16
BLK = 128
PAD = BLK - N_META
N_GROUPS = 64
GROUP = 16
STATE = 64
N_Q_HEADS = 16
N_KV_HEADS = 2
HEAD_DIM = 64
ROT_HALF = 8
ROPE_THETA = 500000.0
N_EXPERTS = 32
TOP_K = 4
D_FF = 1024
SWIGLU_LIMIT = 7.0
SWIGLU_ALPHA = 1.702
RMS_EPS = 1e-5
NEG_INF = -1e30
LOG2_E = 1.4426950408889634
SSM_STEP = 8
SSM_SEGS = 16
SSM_GBLK = 8
LANES = 128
MOE_BLK = 256
GRAN = 8
MOE_GRANS = MOE_BLK // GRAN
SORT_CHUNK = 256
COMBINE_CHUNK = 256
SORT_W = 512 + LANES
VMEM_LIMIT = 56 * 1024 * 1024


def _cparams(sem):
    return pltpu.CompilerParams(dimension_semantics=sem, vmem_limit_bytes=VMEM_LIMIT)


def _dot(a, b):
    return jnp.dot(a, b, preferred_element_type=F32)


def _dot_nt(a, b):
    return lax.dot_general(a, b, (((1,), (1,)), ((), ())), preferred_element_type=F32)


def _pack_bf16_pairs(x, exact=False):
    w = x.shape[1] // 2
    bits = pltpu.bitcast(x if exact else x.astype(BF16).astype(F32), U32)
    return (bits[:, w:] & jnp.uint32(0xFFFF0000)) | (bits[:, :w] >> 16)


def _unpack_bf16_pairs(wd):
    lo = pltpu.bitcast(wd << 16, F32)
    hi = pltpu.bitcast(wd & jnp.uint32(0xFFFF0000), F32)
    return jnp.concatenate([lo, hi], axis=1)


def _padded_rows(x_refs, meta_ref, first_tile):
    head = jnp.concatenate([jnp.zeros((PAD, D_MODEL), F32), meta_ref[...]], axis=0)
    first = jnp.where(first_tile, head, x_refs[0][...])
    return [first] + [r[...] for r in x_refs[1:]]


def _padded_row_specs(tm, tiles_per_seq):
    per = tm // BLK

    def spec(r):
        return pl.BlockSpec((None, BLK, D_MODEL),
                            lambda i: (i // tiles_per_seq, jnp.maximum((i % tiles_per_seq) * per + r - 1, 0), 0))

    return [spec(r) for r in range(per)]


def _inproj_kernel(*refs, tiles_per_seq, per):
    x_refs, (meta_ref, nrm_ref, w_ref, rc_ref, rs1_ref, rs2_ref,
             u_ref, q_ref, kx_ref, vx_ref, sgs_ref, sga_ref) = refs[:per], refs[per:]
    normed = []
    for x in _padded_rows(x_refs, meta_ref, pl.program_id(0) % tiles_per_seq == 0):
        ms = jnp.mean(x * x, axis=-1, keepdims=True)
        normed.append((x * lax.rsqrt(ms + RMS_EPS) * nrm_ref[...]).astype(BF16))
    hn = jnp.concatenate(normed, axis=0)

    def proj(lo, hi):
        return _dot(hn, w_ref[:, lo:hi])

    u_ref[...] = proj(0, 1024)

    rc = rc_ref[...]
    rs1 = rs1_ref[...]
    rs2 = rs2_ref[...]

    def rope(blk):
        return blk * rc + pltpu.roll(blk, ROT_HALF, 1) * rs1 + pltpu.roll(blk, LANES - ROT_HALF, 1) * rs2

    qf = proj(1024, 2048)
    scale = LOG2_E / math.sqrt(HEAD_DIM)
    for j in range(8):
        q_ref[:, LANES * j:LANES * (j + 1)] = (rope(qf[:, LANES * j:LANES * (j + 1)]) * scale).astype(BF16)

    kvf = proj(2048, 2304)
    k = rope(kvf[:, :LANES])
    v = kvf[:, LANES:]
    low = lax.broadcasted_iota(I32, k.shape, 1) < HEAD_DIM

    def expand(t, ref):
        tr = pltpu.roll(t, HEAD_DIM, 1)
        zero = jnp.zeros_like(t)
        ref[:, 0:128] = jnp.where(low, t, zero).astype(BF16)
        ref[:, 128:256] = jnp.where(low, zero, tr).astype(BF16)
        ref[:, 256:384] = jnp.where(low, tr, zero).astype(BF16)
        ref[:, 384:512] = jnp.where(low, zero, t).astype(BF16)

    expand(k, kx_ref)
    expand(v, vx_ref)
    sgs_ref[...] = jax.nn.sigmoid(proj(2304, 3328)).astype(BF16)
    sga_ref[...] = jax.nn.sigmoid(proj(3328, 4352)).astype(BF16)


def _inproj(x, meta, nrm, w_bf, rc, rs1, rs2, *, tm, lp):
    np_ = x.shape[0] * lp
    tiles_per_seq = lp // tm
    per = tm // BLK
    row = lambda i: (i, 0)
    fixed = lambda i: (0, 0)
    rope_idx = lambda i: (i % tiles_per_seq, 0)
    out_shape = (
        jax.ShapeDtypeStruct((np_, 1024), F32),
        jax.ShapeDtypeStruct((np_, 1024), BF16),
        jax.ShapeDtypeStruct((np_, 512), BF16),
        jax.ShapeDtypeStruct((np_, 512), BF16),
        jax.ShapeDtypeStruct((np_, 1024), BF16),
        jax.ShapeDtypeStruct((np_, 1024), BF16),
    )
    return pl.pallas_call(
        functools.partial(_inproj_kernel, tiles_per_seq=tiles_per_seq, per=per),
        out_shape=out_shape,
        grid=(np_ // tm,),
        in_specs=_padded_row_specs(tm, tiles_per_seq) + [
            pl.BlockSpec((N_META, D_MODEL), fixed),
            pl.BlockSpec((1, D_MODEL), fixed),
            pl.BlockSpec(w_bf.shape, fixed),
            pl.BlockSpec((tm, LANES), rope_idx),
            pl.BlockSpec((tm, LANES), rope_idx),
            pl.BlockSpec((tm, LANES), rope_idx),
        ],
        out_specs=(
            pl.BlockSpec((tm, 1024), row),
            pl.BlockSpec((tm, 1024), row),
            pl.BlockSpec((tm, 512), row),
            pl.BlockSpec((tm, 512), row),
            pl.BlockSpec((tm, 1024), row),
            pl.BlockSpec((tm, 1024), row),
        ),
        compiler_params=_cparams(("arbitrary",)),
        name="inproj",
    )(*([x] * per), meta, nrm, w_bf, rc, rs1, rs2)


def _expand_block_diag(dst_ref, row0, t, e_ref, row_shift, col_shift):
    n = t.shape[0]
    step = min(n, 256)
    for r0 in range(0, n, step):
        full = _dot(t[r0:r0 + step, :].astype(BF16), e_ref[...])
        rg = (lax.broadcasted_iota(I32, full.shape, 0) + r0) >> row_shift
        cg = lax.broadcasted_iota(I32, full.shape, 1) >> col_shift
        keep = ((rg ^ cg) & (SSM_GBLK - 1)) == 0
        dst_ref[row0 + r0:row0 + r0 + step, :] = jnp.where(keep, full, 0.0).astype(BF16)


def _ssm_kernel(u_ref, bc_ref, pb_ref, cc_ref, kc_ref, eb_ref, ec_ref, tab_ref, z_ref,
                wb_ref, wcd_ref, ust_ref, bu_ref, yn_ref, *, t8, seg, nchunk):
    rc = t8 // nchunk

    @pl.when(pl.program_id(1) == 0)
    def _():
        bc = bc_ref[...]
        bc_swapped = pltpu.roll(bc, STATE, 1)
        for i in range(SSM_STEP):
            blk = pb_ref[i, :, 0:LANES] * bc + pb_ref[i, :, LANES:2 * LANES] * bc_swapped
            _expand_block_diag(wb_ref, LANES * i, blk, eb_ref, 4, 6)
        c_r, c_i = cc_ref[:, 0:LANES], cc_ref[:, LANES:2 * LANES]
        p_r, p_i = cc_ref[:, 2 * LANES:3 * LANES], cc_ref[:, 3 * LANES:4 * LANES]
        _expand_block_diag(wcd_ref, 0, c_r * p_r - c_i * p_i, ec_ref, 6, 4)
        _expand_block_diag(wcd_ref, 512, -(c_r * p_i + c_i * p_r), ec_ref, 6, 4)
        kc = kc_ref[...]
        lane = lax.broadcasted_iota(I32, kc.shape, 1)
        for ip in range(SSM_STEP):
            blk = kc if ip == 0 else jnp.where(lane >= GROUP * ip, pltpu.roll(kc, GROUP * ip, 1), 0.0)
            _expand_block_diag(wcd_ref, 1024 + LANES * ip, blk, ec_ref, 4, 4)

    for i in range(SSM_STEP):
        ust_ref[:, LANES * i:LANES * (i + 1)] = u_ref[pl.ds(i, t8, stride=SSM_STEP), :].astype(BF16)

    for c in range(nchunk):
        r = _dot(ust_ref[c * rc:(c + 1) * rc, :], wb_ref[...])
        for m in range(8):
            bu_ref[m, c * rc:(c + 1) * rc, :] = r[:, LANES * m:LANES * (m + 1)]

    a_re = [jnp.broadcast_to(tab_ref[0:1, LANES * m:LANES * (m + 1)], (SSM_SEGS, LANES)) for m in range(4)]
    a_im = [jnp.broadcast_to(tab_ref[1:2, LANES * m:LANES * (m + 1)], (SSM_SEGS, LANES)) for m in range(4)]
    s_re = [jnp.broadcast_to(tab_ref[2:3, LANES * m:LANES * (m + 1)], (SSM_SEGS, LANES)) for m in range(4)]
    s_im = [jnp.broadcast_to(tab_ref[3:4, LANES * m:LANES * (m + 1)], (SSM_SEGS, LANES)) for m in range(4)]

    def cmul_add(mr, mi, xr, xi, br, bi):
        return mr * xr - mi * xi + br, mr * xi + mi * xr + bi

    def load(k):
        return [bu_ref[m, pl.ds(k, SSM_SEGS, stride=seg), :] for m in range(8)]

    def advance(st, b):
        nr, ni = [], []
        for m in range(4):
            r_, i_ = cmul_add(a_re[m], a_im[m], st[m], st[4 + m], b[m], b[4 + m])
            nr.append(r_)
            ni.append(i_)
        return tuple(nr + ni)

    zeros = tuple(jnp.zeros((SSM_SEGS, LANES), F32) for _ in range(8))
    fin = lax.fori_loop(0, seg, lambda k, st: advance(st, load(k)), zeros)

    first = lax.broadcasted_iota(I32, (SSM_SEGS, LANES), 0) == 0

    def shift_down(t):
        return jnp.where(first, 0.0, pltpu.roll(t, 1, 0))

    tot = fin
    for _ in range(SSM_SEGS - 1):
        nxt_r, nxt_i = [], []
        for m in range(4):
            r_, i_ = cmul_add(s_re[m], s_im[m], shift_down(tot[m]), shift_down(tot[4 + m]), fin[m], fin[4 + m])
            nxt_r.append(r_)
            nxt_i.append(i_)
        tot = tuple(nxt_r + nxt_i)
    init = tuple(shift_down(t) for t in tot)

    def pass2(k, st):
        b = load(k)
        for m in range(8):
            bu_ref[m, pl.ds(k, SSM_SEGS, stride=seg), :] = st[m]
        return advance(st, b)

    lax.fori_loop(0, seg, pass2, init)

    for c in range(nchunk):
        xp = jnp.concatenate([bu_ref[m, c * rc:(c + 1) * rc, :] for m in range(8)], axis=1).astype(BF16)
        lhs = jnp.concatenate([xp, ust_ref[c * rc:(c + 1) * rc, :]], axis=1)
        y = _dot(lhs, wcd_ref[...])
        for i in range(SSM_STEP):
            yn_ref[pl.ds(SSM_STEP * c * rc + i, rc, stride=SSM_STEP), :] = y[:, LANES * i:LANES * (i + 1)]

    y = yn_ref[...] + tab_ref[4:5, 0:LANES] * u_ref[...]
    z_ref[...] = jax.nn.gelu(y).astype(BF16)


def _ssm(u, bc, pb, cc, kc, tab, *, bsz, lp, nchunk):
    np_ = u.shape[0]
    t8 = lp // SSM_STEP
    seg = t8 // SSM_SEGS
    nblk = D_MODEL // LANES
    col = jnp.arange(1024)
    src_b = (col >> 9) * STATE + (col & (STATE - 1))
    src_c = (col >> 7) * GROUP + (col & (GROUP - 1))
    eb = (jnp.arange(LANES)[:, None] == src_b[None, :]).astype(BF16)
    ec = (jnp.arange(LANES)[:, None] == src_c[None, :]).astype(BF16)
    kern = functools.partial(_ssm_kernel, t8=t8, seg=seg, nchunk=nchunk)
    per_q = lambda q, b: (q, 0, 0)
    fixed = lambda q, b: (0, 0)
    return pl.pallas_call(
        kern,
        out_shape=jax.ShapeDtypeStruct((np_, D_MODEL), BF16),
        grid=(nblk, bsz),
        in_specs=[
            pl.BlockSpec((lp, LANES), lambda q, b: (b, q)),
            pl.BlockSpec((None, LANES, LANES), per_q),
            pl.BlockSpec((SSM_STEP, None, LANES, 2 * LANES), lambda q, b: (0, q, 0, 0)),
            pl.BlockSpec((None, SSM_GBLK * STATE, 4 * LANES), per_q),
            pl.BlockSpec((None, LANES, LANES), per_q),
            pl.BlockSpec((LANES, 1024), fixed),
            pl.BlockSpec((LANES, 1024), fixed),
            pl.BlockSpec((None, 8, 512), per_q),
        ],
        out_specs=pl.BlockSpec((lp, LANES), lambda q, b: (b, q)),
        scratch_shapes=[
            pltpu.VMEM((1024, 1024), BF16),
            pltpu.VMEM((2048, 1024), BF16),
            pltpu.VMEM((t8, 1024), BF16),
            pltpu.VMEM((8, t8, LANES), F32),
            pltpu.VMEM((lp, LANES), F32),
        ],
        compiler_params=_cparams(("arbitrary", "arbitrary")),
        name="ssm",
    )(u, bc, pb, cc, kc, eb, ec, tab)


def _ssm_tables(lam_re, lam_im, log_dt, b_re, b_im, c_re, c_im, d, seg):
    hi = lax.Precision.HIGHEST
    lr, li = lam_re.astype(F32), lam_im.astype(F32)
    dt = jnp.exp(log_dt.astype(F32))[:, None]
    mag = jnp.exp(dt * lr)
    ar, ai = mag * jnp.cos(dt * li), mag * jnp.sin(dt * li)
    den = lr * lr + li * li
    nr, ni = ar - 1.0, ai
    fr, fi = (nr * lr + ni * li) / den, (ni * lr - nr * li) / den
    br, bi = b_re.astype(F32), b_im.astype(F32)
    bbr = fr[..., None] * br - fi[..., None] * bi
    bbi = fr[..., None] * bi + fi[..., None] * br

    def cmul(xr, xi, yr, yi):
        return xr * yr - xi * yi, xr * yi + xi * yr

    pr, pi = [jnp.ones_like(ar)], [jnp.zeros_like(ar)]
    for _ in range(SSM_STEP):
        r_, i_ = cmul(pr[-1], pi[-1], ar, ai)
        pr.append(r_)
        pi.append(i_)
    pw_r, pw_i = jnp.stack(pr), jnp.stack(pi)
    a8r, a8i = pw_r[SSM_STEP], pw_i[SSM_STEP]
    sr, si = jnp.ones_like(ar), jnp.zeros_like(ar)
    qr, qi = a8r, a8i
    e = seg
    while e:
        if e & 1:
            sr, si = cmul(sr, si, qr, qi)
        qr, qi = cmul(qr, qi, qr, qi)
        e >>= 1

    cr, ci = c_re.astype(F32), c_im.astype(F32)
    nb = N_GROUPS // SSM_GBLK

    bc = jnp.concatenate([jnp.swapaxes(bbr, 1, 2), jnp.swapaxes(bbi, 1, 2)], axis=2).reshape(nb, LANES, LANES)
    pb = jnp.stack([jnp.concatenate([pr[SSM_STEP - 1 - i]] * 2 + [-pi[SSM_STEP - 1 - i], pi[SSM_STEP - 1 - i]], axis=1)
                    for i in range(SSM_STEP)])
    pb = jnp.broadcast_to(pb[:, :, None, :], (SSM_STEP, N_GROUPS, GROUP, 2 * LANES))
    pb = pb.reshape(SSM_STEP, nb, LANES, 2 * LANES)
    crt = jnp.tile(jnp.swapaxes(cr, 1, 2), (1, 1, SSM_STEP))
    cit = jnp.tile(jnp.swapaxes(ci, 1, 2), (1, 1, SSM_STEP))
    prc = jnp.repeat(jnp.stack(pr[1:], axis=2), GROUP, axis=2)
    pic = jnp.repeat(jnp.stack(pi[1:], axis=2), GROUP, axis=2)
    cc = jnp.concatenate([crt, cit, prc, pic], axis=2).reshape(nb, SSM_GBLK * STATE, 4 * LANES)
    tr, ti = cmul(pw_r[:SSM_STEP, :, :, None], pw_i[:SSM_STEP, :, :, None], bbr[None], bbi[None])
    kt = (jnp.einsum('ghp,tgpk->gkth', cr, tr, precision=hi)
          - jnp.einsum('ghp,tgpk->gkth', ci, ti, precision=hi))
    kc = kt.reshape(nb, LANES, LANES)

    def lanes(t):
        return t.reshape(nb, SSM_GBLK * STATE)

    dl = jnp.pad(d.astype(F32).reshape(nb, SSM_GBLK * GROUP), ((0, 0), (0, 512 - LANES)))
    zero = jnp.zeros((nb, 512), F32)
    tab = jnp.stack([lanes(a8r), lanes(a8i), lanes(sr), lanes(si), dl, zero, zero, zero], axis=1)
    return bc, pb, cc, kc, tab


def _attn_kernel(q_ref, kc_ref, kp_ref, km_ref, vc_ref, vp_ref, vm_ref, sink_ref, o_ref):
    n = pl.program_id(1)
    nk = 3 * BLK
    row = lax.broadcasted_iota(I32, (BLK, nk), 0)
    col = lax.broadcasted_iota(I32, (BLK, nk), 1)
    band = (col > row) & (col <= row + BLK) & (col >= 2 * BLK - BLK * n)
    mrow = col - 2 * BLK
    meta = (mrow >= PAD) & (mrow <= BLK * n + row)
    bias = jnp.where(band | meta, 0.0, NEG_INF)

    for g in range(N_KV_HEADS):
        qg = jnp.concatenate([q_ref[:, LANES * (4 * g + jj):LANES * (4 * g + jj + 1)] for jj in range(4)], axis=0)
        acc = None
        for par in range(2):
            sl = slice(LANES * (2 * g + par), LANES * (2 * g + par + 1))
            kk = jnp.concatenate([kp_ref[:, sl], kc_ref[:, sl], km_ref[:, sl]], axis=0)
            vv = jnp.concatenate([vp_ref[:, sl], vc_ref[:, sl], vm_ref[:, sl]], axis=0)
            s_all = _dot_nt(qg, kk)
            probs, rden = [], []
            for jj in range(4):
                sink = sink_ref[8 * g + 2 * jj + par] * LOG2_E
                s = s_all[BLK * jj:BLK * (jj + 1), :] + bias
                mx = jnp.maximum(jnp.max(s, axis=1, keepdims=True), sink)
                p = jnp.exp2(s - mx)
                den = jnp.sum(p, axis=1, keepdims=True) + jnp.exp2(sink - mx)
                probs.append(p.astype(BF16))
                rden.append(jnp.broadcast_to(1.0 / den, (BLK, LANES)))
            o = _dot(jnp.concatenate(probs, axis=0), vv) * jnp.concatenate(rden, axis=0)
            acc = o if acc is None else acc + o
        for jj in range(4):
            o_ref[:, LANES * (4 * g + jj):LANES * (4 * g + jj + 1)] = acc[BLK * jj:BLK * (jj + 1), :].astype(BF16)


def _attention(q, kx, vx, sinks, *, bsz, nb):
    np_ = q.shape[0]
    cur = lambda b, n: (b * nb + n, 0)
    prev = lambda b, n: (b * nb + jnp.maximum(n - 1, 0), 0)
    first = lambda b, n: (b * nb, 0)
    return pl.pallas_call(
        _attn_kernel,
        out_shape=jax.ShapeDtypeStruct((np_, 1024), BF16),
        grid=(bsz, nb),
        in_specs=[
            pl.BlockSpec((BLK, 1024), cur),
            pl.BlockSpec((BLK, 512), cur),
            pl.BlockSpec((BLK, 512), prev),
            pl.BlockSpec((BLK, 512), first),
            pl.BlockSpec((BLK, 512), cur),
            pl.BlockSpec((BLK, 512), prev),
            pl.BlockSpec((BLK, 512), first),
            pl.BlockSpec(memory_space=pltpu.SMEM),
        ],
        out_specs=pl.BlockSpec((BLK, 1024), cur),
        compiler_params=_cparams(("arbitrary", "arbitrary")),
        name="attn",
    )(q, kx, kx, kx, vx, vx, vx, sinks)


def _post_kernel(*refs, tiles_per_seq, per):
    x_refs, (meta_ref, z_ref, at_ref, sgs_ref, sga_ref, wglu_ref, bglu_ref, wbs_ref, wba_ref, wout_ref,
             nffn_ref, wrh_ref, wrl_ref, br_ref,
             h2_ref, hnp_ref, lslot_ref, ecol_ref, cnt_ref) = refs[:per], refs[per:]
    i = pl.program_id(0)

    z = z_ref[...]
    t = _dot(z, wglu_ref[...]) + bglu_ref[...]
    so = (z.astype(F32) * jax.nn.sigmoid(t)).astype(BF16)
    mix = (sgs_ref[...].astype(F32) * _dot(so, wbs_ref[...])
           + sga_ref[...].astype(F32) * _dot(at_ref[...], wba_ref[...]))
    mixed = _dot(mix.astype(BF16), wout_ref[...])
    logit_blocks = []
    for r, xr in enumerate(_padded_rows(x_refs, meta_ref, i % tiles_per_seq == 0)):
        rows = slice(BLK * r, BLK * (r + 1))
        h2 = xr + mixed[rows, :]
        h2_ref[rows, :] = h2
        ms = jnp.mean(h2 * h2, axis=-1, keepdims=True)
        hn = h2 * lax.rsqrt(ms + RMS_EPS) * nffn_ref[...]
        hb = hn.astype(BF16)
        hnp_ref[rows, :] = hb
        lo = (hn - hb.astype(F32)).astype(BF16)
        logit_blocks.append(_dot(hb, wrh_ref[...]) + _dot(lo, wrh_ref[...]) + _dot(hb, wrl_ref[...]) + br_ref[...])
    logits = jnp.concatenate(logit_blocks, axis=0)

    tm = logits.shape[0]
    lane = lax.broadcasted_iota(I32, (tm, LANES), 1)
    lane_f = lane.astype(F32)
    work = logits
    vals, hots, idxs = [], [], []
    for _ in range(TOP_K):
        mx = jnp.max(work, axis=1, keepdims=True)
        idx = jnp.min(jnp.where(work == mx, lane_f, float(LANES)), axis=1, keepdims=True)
        hot = lane_f == idx
        vals.append(mx)
        idxs.append(idx)
        hots.append(hot)
        work = jnp.where(hot, -jnp.inf, work)

    ex = [jnp.exp(v - vals[0]) for v in vals]
    tot = ex[0] + ex[1] + ex[2] + ex[3]
    gates = [e / tot for e in ex]

    onehot = jnp.zeros((tm, LANES), F32)
    for hot in hots:
        onehot = onehot + jnp.where(hot, 1.0, 0.0)
    r_i = lax.broadcasted_iota(I32, (tm, tm), 0)
    c_i = lax.broadcasted_iota(I32, (tm, tm), 1)
    tri = jnp.where(r_i > c_i, 1.0, 0.0).astype(BF16)
    before = _dot(tri, onehot.astype(BF16))
    count = jnp.sum(onehot, axis=0, keepdims=True)
    gran = jnp.floor((count + (GRAN - 1)) * (1.0 / GRAN))
    e_r = lax.broadcasted_iota(I32, (LANES, LANES), 0)
    e_c = lax.broadcasted_iota(I32, (LANES, LANES), 1)
    upper = jnp.where(e_r < e_c, 1.0, 0.0).astype(BF16)
    gstart = _dot(jnp.broadcast_to(gran, (8, LANES)).astype(BF16), upper)[0:1, :]
    base = before + gstart * float(GRAN)
    slots = [jnp.sum(jnp.where(hot, base, 0.0), axis=1, keepdims=True) for hot in hots]
    cnt_ref[...] = jnp.broadcast_to(gran, cnt_ref.shape)

    def spread(cols, dtype):
        out = jnp.zeros((tm, LANES), dtype)
        for k, cval in enumerate(cols):
            out = jnp.where(lane == k, cval, out)
        return out

    lslot_ref[...] = spread(slots, F32)
    cols = []
    for g in gates:
        g1 = g.astype(BF16).astype(F32)
        g2 = (g - g1).astype(BF16).astype(F32)
        cols += [g1, g2, g - g1 - g2]
    for s in slots:
        hi = jnp.floor(s * (1.0 / 64.0))
        cols += [hi, s - 64.0 * hi]
    ecol_ref[...] = spread(cols, F32)


def _post(x, meta, z, attn, sgs, sga, wglu, bglu, wbs, wba, wout, nffn, wrh, wrl, br, *, tm, lp):
    np_ = z.shape[0]
    tiles_per_seq = lp // tm
    per = tm // BLK
    row = lambda i: (i, 0)
    fixed = lambda i: (0, 0)
    sq = pl.BlockSpec((1024, 1024), fixed)
    vec = pl.BlockSpec((1, 1024), fixed)
    out_shape = (
        jax.ShapeDtypeStruct((np_, 1024), F32),
        jax.ShapeDtypeStruct((np_, 1024), BF16),
        jax.ShapeDtypeStruct((np_, LANES), F32),
        jax.ShapeDtypeStruct((np_, LANES), F32),
        jax.ShapeDtypeStruct((np_ // tm * 8, LANES), F32),
    )
    return pl.pallas_call(
        functools.partial(_post_kernel, tiles_per_seq=tiles_per_seq, per=per),
        out_shape=out_shape,
        grid=(np_ // tm,),
        in_specs=_padded_row_specs(tm, tiles_per_seq) + [
            pl.BlockSpec((N_META, D_MODEL), fixed),
            pl.BlockSpec((tm, 1024), row), pl.BlockSpec((tm, 1024), row),
            pl.BlockSpec((tm, 1024), row), pl.BlockSpec((tm, 1024), row),
            sq, vec, sq, sq, sq, vec,
            pl.BlockSpec((1024, LANES), fixed), pl.BlockSpec((1024, LANES), fixed),
            pl.BlockSpec((1, LANES), fixed),
        ],
        out_specs=(
            pl.BlockSpec((tm, 1024), row), pl.BlockSpec((tm, 1024), row),
            pl.BlockSpec((tm, LANES), row), pl.BlockSpec((tm, LANES), row),
            pl.BlockSpec((8, LANES), row),
        ),
        compiler_params=_cparams(("arbitrary",)),
        name="post",
    )(*([x] * per), meta, z, attn, sgs, sga, wglu, bglu, wbs, wba, wout, nffn, wrh, wrl, br)


def _sort_kernel(hnp_ref, ecol_ref, xloc_ref, *, tm, srows):
    ecol = ecol_ref[...].astype(BF16)
    pay = jnp.concatenate([hnp_ref[...], ecol], axis=1)

    r8 = lax.broadcasted_iota(I32, (8, LANES), 0)
    l8 = lax.broadcasted_iota(I32, (8, LANES), 1)
    digit_w = jnp.where(l8 == 12 + 2 * r8, 64.0, jnp.where(l8 == 13 + 2 * r8, 1.0, 0.0)).astype(BF16)
    slot_rows = _dot_nt(digit_w, ecol)

    rr = lax.broadcasted_iota(I32, (LANES, LANES), 0)
    cc = lax.broadcasted_iota(I32, (LANES, LANES), 1)
    kk = jnp.where(cc < 3 * TOP_K, cc // 3, -LANES)
    spread_w = jnp.where(rr == 12 + 2 * kk, 64.0, jnp.where(rr == 13 + 2 * kk, 1.0, 0.0)).astype(BF16)
    gate_lane = lax.broadcasted_iota(I32, (SORT_CHUNK, LANES), 1) < 3 * TOP_K

    for c in range(srows // SORT_CHUNK):
        s_id = (lax.broadcasted_iota(I32, (SORT_CHUNK, tm), 0) + c * SORT_CHUNK).astype(F32)
        perm = jnp.zeros((SORT_CHUNK, tm), F32)
        for k in range(TOP_K):
            perm = perm + jnp.where(slot_rows[k:k + 1, :] == s_id, 1.0, 0.0)
        got = _dot(perm.astype(BF16), pay)
        rows = slice(c * SORT_CHUNK, (c + 1) * SORT_CHUNK)
        xloc_ref[rows, 0:512] = _pack_bf16_pairs(got[:, :D_MODEL], exact=True)
        extra = got[:, D_MODEL:]
        owner = _dot(extra.astype(BF16), spread_w)
        me = (lax.broadcasted_iota(I32, (SORT_CHUNK, LANES), 0) + c * SORT_CHUNK).astype(F32)
        mine = jnp.where(gate_lane, owner, -1.0) == me
        gate = jnp.sum(jnp.where(mine, extra, 0.0), axis=1, keepdims=True)
        xloc_ref[rows, 512:SORT_W] = pltpu.bitcast(jnp.broadcast_to(gate, (SORT_CHUNK, LANES)), U32)


def _sort(hnp, ecol, *, tm, srows):
    np_ = hnp.shape[0]
    ntile = np_ // tm
    row = lambda i: (i, 0)
    return pl.pallas_call(
        functools.partial(_sort_kernel, tm=tm, srows=srows),
        out_shape=jax.ShapeDtypeStruct((ntile * srows, SORT_W), U32),
        grid=(ntile,),
        in_specs=[pl.BlockSpec((tm, D_MODEL), row), pl.BlockSpec((tm, LANES), row)],
        out_specs=pl.BlockSpec((srows, SORT_W), row),
        compiler_params=_cparams(("arbitrary",)),
        name="sort",
    )(hnp, ecol)


def _moe_kernel(blk_ref, exp_ref, lo_ref, hi_ref, kind_ref, wslot_ref, nexte_ref, gsrc_ref,
                xloc_ref, wgu_ref, bgu_ref, wd_ref, bd_ref, yloc_ref,
                xbuf, ybuf, sem_in, sem_out, wgu_f, wd_f, sem_w, wgu_bf, wd_bf, *, nseg, dump0):
    i = pl.program_id(0)
    prev = jnp.maximum(i - 1, 0)
    nxt = jnp.minimum(i + 1, nseg - 1)
    slot = i % 2
    new_expert = (i == 0) | (exp_ref[i] != exp_ref[prev])

    def gather(seg, dst_slot, start):
        b = blk_ref[seg]
        for g in range(MOE_GRANS):
            src = pl.multiple_of(gsrc_ref[b * MOE_GRANS + g] * GRAN, GRAN)
            cp = pltpu.make_async_copy(xloc_ref.at[pl.ds(src, GRAN)], xbuf.at[dst_slot, pl.ds(g * GRAN, GRAN)],
                                       sem_in.at[dst_slot])
            if start:
                cp.start()
            else:
                cp.wait()

    def scatter(seg, src_slot, start):
        b = blk_ref[seg]
        lo = lo_ref[seg]
        hi = hi_ref[seg]
        for g in range(MOE_GRANS):
            keep = (g >= lo) & (g < hi)
            spare = dump0 + src_slot * MOE_GRANS + g
            dst = pl.multiple_of(jnp.where(keep, gsrc_ref[b * MOE_GRANS + g], spare) * GRAN, GRAN)
            cp = pltpu.make_async_copy(ybuf.at[src_slot, pl.ds(g * GRAN, GRAN)], yloc_ref.at[pl.ds(dst, GRAN)],
                                       sem_out.at[src_slot])
            if start:
                cp.start()
            else:
                cp.wait()

    @pl.when(i == 0)
    def _():
        gather(i, slot, True)
        ybuf[1 - slot] = jnp.zeros((MOE_BLK, 512), U32)
        for g in range(MOE_GRANS):
            pltpu.make_async_copy(ybuf.at[1 - slot, pl.ds(g * GRAN, GRAN)],
                                  yloc_ref.at[pl.ds((dump0 + (1 - slot) * MOE_GRANS + g) * GRAN, GRAN)],
                                  sem_out.at[1 - slot]).start()

    gather(i, slot, False)

    def fetch(e, slot, start):
        for src, dst in ((wgu_ref, wgu_f), (wd_ref, wd_f)):
            cp = pltpu.make_async_copy(src.at[e], dst.at[slot], sem_w.at[slot])
            if start:
                cp.start(priority=1)
            else:
                cp.wait()

    wslot = wslot_ref[i]

    @pl.when(i == 0)
    def _():
        fetch(exp_ref[i], wslot, True)

    @pl.when(new_expert)
    def _():
        fetch(exp_ref[i], wslot, False)
        wgu_bf[...] = wgu_f[wslot].astype(BF16)
        wd_bf[...] = wd_f[wslot].astype(BF16)

    @pl.when(new_expert & (nexte_ref[i] >= 0))
    def _():
        fetch(nexte_ref[i], 1 - wslot, True)

    kind = kind_ref[i]

    @pl.when(kind == 1)
    def _():
        gather(nxt, 1 - slot, True)
        xw = xbuf[slot]
        xb = _unpack_bf16_pairs(xw[:, 0:512]).astype(BF16)
        gate = pltpu.bitcast(xw[:, 512:SORT_W], F32)
        gu = _dot(xb, wgu_bf[...]) + bgu_ref[...]
        g = jnp.minimum(gu[:, :D_FF], SWIGLU_LIMIT)
        up = jnp.clip(gu[:, D_FF:], -SWIGLU_LIMIT, SWIGLU_LIMIT)
        hid = g * jax.nn.sigmoid(SWIGLU_ALPHA * g) * (up + 1.0)
        y = _dot(hid.astype(BF16), wd_bf[...]) + bd_ref[...]
        y = y * jnp.concatenate([gate] * (D_MODEL // LANES), axis=1)
        ybuf[slot] = _pack_bf16_pairs(y)
        scatter(i, slot, True)
        scatter(prev, 1 - slot, False)

    @pl.when(kind != 1)
    def _():
        gather(nxt, 1 - slot, True)
        ybuf[slot] = jnp.zeros((MOE_BLK, 512), U32)
        scatter(i, slot, True)
        scatter(prev, 1 - slot, False)

    @pl.when(i == nseg - 1)
    def _():
        scatter(i, slot, False)
        gather(nxt, 1 - slot, False)


def _moe(blk, exp, lo, hi, kind, gsrc, xloc, wgu, bgu, wd, bd, *, ncap):
    nseg = blk.shape[0]
    dump0 = ncap
    change = jnp.concatenate([jnp.zeros((1,), I32), (exp[1:] != exp[:-1]).astype(I32)])
    wslot = jnp.cumsum(change) % 2
    later = jnp.where(exp[None, :] > exp[:, None], exp[None, :], N_EXPERTS)
    nexte = jnp.min(later, axis=1)
    nexte = jnp.where(nexte >= N_EXPERTS, -1, nexte)
    wmap = lambda i, b, e, *_: (e[i], 0, 0)
    grid_spec = pltpu.PrefetchScalarGridSpec(
        num_scalar_prefetch=8,
        grid=(nseg,),
        in_specs=[
            pl.BlockSpec(memory_space=pl.ANY),
            pl.BlockSpec(memory_space=pl.ANY),
            pl.BlockSpec((None, 1, 2 * D_FF), wmap),
            pl.BlockSpec(memory_space=pl.ANY),
            pl.BlockSpec((None, 1, D_MODEL), wmap),
        ],
        out_specs=pl.BlockSpec(memory_space=pl.ANY),
        scratch_shapes=[
            pltpu.VMEM((2, MOE_BLK, SORT_W), U32),
            pltpu.VMEM((2, MOE_BLK, 512), U32),
            pltpu.SemaphoreType.DMA((2,)),
            pltpu.SemaphoreType.DMA((2,)),
            pltpu.VMEM((2, D_MODEL, 2 * D_FF), F32),
            pltpu.VMEM((2, D_FF, D_MODEL), F32),
            pltpu.SemaphoreType.DMA((2,)),
            pltpu.VMEM((D_MODEL, 2 * D_FF), BF16),
            pltpu.VMEM((D_FF, D_MODEL), BF16),
        ],
    )
    return pl.pallas_call(
        functools.partial(_moe_kernel, nseg=nseg, dump0=dump0),
        out_shape=jax.ShapeDtypeStruct(((ncap + 2 * MOE_GRANS) * GRAN, 512), U32),
        grid_spec=grid_spec,
        compiler_params=_cparams(("arbitrary",)),
        name="moe",
    )(blk, exp, lo, hi, kind, wslot.astype(I32), nexte.astype(I32), gsrc, xloc, wgu, bgu, wd, bd)


def _combine_kernel(y_ref, lslot_ref, h2_ref, nfin_ref, o_ref, obuf, sem, *, tm, srows, tiles_per_seq, ntile):
    i = pl.program_id(0)
    per = tm // BLK
    slot = i % 2
    slots = lslot_ref[...]
    acc = h2_ref[...]
    for c in range(srows // COMBINE_CHUNK):
        s_id = (lax.broadcasted_iota(I32, (tm, COMBINE_CHUNK), 1) + c * COMBINE_CHUNK).astype(F32)
        take = jnp.zeros((tm, COMBINE_CHUNK), F32)
        for k in range(TOP_K):
            take = take + jnp.where(slots[:, k:k + 1] == s_id, 1.0, 0.0)
        rows = _unpack_bf16_pairs(y_ref[c * COMBINE_CHUNK:(c + 1) * COMBINE_CHUNK, :]).astype(BF16)
        acc = acc + _dot(take.astype(BF16), rows)
    ms = jnp.mean(acc * acc, axis=-1, keepdims=True)
    obuf[slot] = acc * lax.rsqrt(ms + RMS_EPS) * nfin_ref[...]

    def copies(step, start):
        b = step // tiles_per_seq
        j = step % tiles_per_seq
        src_slot = step % 2
        for r in range(per):
            n = j * per + r
            cp = pltpu.make_async_copy(obuf.at[src_slot, pl.ds(r * BLK, BLK)],
                                       o_ref.at[b, pl.ds(pl.multiple_of(jnp.maximum(n - 1, 0) * BLK, BLK), BLK)],
                                       sem.at[src_slot])

            def go():
                if start:
                    cp.start()
                else:
                    cp.wait()

            if r == 0:
                pl.when(j > 0)(go)
            else:
                go()

    copies(i, True)

    @pl.when(i > 0)
    def _():
        copies(i - 1, False)

    @pl.when(i == ntile - 1)
    def _():
        copies(i, False)


def _combine(yloc, lslot, h2, nfin, *, bsz, seq, tm, lp, srows):
    np_ = h2.shape[0]
    tiles_per_seq = lp // tm
    row = lambda i: (i, 0)
    return pl.pallas_call(
        functools.partial(_combine_kernel, tm=tm, srows=srows, tiles_per_seq=tiles_per_seq, ntile=np_ // tm),
        out_shape=jax.ShapeDtypeStruct((bsz, seq, D_MODEL), F32),
        grid=(np_ // tm,),
        in_specs=[
            pl.BlockSpec((srows, 512), row),
            pl.BlockSpec((tm, LANES), row),
            pl.BlockSpec((tm, D_MODEL), row),
            pl.BlockSpec((1, D_MODEL), lambda i: (0, 0)),
        ],
        out_specs=pl.BlockSpec(memory_space=pl.ANY),
        scratch_shapes=[pltpu.VMEM((2, tm, D_MODEL), F32), pltpu.SemaphoreType.DMA((2,))],
        compiler_params=_cparams(("arbitrary",)),
        name="combine",
    )(yloc, lslot, h2, nfin)


def _rope_tables(lp):
    pos = np.arange(lp, dtype=np.float32) - np.float32(PAD)
    inv_freq = np.float32(ROPE_THETA) ** (-np.arange(0, 2 * ROT_HALF, 2, dtype=np.float32) / np.float32(2 * ROT_HALF))
    ang = (pos[:, None] * inv_freq[None, :]).astype(np.float32)
    cos, sin = np.cos(ang), np.sin(ang)
    ones = np.ones((lp, HEAD_DIM - 2 * ROT_HALF), np.float32)
    zeros8 = np.zeros((lp, ROT_HALF), np.float32)
    zrest = np.zeros((lp, HEAD_DIM - 2 * ROT_HALF), np.float32)
    c = np.concatenate([cos, cos, ones], axis=1)
    s1 = np.concatenate([zeros8, sin, zrest], axis=1)
    s2 = np.concatenate([-sin, zeros8, zrest], axis=1)
    tile = lambda t: jnp.asarray(np.concatenate([t, t], axis=1), F32)
    return tile(c), tile(s1), tile(s2)


def _route_tables(grans, gt):
    ntile = grans.shape[0]
    ncap = ntile * gt
    nown = N_EXPERTS + 1
    npos = ncap + nown * MOE_GRANS
    nblocks = npos // MOE_GRANS
    used = jnp.sum(grans, axis=1)
    lstart = jnp.cumsum(grans, axis=1) - grans
    tile0 = jnp.arange(ntile, dtype=I32)[:, None] * gt
    lens = jnp.concatenate([grans.T, (gt - used)[None, :]], axis=0)
    srcs = jnp.concatenate([(tile0 + lstart).T, (tile0[:, 0] + used)[None, :]], axis=0)
    total = jnp.sum(lens, axis=1)
    room = (total + MOE_GRANS - 1) // MOE_GRANS * MOE_GRANS
    oend = jnp.cumsum(room)
    ostart = oend - room
    run_start = (ostart[:, None] + jnp.cumsum(lens, axis=1) - lens).reshape(-1)
    off = srcs.reshape(-1) - run_start
    delta = off - jnp.concatenate([jnp.zeros((1,), I32), off[:-1]])
    pos = jnp.arange(npos, dtype=I32)
    gsrc = pos + jnp.sum(jnp.where(run_start[None, :] <= pos[:, None], delta[None, :], 0), axis=1)

    first = jnp.arange(nblocks, dtype=I32) * MOE_GRANS
    owner = jnp.sum((oend[None, :] <= first[:, None]).astype(I32), axis=1)
    mine = owner[:, None] == jnp.arange(nown, dtype=I32)[None, :]
    valid_end = jnp.sum(jnp.where(mine, (ostart + total)[None, :], 0), axis=1)
    hi = jnp.clip(valid_end - first, 0, MOE_GRANS)
    lo = jnp.zeros_like(hi)
    gsrc = jnp.where((pos % MOE_GRANS) < jnp.repeat(hi, MOE_GRANS), gsrc, 0)
    kind = jnp.where(hi > 0, jnp.where(owner >= N_EXPERTS, 2, 1), 0)
    works = kind == 1
    first_work = jnp.min(jnp.where(works, owner, N_EXPERTS - 1))
    exp = jnp.maximum(lax.cummax(jnp.where(works, owner, -1), axis=0), first_work)
    blk = jnp.arange(nblocks, dtype=I32)
    return gsrc.astype(I32), blk, exp.astype(I32), lo.astype(I32), hi.astype(I32), kind.astype(I32)


def kernel(x, meta_tokens, norm_mix, w_in, ssm_lam_re, ssm_lam_im, ssm_log_dt, ssm_b_re, ssm_b_im,
           ssm_c_re, ssm_c_im, ssm_d, w_glu, b_glu, attn_sinks, w_br_ssm, w_br_attn, w_out,
           norm_ffn, w_router, b_router, w_gate_up, b_gate_up, w_down, b_down, norm_final):
    bsz, seq, _ = x.shape
    lp = seq + BLK
    nb = lp // BLK
    np_ = bsz * lp
    tm = 640 if lp % 640 == 0 else BLK
    t8 = lp // SSM_STEP
    seg = t8 // SSM_SEGS
    nchunk = 5 if (t8 % 5 == 0 and (t8 // 5) % 16 == 0) else 1

    meta = meta_tokens.astype(F32)
    rc, rs1, rs2 = _rope_tables(lp)
    u, q, kx, vx, sgs, sga = _inproj(x, meta, norm_mix[0][None], w_in[0].astype(BF16), rc, rs1, rs2, tm=tm, lp=lp)

    bc, pb, cc, kc, tab = _ssm_tables(ssm_lam_re[0], ssm_lam_im[0], ssm_log_dt[0], ssm_b_re[0], ssm_b_im[0],
                                      ssm_c_re[0], ssm_c_im[0], ssm_d[0], seg)
    z = _ssm(u, bc, pb, cc, kc, tab, bsz=bsz, lp=lp, nchunk=nchunk)

    attn = _attention(q, kx, vx, attn_sinks[0].astype(F32), bsz=bsz, nb=nb)

    wr = jnp.pad(w_router[0].astype(F32), ((0, 0), (0, LANES - N_EXPERTS)))
    wrh = wr.astype(BF16)
    wrl = (wr - wrh.astype(F32)).astype(BF16)
    br = jnp.concatenate([b_router[0].astype(F32), jnp.full((LANES - N_EXPERTS,), NEG_INF, F32)])[None]
    h2, hnp, lslot, ecol, cnt = _post(
        x, meta, z, attn, sgs, sga, w_glu[0].astype(BF16), b_glu[0][None], w_br_ssm[0].astype(BF16),
        w_br_attn[0].astype(BF16), w_out[0].astype(BF16), norm_ffn[0][None], wrh, wrl, br, tm=tm, lp=lp)

    unit = math.lcm(SORT_CHUNK, COMBINE_CHUNK)
    srows = -(-(tm * TOP_K + N_EXPERTS * (GRAN - 1)) // unit) * unit
    gt = srows // GRAN
    ntile = np_ // tm
    grans = cnt.reshape(ntile, 8, LANES)[:, 0, :N_EXPERTS].astype(I32)
    gsrc, sblk, sexp, slo, shi, skind = _route_tables(grans, gt)

    xloc = _sort(hnp, ecol, tm=tm, srows=srows)
    yloc = _moe(sblk, sexp, slo, shi, skind, gsrc, xloc, w_gate_up[0], b_gate_up[0][:, None, :],
                w_down[0], b_down[0][:, None, :], ncap=ntile * gt)
    return _combine(yloc, lslot, h2, norm_final[None], bsz=bsz, seq=seq, tm=tm, lp=lp, srows=srows)
```

```python
import functools
import math

import jax
import jax.numpy as jnp
import numpy as np
from jax import lax
from jax.experimental import pallas as pl
from jax.experimental.pallas import tpu as pltpu

F32 = jnp.float32
BF16 = jnp.bfloat16
I32 = jnp.int32
U32 = jnp.uint32

D_MODEL = 1024
N_META = 16
BLK = 128
PAD = BLK - N_META
N_GROUPS = 64
GROUP = 16
STATE = 64
N_Q_HEADS = 16
N_KV_HEADS = 2
HEAD_DIM = 64
ROT_HALF = 8
ROPE_THETA = 500000.0
N_EXPERTS = 32
TOP_K = 4
D_FF = 1024
SWIGLU_LIMIT = 7.0
SWIGLU_ALPHA = 1.702
RMS_EPS = 1e-5
NEG_INF = -1e30
LOG2_E = 1.4426950408889634
SSM_STEP = 8
SSM_SEGS = 16
SSM_GBLK = 8
LANES = 128
MOE_BLK = 256
GRAN = 8
MOE_GRANS = MOE_BLK // GRAN
SORT_CHUNK = 256
COMBINE_CHUNK = 256
SORT_W = 512 + LANES
VMEM_LIMIT = 56 * 1024 * 1024


def _cparams(sem):
    return pltpu.CompilerParams(dimension_semantics=sem, vmem_limit_bytes=VMEM_LIMIT)


def _dot(a, b):
    return jnp.dot(a, b, preferred_element_type=F32)


def _dot_nt(a, b):
    return lax.dot_general(a, b, (((1,), (1,)), ((), ())), preferred_element_type=F32)


def _pack_bf16_pairs(x, exact=False):
    w = x.shape[1] // 2
    bits = pltpu.bitcast(x if exact else x.astype(BF16).astype(F32), U32)
    return (bits[:, w:] & jnp.uint32(0xFFFF0000)) | (bits[:, :w] >> 16)


def _unpack_bf16_pairs(wd):
    lo = pltpu.bitcast(wd << 16, F32)
    hi = pltpu.bitcast(wd & jnp.uint32(0xFFFF0000), F32)
    return jnp.concatenate([lo, hi], axis=1)


def _padded_rows(x_refs, meta_ref, first_tile):
    head = jnp.concatenate([jnp.zeros((PAD, D_MODEL), F32), meta_ref[...]], axis=0)
    first = jnp.where(first_tile, head, x_refs[0][...])
    return [first] + [r[...] for r in x_refs[1:]]


def _padded_row_specs(tm, tiles_per_seq):
    per = tm // BLK

    def spec(r):
        return pl.BlockSpec((None, BLK, D_MODEL),
                            lambda i: (i // tiles_per_seq, jnp.maximum((i % tiles_per_seq) * per + r - 1, 0), 0))

    return [spec(r) for r in range(per)]


def _inproj_kernel(*refs, tiles_per_seq, per):
    x_refs, (meta_ref, nrm_ref, w_ref, rc_ref, rs1_ref, rs2_ref,
             u_ref, q_ref, kx_ref, vx_ref, sgs_ref, sga_ref) = refs[:per], refs[per:]
    normed = []
    for x in _padded_rows(x_refs, meta_ref, pl.program_id(0) % tiles_per_seq == 0):
        ms = jnp.mean(x * x, axis=-1, keepdims=True)
        normed.append((x * lax.rsqrt(ms + RMS_EPS) * nrm_ref[...]).astype(BF16))
    hn = jnp.concatenate(normed, axis=0)

    def proj(lo, hi):
        return _dot(hn, w_ref[:, lo:hi])

    u_ref[...] = proj(0, 1024)

    rc = rc_ref[...]
    rs1 = rs1_ref[...]
    rs2 = rs2_ref[...]

    def rope(blk):
        return blk * rc + pltpu.roll(blk, ROT_HALF, 1) * rs1 + pltpu.roll(blk, LANES - ROT_HALF, 1) * rs2

    qf = proj(1024, 2048)
    scale = LOG2_E / math.sqrt(HEAD_DIM)
    for j in range(8):
        q_ref[:, LANES * j:LANES * (j + 1)] = (rope(qf[:, LANES * j:LANES * (j + 1)]) * scale).astype(BF16)

    kvf = proj(2048, 2304)
    k = rope(kvf[:, :LANES])
    v = kvf[:, LANES:]
    low = lax.broadcasted_iota(I32, k.shape, 1) < HEAD_DIM

    def expand(t, ref):
        tr = pltpu.roll(t, HEAD_DIM, 1)
        zero = jnp.zeros_like(t)
        ref[:, 0:128] = jnp.where(low, t, zero).astype(BF16)
        ref[:, 128:256] = jnp.where(low, zero, tr).astype(BF16)
        ref[:, 256:384] = jnp.where(low, tr, zero).astype(BF16)
        ref[:, 384:512] = jnp.where(low, zero, t).astype(BF16)

    expand(k, kx_ref)
    expand(v, vx_ref)
    sgs_ref[...] = jax.nn.sigmoid(proj(2304, 3328)).astype(BF16)
    sga_ref[...] = jax.nn.sigmoid(proj(3328, 4352)).astype(BF16)


def _inproj(x, meta, nrm, w_bf, rc, rs1, rs2, *, tm, lp):
    np_ = x.shape[0] * lp
    tiles_per_seq = lp // tm
    per = tm // BLK
    row = lambda i: (i, 0)
    fixed = lambda i: (0, 0)
    rope_idx = lambda i: (i % tiles_per_seq, 0)
    out_shape = (
        jax.ShapeDtypeStruct((np_, 1024), F32),
        jax.ShapeDtypeStruct((np_, 1024), BF16),
        jax.ShapeDtypeStruct((np_, 512), BF16),
        jax.ShapeDtypeStruct((np_, 512), BF16),
        jax.ShapeDtypeStruct((np_, 1024), BF16),
        jax.ShapeDtypeStruct((np_, 1024), BF16),
    )
    return pl.pallas_call(
        functools.partial(_inproj_kernel, tiles_per_seq=tiles_per_seq, per=per),
        out_shape=out_shape,
        grid=(np_ // tm,),
        in_specs=_padded_row_specs(tm, tiles_per_seq) + [
            pl.BlockSpec((N_META, D_MODEL), fixed),
            pl.BlockSpec((1, D_MODEL), fixed),
            pl.BlockSpec(w_bf.shape, fixed),
            pl.BlockSpec((tm, LANES), rope_idx),
            pl.BlockSpec((tm, LANES), rope_idx),
            pl.BlockSpec((tm, LANES), rope_idx),
        ],
        out_specs=(
            pl.BlockSpec((tm, 1024), row),
            pl.BlockSpec((tm, 1024), row),
            pl.BlockSpec((tm, 512), row),
            pl.BlockSpec((tm, 512), row),
            pl.BlockSpec((tm, 1024), row),
            pl.BlockSpec((tm, 1024), row),
        ),
        compiler_params=_cparams(("arbitrary",)),
        name="inproj",
    )(*([x] * per), meta, nrm, w_bf, rc, rs1, rs2)


def _expand_block_diag(dst_ref, row0, t, e_ref, row_shift, col_shift):
    n = t.shape[0]
    step = min(n, 256)
    for r0 in range(0, n, step):
        full = _dot(t[r0:r0 + step, :].astype(BF16), e_ref[...])
        rg = (lax.broadcasted_iota(I32, full.shape, 0) + r0) >> row_shift
        cg = lax.broadcasted_iota(I32, full.shape, 1) >> col_shift
        keep = ((rg ^ cg) & (SSM_GBLK - 1)) == 0
        dst_ref[row0 + r0:row0 + r0 + step, :] = jnp.where(keep, full, 0.0).astype(BF16)


def _ssm_kernel(u_ref, bc_ref, pb_ref, cc_ref, kc_ref, eb_ref, ec_ref, tab_ref, z_ref,
                wb_ref, wcd_ref, ust_ref, bu_ref, yn_ref, *, t8, seg, nchunk):
    rc = t8 // nchunk

    @pl.when(pl.program_id(1) == 0)
    def _():
        bc = bc_ref[...]
        bc_swapped = pltpu.roll(bc, STATE, 1)
        for i in range(SSM_STEP):
            blk = pb_ref[i, :, 0:LANES] * bc + pb_ref[i, :, LANES:2 * LANES] * bc_swapped
            _expand_block_diag(wb_ref, LANES * i, blk, eb_ref, 4, 6)
        c_r, c_i = cc_ref[:, 0:LANES], cc_ref[:, LANES:2 * LANES]
        p_r, p_i = cc_ref[:, 2 * LANES:3 * LANES], cc_ref[:, 3 * LANES:4 * LANES]
        _expand_block_diag(wcd_ref, 0, c_r * p_r - c_i * p_i, ec_ref, 6, 4)
        _expand_block_diag(wcd_ref, 512, -(c_r * p_i + c_i * p_r), ec_ref, 6, 4)
        kc = kc_ref[...]
        lane = lax.broadcasted_iota(I32, kc.shape, 1)
        for ip in range(SSM_STEP):
            blk = kc if ip == 0 else jnp.where(lane >= GROUP * ip, pltpu.roll(kc, GROUP * ip, 1), 0.0)
            _expand_block_diag(wcd_ref, 1024 + LANES * ip, blk, ec_ref, 4, 4)

    for i in range(SSM_STEP):
        ust_ref[:, LANES * i:LANES * (i + 1)] = u_ref[pl.ds(i, t8, stride=SSM_STEP), :].astype(BF16)

    for c in range(nchunk):
        r = _dot(ust_ref[c * rc:(c + 1) * rc, :], wb_ref[...])
        for m in range(8):
            bu_ref[m, c * rc:(c + 1) * rc, :] = r[:, LANES * m:LANES * (m + 1)]

    a_re = [jnp.broadcast_to(tab_ref[0:1, LANES * m:LANES * (m + 1)], (SSM_SEGS, LANES)) for m in range(4)]
    a_im = [jnp.broadcast_to(tab_ref[1:2, LANES * m:LANES * (m + 1)], (SSM_SEGS, LANES)) for m in range(4)]
    s_re = [jnp.broadcast_to(tab_ref[2:3, LANES * m:LANES * (m + 1)], (SSM_SEGS, LANES)) for m in range(4)]
    s_im = [jnp.broadcast_to(tab_ref[3:4, LANES * m:LANES * (m + 1)], (SSM_SEGS, LANES)) for m in range(4)]

    def cmul_add(mr, mi, xr, xi, br, bi):
        return mr * xr - mi * xi + br, mr * xi + mi * xr + bi

    def load(k):
        return [bu_ref[m, pl.ds(k, SSM_SEGS, stride=seg), :] for m in range(8)]

    def advance(st, b):
        nr, ni = [], []
        for m in range(4):
            r_, i_ = cmul_add(a_re[m], a_im[m], st[m], st[4 + m], b[m], b[4 + m])
            nr.append(r_)
            ni.append(i_)
        return tuple(nr + ni)

    zeros = tuple(jnp.zeros((SSM_SEGS, LANES), F32) for _ in range(8))
    fin = lax.fori_loop(0, seg, lambda k, st: advance(st, load(k)), zeros)

    first = lax.broadcasted_iota(I32, (SSM_SEGS, LANES), 0) == 0

    def shift_down(t):
        return jnp.where(first, 0.0, pltpu.roll(t, 1, 0))

    tot = fin
    for _ in range(SSM_SEGS - 1):
        nxt_r, nxt_i = [], []
        for m in range(4):
            r_, i_ = cmul_add(s_re[m], s_im[m], shift_down(tot[m]), shift_down(tot[4 + m]), fin[m], fin[4 + m])
            nxt_r.append(r_)
            nxt_i.append(i_)
        tot = tuple(nxt_r + nxt_i)
    init = tuple(shift_down(t) for t in tot)

    def pass2(k, st):
        b = load(k)
        for m in range(8):
            bu_ref[m, pl.ds(k, SSM_SEGS, stride=seg), :] = st[m]
        return advance(st, b)

    lax.fori_loop(0, seg, pass2, init)

    for c in range(nchunk):
        xp = jnp.concatenate([bu_ref[m, c * rc:(c + 1) * rc, :] for m in range(8)], axis=1).astype(BF16)
        lhs = jnp.concatenate([xp, ust_ref[c * rc:(c + 1) * rc, :]], axis=1)
        y = _dot(lhs, wcd_ref[...])
        for i in range(SSM_STEP):
            yn_ref[pl.ds(SSM_STEP * c * rc + i, rc, stride=SSM_STEP), :] = y[:, LANES * i:LANES * (i + 1)]

    y = yn_ref[...] + tab_ref[4:5, 0:LANES] * u_ref[...]
    z_ref[...] = jax.nn.gelu(y).astype(BF16)


def _ssm(u, bc, pb, cc, kc, tab, *, bsz, lp, nchunk):
    np_ = u.shape[0]
    t8 = lp // SSM_STEP
    seg = t8 // SSM_SEGS
    nblk = D_MODEL // LANES
    col = jnp.arange(1024)
    src_b = (col >> 9) * STATE + (col & (STATE - 1))
    src_c = (col >> 7) * GROUP + (col & (GROUP - 1))
    eb = (jnp.arange(LANES)[:, None] == src_b[None, :]).astype(BF16)
    ec = (jnp.arange(LANES)[:, None] == src_c[None, :]).astype(BF16)
    kern = functools.partial(_ssm_kernel, t8=t8, seg=seg, nchunk=nchunk)
    per_q = lambda q, b: (q, 0, 0)
    fixed = lambda q, b: (0, 0)
    return pl.pallas_call(
        kern,
        out_shape=jax.ShapeDtypeStruct((np_, D_MODEL), BF16),
        grid=(nblk, bsz),
        in_specs=[
            pl.BlockSpec((lp, LANES), lambda q, b: (b, q)),
            pl.BlockSpec((None, LANES, LANES), per_q),
            pl.BlockSpec((SSM_STEP, None, LANES, 2 * LANES), lambda q, b: (0, q, 0, 0)),
            pl.BlockSpec((None, SSM_GBLK * STATE, 4 * LANES), per_q),
            pl.BlockSpec((None, LANES, LANES), per_q),
            pl.BlockSpec((LANES, 1024), fixed),
            pl.BlockSpec((LANES, 1024), fixed),
            pl.BlockSpec((None, 8, 512), per_q),
        ],
        out_specs=pl.BlockSpec((lp, LANES), lambda q, b: (b, q)),
        scratch_shapes=[
            pltpu.VMEM((1024, 1024), BF16),
            pltpu.VMEM((2048, 1024), BF16),
            pltpu.VMEM((t8, 1024), BF16),
            pltpu.VMEM((8, t8, LANES), F32),
            pltpu.VMEM((lp, LANES), F32),
        ],
        compiler_params=_cparams(("arbitrary", "arbitrary")),
        name="ssm",
    )(u, bc, pb, cc, kc, eb, ec, tab)


def _ssm_tables(lam_re, lam_im, log_dt, b_re, b_im, c_re, c_im, d, seg):
    hi = lax.Precision.HIGHEST
    lr, li = lam_re.astype(F32), lam_im.astype(F32)
    dt = jnp.exp(log_dt.astype(F32))[:, None]
    mag = jnp.exp(dt * lr)
    ar, ai = mag * jnp.cos(dt * li), mag * jnp.sin(dt * li)
    den = lr * lr + li * li
    nr, ni = ar - 1.0, ai
    fr, fi = (nr * lr + ni * li) / den, (ni * lr - nr * li) / den
    br, bi = b_re.astype(F32), b_im.astype(F32)
    bbr = fr[..., None] * br - fi[..., None] * bi
    bbi = fr[..., None] * bi + fi[..., None] * br

    def cmul(xr, xi, yr, yi):
        return xr * yr - xi * yi, xr * yi + xi * yr

    pr, pi = [jnp.ones_like(ar)], [jnp.zeros_like(ar)]
    for _ in range(SSM_STEP):
        r_, i_ = cmul(pr[-1], pi[-1], ar, ai)
        pr.append(r_)
        pi.append(i_)
    pw_r, pw_i = jnp.stack(pr), jnp.stack(pi)
    a8r, a8i = pw_r[SSM_STEP], pw_i[SSM_STEP]
    sr, si = jnp.ones_like(ar), jnp.zeros_like(ar)
    qr, qi = a8r, a8i
    e = seg
    while e:
        if e & 1:
            sr, si = cmul(sr, si, qr, qi)
        qr, qi = cmul(qr, qi, qr, qi)
        e >>= 1

    cr, ci = c_re.astype(F32), c_im.astype(F32)
    nb = N_GROUPS // SSM_GBLK

    bc = jnp.concatenate([jnp.swapaxes(bbr, 1, 2), jnp.swapaxes(bbi, 1, 2)], axis=2).reshape(nb, LANES, LANES)
    pb = jnp.stack([jnp.concatenate([pr[SSM_STEP - 1 - i]] * 2 + [-pi[SSM_STEP - 1 - i], pi[SSM_STEP - 1 - i]], axis=1)
                    for i in range(SSM_STEP)])
    pb = jnp.broadcast_to(pb[:, :, None, :], (SSM_STEP, N_GROUPS, GROUP, 2 * LANES))
    pb = pb.reshape(SSM_STEP, nb, LANES, 2 * LANES)
    crt = jnp.tile(jnp.swapaxes(cr, 1, 2), (1, 1, SSM_STEP))
    cit = jnp.tile(jnp.swapaxes(ci, 1, 2), (1, 1, SSM_STEP))
    prc = jnp.repeat(jnp.stack(pr[1:], axis=2), GROUP, axis=2)
    pic = jnp.repeat(jnp.stack(pi[1:], axis=2), GROUP, axis=2)
    cc = jnp.concatenate([crt, cit, prc, pic], axis=2).reshape(nb, SSM_GBLK * STATE, 4 * LANES)
    tr, ti = cmul(pw_r[:SSM_STEP, :, :, None], pw_i[:SSM_STEP, :, :, None], bbr[None], bbi[None])
    kt = (jnp.einsum('ghp,tgpk->gkth', cr, tr, precision=hi)
          - jnp.einsum('ghp,tgpk->gkth', ci, ti, precision=hi))
    kc = kt.reshape(nb, LANES, LANES)

    def lanes(t):
        return t.reshape(nb, SSM_GBLK * STATE)

    dl = jnp.pad(d.astype(F32).reshape(nb, SSM_GBLK * GROUP), ((0, 0), (0, 512 - LANES)))
    zero = jnp.zeros((nb, 512), F32)
    tab = jnp.stack([lanes(a8r), lanes(a8i), lanes(sr), lanes(si), dl, zero, zero, zero], axis=1)
    return bc, pb, cc, kc, tab


def _attn_kernel(q_ref, kc_ref, kp_ref, km_ref, vc_ref, vp_ref, vm_ref, sink_ref, o_ref):
    n = pl.program_id(1)
    nk = 3 * BLK
    row = lax.broadcasted_iota(I32, (BLK, nk), 0)
    col = lax.broadcasted_iota(I32, (BLK, nk), 1)
    band = (col > row) & (col <= row + BLK) & (col >= 2 * BLK - BLK * n)
    mrow = col - 2 * BLK
    meta = (mrow >= PAD) & (mrow <= BLK * n + row)
    bias = jnp.where(band | meta, 0.0, NEG_INF)

    for g in range(N_KV_HEADS):
        qg = jnp.concatenate([q_ref[:, LANES * (4 * g + jj):LANES * (4 * g + jj + 1)] for jj in range(4)], axis=0)
        probs, rdens, keys, vals = [], [], [], []
        for par in range(2):
            sl = slice(LANES * (2 * g + par), LANES * (2 * g + par + 1))
            keys += [kp_ref[:, sl], kc_ref[:, sl], km_ref[:, sl]]
            vals += [vp_ref[:, sl], vc_ref[:, sl], vm_ref[:, sl]]
        s_both = _dot_nt(qg, jnp.concatenate(keys, axis=0))
        for par in range(2):
            s_all = s_both[:, 3 * BLK * par:3 * BLK * (par + 1)]
            pp, rden = [], []
            for jj in range(4):
                sink = sink_ref[8 * g + 2 * jj + par] * LOG2_E
                s = s_all[BLK * jj:BLK * (jj + 1), :] + bias
                mx = jnp.maximum(jnp.max(s, axis=1, keepdims=True), sink)
                p = jnp.exp2(s - mx)
                den = jnp.sum(p, axis=1, keepdims=True) + jnp.exp2(sink - mx)
                pp.append(p.astype(BF16))
                rden.append(jnp.broadcast_to(1.0 / den, (BLK, LANES)))
            probs.append(jnp.concatenate(pp, axis=0))
            rdens.append(jnp.concatenate(rden, axis=0))
        o = _dot(jnp.concatenate(probs, axis=1), jnp.concatenate(vals, axis=0))
        low = lax.broadcasted_iota(I32, o.shape, 1) < HEAD_DIM
        o = o * jnp.where(low, rdens[0], rdens[1])
        for jj in range(4):
            o_ref[:, LANES * (4 * g + jj):LANES * (4 * g + jj + 1)] = o[BLK * jj:BLK * (jj + 1), :].astype(BF16)


def _attention(q, kx, vx, sinks, *, bsz, nb):
    np_ = q.shape[0]
    cur = lambda b, n: (b * nb + n, 0)
    prev = lambda b, n: (b * nb + jnp.maximum(n - 1, 0), 0)
    first = lambda b, n: (b * nb, 0)
    return pl.pallas_call(
        _attn_kernel,
        out_shape=jax.ShapeDtypeStruct((np_, 1024), BF16),
        grid=(bsz, nb),
        in_specs=[
            pl.BlockSpec((BLK, 1024), cur),
            pl.BlockSpec((BLK, 512), cur),
            pl.BlockSpec((BLK, 512), prev),
            pl.BlockSpec((BLK, 512), first),
            pl.BlockSpec((BLK, 512), cur),
            pl.BlockSpec((BLK, 512), prev),
            pl.BlockSpec((BLK, 512), first),
            pl.BlockSpec(memory_space=pltpu.SMEM),
        ],
        out_specs=pl.BlockSpec((BLK, 1024), cur),
        compiler_params=_cparams(("arbitrary", "arbitrary")),
        name="attn",
    )(q, kx, kx, kx, vx, vx, vx, sinks)


def _post_kernel(*refs, tiles_per_seq, per):
    x_refs, (meta_ref, z_ref, at_ref, sgs_ref, sga_ref, wglu_ref, bglu_ref, wbs_ref, wba_ref, wout_ref,
             nffn_ref, wrh_ref, wrl_ref, br_ref,
             h2_ref, hnp_ref, lslot_ref, ecol_ref, cnt_ref) = refs[:per], refs[per:]
    i = pl.program_id(0)

    z = z_ref[...]
    t = _dot(z, wglu_ref[...]) + bglu_ref[...]
    so = (z.astype(F32) * jax.nn.sigmoid(t)).astype(BF16)
    mix = (sgs_ref[...].astype(F32) * _dot(so, wbs_ref[...])
           + sga_ref[...].astype(F32) * _dot(at_ref[...], wba_ref[...]))
    mixed = _dot(mix.astype(BF16), wout_ref[...])
    logit_blocks = []
    for r, xr in enumerate(_padded_rows(x_refs, meta_ref, i % tiles_per_seq == 0)):
        rows = slice(BLK * r, BLK * (r + 1))
        h2 = xr + mixed[rows, :]
        h2_ref[rows, :] = h2
        ms = jnp.mean(h2 * h2, axis=-1, keepdims=True)
        hn = h2 * lax.rsqrt(ms + RMS_EPS) * nffn_ref[...]
        hb = hn.astype(BF16)
        hnp_ref[rows, :] = hb
        lo = (hn - hb.astype(F32)).astype(BF16)
        logit_blocks.append(_dot(hb, wrh_ref[...]) + _dot(lo, wrh_ref[...]) + _dot(hb, wrl_ref[...]) + br_ref[...])
    logits = jnp.concatenate(logit_blocks, axis=0)

    tm = logits.shape[0]
    lane = lax.broadcasted_iota(I32, (tm, LANES), 1)
    lane_f = lane.astype(F32)
    work = logits
    vals, hots, idxs = [], [], []
    for _ in range(TOP_K):
        mx = jnp.max(work, axis=1, keepdims=True)
        idx = jnp.min(jnp.where(work == mx, lane_f, float(LANES)), axis=1, keepdims=True)
        hot = lane_f == idx
        vals.append(mx)
        idxs.append(idx)
        hots.append(hot)
        work = jnp.where(hot, -jnp.inf, work)

    ex = [jnp.exp(v - vals[0]) for v in vals]
    tot = ex[0] + ex[1] + ex[2] + ex[3]
    gates = [e / tot for e in ex]

    onehot = jnp.zeros((tm, LANES), F32)
    for hot in hots:
        onehot = onehot + jnp.where(hot, 1.0, 0.0)
    r_i = lax.broadcasted_iota(I32, (tm, tm), 0)
    c_i = lax.broadcasted_iota(I32, (tm, tm), 1)
    tri = jnp.where(r_i > c_i, 1.0, 0.0).astype(BF16)
    before = _dot(tri, onehot.astype(BF16))
    count = jnp.sum(onehot, axis=0, keepdims=True)
    gran = jnp.floor((count + (GRAN - 1)) * (1.0 / GRAN))
    e_r = lax.broadcasted_iota(I32, (LANES, LANES), 0)
    e_c = lax.broadcasted_iota(I32, (LANES, LANES), 1)
    upper = jnp.where(e_r < e_c, 1.0, 0.0).astype(BF16)
    gstart = _dot(jnp.broadcast_to(gran, (8, LANES)).astype(BF16), upper)[0:1, :]
    base = before + gstart * float(GRAN)
    slots = [jnp.sum(jnp.where(hot, base, 0.0), axis=1, keepdims=True) for hot in hots]
    cnt_ref[...] = jnp.broadcast_to(gran, cnt_ref.shape)

    def spread(cols, dtype):
        out = jnp.zeros((tm, LANES), dtype)
        for k, cval in enumerate(cols):
            out = jnp.where(lane == k, cval, out)
        return out

    lslot_ref[...] = spread(slots, F32)
    cols = []
    for g in gates:
        g1 = g.astype(BF16).astype(F32)
        g2 = (g - g1).astype(BF16).astype(F32)
        cols += [g1, g2, g - g1 - g2]
    for s in slots:
        hi = jnp.floor(s * (1.0 / 64.0))
        cols += [hi, s - 64.0 * hi]
    ecol_ref[...] = spread(cols, F32)


def _post(x, meta, z, attn, sgs, sga, wglu, bglu, wbs, wba, wout, nffn, wrh, wrl, br, *, tm, lp):
    np_ = z.shape[0]
    tiles_per_seq = lp // tm
    per = tm // BLK
    row = lambda i: (i, 0)
    fixed = lambda i: (0, 0)
    sq = pl.BlockSpec((1024, 1024), fixed)
    vec = pl.BlockSpec((1, 1024), fixed)
    out_shape = (
        jax.ShapeDtypeStruct((np_, 1024), F32),
        jax.ShapeDtypeStruct((np_, 1024), BF16),
        jax.ShapeDtypeStruct((np_, LANES), F32),
        jax.ShapeDtypeStruct((np_, LANES), F32),
        jax.ShapeDtypeStruct((np_ // tm * 8, LANES), F32),
    )
    return pl.pallas_call(
        functools.partial(_post_kernel, tiles_per_seq=tiles_per_seq, per=per),
        out_shape=out_shape,
        grid=(np_ // tm,),
        in_specs=_padded_row_specs(tm, tiles_per_seq) + [
            pl.BlockSpec((N_META, D_MODEL), fixed),
            pl.BlockSpec((tm, 1024), row), pl.BlockSpec((tm, 1024), row),
            pl.BlockSpec((tm, 1024), row), pl.BlockSpec((tm, 1024), row),
            sq, vec, sq, sq, sq, vec,
            pl.BlockSpec((1024, LANES), fixed), pl.BlockSpec((1024, LANES), fixed),
            pl.BlockSpec((1, LANES), fixed),
        ],
        out_specs=(
            pl.BlockSpec((tm, 1024), row), pl.BlockSpec((tm, 1024), row),
            pl.BlockSpec((tm, LANES), row), pl.BlockSpec((tm, LANES), row),
            pl.BlockSpec((8, LANES), row),
        ),
        compiler_params=_cparams(("arbitrary",)),
        name="post",
    )(*([x] * per), meta, z, attn, sgs, sga, wglu, bglu, wbs, wba, wout, nffn, wrh, wrl, br)


def _sort_kernel(hnp_ref, ecol_ref, xloc_ref, *, tm, srows):
    ecol = ecol_ref[...].astype(BF16)
    pay = jnp.concatenate([hnp_ref[...], ecol], axis=1)

    r8 = lax.broadcasted_iota(I32, (8, LANES), 0)
    l8 = lax.broadcasted_iota(I32, (8, LANES), 1)
    digit_w = jnp.where(l8 == 12 + 2 * r8, 64.0, jnp.where(l8 == 13 + 2 * r8, 1.0, 0.0)).astype(BF16)
    slot_rows = _dot_nt(digit_w, ecol)

    rr = lax.broadcasted_iota(I32, (LANES, LANES), 0)
    cc = lax.broadcasted_iota(I32, (LANES, LANES), 1)
    kk = jnp.where(cc < 3 * TOP_K, cc // 3, -LANES)
    spread_w = jnp.where(rr == 12 + 2 * kk, 64.0, jnp.where(rr == 13 + 2 * kk, 1.0, 0.0)).astype(BF16)
    gate_lane = lax.broadcasted_iota(I32, (SORT_CHUNK, LANES), 1) < 3 * TOP_K

    for c in range(srows // SORT_CHUNK):
        s_id = (lax.broadcasted_iota(I32, (SORT_CHUNK, tm), 0) + c * SORT_CHUNK).astype(F32)
        perm = jnp.zeros((SORT_CHUNK, tm), F32)
        for k in range(TOP_K):
            perm = perm + jnp.where(slot_rows[k:k + 1, :] == s_id, 1.0, 0.0)
        got = _dot(perm.astype(BF16), pay)
        rows = slice(c * SORT_CHUNK, (c + 1) * SORT_CHUNK)
        xloc_ref[rows, 0:512] = _pack_bf16_pairs(got[:, :D_MODEL], exact=True)
        extra = got[:, D_MODEL:]
        owner = _dot(extra.astype(BF16), spread_w)
        me = (lax.broadcasted_iota(I32, (SORT_CHUNK, LANES), 0) + c * SORT_CHUNK).astype(F32)
        mine = jnp.where(gate_lane, owner, -1.0) == me
        gate = jnp.sum(jnp.where(mine, extra, 0.0), axis=1, keepdims=True)
        xloc_ref[rows, 512:SORT_W] = pltpu.bitcast(jnp.broadcast_to(gate, (SORT_CHUNK, LANES)), U32)


def _sort(hnp, ecol, *, tm, srows):
    np_ = hnp.shape[0]
    ntile = np_ // tm
    row = lambda i: (i, 0)
    return pl.pallas_call(
        functools.partial(_sort_kernel, tm=tm, srows=srows),
        out_shape=jax.ShapeDtypeStruct((ntile * srows, SORT_W), U32),
        grid=(ntile,),
        in_specs=[pl.BlockSpec((tm, D_MODEL), row), pl.BlockSpec((tm, LANES), row)],
        out_specs=pl.BlockSpec((srows, SORT_W), row),
        compiler_params=_cparams(("arbitrary",)),
        name="sort",
    )(hnp, ecol)


def _moe_kernel(blk_ref, exp_ref, lo_ref, hi_ref, kind_ref, wslot_ref, nexte_ref, gsrc_ref,
                xloc_ref, wgu_ref, bgu_ref, wd_ref, bd_ref, yloc_ref,
                xbuf, ybuf, sem_in, sem_out, wgu_f, wd_f, sem_w, wgu_bf, wd_bf, *, nseg, dump0):
    i = pl.program_id(0)
    prev = jnp.maximum(i - 1, 0)
    nxt = jnp.minimum(i + 1, nseg - 1)
    slot = i % 2
    new_expert = (i == 0) | (exp_ref[i] != exp_ref[prev])

    def gather(seg, dst_slot, start):
        b = blk_ref[seg]
        for g in range(MOE_GRANS):
            src = pl.multiple_of(gsrc_ref[b * MOE_GRANS + g] * GRAN, GRAN)
            cp = pltpu.make_async_copy(xloc_ref.at[pl.ds(src, GRAN)], xbuf.at[dst_slot, pl.ds(g * GRAN, GRAN)],
                                       sem_in.at[dst_slot])
            if start:
                cp.start()
            else:
                cp.wait()

    def scatter(seg, src_slot, start):
        b = blk_ref[seg]
        lo = lo_ref[seg]
        hi = hi_ref[seg]
        for g in range(MOE_GRANS):
            keep = (g >= lo) & (g < hi)
            spare = dump0 + src_slot * MOE_GRANS + g
            dst = pl.multiple_of(jnp.where(keep, gsrc_ref[b * MOE_GRANS + g], spare) * GRAN, GRAN)
            cp = pltpu.make_async_copy(ybuf.at[src_slot, pl.ds(g * GRAN, GRAN)], yloc_ref.at[pl.ds(dst, GRAN)],
                                       sem_out.at[src_slot])
            if start:
                cp.start()
            else:
                cp.wait()

    @pl.when(i == 0)
    def _():
        gather(i, slot, True)
        ybuf[1 - slot] = jnp.zeros((MOE_BLK, 512), U32)
        for g in range(MOE_GRANS):
            pltpu.make_async_copy(ybuf.at[1 - slot, pl.ds(g * GRAN, GRAN)],
                                  yloc_ref.at[pl.ds((dump0 + (1 - slot) * MOE_GRANS + g) * GRAN, GRAN)],
                                  sem_out.at[1 - slot]).start()

    gather(i, slot, False)

    def fetch(e, slot, start):
        for src, dst in ((wgu_ref, wgu_f), (wd_ref, wd_f)):
            cp = pltpu.make_async_copy(src.at[e], dst.at[slot], sem_w.at[slot])
            if start:
                cp.start(priority=1)
            else:
                cp.wait()

    wslot = wslot_ref[i]

    @pl.when(i == 0)
    def _():
        fetch(exp_ref[i], wslot, True)

    @pl.when(new_expert)
    def _():
        fetch(exp_ref[i], wslot, False)
        wgu_bf[...] = wgu_f[wslot].astype(BF16)
        wd_bf[...] = wd_f[wslot].astype(BF16)

    @pl.when(new_expert & (nexte_ref[i] >= 0))
    def _():
        fetch(nexte_ref[i], 1 - wslot, True)

    kind = kind_ref[i]

    @pl.when(kind == 1)
    def _():
        gather(nxt, 1 - slot, True)
        xw = xbuf[slot]
        xb = _unpack_bf16_pairs(xw[:, 0:512]).astype(BF16)
        gate = pltpu.bitcast(xw[:, 512:SORT_W], F32)
        gu = _dot(xb, wgu_bf[...]) + bgu_ref[...]
        g = jnp.minimum(gu[:, :D_FF], SWIGLU_LIMIT)
        up = jnp.clip(gu[:, D_FF:], -SWIGLU_LIMIT, SWIGLU_LIMIT)
        hid = g * jax.nn.sigmoid(SWIGLU_ALPHA * g) * (up + 1.0)
        y = _dot(hid.astype(BF16), wd_bf[...]) + bd_ref[...]
        y = y * jnp.concatenate([gate] * (D_MODEL // LANES), axis=1)
        ybuf[slot] = _pack_bf16_pairs(y)
        scatter(i, slot, True)
        scatter(prev, 1 - slot, False)

    @pl.when(kind != 1)
    def _():
        gather(nxt, 1 - slot, True)
        ybuf[slot] = jnp.zeros((MOE_BLK, 512), U32)
        scatter(i, slot, True)
        scatter(prev, 1 - slot, False)

    @pl.when(i == nseg - 1)
    def _():
        scatter(i, slot, False)
        gather(nxt, 1 - slot, False)


def _moe(blk, exp, lo, hi, kind, gsrc, xloc, wgu, bgu, wd, bd, *, ncap):
    nseg = blk.shape[0]
    dump0 = ncap
    change = jnp.concatenate([jnp.zeros((1,), I32), (exp[1:] != exp[:-1]).astype(I32)])
    wslot = jnp.cumsum(change) % 2
    later = jnp.where(exp[None, :] > exp[:, None], exp[None, :], N_EXPERTS)
    nexte = jnp.min(later, axis=1)
    nexte = jnp.where(nexte >= N_EXPERTS, -1, nexte)
    wmap = lambda i, b, e, *_: (e[i], 0, 0)
    grid_spec = pltpu.PrefetchScalarGridSpec(
        num_scalar_prefetch=8,
        grid=(nseg,),
        in_specs=[
            pl.BlockSpec(memory_space=pl.ANY),
            pl.BlockSpec(memory_space=pl.ANY),
            pl.BlockSpec((None, 1, 2 * D_FF), wmap),
            pl.BlockSpec(memory_space=pl.ANY),
            pl.BlockSpec((None, 1, D_MODEL), wmap),
        ],
        out_specs=pl.BlockSpec(memory_space=pl.ANY),
        scratch_shapes=[
            pltpu.VMEM((2, MOE_BLK, SORT_W), U32),
            pltpu.VMEM((2, MOE_BLK, 512), U32),
            pltpu.SemaphoreType.DMA((2,)),
            pltpu.SemaphoreType.DMA((2,)),
            pltpu.VMEM((2, D_MODEL, 2 * D_FF), F32),
            pltpu.VMEM((2, D_FF, D_MODEL), F32),
            pltpu.SemaphoreType.DMA((2,)),
            pltpu.VMEM((D_MODEL, 2 * D_FF), BF16),
            pltpu.VMEM((D_FF, D_MODEL), BF16),
        ],
    )
    return pl.pallas_call(
        functools.partial(_moe_kernel, nseg=nseg, dump0=dump0),
        out_shape=jax.ShapeDtypeStruct(((ncap + 2 * MOE_GRANS) * GRAN, 512), U32),
        grid_spec=grid_spec,
        compiler_params=_cparams(("arbitrary",)),
        name="moe",
    )(blk, exp, lo, hi, kind, wslot.astype(I32), nexte.astype(I32), gsrc, xloc, wgu, bgu, wd, bd)


def _combine_kernel(y_ref, lslot_ref, h2_ref, nfin_ref, o_ref, obuf, sem, *, tm, srows, tiles_per_seq, ntile):
    i = pl.program_id(0)
    per = tm // BLK
    slot = i % 2
    slots = lslot_ref[...]
    acc = h2_ref[...]
    for c in range(srows // COMBINE_CHUNK):
        s_id = (lax.broadcasted_iota(I32, (tm, COMBINE_CHUNK), 1) + c * COMBINE_CHUNK).astype(F32)
        take = jnp.zeros((tm, COMBINE_CHUNK), F32)
        for k in range(TOP_K):
            take = take + jnp.where(slots[:, k:k + 1] == s_id, 1.0, 0.0)
        rows = _unpack_bf16_pairs(y_ref[c * COMBINE_CHUNK:(c + 1) * COMBINE_CHUNK, :]).astype(BF16)
        acc = acc + _dot(take.astype(BF16), rows)
    ms = jnp.mean(acc * acc, axis=-1, keepdims=True)
    obuf[slot] = acc * lax.rsqrt(ms + RMS_EPS) * nfin_ref[...]

    def copies(step, start):
        b = step // tiles_per_seq
        j = step % tiles_per_seq
        src_slot = step % 2
        for r in range(per):
            n = j * per + r
            cp = pltpu.make_async_copy(obuf.at[src_slot, pl.ds(r * BLK, BLK)],
                                       o_ref.at[b, pl.ds(pl.multiple_of(jnp.maximum(n - 1, 0) * BLK, BLK), BLK)],
                                       sem.at[src_slot])

            def go():
                if start:
                    cp.start()
                else:
                    cp.wait()

            if r == 0:
                pl.when(j > 0)(go)
            else:
                go()

    copies(i, True)

    @pl.when(i > 0)
    def _():
        copies(i - 1, False)

    @pl.when(i == ntile - 1)
    def _():
        copies(i, False)


def _combine(yloc, lslot, h2, nfin, *, bsz, seq, tm, lp, srows):
    np_ = h2.shape[0]
    tiles_per_seq = lp // tm
    row = lambda i: (i, 0)
    return pl.pallas_call(
        functools.partial(_combine_kernel, tm=tm, srows=srows, tiles_per_seq=tiles_per_seq, ntile=np_ // tm),
        out_shape=jax.ShapeDtypeStruct((bsz, seq, D_MODEL), F32),
        grid=(np_ // tm,),
        in_specs=[
            pl.BlockSpec((srows, 512), row),
            pl.BlockSpec((tm, LANES), row),
            pl.BlockSpec((tm, D_MODEL), row),
            pl.BlockSpec((1, D_MODEL), lambda i: (0, 0)),
        ],
        out_specs=pl.BlockSpec(memory_space=pl.ANY),
        scratch_shapes=[pltpu.VMEM((2, tm, D_MODEL), F32), pltpu.SemaphoreType.DMA((2,))],
        compiler_params=_cparams(("arbitrary",)),
        name="combine",
    )(yloc, lslot, h2, nfin)


def _rope_tables(lp):
    pos = np.arange(lp, dtype=np.float32) - np.float32(PAD)
    inv_freq = np.float32(ROPE_THETA) ** (-np.arange(0, 2 * ROT_HALF, 2, dtype=np.float32) / np.float32(2 * ROT_HALF))
    ang = (pos[:, None] * inv_freq[None, :]).astype(np.float32)
    cos, sin = np.cos(ang), np.sin(ang)
    ones = np.ones((lp, HEAD_DIM - 2 * ROT_HALF), np.float32)
    zeros8 = np.zeros((lp, ROT_HALF), np.float32)
    zrest = np.zeros((lp, HEAD_DIM - 2 * ROT_HALF), np.float32)
    c = np.concatenate([cos, cos, ones], axis=1)
    s1 = np.concatenate([zeros8, sin, zrest], axis=1)
    s2 = np.concatenate([-sin, zeros8, zrest], axis=1)
    tile = lambda t: jnp.asarray(np.concatenate([t, t], axis=1), F32)
    return tile(c), tile(s1), tile(s2)


def _route_tables(grans, gt):
    ntile = grans.shape[0]
    ncap = ntile * gt
    nown = N_EXPERTS + 1
    npos = ncap + nown * MOE_GRANS
    nblocks = npos // MOE_GRANS
    used = jnp.sum(grans, axis=1)
    lstart = jnp.cumsum(grans, axis=1) - grans
    tile0 = jnp.arange(ntile, dtype=I32)[:, None] * gt
    lens = jnp.concatenate([grans.T, (gt - used)[None, :]], axis=0)
    srcs = jnp.concatenate([(tile0 + lstart).T, (tile0[:, 0] + used)[None, :]], axis=0)
    total = jnp.sum(lens, axis=1)
    room = (total + MOE_GRANS - 1) // MOE_GRANS * MOE_GRANS
    oend = jnp.cumsum(room)
    ostart = oend - room
    run_start = (ostart[:, None] + jnp.cumsum(lens, axis=1) - lens).reshape(-1)
    off = srcs.reshape(-1) - run_start
    delta = off - jnp.concatenate([jnp.zeros((1,), I32), off[:-1]])
    pos = jnp.arange(npos, dtype=I32)
    gsrc = pos + jnp.sum(jnp.where(run_start[None, :] <= pos[:, None], delta[None, :], 0), axis=1)

    first = jnp.arange(nblocks, dtype=I32) * MOE_GRANS
    owner = jnp.sum((oend[None, :] <= first[:, None]).astype(I32), axis=1)
    mine = owner[:, None] == jnp.arange(nown, dtype=I32)[None, :]
    valid_end = jnp.sum(jnp.where(mine, (ostart + total)[None, :], 0), axis=1)
    hi = jnp.clip(valid_end - first, 0, MOE_GRANS)
    lo = jnp.zeros_like(hi)
    gsrc = jnp.where((pos % MOE_GRANS) < jnp.repeat(hi, MOE_GRANS), gsrc, 0)
    kind = jnp.where(hi > 0, jnp.where(owner >= N_EXPERTS, 2, 1), 0)
    works = kind == 1
    first_work = jnp.min(jnp.where(works, owner, N_EXPERTS - 1))
    exp = jnp.maximum(lax.cummax(jnp.where(works, owner, -1), axis=0), first_work)
    blk = jnp.arange(nblocks, dtype=I32)
    return gsrc.astype(I32), blk, exp.astype(I32), lo.astype(I32), hi.astype(I32), kind.astype(I32)


def kernel(x, meta_tokens, norm_mix, w_in, ssm_lam_re, ssm_lam_im, ssm_log_dt, ssm_b_re, ssm_b_im,
           ssm_c_re, ssm_c_im, ssm_d, w_glu, b_glu, attn_sinks, w_br_ssm, w_br_attn, w_out,
           norm_ffn, w_router, b_router, w_gate_up, b_gate_up, w_down, b_down, norm_final):
    bsz, seq, _ = x.shape
    lp = seq + BLK
    nb = lp // BLK
    np_ = bsz * lp
    tm = 640 if lp % 640 == 0 else BLK
    t8 = lp // SSM_STEP
    seg = t8 // SSM_SEGS
    nchunk = 5 if (t8 % 5 == 0 and (t8 // 5) % 16 == 0) else 1

    meta = meta_tokens.astype(F32)
    rc, rs1, rs2 = _rope_tables(lp)
    u, q, kx, vx, sgs, sga = _inproj(x, meta, norm_mix[0][None], w_in[0].astype(BF16), rc, rs1, rs2, tm=tm, lp=lp)

    bc, pb, cc, kc, tab = _ssm_tables(ssm_lam_re[0], ssm_lam_im[0], ssm_log_dt[0], ssm_b_re[0], ssm_b_im[0],
                                      ssm_c_re[0], ssm_c_im[0], ssm_d[0], seg)
    z = _ssm(u, bc, pb, cc, kc, tab, bsz=bsz, lp=lp, nchunk=nchunk)

    attn = _attention(q, kx, vx, attn_sinks[0].astype(F32), bsz=bsz, nb=nb)

    wr = jnp.pad(w_router[0].astype(F32), ((0, 0), (0, LANES - N_EXPERTS)))
    wrh = wr.astype(BF16)
    wrl = (wr - wrh.astype(F32)).astype(BF16)
    br = jnp.concatenate([b_router[0].astype(F32), jnp.full((LANES - N_EXPERTS,), NEG_INF, F32)])[None]
    h2, hnp, lslot, ecol, cnt = _post(
        x, meta, z, attn, sgs, sga, w_glu[0].astype(BF16), b_glu[0][None], w_br_ssm[0].astype(BF16),
        w_br_attn[0].astype(BF16), w_out[0].astype(BF16), norm_ffn[0][None], wrh, wrl, br, tm=tm, lp=lp)

    unit = math.lcm(SORT_CHUNK, COMBINE_CHUNK)
    srows = -(-(tm * TOP_K + N_EXPERTS * (GRAN - 1)) // unit) * unit
    gt = srows // GRAN
    ntile = np_ // tm
    grans = cnt.reshape(ntile, 8, LANES)[:, 0, :N_EXPERTS].astype(I32)
    gsrc, sblk, sexp, slo, shi, skind = _route_tables(grans, gt)

    xloc = _sort(hnp, ecol, tm=tm, srows=srows)
    yloc = _moe(sblk, sexp, slo, shi, skind, gsrc, xloc, w_gate_up[0], b_gate_up[0][:, None, :],
                w_down[0], b_down[0][:, None, :], ncap=ntile * gt)
    return _combine(yloc, lslot, h2, norm_final[None], bsz=bsz, seq=seq, tm=tm, lp=lp, srows=srows)
```

```python
import functools
import math

import jax
import jax.numpy as jnp
import numpy as np
from jax import lax
from jax.experimental import pallas as pl
from jax.experimental.pallas import tpu as pltpu

F32 = jnp.float32
BF16 = jnp.bfloat16
I32 = jnp.int32
U32 = jnp.uint32

D_MODEL = 1024
N_META = 16
BLK = 128
PAD = BLK - N_META
N_GROUPS = 64
GROUP = 16
STATE = 64
N_Q_HEADS = 16
N_KV_HEADS = 2
HEAD_DIM = 64
ROT_HALF = 8
ROPE_THETA = 500000.0
N_EXPERTS = 32
TOP_K = 4
D_FF = 1024
SWIGLU_LIMIT = 7.0
SWIGLU_ALPHA = 1.702
RMS_EPS = 1e-5
NEG_INF = -1e30
LOG2_E = 1.4426950408889634
SSM_STEP = 8
SSM_SEGS = 16
SSM_GBLK = 8
LANES = 128
MOE_BLK = 256
GRAN = 8
MOE_GRANS = MOE_BLK // GRAN
SORT_CHUNK = 256
COMBINE_CHUNK = 256
SORT_W = 512 + LANES
VMEM_LIMIT = 56 * 1024 * 1024


def _cparams(sem):
    return pltpu.CompilerParams(dimension_semantics=sem, vmem_limit_bytes=VMEM_LIMIT)


def _dot(a, b):
    return jnp.dot(a, b, preferred_element_type=F32)


def _dot_nt(a, b):
    return lax.dot_general(a, b, (((1,), (1,)), ((), ())), preferred_element_type=F32)


def _pack_bf16_pairs(x, exact=False):
    w = x.shape[1] // 2
    bits = pltpu.bitcast(x if exact else x.astype(BF16).astype(F32), U32)
    return (bits[:, w:] & jnp.uint32(0xFFFF0000)) | (bits[:, :w] >> 16)


def _unpack_bf16_pairs(wd):
    lo = pltpu.bitcast(wd << 16, F32)
    hi = pltpu.bitcast(wd & jnp.uint32(0xFFFF0000), F32)
    return jnp.concatenate([lo, hi], axis=1)


def _padded_rows(x_refs, meta_ref, first_tile):
    head = jnp.concatenate([jnp.zeros((PAD, D_MODEL), F32), meta_ref[...]], axis=0)
    first = jnp.where(first_tile, head, x_refs[0][...])
    return [first] + [r[...] for r in x_refs[1:]]


def _padded_row_specs(tm, tiles_per_seq):
    per = tm // BLK

    def spec(r):
        return pl.BlockSpec((None, BLK, D_MODEL),
                            lambda i: (i // tiles_per_seq, jnp.maximum((i % tiles_per_seq) * per + r - 1, 0), 0))

    return [spec(r) for r in range(per)]


def _inproj_kernel(*refs, tiles_per_seq, per):
    x_refs, (meta_ref, nrm_ref, w_ref, rc_ref, rs1_ref, rs2_ref,
             u_ref, q_ref, kx_ref, vx_ref, sgs_ref, sga_ref) = refs[:per], refs[per:]
    normed = []
    for x in _padded_rows(x_refs, meta_ref, pl.program_id(0) % tiles_per_seq == 0):
        ms = jnp.mean(x * x, axis=-1, keepdims=True)
        normed.append((x * lax.rsqrt(ms + RMS_EPS) * nrm_ref[...]).astype(BF16))
    hn = jnp.concatenate(normed, axis=0)

    def proj(lo, hi):
        return _dot(hn, w_ref[:, lo:hi])

    u_ref[...] = proj(0, 1024)

    rc = rc_ref[...]
    rs1 = rs1_ref[...]
    rs2 = rs2_ref[...]

    def rope(blk):
        return blk * rc + pltpu.roll(blk, ROT_HALF, 1) * rs1 + pltpu.roll(blk, LANES - ROT_HALF, 1) * rs2

    qf = proj(1024, 2048)
    scale = LOG2_E / math.sqrt(HEAD_DIM)
    for j in range(8):
        q_ref[:, LANES * j:LANES * (j + 1)] = (rope(qf[:, LANES * j:LANES * (j + 1)]) * scale).astype(BF16)

    kvf = proj(2048, 2304)
    k = rope(kvf[:, :LANES])
    v = kvf[:, LANES:]
    low = lax.broadcasted_iota(I32, k.shape, 1) < HEAD_DIM

    def expand(t, ref):
        tr = pltpu.roll(t, HEAD_DIM, 1)
        zero = jnp.zeros_like(t)
        ref[:, 0:128] = jnp.where(low, t, zero).astype(BF16)
        ref[:, 128:256] = jnp.where(low, zero, tr).astype(BF16)
        ref[:, 256:384] = jnp.where(low, tr, zero).astype(BF16)
        ref[:, 384:512] = jnp.where(low, zero, t).astype(BF16)

    expand(k, kx_ref)
    expand(v, vx_ref)
    sgs_ref[...] = jax.nn.sigmoid(proj(2304, 3328)).astype(BF16)
    sga_ref[...] = jax.nn.sigmoid(proj(3328, 4352)).astype(BF16)


def _inproj(x, meta, nrm, w_bf, rc, rs1, rs2, *, tm, lp):
    np_ = x.shape[0] * lp
    tiles_per_seq = lp // tm
    per = tm // BLK
    row = lambda i: (i, 0)
    fixed = lambda i: (0, 0)
    rope_idx = lambda i: (i % tiles_per_seq, 0)
    out_shape = (
        jax.ShapeDtypeStruct((np_, 1024), F32),
        jax.ShapeDtypeStruct((np_, 1024), BF16),
        jax.ShapeDtypeStruct((np_, 512), BF16),
        jax.ShapeDtypeStruct((np_, 512), BF16),
        jax.ShapeDtypeStruct((np_, 1024), BF16),
        jax.ShapeDtypeStruct((np_, 1024), BF16),
    )
    return pl.pallas_call(
        functools.partial(_inproj_kernel, tiles_per_seq=tiles_per_seq, per=per),
        out_shape=out_shape,
        grid=(np_ // tm,),
        in_specs=_padded_row_specs(tm, tiles_per_seq) + [
            pl.BlockSpec((N_META, D_MODEL), fixed),
            pl.BlockSpec((1, D_MODEL), fixed),
            pl.BlockSpec(w_bf.shape, fixed),
            pl.BlockSpec((tm, LANES), rope_idx),
            pl.BlockSpec((tm, LANES), rope_idx),
            pl.BlockSpec((tm, LANES), rope_idx),
        ],
        out_specs=(
            pl.BlockSpec((tm, 1024), row),
            pl.BlockSpec((tm, 1024), row),
            pl.BlockSpec((tm, 512), row),
            pl.BlockSpec((tm, 512), row),
            pl.BlockSpec((tm, 1024), row),
            pl.BlockSpec((tm, 1024), row),
        ),
        compiler_params=_cparams(("arbitrary",)),
        name="inproj",
    )(*([x] * per), meta, nrm, w_bf, rc, rs1, rs2)


def _expand_block_diag(dst_ref, row0, t, e_ref, row_shift, col_shift):
    n = t.shape[0]
    step = min(n, 256)
    for r0 in range(0, n, step):
        full = _dot(t[r0:r0 + step, :].astype(BF16), e_ref[...])
        rg = (lax.broadcasted_iota(I32, full.shape, 0) + r0) >> row_shift
        cg = lax.broadcasted_iota(I32, full.shape, 1) >> col_shift
        keep = ((rg ^ cg) & (SSM_GBLK - 1)) == 0
        dst_ref[row0 + r0:row0 + r0 + step, :] = jnp.where(keep, full, 0.0).astype(BF16)


def _ssm_kernel(u_ref, bc_ref, pb_ref, cc_ref, kc_ref, eb_ref, ec_ref, tab_ref, z_ref,
                wb_ref, wcd_ref, ust_ref, bu_ref, yn_ref, *, t8, seg, nchunk):
    rc = t8 // nchunk

    @pl.when(pl.program_id(1) == 0)
    def _():
        bc = bc_ref[...]
        bc_swapped = pltpu.roll(bc, STATE, 1)
        for i in range(SSM_STEP):
            blk = pb_ref[i, :, 0:LANES] * bc + pb_ref[i, :, LANES:2 * LANES] * bc_swapped
            _expand_block_diag(wb_ref, LANES * i, blk, eb_ref, 4, 6)
        c_r, c_i = cc_ref[:, 0:LANES], cc_ref[:, LANES:2 * LANES]
        p_r, p_i = cc_ref[:, 2 * LANES:3 * LANES], cc_ref[:, 3 * LANES:4 * LANES]
        _expand_block_diag(wcd_ref, 0, c_r * p_r - c_i * p_i, ec_ref, 6, 4)
        _expand_block_diag(wcd_ref, 512, -(c_r * p_i + c_i * p_r), ec_ref, 6, 4)
        kc = kc_ref[...]
        lane = lax.broadcasted_iota(I32, kc.shape, 1)
        for ip in range(SSM_STEP):
            blk = kc if ip == 0 else jnp.where(lane >= GROUP * ip, pltpu.roll(kc, GROUP * ip, 1), 0.0)
            _expand_block_diag(wcd_ref, 1024 + LANES * ip, blk, ec_ref, 4, 4)

    for i in range(SSM_STEP):
        ust_ref[:, LANES * i:LANES * (i + 1)] = u_ref[pl.ds(i, t8, stride=SSM_STEP), :].astype(BF16)

    for c in range(nchunk):
        r = _dot(ust_ref[c * rc:(c + 1) * rc, :], wb_ref[...])
        for m in range(8):
            bu_ref[m, c * rc:(c + 1) * rc, :] = r[:, LANES * m:LANES * (m + 1)]

    a_re = [jnp.broadcast_to(tab_ref[0:1, LANES * m:LANES * (m + 1)], (SSM_SEGS, LANES)) for m in range(4)]
    a_im = [jnp.broadcast_to(tab_ref[1:2, LANES * m:LANES * (m + 1)], (SSM_SEGS, LANES)) for m in range(4)]
    s_re = [jnp.broadcast_to(tab_ref[2:3, LANES * m:LANES * (m + 1)], (SSM_SEGS, LANES)) for m in range(4)]
    s_im = [jnp.broadcast_to(tab_ref[3:4, LANES * m:LANES * (m + 1)], (SSM_SEGS, LANES)) for m in range(4)]

    def cmul_add(mr, mi, xr, xi, br, bi):
        return mr * xr - mi * xi + br, mr * xi + mi * xr + bi

    def load(k):
        return [bu_ref[m, pl.ds(k, SSM_SEGS, stride=seg), :] for m in range(8)]

    def advance(st, b):
        nr, ni = [], []
        for m in range(4):
            r_, i_ = cmul_add(a_re[m], a_im[m], st[m], st[4 + m], b[m], b[4 + m])
            nr.append(r_)
            ni.append(i_)
        return tuple(nr + ni)

    zeros = tuple(jnp.zeros((SSM_SEGS, LANES), F32) for _ in range(8))
    fin = lax.fori_loop(0, seg, lambda k, st: advance(st, load(k)), zeros)

    first = lax.broadcasted_iota(I32, (SSM_SEGS, LANES), 0) == 0

    def shift_down(t):
        return jnp.where(first, 0.0, pltpu.roll(t, 1, 0))

    tot = fin
    for _ in range(SSM_SEGS - 1):
        nxt_r, nxt_i = [], []
        for m in range(4):
            r_, i_ = cmul_add(s_re[m], s_im[m], shift_down(tot[m]), shift_down(tot[4 + m]), fin[m], fin[4 + m])
            nxt_r.append(r_)
            nxt_i.append(i_)
        tot = tuple(nxt_r + nxt_i)
    init = tuple(shift_down(t) for t in tot)

    def pass2(k, st):
        b = load(k)
        for m in range(8):
            bu_ref[m, pl.ds(k, SSM_SEGS, stride=seg), :] = st[m]
        return advance(st, b)

    lax.fori_loop(0, seg, pass2, init)

    for c in range(nchunk):
        xp = jnp.concatenate([bu_ref[m, c * rc:(c + 1) * rc, :] for m in range(8)], axis=1).astype(BF16)
        lhs = jnp.concatenate([xp, ust_ref[c * rc:(c + 1) * rc, :]], axis=1)
        y = _dot(lhs, wcd_ref[...])
        for i in range(SSM_STEP):
            yn_ref[pl.ds(SSM_STEP * c * rc + i, rc, stride=SSM_STEP), :] = y[:, LANES * i:LANES * (i + 1)]

    y = yn_ref[...] + tab_ref[4:5, 0:LANES] * u_ref[...]
    z_ref[...] = jax.nn.gelu(y).astype(BF16)


def _ssm(u, bc, pb, cc, kc, tab, *, bsz, lp, nchunk):
    np_ = u.shape[0]
    t8 = lp // SSM_STEP
    seg = t8 // SSM_SEGS
    nblk = D_MODEL // LANES
    col = jnp.arange(1024)
    src_b = (col >> 9) * STATE + (col & (STATE - 1))
    src_c = (col >> 7) * GROUP + (col & (GROUP - 1))
    eb = (jnp.arange(LANES)[:, None] == src_b[None, :]).astype(BF16)
    ec = (jnp.arange(LANES)[:, None] == src_c[None, :]).astype(BF16)
    kern = functools.partial(_ssm_kernel, t8=t8, seg=seg, nchunk=nchunk)
    per_q = lambda q, b: (q, 0, 0)
    fixed = lambda q, b: (0, 0)
    return pl.pallas_call(
        kern,
        out_shape=jax.ShapeDtypeStruct((np_, D_MODEL), BF16),
        grid=(nblk, bsz),
        in_specs=[
            pl.BlockSpec((lp, LANES), lambda q, b: (b, q)),
            pl.BlockSpec((None, LANES, LANES), per_q),
            pl.BlockSpec((SSM_STEP, None, LANES, 2 * LANES), lambda q, b: (0, q, 0, 0)),
            pl.BlockSpec((None, SSM_GBLK * STATE, 4 * LANES), per_q),
            pl.BlockSpec((None, LANES, LANES), per_q),
            pl.BlockSpec((LANES, 1024), fixed),
            pl.BlockSpec((LANES, 1024), fixed),
            pl.BlockSpec((None, 8, 512), per_q),
        ],
        out_specs=pl.BlockSpec((lp, LANES), lambda q, b: (b, q)),
        scratch_shapes=[
            pltpu.VMEM((1024, 1024), BF16),
            pltpu.VMEM((2048, 1024), BF16),
            pltpu.VMEM((t8, 1024), BF16),
            pltpu.VMEM((8, t8, LANES), F32),
            pltpu.VMEM((lp, LANES), F32),
        ],
        compiler_params=_cparams(("arbitrary", "arbitrary")),
        name="ssm",
    )(u, bc, pb, cc, kc, eb, ec, tab)


def _ssm_tables(lam_re, lam_im, log_dt, b_re, b_im, c_re, c_im, d, seg):
    hi = lax.Precision.HIGHEST
    lr, li = lam_re.astype(F32), lam_im.astype(F32)
    dt = jnp.exp(log_dt.astype(F32))[:, None]
    mag = jnp.exp(dt * lr)
    ar, ai = mag * jnp.cos(dt * li), mag * jnp.sin(dt * li)
    den = lr * lr + li * li
    nr, ni = ar - 1.0, ai
    fr, fi = (nr * lr + ni * li) / den, (ni * lr - nr * li) / den
    br, bi = b_re.astype(F32), b_im.astype(F32)
    bbr = fr[..., None] * br - fi[..., None] * bi
    bbi = fr[..., None] * bi + fi[..., None] * br

    def cmul(xr, xi, yr, yi):
        return xr * yr - xi * yi, xr * yi + xi * yr

    pr, pi = [jnp.ones_like(ar)], [jnp.zeros_like(ar)]
    for _ in range(SSM_STEP):
        r_, i_ = cmul(pr[-1], pi[-1], ar, ai)
        pr.append(r_)
        pi.append(i_)
    pw_r, pw_i = jnp.stack(pr), jnp.stack(pi)
    a8r, a8i = pw_r[SSM_STEP], pw_i[SSM_STEP]
    sr, si = jnp.ones_like(ar), jnp.zeros_like(ar)
    qr, qi = a8r, a8i
    e = seg
    while e:
        if e & 1:
            sr, si = cmul(sr, si, qr, qi)
        qr, qi = cmul(qr, qi, qr, qi)
        e >>= 1

    cr, ci = c_re.astype(F32), c_im.astype(F32)
    nb = N_GROUPS // SSM_GBLK

    bc = jnp.concatenate([jnp.swapaxes(bbr, 1, 2), jnp.swapaxes(bbi, 1, 2)], axis=2).reshape(nb, LANES, LANES)
    pb = jnp.stack([jnp.concatenate([pr[SSM_STEP - 1 - i]] * 2 + [-pi[SSM_STEP - 1 - i], pi[SSM_STEP - 1 - i]], axis=1)
                    for i in range(SSM_STEP)])
    pb = jnp.broadcast_to(pb[:, :, None, :], (SSM_STEP, N_GROUPS, GROUP, 2 * LANES))
    pb = pb.reshape(SSM_STEP, nb, LANES, 2 * LANES)
    crt = jnp.tile(jnp.swapaxes(cr, 1, 2), (1, 1, SSM_STEP))
    cit = jnp.tile(jnp.swapaxes(ci, 1, 2), (1, 1, SSM_STEP))
    prc = jnp.repeat(jnp.stack(pr[1:], axis=2), GROUP, axis=2)
    pic = jnp.repeat(jnp.stack(pi[1:], axis=2), GROUP, axis=2)
    cc = jnp.concatenate([crt, cit, prc, pic], axis=2).reshape(nb, SSM_GBLK * STATE, 4 * LANES)
    tr, ti = cmul(pw_r[:SSM_STEP, :, :, None], pw_i[:SSM_STEP, :, :, None], bbr[None], bbi[None])
    kt = (jnp.einsum('ghp,tgpk->gkth', cr, tr, precision=hi)
          - jnp.einsum('ghp,tgpk->gkth', ci, ti, precision=hi))
    kc = kt.reshape(nb, LANES, LANES)

    def lanes(t):
        return t.reshape(nb, SSM_GBLK * STATE)

    dl = jnp.pad(d.astype(F32).reshape(nb, SSM_GBLK * GROUP), ((0, 0), (0, 512 - LANES)))
    zero = jnp.zeros((nb, 512), F32)
    tab = jnp.stack([lanes(a8r), lanes(a8i), lanes(sr), lanes(si), dl, zero, zero, zero], axis=1)
    return bc, pb, cc, kc, tab


def _attn_kernel(q_ref, kc_ref, kp_ref, km_ref, vc_ref, vp_ref, vm_ref, sink_ref, o_ref):
    n = pl.program_id(1)
    nk = 3 * BLK
    row = lax.broadcasted_iota(I32, (BLK, nk), 0)
    col = lax.broadcasted_iota(I32, (BLK, nk), 1)
    band = (col > row) & (col <= row + BLK) & (col >= 2 * BLK - BLK * n)
    mrow = col - 2 * BLK
    meta = (mrow >= PAD) & (mrow <= BLK * n + row)
    bias = jnp.where(band | meta, 0.0, NEG_INF)

    for g in range(N_KV_HEADS):
        qg = jnp.concatenate([q_ref[:, LANES * (4 * g + jj):LANES * (4 * g + jj + 1)] for jj in range(4)], axis=0)
        probs, rdens, keys, vals = [], [], [], []
        for par in range(2):
            sl = slice(LANES * (2 * g + par), LANES * (2 * g + par + 1))
            keys += [kp_ref[:, sl], kc_ref[:, sl], km_ref[:, sl]]
            vals += [vp_ref[:, sl], vc_ref[:, sl], vm_ref[:, sl]]
        s_both = _dot_nt(qg, jnp.concatenate(keys, axis=0))
        for par in range(2):
            s_all = s_both[:, 3 * BLK * par:3 * BLK * (par + 1)]
            pp, rden = [], []
            for jj in range(4):
                sink = sink_ref[8 * g + 2 * jj + par] * LOG2_E
                s = s_all[BLK * jj:BLK * (jj + 1), :] + bias
                mx = jnp.maximum(jnp.max(s, axis=1, keepdims=True), sink)
                p = jnp.exp2(s - mx)
                den = jnp.sum(p, axis=1, keepdims=True) + jnp.exp2(sink - mx)
                pp.append(p.astype(BF16))
                rden.append(jnp.broadcast_to(1.0 / den, (BLK, LANES)))
            probs.append(jnp.concatenate(pp, axis=0))
            rdens.append(jnp.concatenate(rden, axis=0))
        o = _dot(jnp.concatenate(probs, axis=1), jnp.concatenate(vals, axis=0))
        low = lax.broadcasted_iota(I32, o.shape, 1) < HEAD_DIM
        o = o * jnp.where(low, rdens[0], rdens[1])
        for jj in range(4):
            o_ref[:, LANES * (4 * g + jj):LANES * (4 * g + jj + 1)] = o[BLK * jj:BLK * (jj + 1), :].astype(BF16)


def _attention(q, kx, vx, sinks, *, bsz, nb):
    np_ = q.shape[0]
    cur = lambda b, n: (b * nb + n, 0)
    prev = lambda b, n: (b * nb + jnp.maximum(n - 1, 0), 0)
    first = lambda b, n: (b * nb, 0)
    return pl.pallas_call(
        _attn_kernel,
        out_shape=jax.ShapeDtypeStruct((np_, 1024), BF16),
        grid=(bsz, nb),
        in_specs=[
            pl.BlockSpec((BLK, 1024), cur),
            pl.BlockSpec((BLK, 512), cur),
            pl.BlockSpec((BLK, 512), prev),
            pl.BlockSpec((BLK, 512), first),
            pl.BlockSpec((BLK, 512), cur),
            pl.BlockSpec((BLK, 512), prev),
            pl.BlockSpec((BLK, 512), first),
            pl.BlockSpec(memory_space=pltpu.SMEM),
        ],
        out_specs=pl.BlockSpec((BLK, 1024), cur),
        compiler_params=_cparams(("arbitrary", "arbitrary")),
        name="attn",
    )(q, kx, kx, kx, vx, vx, vx, sinks)


def _post_kernel(*refs, tiles_per_seq, per):
    x_refs, (meta_ref, z_ref, at_ref, sgs_ref, sga_ref, wglu_ref, bglu_ref, wbs_ref, wba_ref, wout_ref,
             nffn_ref, wrh_ref, wrl_ref, br_ref,
             h2_ref, hnp_ref, lslot_ref, ecol_ref, cnt_ref) = refs[:per], refs[per:]
    i = pl.program_id(0)

    z = z_ref[...]
    t = _dot(z, wglu_ref[...]) + bglu_ref[...]
    so = (z.astype(F32) * jax.nn.sigmoid(t)).astype(BF16)
    mix = (sgs_ref[...].astype(F32) * _dot(so, wbs_ref[...])
           + sga_ref[...].astype(F32) * _dot(at_ref[...], wba_ref[...]))
    mixed = _dot(mix.astype(BF16), wout_ref[...])
    logit_blocks = []
    for r, xr in enumerate(_padded_rows(x_refs, meta_ref, i % tiles_per_seq == 0)):
        rows = slice(BLK * r, BLK * (r + 1))
        h2 = xr + mixed[rows, :]
        h2_ref[rows, :] = h2
        ms = jnp.mean(h2 * h2, axis=-1, keepdims=True)
        hn = h2 * lax.rsqrt(ms + RMS_EPS) * nffn_ref[...]
        hb = hn.astype(BF16)
        hnp_ref[rows, :] = hb
        lo = (hn - hb.astype(F32)).astype(BF16)
        logit_blocks.append(_dot(hb, wrh_ref[...]) + _dot(lo, wrh_ref[...]) + _dot(hb, wrl_ref[...]) + br_ref[...])
    logits = jnp.concatenate(logit_blocks, axis=0)

    tm = logits.shape[0]
    lane = lax.broadcasted_iota(I32, (tm, LANES), 1)
    lane_f = lane.astype(F32)
    work = logits
    vals, hots, idxs = [], [], []
    for _ in range(TOP_K):
        mx = jnp.max(work, axis=1, keepdims=True)
        idx = jnp.min(jnp.where(work == mx, lane_f, float(LANES)), axis=1, keepdims=True)
        hot = lane_f == idx
        vals.append(mx)
        idxs.append(idx)
        hots.append(hot)
        work = jnp.where(hot, -jnp.inf, work)

    ex = [jnp.exp(v - vals[0]) for v in vals]
    tot = ex[0] + ex[1] + ex[2] + ex[3]
    gates = [e / tot for e in ex]

    onehot = jnp.zeros((tm, LANES), F32)
    for hot in hots:
        onehot = onehot + jnp.where(hot, 1.0, 0.0)
    r_i = lax.broadcasted_iota(I32, (tm, tm), 0)
    c_i = lax.broadcasted_iota(I32, (tm, tm), 1)
    tri = jnp.where(r_i > c_i, 1.0, 0.0).astype(BF16)
    before = _dot(tri, onehot.astype(BF16))
    count = jnp.sum(onehot, axis=0, keepdims=True)
    gran = jnp.floor((count + (GRAN - 1)) * (1.0 / GRAN))
    e_r = lax.broadcasted_iota(I32, (LANES, LANES), 0)
    e_c = lax.broadcasted_iota(I32, (LANES, LANES), 1)
    upper = jnp.where(e_r < e_c, 1.0, 0.0).astype(BF16)
    gstart = _dot(jnp.broadcast_to(gran, (8, LANES)).astype(BF16), upper)[0:1, :]
    base = before + gstart * float(GRAN)
    slots = [jnp.sum(jnp.where(hot, base, 0.0), axis=1, keepdims=True) for hot in hots]
    cnt_ref[...] = jnp.broadcast_to(gran, cnt_ref.shape)

    def spread(cols, dtype):
        out = jnp.zeros((tm, LANES), dtype)
        for k, cval in enumerate(cols):
            out = jnp.where(lane == k, cval, out)
        return out

    lslot_ref[...] = spread(slots, F32)
    cols = []
    for g in gates:
        g1 = g.astype(BF16).astype(F32)
        g2 = (g - g1).astype(BF16).astype(F32)
        cols += [g1, g2, g - g1 - g2]
    for s in slots:
        hi = jnp.floor(s * (1.0 / 64.0))
        cols += [hi, s - 64.0 * hi]
    ecol_ref[...] = spread(cols, F32)


def _post(x, meta, z, attn, sgs, sga, wglu, bglu, wbs, wba, wout, nffn, wrh, wrl, br, *, tm, lp):
    np_ = z.shape[0]
    tiles_per_seq = lp // tm
    per = tm // BLK
    row = lambda i: (i, 0)
    fixed = lambda i: (0, 0)
    sq = pl.BlockSpec((1024, 1024), fixed)
    vec = pl.BlockSpec((1, 1024), fixed)
    out_shape = (
        jax.ShapeDtypeStruct((np_, 1024), F32),
        jax.ShapeDtypeStruct((np_, 1024), BF16),
        jax.ShapeDtypeStruct((np_, LANES), F32),
        jax.ShapeDtypeStruct((np_, LANES), F32),
        jax.ShapeDtypeStruct((np_ // tm * 8, LANES), F32),
    )
    return pl.pallas_call(
        functools.partial(_post_kernel, tiles_per_seq=tiles_per_seq, per=per),
        out_shape=out_shape,
        grid=(np_ // tm,),
        in_specs=_padded_row_specs(tm, tiles_per_seq) + [
            pl.BlockSpec((N_META, D_MODEL), fixed),
            pl.BlockSpec((tm, 1024), row), pl.BlockSpec((tm, 1024), row),
            pl.BlockSpec((tm, 1024), row), pl.BlockSpec((tm, 1024), row),
            sq, vec, sq, sq, sq, vec,
            pl.BlockSpec((1024, LANES), fixed), pl.BlockSpec((1024, LANES), fixed),
            pl.BlockSpec((1, LANES), fixed),
        ],
        out_specs=(
            pl.BlockSpec((tm, 1024), row), pl.BlockSpec((tm, 1024), row),
            pl.BlockSpec((tm, LANES), row), pl.BlockSpec((tm, LANES), row),
            pl.BlockSpec((8, LANES), row),
        ),
        compiler_params=_cparams(("arbitrary",)),
        name="post",
    )(*([x] * per), meta, z, attn, sgs, sga, wglu, bglu, wbs, wba, wout, nffn, wrh, wrl, br)


def _sort_kernel(hnp_ref, ecol_ref, xloc_ref, *, tm, srows):
    ecol = ecol_ref[...].astype(BF16)
    pay = jnp.concatenate([hnp_ref[...], ecol], axis=1)

    r8 = lax.broadcasted_iota(I32, (8, LANES), 0)
    l8 = lax.broadcasted_iota(I32, (8, LANES), 1)
    digit_w = jnp.where(l8 == 12 + 2 * r8, 64.0, jnp.where(l8 == 13 + 2 * r8, 1.0, 0.0)).astype(BF16)
    slot_rows = _dot_nt(digit_w, ecol)

    rr = lax.broadcasted_iota(I32, (LANES, LANES), 0)
    cc = lax.broadcasted_iota(I32, (LANES, LANES), 1)
    kk = jnp.where(cc < 3 * TOP_K, cc // 3, -LANES)
    spread_w = jnp.where(rr == 12 + 2 * kk, 64.0, jnp.where(rr == 13 + 2 * kk, 1.0, 0.0)).astype(BF16)
    gate_lane = lax.broadcasted_iota(I32, (SORT_CHUNK, LANES), 1) < 3 * TOP_K

    for c in range(srows // SORT_CHUNK):
        s_id = (lax.broadcasted_iota(I32, (SORT_CHUNK, tm), 0) + c * SORT_CHUNK).astype(F32)
        perm = jnp.zeros((SORT_CHUNK, tm), F32)
        for k in range(TOP_K):
            perm = perm + jnp.where(slot_rows[k:k + 1, :] == s_id, 1.0, 0.0)
        got = _dot(perm.astype(BF16), pay)
        rows = slice(c * SORT_CHUNK, (c + 1) * SORT_CHUNK)
        xloc_ref[rows, 0:512] = _pack_bf16_pairs(got[:, :D_MODEL], exact=True)
        extra = got[:, D_MODEL:]
        owner = _dot(extra.astype(BF16), spread_w)
        me = (lax.broadcasted_iota(I32, (SORT_CHUNK, LANES), 0) + c * SORT_CHUNK).astype(F32)
        mine = jnp.where(gate_lane, owner, -1.0) == me
        gate = jnp.sum(jnp.where(mine, extra, 0.0), axis=1, keepdims=True)
        xloc_ref[rows, 512:SORT_W] = pltpu.bitcast(jnp.broadcast_to(gate, (SORT_CHUNK, LANES)), U32)


def _sort(hnp, ecol, *, tm, srows):
    np_ = hnp.shape[0]
    ntile = np_ // tm
    row = lambda i: (i, 0)
    return pl.pallas_call(
        functools.partial(_sort_kernel, tm=tm, srows=srows),
        out_shape=jax.ShapeDtypeStruct((ntile * srows, SORT_W), U32),
        grid=(ntile,),
        in_specs=[pl.BlockSpec((tm, D_MODEL), row), pl.BlockSpec((tm, LANES), row)],
        out_specs=pl.BlockSpec((srows, SORT_W), row),
        compiler_params=_cparams(("arbitrary",)),
        name="sort",
    )(hnp, ecol)


def _moe_kernel(exp_ref, kind_ref, wslot_ref, nexte_ref, rsrc_ref, rdst_ref,
                xloc_ref, wgu_ref, bgu_ref, wd_ref, bd_ref, yloc_ref,
                xbuf, ybuf, sem_in, sem_out, wgu_f, wd_f, sem_w, wgu_bf, wd_bf, *, nseg, dump0):
    i = pl.program_id(0)
    prev = jnp.maximum(i - 1, 0)
    nxt = jnp.minimum(i + 1, nseg - 1)
    slot = i % 2
    new_expert = (i == 0) | (exp_ref[i] != exp_ref[prev])

    def gather(seg, dst_slot, start):
        for g in range(MOE_GRANS):
            src = pl.multiple_of(rsrc_ref[seg * MOE_GRANS + g], GRAN)
            cp = pltpu.make_async_copy(xloc_ref.at[pl.ds(src, GRAN)], xbuf.at[dst_slot, pl.ds(g * GRAN, GRAN)],
                                       sem_in.at[dst_slot])
            if start:
                cp.start()
            else:
                cp.wait()

    def scatter(seg, src_slot, start):
        for g in range(MOE_GRANS):
            dst = pl.multiple_of(rdst_ref[seg * MOE_GRANS + g], GRAN)
            cp = pltpu.make_async_copy(ybuf.at[src_slot, pl.ds(g * GRAN, GRAN)], yloc_ref.at[pl.ds(dst, GRAN)],
                                       sem_out.at[src_slot])
            if start:
                cp.start()
            else:
                cp.wait()

    @pl.when(i == 0)
    def _():
        gather(i, slot, True)
        ybuf[1 - slot] = jnp.zeros((MOE_BLK, 512), U32)
        for g in range(MOE_GRANS):
            pltpu.make_async_copy(ybuf.at[1 - slot, pl.ds(g * GRAN, GRAN)],
                                  yloc_ref.at[pl.ds((dump0 + (1 - slot) * MOE_GRANS + g) * GRAN, GRAN)],
                                  sem_out.at[1 - slot]).start()

    gather(i, slot, False)

    def fetch(e, slot, start):
        for src, dst in ((wgu_ref, wgu_f), (wd_ref, wd_f)):
            cp = pltpu.make_async_copy(src.at[e], dst.at[slot], sem_w.at[slot])
            if start:
                cp.start(priority=1)
            else:
                cp.wait()

    wslot = wslot_ref[i]

    @pl.when(i == 0)
    def _():
        fetch(exp_ref[i], wslot, True)

    @pl.when(new_expert)
    def _():
        fetch(exp_ref[i], wslot, False)
        wgu_bf[...] = wgu_f[wslot].astype(BF16)
        wd_bf[...] = wd_f[wslot].astype(BF16)

    @pl.when(new_expert & (nexte_ref[i] >= 0))
    def _():
        fetch(nexte_ref[i], 1 - wslot, True)

    kind = kind_ref[i]

    @pl.when(kind == 1)
    def _():
        gather(nxt, 1 - slot, True)
        xw = xbuf[slot]
        xb = _unpack_bf16_pairs(xw[:, 0:512]).astype(BF16)
        gate = pltpu.bitcast(xw[:, 512:SORT_W], F32)
        gu = _dot(xb, wgu_bf[...]) + bgu_ref[...]
        g = jnp.minimum(gu[:, :D_FF], SWIGLU_LIMIT)
        up = jnp.clip(gu[:, D_FF:], -SWIGLU_LIMIT, SWIGLU_LIMIT)
        hid = g * jax.nn.sigmoid(SWIGLU_ALPHA * g) * (up + 1.0)
        y = _dot(hid.astype(BF16), wd_bf[...]) + bd_ref[...]
        y = y * jnp.concatenate([gate] * (D_MODEL // LANES), axis=1)
        ybuf[slot] = _pack_bf16_pairs(y)
        scatter(i, slot, True)
        scatter(prev, 1 - slot, False)

    @pl.when(kind != 1)
    def _():
        gather(nxt, 1 - slot, True)
        ybuf[slot] = jnp.zeros((MOE_BLK, 512), U32)
        scatter(i, slot, True)
        scatter(prev, 1 - slot, False)

    @pl.when(i == nseg - 1)
    def _():
        scatter(i, slot, False)
        gather(nxt, 1 - slot, False)


def _moe(exp, hi, kind, gsrc, xloc, wgu, bgu, wd, bd, *, ncap):
    nseg = exp.shape[0]
    dump0 = ncap
    lane = jnp.arange(nseg * MOE_GRANS, dtype=I32) % MOE_GRANS
    block = jnp.arange(nseg * MOE_GRANS, dtype=I32) // MOE_GRANS
    spare = dump0 + (block % 2) * MOE_GRANS + lane
    rsrc = gsrc * GRAN
    rdst = jnp.where(lane < jnp.repeat(hi, MOE_GRANS), gsrc, spare) * GRAN
    change = jnp.concatenate([jnp.zeros((1,), I32), (exp[1:] != exp[:-1]).astype(I32)])
    wslot = jnp.cumsum(change) % 2
    later = jnp.where(exp[None, :] > exp[:, None], exp[None, :], N_EXPERTS)
    nexte = jnp.min(later, axis=1)
    nexte = jnp.where(nexte >= N_EXPERTS, -1, nexte)
    wmap = lambda i, e, *_: (e[i], 0, 0)
    grid_spec = pltpu.PrefetchScalarGridSpec(
        num_scalar_prefetch=6,
        grid=(nseg,),
        in_specs=[
            pl.BlockSpec(memory_space=pl.ANY),
            pl.BlockSpec(memory_space=pl.ANY),
            pl.BlockSpec((None, 1, 2 * D_FF), wmap),
            pl.BlockSpec(memory_space=pl.ANY),
            pl.BlockSpec((None, 1, D_MODEL), wmap),
        ],
        out_specs=pl.BlockSpec(memory_space=pl.ANY),
        scratch_shapes=[
            pltpu.VMEM((2, MOE_BLK, SORT_W), U32),
            pltpu.VMEM((2, MOE_BLK, 512), U32),
            pltpu.SemaphoreType.DMA((2,)),
            pltpu.SemaphoreType.DMA((2,)),
            pltpu.VMEM((2, D_MODEL, 2 * D_FF), F32),
            pltpu.VMEM((2, D_FF, D_MODEL), F32),
            pltpu.SemaphoreType.DMA((2,)),
            pltpu.VMEM((D_MODEL, 2 * D_FF), BF16),
            pltpu.VMEM((D_FF, D_MODEL), BF16),
        ],
    )
    return pl.pallas_call(
        functools.partial(_moe_kernel, nseg=nseg, dump0=dump0),
        out_shape=jax.ShapeDtypeStruct(((ncap + 2 * MOE_GRANS) * GRAN, 512), U32),
        grid_spec=grid_spec,
        compiler_params=_cparams(("arbitrary",)),
        name="moe",
    )(exp, kind, wslot.astype(I32), nexte.astype(I32), rsrc.astype(I32), rdst.astype(I32),
      xloc, wgu, bgu, wd, bd)


def _combine_kernel(y_ref, lslot_ref, h2_ref, nfin_ref, o_ref, obuf, sem, *, tm, srows, tiles_per_seq, ntile):
    i = pl.program_id(0)
    per = tm // BLK
    slot = i % 2
    slots = lslot_ref[...]
    acc = h2_ref[...]
    for c in range(srows // COMBINE_CHUNK):
        s_id = (lax.broadcasted_iota(I32, (tm, COMBINE_CHUNK), 1) + c * COMBINE_CHUNK).astype(F32)
        take = jnp.zeros((tm, COMBINE_CHUNK), F32)
        for k in range(TOP_K):
            take = take + jnp.where(slots[:, k:k + 1] == s_id, 1.0, 0.0)
        rows = _unpack_bf16_pairs(y_ref[c * COMBINE_CHUNK:(c + 1) * COMBINE_CHUNK, :]).astype(BF16)
        acc = acc + _dot(take.astype(BF16), rows)
    ms = jnp.mean(acc * acc, axis=-1, keepdims=True)
    obuf[slot] = acc * lax.rsqrt(ms + RMS_EPS) * nfin_ref[...]

    def copies(step, start):
        b = step // tiles_per_seq
        j = step % tiles_per_seq
        src_slot = step % 2
        for r in range(per):
            n = j * per + r
            cp = pltpu.make_async_copy(obuf.at[src_slot, pl.ds(r * BLK, BLK)],
                                       o_ref.at[b, pl.ds(pl.multiple_of(jnp.maximum(n - 1, 0) * BLK, BLK), BLK)],
                                       sem.at[src_slot])

            def go():
                if start:
                    cp.start()
                else:
                    cp.wait()

            if r == 0:
                pl.when(j > 0)(go)
            else:
                go()

    copies(i, True)

    @pl.when(i > 0)
    def _():
        copies(i - 1, False)

    @pl.when(i == ntile - 1)
    def _():
        copies(i, False)


def _combine(yloc, lslot, h2, nfin, *, bsz, seq, tm, lp, srows):
    np_ = h2.shape[0]
    tiles_per_seq = lp // tm
    row = lambda i: (i, 0)
    return pl.pallas_call(
        functools.partial(_combine_kernel, tm=tm, srows=srows, tiles_per_seq=tiles_per_seq, ntile=np_ // tm),
        out_shape=jax.ShapeDtypeStruct((bsz, seq, D_MODEL), F32),
        grid=(np_ // tm,),
        in_specs=[
            pl.BlockSpec((srows, 512), row),
            pl.BlockSpec((tm, LANES), row),
            pl.BlockSpec((tm, D_MODEL), row),
            pl.BlockSpec((1, D_MODEL), lambda i: (0, 0)),
        ],
        out_specs=pl.BlockSpec(memory_space=pl.ANY),
        scratch_shapes=[pltpu.VMEM((2, tm, D_MODEL), F32), pltpu.SemaphoreType.DMA((2,))],
        compiler_params=_cparams(("arbitrary",)),
        name="combine",
    )(yloc, lslot, h2, nfin)


def _rope_tables(lp):
    pos = np.arange(lp, dtype=np.float32) - np.float32(PAD)
    inv_freq = np.float32(ROPE_THETA) ** (-np.arange(0, 2 * ROT_HALF, 2, dtype=np.float32) / np.float32(2 * ROT_HALF))
    ang = (pos[:, None] * inv_freq[None, :]).astype(np.float32)
    cos, sin = np.cos(ang), np.sin(ang)
    ones = np.ones((lp, HEAD_DIM - 2 * ROT_HALF), np.float32)
    zeros8 = np.zeros((lp, ROT_HALF), np.float32)
    zrest = np.zeros((lp, HEAD_DIM - 2 * ROT_HALF), np.float32)
    c = np.concatenate([cos, cos, ones], axis=1)
    s1 = np.concatenate([zeros8, sin, zrest], axis=1)
    s2 = np.concatenate([-sin, zeros8, zrest], axis=1)
    tile = lambda t: jnp.asarray(np.concatenate([t, t], axis=1), F32)
    return tile(c), tile(s1), tile(s2)


def _route_tables(grans, gt):
    ntile = grans.shape[0]
    ncap = ntile * gt
    nown = N_EXPERTS + 1
    npos = ncap + nown * MOE_GRANS
    nblocks = npos // MOE_GRANS
    used = jnp.sum(grans, axis=1)
    lstart = jnp.cumsum(grans, axis=1) - grans
    tile0 = jnp.arange(ntile, dtype=I32)[:, None] * gt
    lens = jnp.concatenate([grans.T, (gt - used)[None, :]], axis=0)
    srcs = jnp.concatenate([(tile0 + lstart).T, (tile0[:, 0] + used)[None, :]], axis=0)
    total = jnp.sum(lens, axis=1)
    room = (total + MOE_GRANS - 1) // MOE_GRANS * MOE_GRANS
    oend = jnp.cumsum(room)
    ostart = oend - room
    run_start = (ostart[:, None] + jnp.cumsum(lens, axis=1) - lens).reshape(-1)
    off = srcs.reshape(-1) - run_start
    delta = off - jnp.concatenate([jnp.zeros((1,), I32), off[:-1]])
    pos = jnp.arange(npos, dtype=I32)
    gsrc = pos + jnp.sum(jnp.where(run_start[None, :] <= pos[:, None], delta[None, :], 0), axis=1)

    first = jnp.arange(nblocks, dtype=I32) * MOE_GRANS
    owner = jnp.sum((oend[None, :] <= first[:, None]).astype(I32), axis=1)
    mine = owner[:, None] == jnp.arange(nown, dtype=I32)[None, :]
    valid_end = jnp.sum(jnp.where(mine, (ostart + total)[None, :], 0), axis=1)
    hi = jnp.clip(valid_end - first, 0, MOE_GRANS)
    gsrc = jnp.where((pos % MOE_GRANS) < jnp.repeat(hi, MOE_GRANS), gsrc, 0)
    kind = jnp.where(hi > 0, jnp.where(owner >= N_EXPERTS, 2, 1), 0)
    works = kind == 1
    first_work = jnp.min(jnp.where(works, owner, N_EXPERTS - 1))
    exp = jnp.maximum(lax.cummax(jnp.where(works, owner, -1), axis=0), first_work)
    return gsrc.astype(I32), exp.astype(I32), hi.astype(I32), kind.astype(I32)


def kernel(x, meta_tokens, norm_mix, w_in, ssm_lam_re, ssm_lam_im, ssm_log_dt, ssm_b_re, ssm_b_im,
           ssm_c_re, ssm_c_im, ssm_d, w_glu, b_glu, attn_sinks, w_br_ssm, w_br_attn, w_out,
           norm_ffn, w_router, b_router, w_gate_up, b_gate_up, w_down, b_down, norm_final):
    bsz, seq, _ = x.shape
    lp = seq + BLK
    nb = lp // BLK
    np_ = bsz * lp
    tm = 640 if lp % 640 == 0 else BLK
    t8 = lp // SSM_STEP
    seg = t8 // SSM_SEGS
    nchunk = 5 if (t8 % 5 == 0 and (t8 // 5) % 16 == 0) else 1

    meta = meta_tokens.astype(F32)
    rc, rs1, rs2 = _rope_tables(lp)
    u, q, kx, vx, sgs, sga = _inproj(x, meta, norm_mix[0][None], w_in[0].astype(BF16), rc, rs1, rs2, tm=tm, lp=lp)

    bc, pb, cc, kc, tab = _ssm_tables(ssm_lam_re[0], ssm_lam_im[0], ssm_log_dt[0], ssm_b_re[0], ssm_b_im[0],
                                      ssm_c_re[0], ssm_c_im[0], ssm_d[0], seg)
    z = _ssm(u, bc, pb, cc, kc, tab, bsz=bsz, lp=lp, nchunk=nchunk)

    attn = _attention(q, kx, vx, attn_sinks[0].astype(F32), bsz=bsz, nb=nb)

    wr = jnp.pad(w_router[0].astype(F32), ((0, 0), (0, LANES - N_EXPERTS)))
    wrh = wr.astype(BF16)
    wrl = (wr - wrh.astype(F32)).astype(BF16)
    br = jnp.concatenate([b_router[0].astype(F32), jnp.full((LANES - N_EXPERTS,), NEG_INF, F32)])[None]
    h2, hnp, lslot, ecol, cnt = _post(
        x, meta, z, attn, sgs, sga, w_glu[0].astype(BF16), b_glu[0][None], w_br_ssm[0].astype(BF16),
        w_br_attn[0].astype(BF16), w_out[0].astype(BF16), norm_ffn[0][None], wrh, wrl, br, tm=tm, lp=lp)

    unit = math.lcm(SORT_CHUNK, COMBINE_CHUNK)
    srows = -(-(tm * TOP_K + N_EXPERTS * (GRAN - 1)) // unit) * unit
    gt = srows // GRAN
    ntile = np_ // tm
    grans = cnt.reshape(ntile, 8, LANES)[:, 0, :N_EXPERTS].astype(I32)
    gsrc, sexp, shi, skind = _route_tables(grans, gt)

    xloc = _sort(hnp, ecol, tm=tm, srows=srows)
    yloc = _moe(sexp, shi, skind, gsrc, xloc, w_gate_up[0], b_gate_up[0][:, None, :],
                w_down[0], b_down[0][:, None, :], ncap=ntile * gt)
    return _combine(yloc, lslot, h2, norm_final[None], bsz=bsz, seq=seq, tm=tm, lp=lp, srows=srows)
```

```python
import functools
import math

import jax
import jax.numpy as jnp
import numpy as np
from jax import lax
from jax.experimental import pallas as pl
from jax.experimental.pallas import tpu as pltpu

F32 = jnp.float32
BF16 = jnp.bfloat16
I32 = jnp.int32
U32 = jnp.uint32

D_MODEL = 1024
N_META = 16
BLK = 128
PAD = BLK - N_META
N_GROUPS = 64
GROUP = 16
STATE = 64
N_Q_HEADS = 16
N_KV_HEADS = 2
HEAD_DIM = 64
ROT_HALF = 8
ROPE_THETA = 500000.0
N_EXPERTS = 32
TOP_K = 4
D_FF = 1024
SWIGLU_LIMIT = 7.0
SWIGLU_ALPHA = 1.702
RMS_EPS = 1e-5
NEG_INF = -1e30
LOG2_E = 1.4426950408889634
SSM_STEP = 8
SSM_SEGS = 16
SSM_GBLK = 8
LANES = 128
MOE_BLK = 256
GRAN = 8
MOE_GRANS = MOE_BLK // GRAN
SORT_CHUNK = 256
COMBINE_CHUNK = 256
SORT_W = 512 + LANES
VMEM_LIMIT = 56 * 1024 * 1024


def _cparams(sem):
    return pltpu.CompilerParams(dimension_semantics=sem, vmem_limit_bytes=VMEM_LIMIT)


def _dot(a, b):
    return jnp.dot(a, b, preferred_element_type=F32)


def _dot_nt(a, b):
    return lax.dot_general(a, b, (((1,), (1,)), ((), ())), preferred_element_type=F32)


def _pack_bf16_pairs(x, exact=False):
    w = x.shape[1] // 2
    bits = pltpu.bitcast(x if exact else x.astype(BF16).astype(F32), U32)
    return (bits[:, w:] & jnp.uint32(0xFFFF0000)) | (bits[:, :w] >> 16)


def _unpack_bf16_pairs(wd):
    lo = pltpu.bitcast(wd << 16, F32)
    hi = pltpu.bitcast(wd & jnp.uint32(0xFFFF0000), F32)
    return jnp.concatenate([lo, hi], axis=1)


def _padded_rows(x_refs, meta_ref, first_tile):
    head = jnp.concatenate([jnp.zeros((PAD, D_MODEL), F32), meta_ref[...]], axis=0)
    first = jnp.where(first_tile, head, x_refs[0][...])
    return [first] + [r[...] for r in x_refs[1:]]


def _padded_row_specs(tm, tiles_per_seq):
    per = tm // BLK

    def spec(r):
        return pl.BlockSpec((None, BLK, D_MODEL),
                            lambda i: (i // tiles_per_seq, jnp.maximum((i % tiles_per_seq) * per + r - 1, 0), 0))

    return [spec(r) for r in range(per)]


def _inproj_kernel(*refs, tiles_per_seq, per):
    x_refs, (meta_ref, nrm_ref, w_ref, rc_ref, rs1_ref, rs2_ref,
             u_ref, q_ref, kx_ref, vx_ref, sgs_ref, sga_ref) = refs[:per], refs[per:]
    normed = []
    for x in _padded_rows(x_refs, meta_ref, pl.program_id(0) % tiles_per_seq == 0):
        ms = jnp.mean(x * x, axis=-1, keepdims=True)
        normed.append((x * lax.rsqrt(ms + RMS_EPS) * nrm_ref[...]).astype(BF16))
    hn = jnp.concatenate(normed, axis=0)

    def proj(lo, hi):
        return _dot(hn, w_ref[:, lo:hi])

    u_ref[...] = proj(0, 1024)

    rc = rc_ref[...]
    rs1 = rs1_ref[...]
    rs2 = rs2_ref[...]

    def rope(blk):
        return blk * rc + pltpu.roll(blk, ROT_HALF, 1) * rs1 + pltpu.roll(blk, LANES - ROT_HALF, 1) * rs2

    qf = proj(1024, 2048)
    scale = LOG2_E / math.sqrt(HEAD_DIM)
    for j in range(8):
        q_ref[:, LANES * j:LANES * (j + 1)] = (rope(qf[:, LANES * j:LANES * (j + 1)]) * scale).astype(BF16)

    kvf = proj(2048, 2304)
    k = rope(kvf[:, :LANES])
    v = kvf[:, LANES:]
    low = lax.broadcasted_iota(I32, k.shape, 1) < HEAD_DIM

    def expand(t, ref):
        tr = pltpu.roll(t, HEAD_DIM, 1)
        zero = jnp.zeros_like(t)
        ref[:, 0:128] = jnp.where(low, t, zero).astype(BF16)
        ref[:, 128:256] = jnp.where(low, zero, tr).astype(BF16)
        ref[:, 256:384] = jnp.where(low, tr, zero).astype(BF16)
        ref[:, 384:512] = jnp.where(low, zero, t).astype(BF16)

    expand(k, kx_ref)
    expand(v, vx_ref)
    sgs_ref[...] = jax.nn.sigmoid(proj(2304, 3328)).astype(BF16)
    sga_ref[...] = jax.nn.sigmoid(proj(3328, 4352)).astype(BF16)


def _inproj(x, meta, nrm, w_bf, rc, rs1, rs2, *, tm, lp):
    np_ = x.shape[0] * lp
    tiles_per_seq = lp // tm
    per = tm // BLK
    row = lambda i: (i, 0)
    fixed = lambda i: (0, 0)
    rope_idx = lambda i: (i % tiles_per_seq, 0)
    out_shape = (
        jax.ShapeDtypeStruct((np_, 1024), F32),
        jax.ShapeDtypeStruct((np_, 1024), BF16),
        jax.ShapeDtypeStruct((np_, 512), BF16),
        jax.ShapeDtypeStruct((np_, 512), BF16),
        jax.ShapeDtypeStruct((np_, 1024), BF16),
        jax.ShapeDtypeStruct((np_, 1024), BF16),
    )
    return pl.pallas_call(
        functools.partial(_inproj_kernel, tiles_per_seq=tiles_per_seq, per=per),
        out_shape=out_shape,
        grid=(np_ // tm,),
        in_specs=_padded_row_specs(tm, tiles_per_seq) + [
            pl.BlockSpec((N_META, D_MODEL), fixed),
            pl.BlockSpec((1, D_MODEL), fixed),
            pl.BlockSpec(w_bf.shape, fixed),
            pl.BlockSpec((tm, LANES), rope_idx),
            pl.BlockSpec((tm, LANES), rope_idx),
            pl.BlockSpec((tm, LANES), rope_idx),
        ],
        out_specs=(
            pl.BlockSpec((tm, 1024), row),
            pl.BlockSpec((tm, 1024), row),
            pl.BlockSpec((tm, 512), row),
            pl.BlockSpec((tm, 512), row),
            pl.BlockSpec((tm, 1024), row),
            pl.BlockSpec((tm, 1024), row),
        ),
        compiler_params=_cparams(("arbitrary",)),
        name="inproj",
    )(*([x] * per), meta, nrm, w_bf, rc, rs1, rs2)


def _expand_block_diag(dst_ref, row0, t, e_ref, row_shift, col_shift):
    n = t.shape[0]
    step = min(n, 256)
    for r0 in range(0, n, step):
        full = _dot(t[r0:r0 + step, :].astype(BF16), e_ref[...])
        rg = (lax.broadcasted_iota(I32, full.shape, 0) + r0) >> row_shift
        cg = lax.broadcasted_iota(I32, full.shape, 1) >> col_shift
        keep = ((rg ^ cg) & (SSM_GBLK - 1)) == 0
        dst_ref[row0 + r0:row0 + r0 + step, :] = jnp.where(keep, full, 0.0).astype(BF16)


def _ssm_kernel(u_ref, bc_ref, pb_ref, cc_ref, kc_ref, eb_ref, ec_ref, tab_ref, z_ref,
                wb_ref, wcd_ref, ust_ref, bu_ref, yn_ref, *, t8, seg, nchunk):
    rc = t8 // nchunk

    @pl.when(pl.program_id(1) == 0)
    def _():
        bc = bc_ref[...]
        bc_swapped = pltpu.roll(bc, STATE, 1)
        for i in range(SSM_STEP):
            blk = pb_ref[i, :, 0:LANES] * bc + pb_ref[i, :, LANES:2 * LANES] * bc_swapped
            _expand_block_diag(wb_ref, LANES * i, blk, eb_ref, 4, 6)
        c_r, c_i = cc_ref[:, 0:LANES], cc_ref[:, LANES:2 * LANES]
        p_r, p_i = cc_ref[:, 2 * LANES:3 * LANES], cc_ref[:, 3 * LANES:4 * LANES]
        _expand_block_diag(wcd_ref, 0, c_r * p_r - c_i * p_i, ec_ref, 6, 4)
        _expand_block_diag(wcd_ref, 512, -(c_r * p_i + c_i * p_r), ec_ref, 6, 4)
        kc = kc_ref[...]
        lane = lax.broadcasted_iota(I32, kc.shape, 1)
        for ip in range(SSM_STEP):
            blk = kc if ip == 0 else jnp.where(lane >= GROUP * ip, pltpu.roll(kc, GROUP * ip, 1), 0.0)
            _expand_block_diag(wcd_ref, 1024 + LANES * ip, blk, ec_ref, 4, 4)

    for i in range(SSM_STEP):
        ust_ref[:, LANES * i:LANES * (i + 1)] = u_ref[pl.ds(i, t8, stride=SSM_STEP), :].astype(BF16)

    for c in range(nchunk):
        r = _dot(ust_ref[c * rc:(c + 1) * rc, :], wb_ref[...])
        for m in range(8):
            bu_ref[m, c * rc:(c + 1) * rc, :] = r[:, LANES * m:LANES * (m + 1)]

    a_re = [jnp.broadcast_to(tab_ref[0:1, LANES * m:LANES * (m + 1)], (SSM_SEGS, LANES)) for m in range(4)]
    a_im = [jnp.broadcast_to(tab_ref[1:2, LANES * m:LANES * (m + 1)], (SSM_SEGS, LANES)) for m in range(4)]
    s_re = [jnp.broadcast_to(tab_ref[2:3, LANES * m:LANES * (m + 1)], (SSM_SEGS, LANES)) for m in range(4)]
    s_im = [jnp.broadcast_to(tab_ref[3:4, LANES * m:LANES * (m + 1)], (SSM_SEGS, LANES)) for m in range(4)]

    def cmul_add(mr, mi, xr, xi, br, bi):
        return mr * xr - mi * xi + br, mr * xi + mi * xr + bi

    def load(k):
        return [bu_ref[m, pl.ds(k, SSM_SEGS, stride=seg), :] for m in range(8)]

    def advance(st, b):
        nr, ni = [], []
        for m in range(4):
            r_, i_ = cmul_add(a_re[m], a_im[m], st[m], st[4 + m], b[m], b[4 + m])
            nr.append(r_)
            ni.append(i_)
        return tuple(nr + ni)

    zeros = tuple(jnp.zeros((SSM_SEGS, LANES), F32) for _ in range(8))
    fin = lax.fori_loop(0, seg, lambda k, st: advance(st, load(k)), zeros)

    first = lax.broadcasted_iota(I32, (SSM_SEGS, LANES), 0) == 0

    def shift_down(t):
        return jnp.where(first, 0.0, pltpu.roll(t, 1, 0))

    tot = fin
    for _ in range(SSM_SEGS - 1):
        nxt_r, nxt_i = [], []
        for m in range(4):
            r_, i_ = cmul_add(s_re[m], s_im[m], shift_down(tot[m]), shift_down(tot[4 + m]), fin[m], fin[4 + m])
            nxt_r.append(r_)
            nxt_i.append(i_)
        tot = tuple(nxt_r + nxt_i)
    init = tuple(shift_down(t) for t in tot)

    def pass2(k, st):
        b = load(k)
        for m in range(8):
            bu_ref[m, pl.ds(k, SSM_SEGS, stride=seg), :] = st[m]
        return advance(st, b)

    lax.fori_loop(0, seg, pass2, init)

    for c in range(nchunk):
        xp = jnp.concatenate([bu_ref[m, c * rc:(c + 1) * rc, :] for m in range(8)], axis=1).astype(BF16)
        lhs = jnp.concatenate([xp, ust_ref[c * rc:(c + 1) * rc, :]], axis=1)
        y = _dot(lhs, wcd_ref[...])
        for i in range(SSM_STEP):
            yn_ref[pl.ds(SSM_STEP * c * rc + i, rc, stride=SSM_STEP), :] = y[:, LANES * i:LANES * (i + 1)]

    y = yn_ref[...] + tab_ref[4:5, 0:LANES] * u_ref[...]
    z_ref[...] = jax.nn.gelu(y).astype(BF16)


def _ssm(u, bc, pb, cc, kc, tab, *, bsz, lp, nchunk):
    np_ = u.shape[0]
    t8 = lp // SSM_STEP
    seg = t8 // SSM_SEGS
    nblk = D_MODEL // LANES
    col = jnp.arange(1024)
    src_b = (col >> 9) * STATE + (col & (STATE - 1))
    src_c = (col >> 7) * GROUP + (col & (GROUP - 1))
    eb = (jnp.arange(LANES)[:, None] == src_b[None, :]).astype(BF16)
    ec = (jnp.arange(LANES)[:, None] == src_c[None, :]).astype(BF16)
    kern = functools.partial(_ssm_kernel, t8=t8, seg=seg, nchunk=nchunk)
    per_q = lambda q, b: (q, 0, 0)
    fixed = lambda q, b: (0, 0)
    return pl.pallas_call(
        kern,
        out_shape=jax.ShapeDtypeStruct((np_, D_MODEL), BF16),
        grid=(nblk, bsz),
        in_specs=[
            pl.BlockSpec((lp, LANES), lambda q, b: (b, q)),
            pl.BlockSpec((None, LANES, LANES), per_q),
            pl.BlockSpec((SSM_STEP, None, LANES, 2 * LANES), lambda q, b: (0, q, 0, 0)),
            pl.BlockSpec((None, SSM_GBLK * STATE, 4 * LANES), per_q),
            pl.BlockSpec((None, LANES, LANES), per_q),
            pl.BlockSpec((LANES, 1024), fixed),
            pl.BlockSpec((LANES, 1024), fixed),
            pl.BlockSpec((None, 8, 512), per_q),
        ],
        out_specs=pl.BlockSpec((lp, LANES), lambda q, b: (b, q)),
        scratch_shapes=[
            pltpu.VMEM((1024, 1024), BF16),
            pltpu.VMEM((2048, 1024), BF16),
            pltpu.VMEM((t8, 1024), BF16),
            pltpu.VMEM((8, t8, LANES), F32),
            pltpu.VMEM((lp, LANES), F32),
        ],
        compiler_params=_cparams(("arbitrary", "arbitrary")),
        name="ssm",
    )(u, bc, pb, cc, kc, eb, ec, tab)


def _ssm_tables(lam_re, lam_im, log_dt, b_re, b_im, c_re, c_im, d, seg):
    hi = lax.Precision.HIGHEST
    lr, li = lam_re.astype(F32), lam_im.astype(F32)
    dt = jnp.exp(log_dt.astype(F32))[:, None]
    mag = jnp.exp(dt * lr)
    ar, ai = mag * jnp.cos(dt * li), mag * jnp.sin(dt * li)
    den = lr * lr + li * li
    nr, ni = ar - 1.0, ai
    fr, fi = (nr * lr + ni * li) / den, (ni * lr - nr * li) / den
    br, bi = b_re.astype(F32), b_im.astype(F32)
    bbr = fr[..., None] * br - fi[..., None] * bi
    bbi = fr[..., None] * bi + fi[..., None] * br

    def cmul(xr, xi, yr, yi):
        return xr * yr - xi * yi, xr * yi + xi * yr

    pr, pi = [jnp.ones_like(ar)], [jnp.zeros_like(ar)]
    for _ in range(SSM_STEP):
        r_, i_ = cmul(pr[-1], pi[-1], ar, ai)
        pr.append(r_)
        pi.append(i_)
    pw_r, pw_i = jnp.stack(pr), jnp.stack(pi)
    a8r, a8i = pw_r[SSM_STEP], pw_i[SSM_STEP]
    sr, si = jnp.ones_like(ar), jnp.zeros_like(ar)
    qr, qi = a8r, a8i
    e = seg
    while e:
        if e & 1:
            sr, si = cmul(sr, si, qr, qi)
        qr, qi = cmul(qr, qi, qr, qi)
        e >>= 1

    cr, ci = c_re.astype(F32), c_im.astype(F32)
    nb = N_GROUPS // SSM_GBLK

    bc = jnp.concatenate([jnp.swapaxes(bbr, 1, 2), jnp.swapaxes(bbi, 1, 2)], axis=2).reshape(nb, LANES, LANES)
    pb = jnp.stack([jnp.concatenate([pr[SSM_STEP - 1 - i]] * 2 + [-pi[SSM_STEP - 1 - i], pi[SSM_STEP - 1 - i]], axis=1)
                    for i in range(SSM_STEP)])
    pb = jnp.broadcast_to(pb[:, :, None, :], (SSM_STEP, N_GROUPS, GROUP, 2 * LANES))
    pb = pb.reshape(SSM_STEP, nb, LANES, 2 * LANES)
    crt = jnp.tile(jnp.swapaxes(cr, 1, 2), (1, 1, SSM_STEP))
    cit = jnp.tile(jnp.swapaxes(ci, 1, 2), (1, 1, SSM_STEP))
    prc = jnp.repeat(jnp.stack(pr[1:], axis=2), GROUP, axis=2)
    pic = jnp.repeat(jnp.stack(pi[1:], axis=2), GROUP, axis=2)
    cc = jnp.concatenate([crt, cit, prc, pic], axis=2).reshape(nb, SSM_GBLK * STATE, 4 * LANES)
    tr, ti = cmul(pw_r[:SSM_STEP, :, :, None], pw_i[:SSM_STEP, :, :, None], bbr[None], bbi[None])
    kt = (jnp.einsum('ghp,tgpk->gkth', cr, tr, precision=hi)
          - jnp.einsum('ghp,tgpk->gkth', ci, ti, precision=hi))
    kc = kt.reshape(nb, LANES, LANES)

    def lanes(t):
        return t.reshape(nb, SSM_GBLK * STATE)

    dl = jnp.pad(d.astype(F32).reshape(nb, SSM_GBLK * GROUP), ((0, 0), (0, 512 - LANES)))
    zero = jnp.zeros((nb, 512), F32)
    tab = jnp.stack([lanes(a8r), lanes(a8i), lanes(sr), lanes(si), dl, zero, zero, zero], axis=1)
    return bc, pb, cc, kc, tab


def _attn_kernel(q_ref, kc_ref, kp_ref, km_ref, vc_ref, vp_ref, vm_ref, sink_ref, o_ref):
    n = pl.program_id(1)
    nk = 3 * BLK
    row = lax.broadcasted_iota(I32, (BLK, nk), 0)
    col = lax.broadcasted_iota(I32, (BLK, nk), 1)
    band = (col > row) & (col <= row + BLK) & (col >= 2 * BLK - BLK * n)
    mrow = col - 2 * BLK
    meta = (mrow >= PAD) & (mrow <= BLK * n + row)
    bias = jnp.where(band | meta, 0.0, NEG_INF)

    for g in range(N_KV_HEADS):
        qg = jnp.concatenate([q_ref[:, LANES * (4 * g + jj):LANES * (4 * g + jj + 1)] for jj in range(4)], axis=0)
        probs, rdens, keys, vals = [], [], [], []
        for par in range(2):
            sl = slice(LANES * (2 * g + par), LANES * (2 * g + par + 1))
            keys += [kp_ref[:, sl], kc_ref[:, sl], km_ref[:, sl]]
            vals += [vp_ref[:, sl], vc_ref[:, sl], vm_ref[:, sl]]
        s_both = _dot_nt(qg, jnp.concatenate(keys, axis=0))
        for par in range(2):
            s_all = s_both[:, 3 * BLK * par:3 * BLK * (par + 1)]
            pp, rden = [], []
            for jj in range(4):
                sink = sink_ref[8 * g + 2 * jj + par] * LOG2_E
                s = s_all[BLK * jj:BLK * (jj + 1), :] + bias
                mx = jnp.maximum(jnp.max(s, axis=1, keepdims=True), sink)
                p = jnp.exp2(s - mx)
                den = jnp.sum(p, axis=1, keepdims=True) + jnp.exp2(sink - mx)
                pp.append(p.astype(BF16))
                rden.append(jnp.broadcast_to(1.0 / den, (BLK, LANES)))
            probs.append(jnp.concatenate(pp, axis=0))
            rdens.append(jnp.concatenate(rden, axis=0))
        o = _dot(jnp.concatenate(probs, axis=1), jnp.concatenate(vals, axis=0))
        low = lax.broadcasted_iota(I32, o.shape, 1) < HEAD_DIM
        o = o * jnp.where(low, rdens[0], rdens[1])
        for jj in range(4):
            o_ref[:, LANES * (4 * g + jj):LANES * (4 * g + jj + 1)] = o[BLK * jj:BLK * (jj + 1), :].astype(BF16)


def _attention(q, kx, vx, sinks, *, bsz, nb):
    np_ = q.shape[0]
    cur = lambda b, n: (b * nb + n, 0)
    prev = lambda b, n: (b * nb + jnp.maximum(n - 1, 0), 0)
    first = lambda b, n: (b * nb, 0)
    return pl.pallas_call(
        _attn_kernel,
        out_shape=jax.ShapeDtypeStruct((np_, 1024), BF16),
        grid=(bsz, nb),
        in_specs=[
            pl.BlockSpec((BLK, 1024), cur),
            pl.BlockSpec((BLK, 512), cur),
            pl.BlockSpec((BLK, 512), prev),
            pl.BlockSpec((BLK, 512), first),
            pl.BlockSpec((BLK, 512), cur),
            pl.BlockSpec((BLK, 512), prev),
            pl.BlockSpec((BLK, 512), first),
            pl.BlockSpec(memory_space=pltpu.SMEM),
        ],
        out_specs=pl.BlockSpec((BLK, 1024), cur),
        compiler_params=_cparams(("arbitrary", "arbitrary")),
        name="attn",
    )(q, kx, kx, kx, vx, vx, vx, sinks)


def _post_kernel(*refs, tiles_per_seq, per):
    x_refs, (meta_ref, z_ref, at_ref, sgs_ref, sga_ref, wglu_ref, bglu_ref, wbs_ref, wba_ref, wout_ref,
             nffn_ref, wrh_ref, wrl_ref, br_ref,
             h2_ref, hnp_ref, lslot_ref, ecol_ref, cnt_ref) = refs[:per], refs[per:]
    i = pl.program_id(0)

    z = z_ref[...]
    t = _dot(z, wglu_ref[...]) + bglu_ref[...]
    so = (z.astype(F32) * jax.nn.sigmoid(t)).astype(BF16)
    mix = (sgs_ref[...].astype(F32) * _dot(so, wbs_ref[...])
           + sga_ref[...].astype(F32) * _dot(at_ref[...], wba_ref[...]))
    mixed = _dot(mix.astype(BF16), wout_ref[...])
    logit_blocks = []
    for r, xr in enumerate(_padded_rows(x_refs, meta_ref, i % tiles_per_seq == 0)):
        rows = slice(BLK * r, BLK * (r + 1))
        h2 = xr + mixed[rows, :]
        h2_ref[rows, :] = h2
        ms = jnp.mean(h2 * h2, axis=-1, keepdims=True)
        hn = h2 * lax.rsqrt(ms + RMS_EPS) * nffn_ref[...]
        hb = hn.astype(BF16)
        hnp_ref[rows, :] = hb
        lo = (hn - hb.astype(F32)).astype(BF16)
        logit_blocks.append(_dot(hb, wrh_ref[...]) + _dot(lo, wrh_ref[...]) + _dot(hb, wrl_ref[...]) + br_ref[...])
    logits = jnp.concatenate(logit_blocks, axis=0)

    tm = logits.shape[0]
    lane = lax.broadcasted_iota(I32, (tm, LANES), 1)
    lane_f = lane.astype(F32)
    work = logits
    vals, hots, idxs = [], [], []
    for _ in range(TOP_K):
        mx = jnp.max(work, axis=1, keepdims=True)
        idx = jnp.min(jnp.where(work == mx, lane_f, float(LANES)), axis=1, keepdims=True)
        hot = lane_f == idx
        vals.append(mx)
        idxs.append(idx)
        hots.append(hot)
        work = jnp.where(hot, -jnp.inf, work)

    ex = [jnp.exp(v - vals[0]) for v in vals]
    tot = ex[0] + ex[1] + ex[2] + ex[3]
    gates = [e / tot for e in ex]

    onehot = jnp.zeros((tm, LANES), F32)
    for hot in hots:
        onehot = onehot + jnp.where(hot, 1.0, 0.0)
    r_i = lax.broadcasted_iota(I32, (tm, tm), 0)
    c_i = lax.broadcasted_iota(I32, (tm, tm), 1)
    tri = jnp.where(r_i > c_i, 1.0, 0.0).astype(BF16)
    before = _dot(tri, onehot.astype(BF16))
    count = jnp.sum(onehot, axis=0, keepdims=True)
    gran = jnp.floor((count + (GRAN - 1)) * (1.0 / GRAN))
    e_r = lax.broadcasted_iota(I32, (LANES, LANES), 0)
    e_c = lax.broadcasted_iota(I32, (LANES, LANES), 1)
    upper = jnp.where(e_r < e_c, 1.0, 0.0).astype(BF16)
    gstart = _dot(jnp.broadcast_to(gran, (8, LANES)).astype(BF16), upper)[0:1, :]
    base = before + gstart * float(GRAN)
    slots = [jnp.sum(jnp.where(hot, base, 0.0), axis=1, keepdims=True) for hot in hots]
    cnt_ref[...] = jnp.broadcast_to(gran, cnt_ref.shape)

    def spread(cols, dtype):
        out = jnp.zeros((tm, LANES), dtype)
        for k, cval in enumerate(cols):
            out = jnp.where(lane == k, cval, out)
        return out

    lslot_ref[...] = spread(slots, F32)
    cols = []
    for g in gates:
        g1 = g.astype(BF16).astype(F32)
        g2 = (g - g1).astype(BF16).astype(F32)
        cols += [g1, g2, g - g1 - g2]
    for s in slots:
        hi = jnp.floor(s * (1.0 / 64.0))
        cols += [hi, s - 64.0 * hi]
    ecol_ref[...] = spread(cols, F32)


def _post(x, meta, z, attn, sgs, sga, wglu, bglu, wbs, wba, wout, nffn, wrh, wrl, br, *, tm, lp):
    np_ = z.shape[0]
    tiles_per_seq = lp // tm
    per = tm // BLK
    row = lambda i: (i, 0)
    fixed = lambda i: (0, 0)
    sq = pl.BlockSpec((1024, 1024), fixed)
    vec = pl.BlockSpec((1, 1024), fixed)
    out_shape = (
        jax.ShapeDtypeStruct((np_, 1024), F32),
        jax.ShapeDtypeStruct((np_, 1024), BF16),
        jax.ShapeDtypeStruct((np_, LANES), F32),
        jax.ShapeDtypeStruct((np_, LANES), F32),
        jax.ShapeDtypeStruct((np_ // tm * 8, LANES), F32),
    )
    return pl.pallas_call(
        functools.partial(_post_kernel, tiles_per_seq=tiles_per_seq, per=per),
        out_shape=out_shape,
        grid=(np_ // tm,),
        in_specs=_padded_row_specs(tm, tiles_per_seq) + [
            pl.BlockSpec((N_META, D_MODEL), fixed),
            pl.BlockSpec((tm, 1024), row), pl.BlockSpec((tm, 1024), row),
            pl.BlockSpec((tm, 1024), row), pl.BlockSpec((tm, 1024), row),
            sq, vec, sq, sq, sq, vec,
            pl.BlockSpec((1024, LANES), fixed), pl.BlockSpec((1024, LANES), fixed),
            pl.BlockSpec((1, LANES), fixed),
        ],
        out_specs=(
            pl.BlockSpec((tm, 1024), row), pl.BlockSpec((tm, 1024), row),
            pl.BlockSpec((tm, LANES), row), pl.BlockSpec((tm, LANES), row),
            pl.BlockSpec((8, LANES), row),
        ),
        compiler_params=_cparams(("arbitrary",)),
        name="post",
    )(*([x] * per), meta, z, attn, sgs, sga, wglu, bglu, wbs, wba, wout, nffn, wrh, wrl, br)


def _sort_kernel(hnp_ref, ecol_ref, xloc_ref, *, tm, srows):
    ecol = ecol_ref[...].astype(BF16)
    pay = jnp.concatenate([hnp_ref[...], ecol], axis=1)

    r8 = lax.broadcasted_iota(I32, (8, LANES), 0)
    l8 = lax.broadcasted_iota(I32, (8, LANES), 1)
    digit_w = jnp.where(l8 == 12 + 2 * r8, 64.0, jnp.where(l8 == 13 + 2 * r8, 1.0, 0.0)).astype(BF16)
    slot_rows = _dot_nt(digit_w, ecol)

    rr = lax.broadcasted_iota(I32, (LANES, LANES), 0)
    cc = lax.broadcasted_iota(I32, (LANES, LANES), 1)
    kk = jnp.where(cc < 3 * TOP_K, cc // 3, -LANES)
    spread_w = jnp.where(rr == 12 + 2 * kk, 64.0, jnp.where(rr == 13 + 2 * kk, 1.0, 0.0)).astype(BF16)
    gate_lane = lax.broadcasted_iota(I32, (SORT_CHUNK, LANES), 1) < 3 * TOP_K

    for c in range(srows // SORT_CHUNK):
        s_id = (lax.broadcasted_iota(I32, (SORT_CHUNK, tm), 0) + c * SORT_CHUNK).astype(F32)
        perm = jnp.zeros((SORT_CHUNK, tm), F32)
        for k in range(TOP_K):
            perm = perm + jnp.where(slot_rows[k:k + 1, :] == s_id, 1.0, 0.0)
        got = _dot(perm.astype(BF16), pay)
        rows = slice(c * SORT_CHUNK, (c + 1) * SORT_CHUNK)
        xloc_ref[rows, 0:512] = _pack_bf16_pairs(got[:, :D_MODEL], exact=True)
        extra = got[:, D_MODEL:]
        owner = _dot(extra.astype(BF16), spread_w)
        me = (lax.broadcasted_iota(I32, (SORT_CHUNK, LANES), 0) + c * SORT_CHUNK).astype(F32)
        mine = jnp.where(gate_lane, owner, -1.0) == me
        gate = jnp.sum(jnp.where(mine, extra, 0.0), axis=1, keepdims=True)
        xloc_ref[rows, 512:SORT_W] = pltpu.bitcast(jnp.broadcast_to(gate, (SORT_CHUNK, LANES)), U32)


def _sort(hnp, ecol, *, tm, srows):
    np_ = hnp.shape[0]
    ntile = np_ // tm
    row = lambda i: (i, 0)
    return pl.pallas_call(
        functools.partial(_sort_kernel, tm=tm, srows=srows),
        out_shape=jax.ShapeDtypeStruct((ntile * srows, SORT_W), U32),
        grid=(ntile,),
        in_specs=[pl.BlockSpec((tm, D_MODEL), row), pl.BlockSpec((tm, LANES), row)],
        out_specs=pl.BlockSpec((srows, SORT_W), row),
        compiler_params=_cparams(("arbitrary",)),
        name="sort",
    )(hnp, ecol)


def _moe_kernel(exp_ref, kind_ref, wslot_ref, nexte_ref, rsrc_ref, rdst_ref,
                xloc_ref, wgu_ref, bgu_ref, wd_ref, bd_ref, yloc_ref,
                xbuf, ybuf, sem_in, sem_out, wgu_f, wd_f, sem_w, wgu_bf, wd_bf, *, nseg, dump0):
    i = pl.program_id(0)
    prev = jnp.maximum(i - 1, 0)
    nxt = jnp.minimum(i + 1, nseg - 1)
    slot = i % 2
    new_expert = (i == 0) | (exp_ref[i] != exp_ref[prev])

    def gather(seg, dst_slot, start):
        for g in range(MOE_GRANS):
            src = pl.multiple_of(rsrc_ref[seg * MOE_GRANS + g], GRAN)
            cp = pltpu.make_async_copy(xloc_ref.at[pl.ds(src, GRAN)], xbuf.at[dst_slot, pl.ds(g * GRAN, GRAN)],
                                       sem_in.at[dst_slot])
            if start:
                cp.start()
            else:
                cp.wait()

    def scatter(seg, src_slot, start):
        for g in range(MOE_GRANS):
            dst = pl.multiple_of(rdst_ref[seg * MOE_GRANS + g], GRAN)
            cp = pltpu.make_async_copy(ybuf.at[src_slot, pl.ds(g * GRAN, GRAN)], yloc_ref.at[pl.ds(dst, GRAN)],
                                       sem_out.at[src_slot])
            if start:
                cp.start()
            else:
                cp.wait()

    @pl.when(i == 0)
    def _():
        gather(i, slot, True)
        ybuf[1 - slot] = jnp.zeros((MOE_BLK, 512), U32)
        for g in range(MOE_GRANS):
            pltpu.make_async_copy(ybuf.at[1 - slot, pl.ds(g * GRAN, GRAN)],
                                  yloc_ref.at[pl.ds((dump0 + (1 - slot) * MOE_GRANS + g) * GRAN, GRAN)],
                                  sem_out.at[1 - slot]).start()

    gather(i, slot, False)

    def fetch(e, slot, start):
        for src, dst in ((wgu_ref, wgu_f), (wd_ref, wd_f)):
            cp = pltpu.make_async_copy(src.at[e], dst.at[slot], sem_w.at[slot])
            if start:
                cp.start(priority=1)
            else:
                cp.wait()

    wslot = wslot_ref[i]

    @pl.when(i == 0)
    def _():
        fetch(exp_ref[i], wslot, True)

    @pl.when(new_expert)
    def _():
        fetch(exp_ref[i], wslot, False)
        wgu_bf[...] = wgu_f[wslot].astype(BF16)
        wd_bf[...] = wd_f[wslot].astype(BF16)

    @pl.when(new_expert & (nexte_ref[i] >= 0))
    def _():
        fetch(nexte_ref[i], 1 - wslot, True)

    kind = kind_ref[i]

    @pl.when(kind == 1)
    def _():
        gather(nxt, 1 - slot, True)
        xw = xbuf[slot]
        xb = _unpack_bf16_pairs(xw[:, 0:512]).astype(BF16)
        gate = pltpu.bitcast(xw[:, 512:SORT_W], F32)
        gu = _dot(xb, wgu_bf[...]) + bgu_ref[...]
        g = jnp.minimum(gu[:, :D_FF], SWIGLU_LIMIT)
        up = jnp.clip(gu[:, D_FF:], -SWIGLU_LIMIT, SWIGLU_LIMIT)
        hid = g * jax.nn.sigmoid(SWIGLU_ALPHA * g) * (up + 1.0)
        y = _dot(hid.astype(BF16), wd_bf[...]) + bd_ref[...]
        y = y * jnp.concatenate([gate] * (D_MODEL // LANES), axis=1)
        ybuf[slot] = _pack_bf16_pairs(y)
        scatter(i, slot, True)
        scatter(prev, 1 - slot, False)

    @pl.when(kind != 1)
    def _():
        gather(nxt, 1 - slot, True)
        ybuf[slot] = jnp.zeros((MOE_BLK, 512), U32)
        scatter(i, slot, True)
        scatter(prev, 1 - slot, False)

    @pl.when(i == nseg - 1)
    def _():
        scatter(i, slot, False)
        gather(nxt, 1 - slot, False)


def _moe(exp, hi, kind, gsrc, xloc, wgu, bgu, wd, bd, *, ncap):
    nseg = exp.shape[0]
    dump0 = ncap
    lane = jnp.arange(nseg * MOE_GRANS, dtype=I32) % MOE_GRANS
    block = jnp.arange(nseg * MOE_GRANS, dtype=I32) // MOE_GRANS
    spare = dump0 + (block % 2) * MOE_GRANS + lane
    rsrc = gsrc * GRAN
    rdst = jnp.where(lane < jnp.repeat(hi, MOE_GRANS), gsrc, spare) * GRAN
    change = jnp.concatenate([jnp.zeros((1,), I32), (exp[1:] != exp[:-1]).astype(I32)])
    wslot = jnp.cumsum(change) % 2
    later = jnp.where(exp[None, :] > exp[:, None], exp[None, :], N_EXPERTS)
    nexte = jnp.min(later, axis=1)
    nexte = jnp.where(nexte >= N_EXPERTS, -1, nexte)
    wmap = lambda i, e, *_: (e[i], 0, 0)
    grid_spec = pltpu.PrefetchScalarGridSpec(
        num_scalar_prefetch=6,
        grid=(nseg,),
        in_specs=[
            pl.BlockSpec(memory_space=pl.ANY),
            pl.BlockSpec(memory_space=pl.ANY),
            pl.BlockSpec((None, 1, 2 * D_FF), wmap),
            pl.BlockSpec(memory_space=pl.ANY),
            pl.BlockSpec((None, 1, D_MODEL), wmap),
        ],
        out_specs=pl.BlockSpec(memory_space=pl.ANY),
        scratch_shapes=[
            pltpu.VMEM((2, MOE_BLK, SORT_W), U32),
            pltpu.VMEM((2, MOE_BLK, 512), U32),
            pltpu.SemaphoreType.DMA((2,)),
            pltpu.SemaphoreType.DMA((2,)),
            pltpu.VMEM((2, D_MODEL, 2 * D_FF), F32),
            pltpu.VMEM((2, D_FF, D_MODEL), F32),
            pltpu.SemaphoreType.DMA((2,)),
            pltpu.VMEM((D_MODEL, 2 * D_FF), BF16),
            pltpu.VMEM((D_FF, D_MODEL), BF16),
        ],
    )
    return pl.pallas_call(
        functools.partial(_moe_kernel, nseg=nseg, dump0=dump0),
        out_shape=jax.ShapeDtypeStruct(((ncap + 2 * MOE_GRANS) * GRAN, 512), U32),
        grid_spec=grid_spec,
        compiler_params=_cparams(("arbitrary",)),
        name="moe",
    )(exp, kind, wslot.astype(I32), nexte.astype(I32), rsrc.astype(I32), rdst.astype(I32),
      xloc, wgu, bgu, wd, bd)


def _combine_kernel(y_ref, lslot_ref, h2_ref, nfin_ref, o_ref, obuf, sem, *, tm, srows, tiles_per_seq, ntile):
    i = pl.program_id(0)
    per = tm // BLK
    slot = i % 2
    slots = lslot_ref[...]
    acc = h2_ref[...]
    for c in range(srows // COMBINE_CHUNK):
        s_id = (lax.broadcasted_iota(I32, (tm, COMBINE_CHUNK), 1) + c * COMBINE_CHUNK).astype(F32)
        take = jnp.zeros((tm, COMBINE_CHUNK), F32)
        for k in range(TOP_K):
            take = take + jnp.where(slots[:, k:k + 1] == s_id, 1.0, 0.0)
        rows = _unpack_bf16_pairs(y_ref[c * COMBINE_CHUNK:(c + 1) * COMBINE_CHUNK, :]).astype(BF16)
        acc = acc + _dot(take.astype(BF16), rows)
    ms = jnp.mean(acc * acc, axis=-1, keepdims=True)
    obuf[slot] = acc * lax.rsqrt(ms + RMS_EPS) * nfin_ref[...]

    def copies(step, start):
        b = step // tiles_per_seq
        j = step % tiles_per_seq
        src_slot = step % 2
        for r in range(per):
            n = j * per + r
            cp = pltpu.make_async_copy(obuf.at[src_slot, pl.ds(r * BLK, BLK)],
                                       o_ref.at[b, pl.ds(pl.multiple_of(jnp.maximum(n - 1, 0) * BLK, BLK), BLK)],
                                       sem.at[src_slot])

            def go():
                if start:
                    cp.start()
                else:
                    cp.wait()

            if r == 0:
                pl.when(j > 0)(go)
            else:
                go()

    copies(i, True)

    @pl.when(i > 0)
    def _():
        copies(i - 1, False)

    @pl.when(i == ntile - 1)
    def _():
        copies(i, False)


def _combine(yloc, lslot, h2, nfin, *, bsz, seq, tm, lp, srows):
    np_ = h2.shape[0]
    tiles_per_seq = lp // tm
    row = lambda i: (i, 0)
    return pl.pallas_call(
        functools.partial(_combine_kernel, tm=tm, srows=srows, tiles_per_seq=tiles_per_seq, ntile=np_ // tm),
        out_shape=jax.ShapeDtypeStruct((bsz, seq, D_MODEL), F32),
        grid=(np_ // tm,),
        in_specs=[
            pl.BlockSpec((srows, 512), row),
            pl.BlockSpec((tm, LANES), row),
            pl.BlockSpec((tm, D_MODEL), row),
            pl.BlockSpec((1, D_MODEL), lambda i: (0, 0)),
        ],
        out_specs=pl.BlockSpec(memory_space=pl.ANY),
        scratch_shapes=[pltpu.VMEM((2, tm, D_MODEL), F32), pltpu.SemaphoreType.DMA((2,))],
        compiler_params=_cparams(("arbitrary",)),
        name="combine",
    )(yloc, lslot, h2, nfin)


def _rope_tables(lp):
    pos = np.arange(lp, dtype=np.float32) - np.float32(PAD)
    inv_freq = np.float32(ROPE_THETA) ** (-np.arange(0, 2 * ROT_HALF, 2, dtype=np.float32) / np.float32(2 * ROT_HALF))
    ang = (pos[:, None] * inv_freq[None, :]).astype(np.float32)
    cos, sin = np.cos(ang), np.sin(ang)
    ones = np.ones((lp, HEAD_DIM - 2 * ROT_HALF), np.float32)
    zeros8 = np.zeros((lp, ROT_HALF), np.float32)
    zrest = np.zeros((lp, HEAD_DIM - 2 * ROT_HALF), np.float32)
    c = np.concatenate([cos, cos, ones], axis=1)
    s1 = np.concatenate([zeros8, sin, zrest], axis=1)
    s2 = np.concatenate([-sin, zeros8, zrest], axis=1)
    tile = lambda t: jnp.asarray(np.concatenate([t, t], axis=1), F32)
    return tile(c), tile(s1), tile(s2)


def _route_tables(grans, gt):
    ntile = grans.shape[0]
    ncap = ntile * gt
    nown = N_EXPERTS + 1
    npos = ncap + nown * MOE_GRANS
    nblocks = npos // MOE_GRANS
    used = jnp.sum(grans, axis=1)
    lstart = jnp.cumsum(grans, axis=1) - grans
    tile0 = jnp.arange(ntile, dtype=I32)[:, None] * gt
    lens = jnp.concatenate([grans.T, (gt - used)[None, :]], axis=0)
    srcs = jnp.concatenate([(tile0 + lstart).T, (tile0[:, 0] + used)[None, :]], axis=0)
    total = jnp.sum(lens, axis=1)
    room = (total + MOE_GRANS - 1) // MOE_GRANS * MOE_GRANS
    oend = jnp.cumsum(room)
    ostart = oend - room
    run_start = (ostart[:, None] + jnp.cumsum(lens, axis=1) - lens).reshape(-1)
    off = srcs.reshape(-1) - run_start
    delta = off - jnp.concatenate([jnp.zeros((1,), I32), off[:-1]])
    pos = jnp.arange(npos, dtype=I32)
    gsrc = pos + jnp.sum(jnp.where(run_start[None, :] <= pos[:, None], delta[None, :], 0), axis=1)

    first = jnp.arange(nblocks, dtype=I32) * MOE_GRANS
    owner = jnp.sum((oend[None, :] <= first[:, None]).astype(I32), axis=1)
    mine = owner[:, None] == jnp.arange(nown, dtype=I32)[None, :]
    valid_end = jnp.sum(jnp.where(mine, (ostart + total)[None, :], 0), axis=1)
    hi = jnp.clip(valid_end - first, 0, MOE_GRANS)
    gsrc = jnp.where((pos % MOE_GRANS) < jnp.repeat(hi, MOE_GRANS), gsrc, 0)
    kind = jnp.where(hi > 0, jnp.where(owner >= N_EXPERTS, 2, 1), 0)
    works = kind == 1
    first_work = jnp.min(jnp.where(works, owner, N_EXPERTS - 1))
    exp = jnp.maximum(lax.cummax(jnp.where(works, owner, -1), axis=0), first_work)
    return gsrc.astype(I32), exp.astype(I32), hi.astype(I32), kind.astype(I32)


def kernel(x, meta_tokens, norm_mix, w_in, ssm_lam_re, ssm_lam_im, ssm_log_dt, ssm_b_re, ssm_b_im,
           ssm_c_re, ssm_c_im, ssm_d, w_glu, b_glu, attn_sinks, w_br_ssm, w_br_attn, w_out,
           norm_ffn, w_router, b_router, w_gate_up, b_gate_up, w_down, b_down, norm_final):
    bsz, seq, width = x.shape
    assert width == D_MODEL and seq % BLK == 0, "expects D_MODEL channels and whole 128-token blocks"
    assert w_in.shape[0] == 1 and w_in.shape[2] == 4352, "one layer with [u | q | k | v | gate_ssm | gate_attn] columns"
    lp = seq + BLK
    nb = lp // BLK
    np_ = bsz * lp
    tm = 640 if lp % 640 == 0 else BLK
    t8 = lp // SSM_STEP
    assert t8 % SSM_SEGS == 0, "sequence must split into SSM_SEGS equal runs of 8-step tiles"
    seg = t8 // SSM_SEGS
    nchunk = 5 if (t8 % 5 == 0 and (t8 // 5) % 16 == 0) else 1

    meta = meta_tokens.astype(F32)
    rc, rs1, rs2 = _rope_tables(lp)
    u, q, kx, vx, sgs, sga = _inproj(x, meta, norm_mix[0][None], w_in[0].astype(BF16), rc, rs1, rs2, tm=tm, lp=lp)

    bc, pb, cc, kc, tab = _ssm_tables(ssm_lam_re[0], ssm_lam_im[0], ssm_log_dt[0], ssm_b_re[0], ssm_b_im[0],
                                      ssm_c_re[0], ssm_c_im[0], ssm_d[0], seg)
    z = _ssm(u, bc, pb, cc, kc, tab, bsz=bsz, lp=lp, nchunk=nchunk)

    attn = _attention(q, kx, vx, attn_sinks[0].astype(F32), bsz=bsz, nb=nb)

    wr = jnp.pad(w_router[0].astype(F32), ((0, 0), (0, LANES - N_EXPERTS)))
    wrh = wr.astype(BF16)
    wrl = (wr - wrh.astype(F32)).astype(BF16)
    br = jnp.concatenate([b_router[0].astype(F32), jnp.full((LANES - N_EXPERTS,), NEG_INF, F32)])[None]
    h2, hnp, lslot, ecol, cnt = _post(
        x, meta, z, attn, sgs, sga, w_glu[0].astype(BF16), b_glu[0][None], w_br_ssm[0].astype(BF16),
        w_br_attn[0].astype(BF16), w_out[0].astype(BF16), norm_ffn[0][None], wrh, wrl, br, tm=tm, lp=lp)

    unit = math.lcm(SORT_CHUNK, COMBINE_CHUNK)
    srows = -(-(tm * TOP_K + N_EXPERTS * (GRAN - 1)) // unit) * unit
    gt = srows // GRAN
    ntile = np_ // tm
    grans = cnt.reshape(ntile, 8, LANES)[:, 0, :N_EXPERTS].astype(I32)
    gsrc, sexp, shi, skind = _route_tables(grans, gt)

    xloc = _sort(hnp, ecol, tm=tm, srows=srows)
    yloc = _moe(sexp, shi, skind, gsrc, xloc, w_gate_up[0], b_gate_up[0][:, None, :],
                w_down[0], b_down[0][:, None, :], ncap=ntile * gt)
    return _combine(yloc, lslot, h2, norm_final[None], bsz=bsz, seq=seq, tm=tm, lp=lp, srows=srows)
```

```python
import functools
import math

import jax
import jax.numpy as jnp
import numpy as np
from jax import lax
from jax.experimental import pallas as pl
from jax.experimental.pallas import tpu as pltpu

F32 = jnp.float32
BF16 = jnp.bfloat16
I32 = jnp.int32
U32 = jnp.uint32

D_MODEL = 1024
N_META = 16
BLK = 128
PAD = BLK - N_META
N_GROUPS = 64
GROUP = 16
STATE = 64
N_Q_HEADS = 16
N_KV_HEADS = 2
HEAD_DIM = 64
ROT_HALF = 8
ROPE_THETA = 500000.0
N_EXPERTS = 32
TOP_K = 4
D_FF = 1024
SWIGLU_LIMIT = 7.0
SWIGLU_ALPHA = 1.702
RMS_EPS = 1e-5
NEG_INF = -1e30
LOG2_E = 1.4426950408889634
SSM_STEP = 8
SSM_SEGS = 16
SSM_GBLK = 8
LANES = 128
MOE_BLK = 256
GRAN = 8
MOE_GRANS = MOE_BLK // GRAN
SORT_CHUNK = 256
COMBINE_CHUNK = 256
SORT_W = 512 + LANES
VMEM_LIMIT = 56 * 1024 * 1024


def _cparams(sem):
    return pltpu.CompilerParams(dimension_semantics=sem, vmem_limit_bytes=VMEM_LIMIT)


def _dot(a, b):
    return jnp.dot(a, b, preferred_element_type=F32)


def _dot_nt(a, b):
    return lax.dot_general(a, b, (((1,), (1,)), ((), ())), preferred_element_type=F32)


def _pack_bf16_pairs(x, exact=False):
    w = x.shape[1] // 2
    bits = pltpu.bitcast(x if exact else x.astype(BF16).astype(F32), U32)
    return (bits[:, w:] & jnp.uint32(0xFFFF0000)) | (bits[:, :w] >> 16)


def _unpack_bf16_pairs(wd):
    lo = pltpu.bitcast(wd << 16, F32)
    hi = pltpu.bitcast(wd & jnp.uint32(0xFFFF0000), F32)
    return jnp.concatenate([lo, hi], axis=1)


def _padded_rows(x_refs, meta_ref, first_tile):
    head = jnp.concatenate([jnp.zeros((PAD, D_MODEL), F32), meta_ref[...]], axis=0)
    first = jnp.where(first_tile, head, x_refs[0][...])
    return [first] + [r[...] for r in x_refs[1:]]


def _padded_row_specs(tm, tiles_per_seq):
    per = tm // BLK

    def spec(r):
        return pl.BlockSpec((None, BLK, D_MODEL),
                            lambda i: (i // tiles_per_seq, jnp.maximum((i % tiles_per_seq) * per + r - 1, 0), 0))

    return [spec(r) for r in range(per)]


def _inproj_kernel(*refs, tiles_per_seq, per):
    x_refs, (meta_ref, nrm_ref, w_ref, rc_ref, rs1_ref, rs2_ref,
             u_ref, q_ref, kx_ref, vx_ref, sgs_ref, sga_ref) = refs[:per], refs[per:]
    normed = []
    for x in _padded_rows(x_refs, meta_ref, pl.program_id(0) % tiles_per_seq == 0):
        ms = jnp.mean(x * x, axis=-1, keepdims=True)
        normed.append((x * lax.rsqrt(ms + RMS_EPS) * nrm_ref[...]).astype(BF16))
    hn = jnp.concatenate(normed, axis=0)

    def proj(lo, hi):
        return _dot(hn, w_ref[:, lo:hi])

    u_ref[...] = proj(0, 1024)

    rc = rc_ref[...]
    rs1 = rs1_ref[...]
    rs2 = rs2_ref[...]

    def rope(blk):
        return blk * rc + pltpu.roll(blk, ROT_HALF, 1) * rs1 + pltpu.roll(blk, LANES - ROT_HALF, 1) * rs2

    qf = proj(1024, 2048)
    scale = LOG2_E / math.sqrt(HEAD_DIM)
    for j in range(8):
        q_ref[:, LANES * j:LANES * (j + 1)] = (rope(qf[:, LANES * j:LANES * (j + 1)]) * scale).astype(BF16)

    kvf = proj(2048, 2304)
    k = rope(kvf[:, :LANES])
    v = kvf[:, LANES:]
    low = lax.broadcasted_iota(I32, k.shape, 1) < HEAD_DIM

    def expand(t, ref):
        tr = pltpu.roll(t, HEAD_DIM, 1)
        zero = jnp.zeros_like(t)
        ref[:, 0:128] = jnp.where(low, t, zero).astype(BF16)
        ref[:, 128:256] = jnp.where(low, zero, tr).astype(BF16)
        ref[:, 256:384] = jnp.where(low, tr, zero).astype(BF16)
        ref[:, 384:512] = jnp.where(low, zero, t).astype(BF16)

    expand(k, kx_ref)
    expand(v, vx_ref)
    sgs_ref[...] = jax.nn.sigmoid(proj(2304, 3328)).astype(BF16)
    sga_ref[...] = jax.nn.sigmoid(proj(3328, 4352)).astype(BF16)


def _inproj(x, meta, nrm, w_bf, rc, rs1, rs2, *, tm, lp):
    np_ = x.shape[0] * lp
    tiles_per_seq = lp // tm
    per = tm // BLK
    row = lambda i: (i, 0)
    fixed = lambda i: (0, 0)
    rope_idx = lambda i: (i % tiles_per_seq, 0)
    out_shape = (
        jax.ShapeDtypeStruct((np_, 1024), F32),
        jax.ShapeDtypeStruct((np_, 1024), BF16),
        jax.ShapeDtypeStruct((np_, 512), BF16),
        jax.ShapeDtypeStruct((np_, 512), BF16),
        jax.ShapeDtypeStruct((np_, 1024), BF16),
        jax.ShapeDtypeStruct((np_, 1024), BF16),
    )
    return pl.pallas_call(
        functools.partial(_inproj_kernel, tiles_per_seq=tiles_per_seq, per=per),
        out_shape=out_shape,
        grid=(np_ // tm,),
        in_specs=_padded_row_specs(tm, tiles_per_seq) + [
            pl.BlockSpec((N_META, D_MODEL), fixed),
            pl.BlockSpec((1, D_MODEL), fixed),
            pl.BlockSpec(w_bf.shape, fixed),
            pl.BlockSpec((tm, LANES), rope_idx),
            pl.BlockSpec((tm, LANES), rope_idx),
            pl.BlockSpec((tm, LANES), rope_idx),
        ],
        out_specs=(
            pl.BlockSpec((tm, 1024), row),
            pl.BlockSpec((tm, 1024), row),
            pl.BlockSpec((tm, 512), row),
            pl.BlockSpec((tm, 512), row),
            pl.BlockSpec((tm, 1024), row),
            pl.BlockSpec((tm, 1024), row),
        ),
        compiler_params=_cparams(("arbitrary",)),
        name="inproj",
    )(*([x] * per), meta, nrm, w_bf, rc, rs1, rs2)


def _expand_block_diag(dst_ref, row0, t, e_ref, row_shift, col_shift):
    n = t.shape[0]
    step = min(n, 256)
    for r0 in range(0, n, step):
        full = _dot(t[r0:r0 + step, :].astype(BF16), e_ref[...])
        rg = (lax.broadcasted_iota(I32, full.shape, 0) + r0) >> row_shift
        cg = lax.broadcasted_iota(I32, full.shape, 1) >> col_shift
        keep = ((rg ^ cg) & (SSM_GBLK - 1)) == 0
        dst_ref[row0 + r0:row0 + r0 + step, :] = jnp.where(keep, full, 0.0).astype(BF16)


def _ssm_kernel(u_ref, bc_ref, pb_ref, cc_ref, kc_ref, eb_ref, ec_ref, tab_ref, z_ref,
                wb_ref, wcd_ref, ust_ref, bu_ref, yn_ref, *, t8, seg, nchunk):
    rc = t8 // nchunk

    @pl.when(pl.program_id(1) == 0)
    def _():
        bc = bc_ref[...]
        bc_swapped = pltpu.roll(bc, STATE, 1)
        for i in range(SSM_STEP):
            blk = pb_ref[i, :, 0:LANES] * bc + pb_ref[i, :, LANES:2 * LANES] * bc_swapped
            _expand_block_diag(wb_ref, LANES * i, blk, eb_ref, 4, 6)
        c_r, c_i = cc_ref[:, 0:LANES], cc_ref[:, LANES:2 * LANES]
        p_r, p_i = cc_ref[:, 2 * LANES:3 * LANES], cc_ref[:, 3 * LANES:4 * LANES]
        _expand_block_diag(wcd_ref, 0, c_r * p_r - c_i * p_i, ec_ref, 6, 4)
        _expand_block_diag(wcd_ref, 512, -(c_r * p_i + c_i * p_r), ec_ref, 6, 4)
        kc = kc_ref[...]
        lane = lax.broadcasted_iota(I32, kc.shape, 1)
        for ip in range(SSM_STEP):
            blk = kc if ip == 0 else jnp.where(lane >= GROUP * ip, pltpu.roll(kc, GROUP * ip, 1), 0.0)
            _expand_block_diag(wcd_ref, 1024 + LANES * ip, blk, ec_ref, 4, 4)

    for i in range(SSM_STEP):
        ust_ref[:, LANES * i:LANES * (i + 1)] = u_ref[pl.ds(i, t8, stride=SSM_STEP), :].astype(BF16)

    for c in range(nchunk):
        r = _dot(ust_ref[c * rc:(c + 1) * rc, :], wb_ref[...])
        for m in range(8):
            bu_ref[m, c * rc:(c + 1) * rc, :] = r[:, LANES * m:LANES * (m + 1)]

    a_re = [jnp.broadcast_to(tab_ref[0:1, LANES * m:LANES * (m + 1)], (SSM_SEGS, LANES)) for m in range(4)]
    a_im = [jnp.broadcast_to(tab_ref[1:2, LANES * m:LANES * (m + 1)], (SSM_SEGS, LANES)) for m in range(4)]
    s_re = [jnp.broadcast_to(tab_ref[2:3, LANES * m:LANES * (m + 1)], (SSM_SEGS, LANES)) for m in range(4)]
    s_im = [jnp.broadcast_to(tab_ref[3:4, LANES * m:LANES * (m + 1)], (SSM_SEGS, LANES)) for m in range(4)]

    def cmul_add(mr, mi, xr, xi, br, bi):
        return mr * xr - mi * xi + br, mr * xi + mi * xr + bi

    def load(k):
        return [bu_ref[m, pl.ds(k, SSM_SEGS, stride=seg), :] for m in range(8)]

    def advance(st, b):
        nr, ni = [], []
        for m in range(4):
            r_, i_ = cmul_add(a_re[m], a_im[m], st[m], st[4 + m], b[m], b[4 + m])
            nr.append(r_)
            ni.append(i_)
        return tuple(nr + ni)

    zeros = tuple(jnp.zeros((SSM_SEGS, LANES), F32) for _ in range(8))
    fin = lax.fori_loop(0, seg, lambda k, st: advance(st, load(k)), zeros)

    first = lax.broadcasted_iota(I32, (SSM_SEGS, LANES), 0) == 0

    def shift_down(t):
        return jnp.where(first, 0.0, pltpu.roll(t, 1, 0))

    tot = fin
    for _ in range(SSM_SEGS - 1):
        nxt_r, nxt_i = [], []
        for m in range(4):
            r_, i_ = cmul_add(s_re[m], s_im[m], shift_down(tot[m]), shift_down(tot[4 + m]), fin[m], fin[4 + m])
            nxt_r.append(r_)
            nxt_i.append(i_)
        tot = tuple(nxt_r + nxt_i)
    init = tuple(shift_down(t) for t in tot)

    def pass2(k, st):
        b = load(k)
        for m in range(8):
            bu_ref[m, pl.ds(k, SSM_SEGS, stride=seg), :] = st[m]
        return advance(st, b)

    lax.fori_loop(0, seg, pass2, init)

    for c in range(nchunk):
        xp = jnp.concatenate([bu_ref[m, c * rc:(c + 1) * rc, :] for m in range(8)], axis=1).astype(BF16)
        lhs = jnp.concatenate([xp, ust_ref[c * rc:(c + 1) * rc, :]], axis=1)
        y = _dot(lhs, wcd_ref[...])
        for i in range(SSM_STEP):
            yn_ref[pl.ds(SSM_STEP * c * rc + i, rc, stride=SSM_STEP), :] = y[:, LANES * i:LANES * (i + 1)]

    y = yn_ref[...] + tab_ref[4:5, 0:LANES] * u_ref[...]
    z_ref[...] = jax.nn.gelu(y).astype(BF16)


def _ssm(u, bc, pb, cc, kc, tab, *, bsz, lp, nchunk):
    np_ = u.shape[0]
    t8 = lp // SSM_STEP
    seg = t8 // SSM_SEGS
    nblk = D_MODEL // LANES
    col = jnp.arange(1024)
    src_b = (col >> 9) * STATE + (col & (STATE - 1))
    src_c = (col >> 7) * GROUP + (col & (GROUP - 1))
    eb = (jnp.arange(LANES)[:, None] == src_b[None, :]).astype(BF16)
    ec = (jnp.arange(LANES)[:, None] == src_c[None, :]).astype(BF16)
    kern = functools.partial(_ssm_kernel, t8=t8, seg=seg, nchunk=nchunk)
    per_q = lambda q, b: (q, 0, 0)
    fixed = lambda q, b: (0, 0)
    return pl.pallas_call(
        kern,
        out_shape=jax.ShapeDtypeStruct((np_, D_MODEL), BF16),
        grid=(nblk, bsz),
        in_specs=[
            pl.BlockSpec((lp, LANES), lambda q, b: (b, q)),
            pl.BlockSpec((None, LANES, LANES), per_q),
            pl.BlockSpec((SSM_STEP, None, LANES, 2 * LANES), lambda q, b: (0, q, 0, 0)),
            pl.BlockSpec((None, SSM_GBLK * STATE, 4 * LANES), per_q),
            pl.BlockSpec((None, LANES, LANES), per_q),
            pl.BlockSpec((LANES, 1024), fixed),
            pl.BlockSpec((LANES, 1024), fixed),
            pl.BlockSpec((None, 8, 512), per_q),
        ],
        out_specs=pl.BlockSpec((lp, LANES), lambda q, b: (b, q)),
        scratch_shapes=[
            pltpu.VMEM((1024, 1024), BF16),
            pltpu.VMEM((2048, 1024), BF16),
            pltpu.VMEM((t8, 1024), BF16),
            pltpu.VMEM((8, t8, LANES), F32),
            pltpu.VMEM((lp, LANES), F32),
        ],
        compiler_params=_cparams(("arbitrary", "arbitrary")),
        name="ssm",
    )(u, bc, pb, cc, kc, eb, ec, tab)


def _ssm_tables(lam_re, lam_im, log_dt, b_re, b_im, c_re, c_im, d, seg):
    hi = lax.Precision.HIGHEST
    lr, li = lam_re.astype(F32), lam_im.astype(F32)
    dt = jnp.exp(log_dt.astype(F32))[:, None]
    mag = jnp.exp(dt * lr)
    ar, ai = mag * jnp.cos(dt * li), mag * jnp.sin(dt * li)
    den = lr * lr + li * li
    nr, ni = ar - 1.0, ai
    fr, fi = (nr * lr + ni * li) / den, (ni * lr - nr * li) / den
    br, bi = b_re.astype(F32), b_im.astype(F32)
    bbr = fr[..., None] * br - fi[..., None] * bi
    bbi = fr[..., None] * bi + fi[..., None] * br

    def cmul(xr, xi, yr, yi):
        return xr * yr - xi * yi, xr * yi + xi * yr

    pr, pi = [jnp.ones_like(ar)], [jnp.zeros_like(ar)]
    for _ in range(SSM_STEP):
        r_, i_ = cmul(pr[-1], pi[-1], ar, ai)
        pr.append(r_)
        pi.append(i_)
    pw_r, pw_i = jnp.stack(pr), jnp.stack(pi)
    a8r, a8i = pw_r[SSM_STEP], pw_i[SSM_STEP]
    sr, si = jnp.ones_like(ar), jnp.zeros_like(ar)
    qr, qi = a8r, a8i
    e = seg
    while e:
        if e & 1:
            sr, si = cmul(sr, si, qr, qi)
        qr, qi = cmul(qr, qi, qr, qi)
        e >>= 1

    cr, ci = c_re.astype(F32), c_im.astype(F32)
    nb = N_GROUPS // SSM_GBLK

    bc = jnp.concatenate([jnp.swapaxes(bbr, 1, 2), jnp.swapaxes(bbi, 1, 2)], axis=2).reshape(nb, LANES, LANES)
    pb = jnp.stack([jnp.concatenate([pr[SSM_STEP - 1 - i]] * 2 + [-pi[SSM_STEP - 1 - i], pi[SSM_STEP - 1 - i]], axis=1)
                    for i in range(SSM_STEP)])
    pb = jnp.broadcast_to(pb[:, :, None, :], (SSM_STEP, N_GROUPS, GROUP, 2 * LANES))
    pb = pb.reshape(SSM_STEP, nb, LANES, 2 * LANES)
    crt = jnp.tile(jnp.swapaxes(cr, 1, 2), (1, 1, SSM_STEP))
    cit = jnp.tile(jnp.swapaxes(ci, 1, 2), (1, 1, SSM_STEP))
    prc = jnp.repeat(jnp.stack(pr[1:], axis=2), GROUP, axis=2)
    pic = jnp.repeat(jnp.stack(pi[1:], axis=2), GROUP, axis=2)
    cc = jnp.concatenate([crt, cit, prc, pic], axis=2).reshape(nb, SSM_GBLK * STATE, 4 * LANES)
    tr, ti = cmul(pw_r[:SSM_STEP, :, :, None], pw_i[:SSM_STEP, :, :, None], bbr[None], bbi[None])
    kt = (jnp.einsum('ghp,tgpk->gkth', cr, tr, precision=hi)
          - jnp.einsum('ghp,tgpk->gkth', ci, ti, precision=hi))
    kc = kt.reshape(nb, LANES, LANES)

    def lanes(t):
        return t.reshape(nb, SSM_GBLK * STATE)

    dl = jnp.pad(d.astype(F32).reshape(nb, SSM_GBLK * GROUP), ((0, 0), (0, 512 - LANES)))
    zero = jnp.zeros((nb, 512), F32)
    tab = jnp.stack([lanes(a8r), lanes(a8i), lanes(sr), lanes(si), dl, zero, zero, zero], axis=1)
    return bc, pb, cc, kc, tab


def _attn_kernel(q_ref, kc_ref, kp_ref, km_ref, vc_ref, vp_ref, vm_ref, sink_ref, o_ref):
    n = pl.program_id(0)
    nk = 3 * BLK
    row = lax.broadcasted_iota(I32, (BLK, nk), 0)
    col = lax.broadcasted_iota(I32, (BLK, nk), 1)
    band = (col > row) & (col <= row + BLK) & (col >= 2 * BLK - BLK * n)
    mrow = col - 2 * BLK
    meta = (mrow >= PAD) & (mrow <= BLK * n + row)
    bias = jnp.where(band | meta, 0.0, NEG_INF)

    for b, g in [(b, g) for b in range(q_ref.shape[0]) for g in range(N_KV_HEADS)]:
        qg = jnp.concatenate([q_ref[b, :, LANES * (4 * g + jj):LANES * (4 * g + jj + 1)] for jj in range(4)], axis=0)
        probs, rdens, keys, vals = [], [], [], []
        for par in range(2):
            sl = slice(LANES * (2 * g + par), LANES * (2 * g + par + 1))
            keys += [kp_ref[b, :, sl], kc_ref[b, :, sl], km_ref[b, :, sl]]
            vals += [vp_ref[b, :, sl], vc_ref[b, :, sl], vm_ref[b, :, sl]]
        s_both = _dot_nt(qg, jnp.concatenate(keys, axis=0))
        for par in range(2):
            s_all = s_both[:, 3 * BLK * par:3 * BLK * (par + 1)]
            pp, rden = [], []
            for jj in range(4):
                sink = sink_ref[8 * g + 2 * jj + par] * LOG2_E
                s = s_all[BLK * jj:BLK * (jj + 1), :] + bias
                mx = jnp.maximum(jnp.max(s, axis=1, keepdims=True), sink)
                p = jnp.exp2(s - mx)
                den = jnp.sum(p, axis=1, keepdims=True) + jnp.exp2(sink - mx)
                pp.append(p.astype(BF16))
                rden.append(jnp.broadcast_to(1.0 / den, (BLK, LANES)))
            probs.append(jnp.concatenate(pp, axis=0))
            rdens.append(jnp.concatenate(rden, axis=0))
        o = _dot(jnp.concatenate(probs, axis=1), jnp.concatenate(vals, axis=0))
        low = lax.broadcasted_iota(I32, o.shape, 1) < HEAD_DIM
        o = o * jnp.where(low, rdens[0], rdens[1])
        for jj in range(4):
            o_ref[b, :, LANES * (4 * g + jj):LANES * (4 * g + jj + 1)] = o[BLK * jj:BLK * (jj + 1), :].astype(BF16)


def _attention(q, kx, vx, sinks, *, bsz, nb):
    np_ = q.shape[0]
    lp = nb * BLK
    q3, kx3, vx3 = q.reshape(bsz, lp, 1024), kx.reshape(bsz, lp, 512), vx.reshape(bsz, lp, 512)
    cur = lambda n: (0, n, 0)
    prev = lambda n: (0, jnp.maximum(n - 1, 0), 0)
    first = lambda n: (0, 0, 0)
    out = pl.pallas_call(
        _attn_kernel,
        out_shape=jax.ShapeDtypeStruct((bsz, lp, 1024), BF16),
        grid=(nb,),
        in_specs=[
            pl.BlockSpec((bsz, BLK, 1024), cur),
            pl.BlockSpec((bsz, BLK, 512), cur),
            pl.BlockSpec((bsz, BLK, 512), prev),
            pl.BlockSpec((bsz, BLK, 512), first),
            pl.BlockSpec((bsz, BLK, 512), cur),
            pl.BlockSpec((bsz, BLK, 512), prev),
            pl.BlockSpec((bsz, BLK, 512), first),
            pl.BlockSpec(memory_space=pltpu.SMEM),
        ],
        out_specs=pl.BlockSpec((bsz, BLK, 1024), cur),
        compiler_params=_cparams(("arbitrary",)),
        name="attn",
    )(q3, kx3, kx3, kx3, vx3, vx3, vx3, sinks)
    return out.reshape(np_, 1024)


def _post_kernel(*refs, tiles_per_seq, per):
    x_refs, (meta_ref, z_ref, at_ref, sgs_ref, sga_ref, wglu_ref, bglu_ref, wbs_ref, wba_ref, wout_ref,
             nffn_ref, wrh_ref, wrl_ref, br_ref,
             h2_ref, hnp_ref, lslot_ref, ecol_ref, cnt_ref) = refs[:per], refs[per:]
    i = pl.program_id(0)

    z = z_ref[...]
    t = _dot(z, wglu_ref[...]) + bglu_ref[...]
    so = (z.astype(F32) * jax.nn.sigmoid(t)).astype(BF16)
    mix = (sgs_ref[...].astype(F32) * _dot(so, wbs_ref[...])
           + sga_ref[...].astype(F32) * _dot(at_ref[...], wba_ref[...]))
    mixed = _dot(mix.astype(BF16), wout_ref[...])
    logit_blocks = []
    for r, xr in enumerate(_padded_rows(x_refs, meta_ref, i % tiles_per_seq == 0)):
        rows = slice(BLK * r, BLK * (r + 1))
        h2 = xr + mixed[rows, :]
        h2_ref[rows, :] = h2
        ms = jnp.mean(h2 * h2, axis=-1, keepdims=True)
        hn = h2 * lax.rsqrt(ms + RMS_EPS) * nffn_ref[...]
        hb = hn.astype(BF16)
        hnp_ref[rows, :] = hb
        lo = (hn - hb.astype(F32)).astype(BF16)
        logit_blocks.append(_dot(hb, wrh_ref[...]) + _dot(lo, wrh_ref[...]) + _dot(hb, wrl_ref[...]) + br_ref[...])
    logits = jnp.concatenate(logit_blocks, axis=0)

    tm = logits.shape[0]
    lane = lax.broadcasted_iota(I32, (tm, LANES), 1)
    lane_f = lane.astype(F32)
    work = logits
    vals, hots, idxs = [], [], []
    for _ in range(TOP_K):
        mx = jnp.max(work, axis=1, keepdims=True)
        idx = jnp.min(jnp.where(work == mx, lane_f, float(LANES)), axis=1, keepdims=True)
        hot = lane_f == idx
        vals.append(mx)
        idxs.append(idx)
        hots.append(hot)
        work = jnp.where(hot, -jnp.inf, work)

    ex = [jnp.exp(v - vals[0]) for v in vals]
    tot = ex[0] + ex[1] + ex[2] + ex[3]
    gates = [e / tot for e in ex]

    onehot = jnp.zeros((tm, LANES), F32)
    for hot in hots:
        onehot = onehot + jnp.where(hot, 1.0, 0.0)
    r_i = lax.broadcasted_iota(I32, (tm, tm), 0)
    c_i = lax.broadcasted_iota(I32, (tm, tm), 1)
    tri = jnp.where(r_i > c_i, 1.0, 0.0).astype(BF16)
    before = _dot(tri, onehot.astype(BF16))
    count = jnp.sum(onehot, axis=0, keepdims=True)
    gran = jnp.floor((count + (GRAN - 1)) * (1.0 / GRAN))
    e_r = lax.broadcasted_iota(I32, (LANES, LANES), 0)
    e_c = lax.broadcasted_iota(I32, (LANES, LANES), 1)
    upper = jnp.where(e_r < e_c, 1.0, 0.0).astype(BF16)
    gstart = _dot(jnp.broadcast_to(gran, (8, LANES)).astype(BF16), upper)[0:1, :]
    base = before + gstart * float(GRAN)
    slots = [jnp.sum(jnp.where(hot, base, 0.0), axis=1, keepdims=True) for hot in hots]
    cnt_ref[...] = jnp.broadcast_to(gran, cnt_ref.shape)

    def spread(cols, dtype):
        out = jnp.zeros((tm, LANES), dtype)
        for k, cval in enumerate(cols):
            out = jnp.where(lane == k, cval, out)
        return out

    lslot_ref[...] = spread(slots, F32)
    cols = []
    for g in gates:
        g1 = g.astype(BF16).astype(F32)
        g2 = (g - g1).astype(BF16).astype(F32)
        cols += [g1, g2, g - g1 - g2]
    for s in slots:
        hi = jnp.floor(s * (1.0 / 64.0))
        cols += [hi, s - 64.0 * hi]
    ecol_ref[...] = spread(cols, F32)


def _post(x, meta, z, attn, sgs, sga, wglu, bglu, wbs, wba, wout, nffn, wrh, wrl, br, *, tm, lp):
    np_ = z.shape[0]
    tiles_per_seq = lp // tm
    per = tm // BLK
    row = lambda i: (i, 0)
    fixed = lambda i: (0, 0)
    sq = pl.BlockSpec((1024, 1024), fixed)
    vec = pl.BlockSpec((1, 1024), fixed)
    out_shape = (
        jax.ShapeDtypeStruct((np_, 1024), F32),
        jax.ShapeDtypeStruct((np_, 1024), BF16),
        jax.ShapeDtypeStruct((np_, LANES), F32),
        jax.ShapeDtypeStruct((np_, LANES), F32),
        jax.ShapeDtypeStruct((np_ // tm * 8, LANES), F32),
    )
    return pl.pallas_call(
        functools.partial(_post_kernel, tiles_per_seq=tiles_per_seq, per=per),
        out_shape=out_shape,
        grid=(np_ // tm,),
        in_specs=_padded_row_specs(tm, tiles_per_seq) + [
            pl.BlockSpec((N_META, D_MODEL), fixed),
            pl.BlockSpec((tm, 1024), row), pl.BlockSpec((tm, 1024), row),
            pl.BlockSpec((tm, 1024), row), pl.BlockSpec((tm, 1024), row),
            sq, vec, sq, sq, sq, vec,
            pl.BlockSpec((1024, LANES), fixed), pl.BlockSpec((1024, LANES), fixed),
            pl.BlockSpec((1, LANES), fixed),
        ],
        out_specs=(
            pl.BlockSpec((tm, 1024), row), pl.BlockSpec((tm, 1024), row),
            pl.BlockSpec((tm, LANES), row), pl.BlockSpec((tm, LANES), row),
            pl.BlockSpec((8, LANES), row),
        ),
        compiler_params=_cparams(("arbitrary",)),
        name="post",
    )(*([x] * per), meta, z, attn, sgs, sga, wglu, bglu, wbs, wba, wout, nffn, wrh, wrl, br)


def _sort_kernel(hnp_ref, ecol_ref, xloc_ref, *, tm, srows):
    ecol = ecol_ref[...].astype(BF16)
    pay = jnp.concatenate([hnp_ref[...], ecol], axis=1)

    r8 = lax.broadcasted_iota(I32, (8, LANES), 0)
    l8 = lax.broadcasted_iota(I32, (8, LANES), 1)
    digit_w = jnp.where(l8 == 12 + 2 * r8, 64.0, jnp.where(l8 == 13 + 2 * r8, 1.0, 0.0)).astype(BF16)
    slot_rows = _dot_nt(digit_w, ecol)

    rr = lax.broadcasted_iota(I32, (LANES, LANES), 0)
    cc = lax.broadcasted_iota(I32, (LANES, LANES), 1)
    kk = jnp.where(cc < 3 * TOP_K, cc // 3, -LANES)
    spread_w = jnp.where(rr == 12 + 2 * kk, 64.0, jnp.where(rr == 13 + 2 * kk, 1.0, 0.0)).astype(BF16)
    gate_lane = lax.broadcasted_iota(I32, (SORT_CHUNK, LANES), 1) < 3 * TOP_K

    for c in range(srows // SORT_CHUNK):
        s_id = (lax.broadcasted_iota(I32, (SORT_CHUNK, tm), 0) + c * SORT_CHUNK).astype(F32)
        perm = jnp.zeros((SORT_CHUNK, tm), F32)
        for k in range(TOP_K):
            perm = perm + jnp.where(slot_rows[k:k + 1, :] == s_id, 1.0, 0.0)
        got = _dot(perm.astype(BF16), pay)
        rows = slice(c * SORT_CHUNK, (c + 1) * SORT_CHUNK)
        xloc_ref[rows, 0:512] = _pack_bf16_pairs(got[:, :D_MODEL], exact=True)
        extra = got[:, D_MODEL:]
        owner = _dot(extra.astype(BF16), spread_w)
        me = (lax.broadcasted_iota(I32, (SORT_CHUNK, LANES), 0) + c * SORT_CHUNK).astype(F32)
        mine = jnp.where(gate_lane, owner, -1.0) == me
        gate = jnp.sum(jnp.where(mine, extra, 0.0), axis=1, keepdims=True)
        xloc_ref[rows, 512:SORT_W] = pltpu.bitcast(jnp.broadcast_to(gate, (SORT_CHUNK, LANES)), U32)


def _sort(hnp, ecol, *, tm, srows):
    np_ = hnp.shape[0]
    ntile = np_ // tm
    row = lambda i: (i, 0)
    return pl.pallas_call(
        functools.partial(_sort_kernel, tm=tm, srows=srows),
        out_shape=jax.ShapeDtypeStruct((ntile * srows, SORT_W), U32),
        grid=(ntile,),
        in_specs=[pl.BlockSpec((tm, D_MODEL), row), pl.BlockSpec((tm, LANES), row)],
        out_specs=pl.BlockSpec((srows, SORT_W), row),
        compiler_params=_cparams(("arbitrary",)),
        name="sort",
    )(hnp, ecol)


def _moe_kernel(exp_ref, kind_ref, wslot_ref, nexte_ref, rsrc_ref, rdst_ref,
                xloc_ref, wgu_ref, bgu_ref, wd_ref, bd_ref, yloc_ref,
                xbuf, ybuf, sem_in, sem_out, wgu_f, wd_f, sem_w, wgu_bf, wd_bf, *, nseg, dump0):
    i = pl.program_id(0)
    prev = jnp.maximum(i - 1, 0)
    nxt = jnp.minimum(i + 1, nseg - 1)
    slot = i % 2
    new_expert = (i == 0) | (exp_ref[i] != exp_ref[prev])

    def gather(seg, dst_slot, start):
        for g in range(MOE_GRANS):
            src = pl.multiple_of(rsrc_ref[seg * MOE_GRANS + g], GRAN)
            cp = pltpu.make_async_copy(xloc_ref.at[pl.ds(src, GRAN)], xbuf.at[dst_slot, pl.ds(g * GRAN, GRAN)],
                                       sem_in.at[dst_slot])
            if start:
                cp.start()
            else:
                cp.wait()

    def scatter(seg, src_slot, start):
        for g in range(MOE_GRANS):
            dst = pl.multiple_of(rdst_ref[seg * MOE_GRANS + g], GRAN)
            cp = pltpu.make_async_copy(ybuf.at[src_slot, pl.ds(g * GRAN, GRAN)], yloc_ref.at[pl.ds(dst, GRAN)],
                                       sem_out.at[src_slot])
            if start:
                cp.start()
            else:
                cp.wait()

    @pl.when(i == 0)
    def _():
        gather(i, slot, True)
        ybuf[1 - slot] = jnp.zeros((MOE_BLK, 512), U32)
        for g in range(MOE_GRANS):
            pltpu.make_async_copy(ybuf.at[1 - slot, pl.ds(g * GRAN, GRAN)],
                                  yloc_ref.at[pl.ds((dump0 + (1 - slot) * MOE_GRANS + g) * GRAN, GRAN)],
                                  sem_out.at[1 - slot]).start()

    gather(i, slot, False)

    def fetch(e, slot, start):
        for src, dst in ((wgu_ref, wgu_f), (wd_ref, wd_f)):
            cp = pltpu.make_async_copy(src.at[e], dst.at[slot], sem_w.at[slot])
            if start:
                cp.start(priority=1)
            else:
                cp.wait()

    wslot = wslot_ref[i]

    @pl.when(i == 0)
    def _():
        fetch(exp_ref[i], wslot, True)

    @pl.when(new_expert)
    def _():
        fetch(exp_ref[i], wslot, False)
        wgu_bf[...] = wgu_f[wslot].astype(BF16)
        wd_bf[...] = wd_f[wslot].astype(BF16)

    @pl.when(new_expert & (nexte_ref[i] >= 0))
    def _():
        fetch(nexte_ref[i], 1 - wslot, True)

    kind = kind_ref[i]

    @pl.when(kind == 1)
    def _():
        gather(nxt, 1 - slot, True)
        xw = xbuf[slot]
        xb = _unpack_bf16_pairs(xw[:, 0:512]).astype(BF16)
        gate = pltpu.bitcast(xw[:, 512:SORT_W], F32)
        gu = _dot(xb, wgu_bf[...]) + bgu_ref[...]
        g = jnp.minimum(gu[:, :D_FF], SWIGLU_LIMIT)
        up = jnp.clip(gu[:, D_FF:], -SWIGLU_LIMIT, SWIGLU_LIMIT)
        hid = g * jax.nn.sigmoid(SWIGLU_ALPHA * g) * (up + 1.0)
        y = _dot(hid.astype(BF16), wd_bf[...]) + bd_ref[...]
        y = y * jnp.concatenate([gate] * (D_MODEL // LANES), axis=1)
        ybuf[slot] = _pack_bf16_pairs(y)
        scatter(i, slot, True)
        scatter(prev, 1 - slot, False)

    @pl.when(kind != 1)
    def _():
        gather(nxt, 1 - slot, True)
        ybuf[slot] = jnp.zeros((MOE_BLK, 512), U32)
        scatter(i, slot, True)
        scatter(prev, 1 - slot, False)

    @pl.when(i == nseg - 1)
    def _():
        scatter(i, slot, False)
        gather(nxt, 1 - slot, False)


def _moe(exp, hi, kind, gsrc, xloc, wgu, bgu, wd, bd, *, ncap):
    nseg = exp.shape[0]
    dump0 = ncap
    lane = jnp.arange(nseg * MOE_GRANS, dtype=I32) % MOE_GRANS
    block = jnp.arange(nseg * MOE_GRANS, dtype=I32) // MOE_GRANS
    spare = dump0 + (block % 2) * MOE_GRANS + lane
    rsrc = gsrc * GRAN
    rdst = jnp.where(lane < jnp.repeat(hi, MOE_GRANS), gsrc, spare) * GRAN
    change = jnp.concatenate([jnp.zeros((1,), I32), (exp[1:] != exp[:-1]).astype(I32)])
    wslot = jnp.cumsum(change) % 2
    later = jnp.where(exp[None, :] > exp[:, None], exp[None, :], N_EXPERTS)
    nexte = jnp.min(later, axis=1)
    nexte = jnp.where(nexte >= N_EXPERTS, -1, nexte)
    wmap = lambda i, e, *_: (e[i], 0, 0)
    grid_spec = pltpu.PrefetchScalarGridSpec(
        num_scalar_prefetch=6,
        grid=(nseg,),
        in_specs=[
            pl.BlockSpec(memory_space=pl.ANY),
            pl.BlockSpec(memory_space=pl.ANY),
            pl.BlockSpec((None, 1, 2 * D_FF), wmap),
            pl.BlockSpec(memory_space=pl.ANY),
            pl.BlockSpec((None, 1, D_MODEL), wmap),
        ],
        out_specs=pl.BlockSpec(memory_space=pl.ANY),
        scratch_shapes=[
            pltpu.VMEM((2, MOE_BLK, SORT_W), U32),
            pltpu.VMEM((2, MOE_BLK, 512), U32),
            pltpu.SemaphoreType.DMA((2,)),
            pltpu.SemaphoreType.DMA((2,)),
            pltpu.VMEM((2, D_MODEL, 2 * D_FF), F32),
            pltpu.VMEM((2, D_FF, D_MODEL), F32),
            pltpu.SemaphoreType.DMA((2,)),
            pltpu.VMEM((D_MODEL, 2 * D_FF), BF16),
            pltpu.VMEM((D_FF, D_MODEL), BF16),
        ],
    )
    return pl.pallas_call(
        functools.partial(_moe_kernel, nseg=nseg, dump0=dump0),
        out_shape=jax.ShapeDtypeStruct(((ncap + 2 * MOE_GRANS) * GRAN, 512), U32),
        grid_spec=grid_spec,
        compiler_params=_cparams(("arbitrary",)),
        name="moe",
    )(exp, kind, wslot.astype(I32), nexte.astype(I32), rsrc.astype(I32), rdst.astype(I32),
      xloc, wgu, bgu, wd, bd)


def _combine_kernel(y_ref, lslot_ref, h2_ref, nfin_ref, o_ref, obuf, sem, *, tm, srows, tiles_per_seq, ntile):
    i = pl.program_id(0)
    per = tm // BLK
    slot = i % 2
    slots = lslot_ref[...]
    acc = h2_ref[...]
    for c in range(srows // COMBINE_CHUNK):
        s_id = (lax.broadcasted_iota(I32, (tm, COMBINE_CHUNK), 1) + c * COMBINE_CHUNK).astype(F32)
        take = jnp.zeros((tm, COMBINE_CHUNK), F32)
        for k in range(TOP_K):
            take = take + jnp.where(slots[:, k:k + 1] == s_id, 1.0, 0.0)
        rows = _unpack_bf16_pairs(y_ref[c * COMBINE_CHUNK:(c + 1) * COMBINE_CHUNK, :]).astype(BF16)
        acc = acc + _dot(take.astype(BF16), rows)
    ms = jnp.mean(acc * acc, axis=-1, keepdims=True)
    obuf[slot] = acc * lax.rsqrt(ms + RMS_EPS) * nfin_ref[...]

    def copies(step, start):
        b = step // tiles_per_seq
        j = step % tiles_per_seq
        src_slot = step % 2
        for r in range(per):
            n = j * per + r
            cp = pltpu.make_async_copy(obuf.at[src_slot, pl.ds(r * BLK, BLK)],
                                       o_ref.at[b, pl.ds(pl.multiple_of(jnp.maximum(n - 1, 0) * BLK, BLK), BLK)],
                                       sem.at[src_slot])

            def go():
                if start:
                    cp.start()
                else:
                    cp.wait()

            if r == 0:
                pl.when(j > 0)(go)
            else:
                go()

    copies(i, True)

    @pl.when(i > 0)
    def _():
        copies(i - 1, False)

    @pl.when(i == ntile - 1)
    def _():
        copies(i, False)


def _combine(yloc, lslot, h2, nfin, *, bsz, seq, tm, lp, srows):
    np_ = h2.shape[0]
    tiles_per_seq = lp // tm
    row = lambda i: (i, 0)
    return pl.pallas_call(
        functools.partial(_combine_kernel, tm=tm, srows=srows, tiles_per_seq=tiles_per_seq, ntile=np_ // tm),
        out_shape=jax.ShapeDtypeStruct((bsz, seq, D_MODEL), F32),
        grid=(np_ // tm,),
        in_specs=[
            pl.BlockSpec((srows, 512), row),
            pl.BlockSpec((tm, LANES), row),
            pl.BlockSpec((tm, D_MODEL), row),
            pl.BlockSpec((1, D_MODEL), lambda i: (0, 0)),
        ],
        out_specs=pl.BlockSpec(memory_space=pl.ANY),
        scratch_shapes=[pltpu.VMEM((2, tm, D_MODEL), F32), pltpu.SemaphoreType.DMA((2,))],
        compiler_params=_cparams(("arbitrary",)),
        name="combine",
    )(yloc, lslot, h2, nfin)


def _rope_tables(lp):
    pos = np.arange(lp, dtype=np.float32) - np.float32(PAD)
    inv_freq = np.float32(ROPE_THETA) ** (-np.arange(0, 2 * ROT_HALF, 2, dtype=np.float32) / np.float32(2 * ROT_HALF))
    ang = (pos[:, None] * inv_freq[None, :]).astype(np.float32)
    cos, sin = np.cos(ang), np.sin(ang)
    ones = np.ones((lp, HEAD_DIM - 2 * ROT_HALF), np.float32)
    zeros8 = np.zeros((lp, ROT_HALF), np.float32)
    zrest = np.zeros((lp, HEAD_DIM - 2 * ROT_HALF), np.float32)
    c = np.concatenate([cos, cos, ones], axis=1)
    s1 = np.concatenate([zeros8, sin, zrest], axis=1)
    s2 = np.concatenate([-sin, zeros8, zrest], axis=1)
    tile = lambda t: jnp.asarray(np.concatenate([t, t], axis=1), F32)
    return tile(c), tile(s1), tile(s2)


def _route_tables(grans, gt):
    ntile = grans.shape[0]
    ncap = ntile * gt
    nown = N_EXPERTS + 1
    npos = ncap + nown * MOE_GRANS
    nblocks = npos // MOE_GRANS
    used = jnp.sum(grans, axis=1)
    lstart = jnp.cumsum(grans, axis=1) - grans
    tile0 = jnp.arange(ntile, dtype=I32)[:, None] * gt
    lens = jnp.concatenate([grans.T, (gt - used)[None, :]], axis=0)
    srcs = jnp.concatenate([(tile0 + lstart).T, (tile0[:, 0] + used)[None, :]], axis=0)
    total = jnp.sum(lens, axis=1)
    room = (total + MOE_GRANS - 1) // MOE_GRANS * MOE_GRANS
    oend = jnp.cumsum(room)
    ostart = oend - room
    run_start = (ostart[:, None] + jnp.cumsum(lens, axis=1) - lens).reshape(-1)
    off = srcs.reshape(-1) - run_start
    delta = off - jnp.concatenate([jnp.zeros((1,), I32), off[:-1]])
    pos = jnp.arange(npos, dtype=I32)
    gsrc = pos + jnp.sum(jnp.where(run_start[None, :] <= pos[:, None], delta[None, :], 0), axis=1)

    first = jnp.arange(nblocks, dtype=I32) * MOE_GRANS
    owner = jnp.sum((oend[None, :] <= first[:, None]).astype(I32), axis=1)
    mine = owner[:, None] == jnp.arange(nown, dtype=I32)[None, :]
    valid_end = jnp.sum(jnp.where(mine, (ostart + total)[None, :], 0), axis=1)
    hi = jnp.clip(valid_end - first, 0, MOE_GRANS)
    gsrc = jnp.where((pos % MOE_GRANS) < jnp.repeat(hi, MOE_GRANS), gsrc, 0)
    kind = jnp.where(hi > 0, jnp.where(owner >= N_EXPERTS, 2, 1), 0)
    works = kind == 1
    first_work = jnp.min(jnp.where(works, owner, N_EXPERTS - 1))
    exp = jnp.maximum(lax.cummax(jnp.where(works, owner, -1), axis=0), first_work)
    return gsrc.astype(I32), exp.astype(I32), hi.astype(I32), kind.astype(I32)


def kernel(x, meta_tokens, norm_mix, w_in, ssm_lam_re, ssm_lam_im, ssm_log_dt, ssm_b_re, ssm_b_im,
           ssm_c_re, ssm_c_im, ssm_d, w_glu, b_glu, attn_sinks, w_br_ssm, w_br_attn, w_out,
           norm_ffn, w_router, b_router, w_gate_up, b_gate_up, w_down, b_down, norm_final):
    bsz, seq, width = x.shape
    assert width == D_MODEL and seq % BLK == 0, "expects D_MODEL channels and whole 128-token blocks"
    assert w_in.shape[0] == 1 and w_in.shape[2] == 4352, "one layer with [u | q | k | v | gate_ssm | gate_attn] columns"
    lp = seq + BLK
    nb = lp // BLK
    np_ = bsz * lp
    tm = 640 if lp % 640 == 0 else BLK
    t8 = lp // SSM_STEP
    assert t8 % SSM_SEGS == 0, "sequence must split into SSM_SEGS equal runs of 8-step tiles"
    seg = t8 // SSM_SEGS
    nchunk = 5 if (t8 % 5 == 0 and (t8 // 5) % 16 == 0) else 1

    meta = meta_tokens.astype(F32)
    rc, rs1, rs2 = _rope_tables(lp)
    u, q, kx, vx, sgs, sga = _inproj(x, meta, norm_mix[0][None], w_in[0].astype(BF16), rc, rs1, rs2, tm=tm, lp=lp)

    bc, pb, cc, kc, tab = _ssm_tables(ssm_lam_re[0], ssm_lam_im[0], ssm_log_dt[0], ssm_b_re[0], ssm_b_im[0],
                                      ssm_c_re[0], ssm_c_im[0], ssm_d[0], seg)
    z = _ssm(u, bc, pb, cc, kc, tab, bsz=bsz, lp=lp, nchunk=nchunk)

    attn = _attention(q, kx, vx, attn_sinks[0].astype(F32), bsz=bsz, nb=nb)

    wr = jnp.pad(w_router[0].astype(F32), ((0, 0), (0, LANES - N_EXPERTS)))
    wrh = wr.astype(BF16)
    wrl = (wr - wrh.astype(F32)).astype(BF16)
    br = jnp.concatenate([b_router[0].astype(F32), jnp.full((LANES - N_EXPERTS,), NEG_INF, F32)])[None]
    h2, hnp, lslot, ecol, cnt = _post(
        x, meta, z, attn, sgs, sga, w_glu[0].astype(BF16), b_glu[0][None], w_br_ssm[0].astype(BF16),
        w_br_attn[0].astype(BF16), w_out[0].astype(BF16), norm_ffn[0][None], wrh, wrl, br, tm=tm, lp=lp)

    unit = math.lcm(SORT_CHUNK, COMBINE_CHUNK)
    srows = -(-(tm * TOP_K + N_EXPERTS * (GRAN - 1)) // unit) * unit
    gt = srows // GRAN
    ntile = np_ // tm
    grans = cnt.reshape(ntile, 8, LANES)[:, 0, :N_EXPERTS].astype(I32)
    gsrc, sexp, shi, skind = _route_tables(grans, gt)

    xloc = _sort(hnp, ecol, tm=tm, srows=srows)
    yloc = _moe(sexp, shi, skind, gsrc, xloc, w_gate_up[0], b_gate_up[0][:, None, :],
                w_down[0], b_down[0][:, None, :], ncap=ntile * gt)
    return _combine(yloc, lslot, h2, norm_final[None], bsz=bsz, seq=seq, tm=tm, lp=lp, srows=srows)
```

```python
import functools
import math

import jax
import jax.numpy as jnp
import numpy as np
from jax import lax
from jax.experimental import pallas as pl
from jax.experimental.pallas import tpu as pltpu

F32 = jnp.float32
BF16 = jnp.bfloat16
I32 = jnp.int32
U32 = jnp.uint32

D_MODEL = 1024
N_META = 16
BLK = 128
PAD = BLK - N_META
N_GROUPS = 64
GROUP = 16
STATE = 64
N_Q_HEADS = 16
N_KV_HEADS = 2
HEAD_DIM = 64
ROT_HALF = 8
ROPE_THETA = 500000.0
N_EXPERTS = 32
TOP_K = 4
D_FF = 1024
SWIGLU_LIMIT = 7.0
SWIGLU_ALPHA = 1.702
RMS_EPS = 1e-5
NEG_INF = -1e30
LOG2_E = 1.4426950408889634
SSM_STEP = 8
SSM_SEGS = 16
SSM_GBLK = 8
LANES = 128
MOE_BLK = 256
GRAN = 8
MOE_GRANS = MOE_BLK // GRAN
SORT_CHUNK = 256
COMBINE_CHUNK = 256
SORT_W = 512 + LANES
VMEM_LIMIT = 56 * 1024 * 1024


def _cparams(sem):
    return pltpu.CompilerParams(dimension_semantics=sem, vmem_limit_bytes=VMEM_LIMIT)


def _dot(a, b):
    return jnp.dot(a, b, preferred_element_type=F32)


def _dot_nt(a, b):
    return lax.dot_general(a, b, (((1,), (1,)), ((), ())), preferred_element_type=F32)


def _pack_bf16_pairs(x, exact=False):
    w = x.shape[1] // 2
    bits = pltpu.bitcast(x if exact else x.astype(BF16).astype(F32), U32)
    return (bits[:, w:] & jnp.uint32(0xFFFF0000)) | (bits[:, :w] >> 16)


def _unpack_bf16_pairs(wd):
    lo = pltpu.bitcast(wd << 16, F32)
    hi = pltpu.bitcast(wd & jnp.uint32(0xFFFF0000), F32)
    return jnp.concatenate([lo, hi], axis=1)


def _padded_rows(x_refs, meta_ref, first_tile):
    head = jnp.concatenate([jnp.zeros((PAD, D_MODEL), F32), meta_ref[...]], axis=0)
    first = jnp.where(first_tile, head, x_refs[0][...])
    return [first] + [r[...] for r in x_refs[1:]]


def _padded_row_specs(tm, tiles_per_seq):
    per = tm // BLK

    def spec(r):
        return pl.BlockSpec((None, BLK, D_MODEL),
                            lambda i: (i // tiles_per_seq, jnp.maximum((i % tiles_per_seq) * per + r - 1, 0), 0))

    return [spec(r) for r in range(per)]


def _inproj_kernel(*refs, tiles_per_seq, per):
    x_refs, (meta_ref, nrm_ref, w_ref, rc_ref, rs1_ref, rs2_ref,
             u_ref, q_ref, kx_ref, vx_ref, sgs_ref, sga_ref) = refs[:per], refs[per:]
    normed = []
    for x in _padded_rows(x_refs, meta_ref, pl.program_id(0) % tiles_per_seq == 0):
        ms = jnp.mean(x * x, axis=-1, keepdims=True)
        normed.append((x * lax.rsqrt(ms + RMS_EPS) * nrm_ref[...]).astype(BF16))
    hn = jnp.concatenate(normed, axis=0)

    def proj(lo, hi):
        return _dot(hn, w_ref[:, lo:hi])

    u_ref[...] = proj(0, 1024)

    rc = rc_ref[...]
    rs1 = rs1_ref[...]
    rs2 = rs2_ref[...]

    def rope(blk):
        return blk * rc + pltpu.roll(blk, ROT_HALF, 1) * rs1 + pltpu.roll(blk, LANES - ROT_HALF, 1) * rs2

    qf = proj(1024, 2048)
    scale = LOG2_E / math.sqrt(HEAD_DIM)
    for j in range(8):
        q_ref[:, LANES * j:LANES * (j + 1)] = (rope(qf[:, LANES * j:LANES * (j + 1)]) * scale).astype(BF16)

    kvf = proj(2048, 2304)
    k = rope(kvf[:, :LANES])
    v = kvf[:, LANES:]
    low = lax.broadcasted_iota(I32, k.shape, 1) < HEAD_DIM

    def expand(t, ref):
        tr = pltpu.roll(t, HEAD_DIM, 1)
        zero = jnp.zeros_like(t)
        ref[:, 0:128] = jnp.where(low, t, zero).astype(BF16)
        ref[:, 128:256] = jnp.where(low, zero, tr).astype(BF16)
        ref[:, 256:384] = jnp.where(low, tr, zero).astype(BF16)
        ref[:, 384:512] = jnp.where(low, zero, t).astype(BF16)

    expand(k, kx_ref)
    expand(v, vx_ref)
    sgs_ref[...] = jax.nn.sigmoid(proj(2304, 3328)).astype(BF16)
    sga_ref[...] = jax.nn.sigmoid(proj(3328, 4352)).astype(BF16)


def _inproj(x, meta, nrm, w_bf, rc, rs1, rs2, *, tm, lp):
    np_ = x.shape[0] * lp
    tiles_per_seq = lp // tm
    per = tm // BLK
    row = lambda i: (i, 0)
    fixed = lambda i: (0, 0)
    rope_idx = lambda i: (i % tiles_per_seq, 0)
    out_shape = (
        jax.ShapeDtypeStruct((np_, 1024), F32),
        jax.ShapeDtypeStruct((np_, 1024), BF16),
        jax.ShapeDtypeStruct((np_, 512), BF16),
        jax.ShapeDtypeStruct((np_, 512), BF16),
        jax.ShapeDtypeStruct((np_, 1024), BF16),
        jax.ShapeDtypeStruct((np_, 1024), BF16),
    )
    return pl.pallas_call(
        functools.partial(_inproj_kernel, tiles_per_seq=tiles_per_seq, per=per),
        out_shape=out_shape,
        grid=(np_ // tm,),
        in_specs=_padded_row_specs(tm, tiles_per_seq) + [
            pl.BlockSpec((N_META, D_MODEL), fixed),
            pl.BlockSpec((1, D_MODEL), fixed),
            pl.BlockSpec(w_bf.shape, fixed),
            pl.BlockSpec((tm, LANES), rope_idx),
            pl.BlockSpec((tm, LANES), rope_idx),
            pl.BlockSpec((tm, LANES), rope_idx),
        ],
        out_specs=(
            pl.BlockSpec((tm, 1024), row),
            pl.BlockSpec((tm, 1024), row),
            pl.BlockSpec((tm, 512), row),
            pl.BlockSpec((tm, 512), row),
            pl.BlockSpec((tm, 1024), row),
            pl.BlockSpec((tm, 1024), row),
        ),
        compiler_params=_cparams(("arbitrary",)),
        name="inproj",
    )(*([x] * per), meta, nrm, w_bf, rc, rs1, rs2)


def _expand_block_diag(dst_ref, row0, t, e_ref, row_shift, col_shift):
    n = t.shape[0]
    step = min(n, 256)
    for r0 in range(0, n, step):
        full = _dot(t[r0:r0 + step, :].astype(BF16), e_ref[...])
        rg = (lax.broadcasted_iota(I32, full.shape, 0) + r0) >> row_shift
        cg = lax.broadcasted_iota(I32, full.shape, 1) >> col_shift
        keep = ((rg ^ cg) & (SSM_GBLK - 1)) == 0
        dst_ref[row0 + r0:row0 + r0 + step, :] = jnp.where(keep, full, 0.0).astype(BF16)


def _ssm_kernel(u_ref, bc_ref, pb_ref, cc_ref, kc_ref, eb_ref, ec_ref, tab_ref, z_ref,
                wb_ref, wcd_ref, ust_ref, bu_ref, yn_ref, *, t8, seg, nchunk):
    def expand_tables():
        bc = bc_ref[...]
        bc_swapped = pltpu.roll(bc, STATE, 1)
        for i in range(SSM_STEP):
            blk = pb_ref[i, :, 0:LANES] * bc + pb_ref[i, :, LANES:2 * LANES] * bc_swapped
            _expand_block_diag(wb_ref, LANES * i, blk, eb_ref, 4, 6)
        c_r, c_i = cc_ref[:, 0:LANES], cc_ref[:, LANES:2 * LANES]
        p_r, p_i = cc_ref[:, 2 * LANES:3 * LANES], cc_ref[:, 3 * LANES:4 * LANES]
        _expand_block_diag(wcd_ref, 0, c_r * p_r - c_i * p_i, ec_ref, 6, 4)
        _expand_block_diag(wcd_ref, 512, -(c_r * p_i + c_i * p_r), ec_ref, 6, 4)
        kc = kc_ref[...]
        lane = lax.broadcasted_iota(I32, kc.shape, 1)
        for ip in range(SSM_STEP):
            blk = kc if ip == 0 else jnp.where(lane >= GROUP * ip, pltpu.roll(kc, GROUP * ip, 1), 0.0)
            _expand_block_diag(wcd_ref, 1024 + LANES * ip, blk, ec_ref, 4, 4)

    expand_tables()
    for row in range(u_ref.shape[0]):
        _ssm_one_row(u_ref.at[row], z_ref.at[row], wb_ref, wcd_ref, tab_ref, ust_ref, bu_ref, yn_ref,
                     t8=t8, seg=seg, nchunk=nchunk)


def _ssm_one_row(u_ref, z_ref, wb_ref, wcd_ref, tab_ref, ust_ref, bu_ref, yn_ref, *, t8, seg, nchunk):
    rc = t8 // nchunk
    for i in range(SSM_STEP):
        ust_ref[:, LANES * i:LANES * (i + 1)] = u_ref[pl.ds(i, t8, stride=SSM_STEP), :].astype(BF16)

    for c in range(nchunk):
        r = _dot(ust_ref[c * rc:(c + 1) * rc, :], wb_ref[...])
        for m in range(8):
            bu_ref[m, c * rc:(c + 1) * rc, :] = r[:, LANES * m:LANES * (m + 1)]

    a_re = [jnp.broadcast_to(tab_ref[0:1, LANES * m:LANES * (m + 1)], (SSM_SEGS, LANES)) for m in range(4)]
    a_im = [jnp.broadcast_to(tab_ref[1:2, LANES * m:LANES * (m + 1)], (SSM_SEGS, LANES)) for m in range(4)]
    s_re = [jnp.broadcast_to(tab_ref[2:3, LANES * m:LANES * (m + 1)], (SSM_SEGS, LANES)) for m in range(4)]
    s_im = [jnp.broadcast_to(tab_ref[3:4, LANES * m:LANES * (m + 1)], (SSM_SEGS, LANES)) for m in range(4)]

    def cmul_add(mr, mi, xr, xi, br, bi):
        return mr * xr - mi * xi + br, mr * xi + mi * xr + bi

    def load(k):
        return [bu_ref[m, pl.ds(k, SSM_SEGS, stride=seg), :] for m in range(8)]

    def advance(st, b):
        nr, ni = [], []
        for m in range(4):
            r_, i_ = cmul_add(a_re[m], a_im[m], st[m], st[4 + m], b[m], b[4 + m])
            nr.append(r_)
            ni.append(i_)
        return tuple(nr + ni)

    zeros = tuple(jnp.zeros((SSM_SEGS, LANES), F32) for _ in range(8))
    fin = lax.fori_loop(0, seg, lambda k, st: advance(st, load(k)), zeros)

    first = lax.broadcasted_iota(I32, (SSM_SEGS, LANES), 0) == 0

    def shift_down(t):
        return jnp.where(first, 0.0, pltpu.roll(t, 1, 0))

    tot = fin
    for _ in range(SSM_SEGS - 1):
        nxt_r, nxt_i = [], []
        for m in range(4):
            r_, i_ = cmul_add(s_re[m], s_im[m], shift_down(tot[m]), shift_down(tot[4 + m]), fin[m], fin[4 + m])
            nxt_r.append(r_)
            nxt_i.append(i_)
        tot = tuple(nxt_r + nxt_i)
    init = tuple(shift_down(t) for t in tot)

    def pass2(k, st):
        b = load(k)
        for m in range(8):
            bu_ref[m, pl.ds(k, SSM_SEGS, stride=seg), :] = st[m]
        return advance(st, b)

    lax.fori_loop(0, seg, pass2, init)

    for c in range(nchunk):
        xp = jnp.concatenate([bu_ref[m, c * rc:(c + 1) * rc, :] for m in range(8)], axis=1).astype(BF16)
        lhs = jnp.concatenate([xp, ust_ref[c * rc:(c + 1) * rc, :]], axis=1)
        y = _dot(lhs, wcd_ref[...])
        for i in range(SSM_STEP):
            yn_ref[pl.ds(SSM_STEP * c * rc + i, rc, stride=SSM_STEP), :] = y[:, LANES * i:LANES * (i + 1)]

    y = yn_ref[...] + tab_ref[4:5, 0:LANES] * u_ref[...]
    z_ref[...] = jax.nn.gelu(y).astype(BF16)


def _ssm(u, bc, pb, cc, kc, tab, *, bsz, lp, nchunk):
    np_ = u.shape[0]
    t8 = lp // SSM_STEP
    seg = t8 // SSM_SEGS
    nblk = D_MODEL // LANES
    col = jnp.arange(1024)
    src_b = (col >> 9) * STATE + (col & (STATE - 1))
    src_c = (col >> 7) * GROUP + (col & (GROUP - 1))
    eb = (jnp.arange(LANES)[:, None] == src_b[None, :]).astype(BF16)
    ec = (jnp.arange(LANES)[:, None] == src_c[None, :]).astype(BF16)
    kern = functools.partial(_ssm_kernel, t8=t8, seg=seg, nchunk=nchunk)
    per_q = lambda q: (q, 0, 0)
    fixed = lambda q: (0, 0)
    rows = pl.BlockSpec((bsz, lp, LANES), lambda q: (0, 0, q))
    z = pl.pallas_call(
        kern,
        out_shape=jax.ShapeDtypeStruct((bsz, lp, D_MODEL), BF16),
        grid=(nblk,),
        in_specs=[
            rows,
            pl.BlockSpec((None, LANES, LANES), per_q),
            pl.BlockSpec((SSM_STEP, None, LANES, 2 * LANES), lambda q: (0, q, 0, 0)),
            pl.BlockSpec((None, SSM_GBLK * STATE, 4 * LANES), per_q),
            pl.BlockSpec((None, LANES, LANES), per_q),
            pl.BlockSpec((LANES, 1024), fixed),
            pl.BlockSpec((LANES, 1024), fixed),
            pl.BlockSpec((None, 8, 512), per_q),
        ],
        out_specs=rows,
        scratch_shapes=[
            pltpu.VMEM((1024, 1024), BF16),
            pltpu.VMEM((2048, 1024), BF16),
            pltpu.VMEM((t8, 1024), BF16),
            pltpu.VMEM((8, t8, LANES), F32),
            pltpu.VMEM((lp, LANES), F32),
        ],
        compiler_params=_cparams(("arbitrary",)),
        name="ssm",
    )(u.reshape(bsz, lp, D_MODEL), bc, pb, cc, kc, eb, ec, tab)
    return z.reshape(np_, D_MODEL)


def _ssm_tables(lam_re, lam_im, log_dt, b_re, b_im, c_re, c_im, d, seg):
    hi = lax.Precision.HIGHEST
    lr, li = lam_re.astype(F32), lam_im.astype(F32)
    dt = jnp.exp(log_dt.astype(F32))[:, None]
    mag = jnp.exp(dt * lr)
    ar, ai = mag * jnp.cos(dt * li), mag * jnp.sin(dt * li)
    den = lr * lr + li * li
    nr, ni = ar - 1.0, ai
    fr, fi = (nr * lr + ni * li) / den, (ni * lr - nr * li) / den
    br, bi = b_re.astype(F32), b_im.astype(F32)
    bbr = fr[..., None] * br - fi[..., None] * bi
    bbi = fr[..., None] * bi + fi[..., None] * br

    def cmul(xr, xi, yr, yi):
        return xr * yr - xi * yi, xr * yi + xi * yr

    pr, pi = [jnp.ones_like(ar)], [jnp.zeros_like(ar)]
    for _ in range(SSM_STEP):
        r_, i_ = cmul(pr[-1], pi[-1], ar, ai)
        pr.append(r_)
        pi.append(i_)
    pw_r, pw_i = jnp.stack(pr), jnp.stack(pi)
    a8r, a8i = pw_r[SSM_STEP], pw_i[SSM_STEP]
    sr, si = jnp.ones_like(ar), jnp.zeros_like(ar)
    qr, qi = a8r, a8i
    e = seg
    while e:
        if e & 1:
            sr, si = cmul(sr, si, qr, qi)
        qr, qi = cmul(qr, qi, qr, qi)
        e >>= 1

    cr, ci = c_re.astype(F32), c_im.astype(F32)
    nb = N_GROUPS // SSM_GBLK

    bc = jnp.concatenate([jnp.swapaxes(bbr, 1, 2), jnp.swapaxes(bbi, 1, 2)], axis=2).reshape(nb, LANES, LANES)
    pb = jnp.stack([jnp.concatenate([pr[SSM_STEP - 1 - i]] * 2 + [-pi[SSM_STEP - 1 - i], pi[SSM_STEP - 1 - i]], axis=1)
                    for i in range(SSM_STEP)])
    pb = jnp.broadcast_to(pb[:, :, None, :], (SSM_STEP, N_GROUPS, GROUP, 2 * LANES))
    pb = pb.reshape(SSM_STEP, nb, LANES, 2 * LANES)
    crt = jnp.tile(jnp.swapaxes(cr, 1, 2), (1, 1, SSM_STEP))
    cit = jnp.tile(jnp.swapaxes(ci, 1, 2), (1, 1, SSM_STEP))
    prc = jnp.repeat(jnp.stack(pr[1:], axis=2), GROUP, axis=2)
    pic = jnp.repeat(jnp.stack(pi[1:], axis=2), GROUP, axis=2)
    cc = jnp.concatenate([crt, cit, prc, pic], axis=2).reshape(nb, SSM_GBLK * STATE, 4 * LANES)
    tr, ti = cmul(pw_r[:SSM_STEP, :, :, None], pw_i[:SSM_STEP, :, :, None], bbr[None], bbi[None])
    kt = (jnp.einsum('ghp,tgpk->gkth', cr, tr, precision=hi)
          - jnp.einsum('ghp,tgpk->gkth', ci, ti, precision=hi))
    kc = kt.reshape(nb, LANES, LANES)

    def lanes(t):
        return t.reshape(nb, SSM_GBLK * STATE)

    dl = jnp.pad(d.astype(F32).reshape(nb, SSM_GBLK * GROUP), ((0, 0), (0, 512 - LANES)))
    zero = jnp.zeros((nb, 512), F32)
    tab = jnp.stack([lanes(a8r), lanes(a8i), lanes(sr), lanes(si), dl, zero, zero, zero], axis=1)
    return bc, pb, cc, kc, tab


def _attn_kernel(q_ref, kc_ref, kp_ref, km_ref, vc_ref, vp_ref, vm_ref, sink_ref, o_ref):
    n = pl.program_id(0)
    nk = 3 * BLK
    row = lax.broadcasted_iota(I32, (BLK, nk), 0)
    col = lax.broadcasted_iota(I32, (BLK, nk), 1)
    band = (col > row) & (col <= row + BLK) & (col >= 2 * BLK - BLK * n)
    mrow = col - 2 * BLK
    meta = (mrow >= PAD) & (mrow <= BLK * n + row)
    bias = jnp.where(band | meta, 0.0, NEG_INF)

    for b, g in [(b, g) for b in range(q_ref.shape[0]) for g in range(N_KV_HEADS)]:
        qg = jnp.concatenate([q_ref[b, :, LANES * (4 * g + jj):LANES * (4 * g + jj + 1)] for jj in range(4)], axis=0)
        probs, rdens, keys, vals = [], [], [], []
        for par in range(2):
            sl = slice(LANES * (2 * g + par), LANES * (2 * g + par + 1))
            keys += [kp_ref[b, :, sl], kc_ref[b, :, sl], km_ref[b, :, sl]]
            vals += [vp_ref[b, :, sl], vc_ref[b, :, sl], vm_ref[b, :, sl]]
        s_both = _dot_nt(qg, jnp.concatenate(keys, axis=0))
        for par in range(2):
            s_all = s_both[:, 3 * BLK * par:3 * BLK * (par + 1)]
            pp, rden = [], []
            for jj in range(4):
                sink = sink_ref[8 * g + 2 * jj + par] * LOG2_E
                s = s_all[BLK * jj:BLK * (jj + 1), :] + bias
                mx = jnp.maximum(jnp.max(s, axis=1, keepdims=True), sink)
                p = jnp.exp2(s - mx)
                den = jnp.sum(p, axis=1, keepdims=True) + jnp.exp2(sink - mx)
                pp.append(p.astype(BF16))
                rden.append(jnp.broadcast_to(1.0 / den, (BLK, LANES)))
            probs.append(jnp.concatenate(pp, axis=0))
            rdens.append(jnp.concatenate(rden, axis=0))
        o = _dot(jnp.concatenate(probs, axis=1), jnp.concatenate(vals, axis=0))
        low = lax.broadcasted_iota(I32, o.shape, 1) < HEAD_DIM
        o = o * jnp.where(low, rdens[0], rdens[1])
        for jj in range(4):
            o_ref[b, :, LANES * (4 * g + jj):LANES * (4 * g + jj + 1)] = o[BLK * jj:BLK * (jj + 1), :].astype(BF16)


def _attention(q, kx, vx, sinks, *, bsz, nb):
    np_ = q.shape[0]
    lp = nb * BLK
    q3, kx3, vx3 = q.reshape(bsz, lp, 1024), kx.reshape(bsz, lp, 512), vx.reshape(bsz, lp, 512)
    cur = lambda n: (0, n, 0)
    prev = lambda n: (0, jnp.maximum(n - 1, 0), 0)
    first = lambda n: (0, 0, 0)
    out = pl.pallas_call(
        _attn_kernel,
        out_shape=jax.ShapeDtypeStruct((bsz, lp, 1024), BF16),
        grid=(nb,),
        in_specs=[
            pl.BlockSpec((bsz, BLK, 1024), cur),
            pl.BlockSpec((bsz, BLK, 512), cur),
            pl.BlockSpec((bsz, BLK, 512), prev),
            pl.BlockSpec((bsz, BLK, 512), first),
            pl.BlockSpec((bsz, BLK, 512), cur),
            pl.BlockSpec((bsz, BLK, 512), prev),
            pl.BlockSpec((bsz, BLK, 512), first),
            pl.BlockSpec(memory_space=pltpu.SMEM),
        ],
        out_specs=pl.BlockSpec((bsz, BLK, 1024), cur),
        compiler_params=_cparams(("arbitrary",)),
        name="attn",
    )(q3, kx3, kx3, kx3, vx3, vx3, vx3, sinks)
    return out.reshape(np_, 1024)


def _post_kernel(*refs, tiles_per_seq, per):
    x_refs, (meta_ref, z_ref, at_ref, sgs_ref, sga_ref, wglu_ref, bglu_ref, wbs_ref, wba_ref, wout_ref,
             nffn_ref, wrh_ref, wrl_ref, br_ref,
             h2_ref, hnp_ref, lslot_ref, ecol_ref, cnt_ref) = refs[:per], refs[per:]
    i = pl.program_id(0)

    z = z_ref[...]
    t = _dot(z, wglu_ref[...]) + bglu_ref[...]
    so = (z.astype(F32) * jax.nn.sigmoid(t)).astype(BF16)
    mix = (sgs_ref[...].astype(F32) * _dot(so, wbs_ref[...])
           + sga_ref[...].astype(F32) * _dot(at_ref[...], wba_ref[...]))
    mixed = _dot(mix.astype(BF16), wout_ref[...])
    logit_blocks = []
    for r, xr in enumerate(_padded_rows(x_refs, meta_ref, i % tiles_per_seq == 0)):
        rows = slice(BLK * r, BLK * (r + 1))
        h2 = xr + mixed[rows, :]
        h2_ref[rows, :] = h2
        ms = jnp.mean(h2 * h2, axis=-1, keepdims=True)
        hn = h2 * lax.rsqrt(ms + RMS_EPS) * nffn_ref[...]
        hb = hn.astype(BF16)
        hnp_ref[rows, :] = hb
        lo = (hn - hb.astype(F32)).astype(BF16)
        logit_blocks.append(_dot(hb, wrh_ref[...]) + _dot(lo, wrh_ref[...]) + _dot(hb, wrl_ref[...]) + br_ref[...])
    logits = jnp.concatenate(logit_blocks, axis=0)

    tm = logits.shape[0]
    lane = lax.broadcasted_iota(I32, (tm, LANES), 1)
    lane_f = lane.astype(F32)
    work = logits
    vals, hots, idxs = [], [], []
    for _ in range(TOP_K):
        mx = jnp.max(work, axis=1, keepdims=True)
        idx = jnp.min(jnp.where(work == mx, lane_f, float(LANES)), axis=1, keepdims=True)
        hot = lane_f == idx
        vals.append(mx)
        idxs.append(idx)
        hots.append(hot)
        work = jnp.where(hot, -jnp.inf, work)

    ex = [jnp.exp(v - vals[0]) for v in vals]
    tot = ex[0] + ex[1] + ex[2] + ex[3]
    gates = [e / tot for e in ex]

    onehot = jnp.zeros((tm, LANES), F32)
    for hot in hots:
        onehot = onehot + jnp.where(hot, 1.0, 0.0)
    r_i = lax.broadcasted_iota(I32, (tm, tm), 0)
    c_i = lax.broadcasted_iota(I32, (tm, tm), 1)
    tri = jnp.where(r_i > c_i, 1.0, 0.0).astype(BF16)
    before = _dot(tri, onehot.astype(BF16))
    count = jnp.sum(onehot, axis=0, keepdims=True)
    gran = jnp.floor((count + (GRAN - 1)) * (1.0 / GRAN))
    e_r = lax.broadcasted_iota(I32, (LANES, LANES), 0)
    e_c = lax.broadcasted_iota(I32, (LANES, LANES), 1)
    upper = jnp.where(e_r < e_c, 1.0, 0.0).astype(BF16)
    gstart = _dot(jnp.broadcast_to(gran, (8, LANES)).astype(BF16), upper)[0:1, :]
    base = before + gstart * float(GRAN)
    slots = [jnp.sum(jnp.where(hot, base, 0.0), axis=1, keepdims=True) for hot in hots]
    cnt_ref[...] = jnp.broadcast_to(gran, cnt_ref.shape)

    def spread(cols, dtype):
        out = jnp.zeros((tm, LANES), dtype)
        for k, cval in enumerate(cols):
            out = jnp.where(lane == k, cval, out)
        return out

    lslot_ref[...] = spread(slots, F32)
    cols = []
    for g in gates:
        g1 = g.astype(BF16).astype(F32)
        g2 = (g - g1).astype(BF16).astype(F32)
        cols += [g1, g2, g - g1 - g2]
    for s in slots:
        hi = jnp.floor(s * (1.0 / 64.0))
        cols += [hi, s - 64.0 * hi]
    ecol_ref[...] = spread(cols, F32)


def _post(x, meta, z, attn, sgs, sga, wglu, bglu, wbs, wba, wout, nffn, wrh, wrl, br, *, tm, lp):
    np_ = z.shape[0]
    tiles_per_seq = lp // tm
    per = tm // BLK
    row = lambda i: (i, 0)
    fixed = lambda i: (0, 0)
    sq = pl.BlockSpec((1024, 1024), fixed)
    vec = pl.BlockSpec((1, 1024), fixed)
    out_shape = (
        jax.ShapeDtypeStruct((np_, 1024), F32),
        jax.ShapeDtypeStruct((np_, 1024), BF16),
        jax.ShapeDtypeStruct((np_, LANES), F32),
        jax.ShapeDtypeStruct((np_, LANES), F32),
        jax.ShapeDtypeStruct((np_ // tm * 8, LANES), F32),
    )
    return pl.pallas_call(
        functools.partial(_post_kernel, tiles_per_seq=tiles_per_seq, per=per),
        out_shape=out_shape,
        grid=(np_ // tm,),
        in_specs=_padded_row_specs(tm, tiles_per_seq) + [
            pl.BlockSpec((N_META, D_MODEL), fixed),
            pl.BlockSpec((tm, 1024), row), pl.BlockSpec((tm, 1024), row),
            pl.BlockSpec((tm, 1024), row), pl.BlockSpec((tm, 1024), row),
            sq, vec, sq, sq, sq, vec,
            pl.BlockSpec((1024, LANES), fixed), pl.BlockSpec((1024, LANES), fixed),
            pl.BlockSpec((1, LANES), fixed),
        ],
        out_specs=(
            pl.BlockSpec((tm, 1024), row), pl.BlockSpec((tm, 1024), row),
            pl.BlockSpec((tm, LANES), row), pl.BlockSpec((tm, LANES), row),
            pl.BlockSpec((8, LANES), row),
        ),
        compiler_params=_cparams(("arbitrary",)),
        name="post",
    )(*([x] * per), meta, z, attn, sgs, sga, wglu, bglu, wbs, wba, wout, nffn, wrh, wrl, br)


def _sort_kernel(hnp_ref, ecol_ref, xloc_ref, *, tm, srows):
    ecol = ecol_ref[...].astype(BF16)
    pay = jnp.concatenate([hnp_ref[...], ecol], axis=1)

    r8 = lax.broadcasted_iota(I32, (8, LANES), 0)
    l8 = lax.broadcasted_iota(I32, (8, LANES), 1)
    digit_w = jnp.where(l8 == 12 + 2 * r8, 64.0, jnp.where(l8 == 13 + 2 * r8, 1.0, 0.0)).astype(BF16)
    slot_rows = _dot_nt(digit_w, ecol)

    rr = lax.broadcasted_iota(I32, (LANES, LANES), 0)
    cc = lax.broadcasted_iota(I32, (LANES, LANES), 1)
    kk = jnp.where(cc < 3 * TOP_K, cc // 3, -LANES)
    spread_w = jnp.where(rr == 12 + 2 * kk, 64.0, jnp.where(rr == 13 + 2 * kk, 1.0, 0.0)).astype(BF16)
    gate_lane = lax.broadcasted_iota(I32, (SORT_CHUNK, LANES), 1) < 3 * TOP_K

    for c in range(srows // SORT_CHUNK):
        s_id = (lax.broadcasted_iota(I32, (SORT_CHUNK, tm), 0) + c * SORT_CHUNK).astype(F32)
        perm = jnp.zeros((SORT_CHUNK, tm), F32)
        for k in range(TOP_K):
            perm = perm + jnp.where(slot_rows[k:k + 1, :] == s_id, 1.0, 0.0)
        got = _dot(perm.astype(BF16), pay)
        rows = slice(c * SORT_CHUNK, (c + 1) * SORT_CHUNK)
        xloc_ref[rows, 0:512] = _pack_bf16_pairs(got[:, :D_MODEL], exact=True)
        extra = got[:, D_MODEL:]
        owner = _dot(extra.astype(BF16), spread_w)
        me = (lax.broadcasted_iota(I32, (SORT_CHUNK, LANES), 0) + c * SORT_CHUNK).astype(F32)
        mine = jnp.where(gate_lane, owner, -1.0) == me
        gate = jnp.sum(jnp.where(mine, extra, 0.0), axis=1, keepdims=True)
        xloc_ref[rows, 512:SORT_W] = pltpu.bitcast(jnp.broadcast_to(gate, (SORT_CHUNK, LANES)), U32)


def _sort(hnp, ecol, *, tm, srows):
    np_ = hnp.shape[0]
    ntile = np_ // tm
    row = lambda i: (i, 0)
    return pl.pallas_call(
        functools.partial(_sort_kernel, tm=tm, srows=srows),
        out_shape=jax.ShapeDtypeStruct((ntile * srows, SORT_W), U32),
        grid=(ntile,),
        in_specs=[pl.BlockSpec((tm, D_MODEL), row), pl.BlockSpec((tm, LANES), row)],
        out_specs=pl.BlockSpec((srows, SORT_W), row),
        compiler_params=_cparams(("arbitrary",)),
        name="sort",
    )(hnp, ecol)


def _moe_kernel(exp_ref, kind_ref, wslot_ref, nexte_ref, rsrc_ref, rdst_ref,
                xloc_ref, wgu_ref, bgu_ref, wd_ref, bd_ref, yloc_ref,
                xbuf, ybuf, sem_in, sem_out, wgu_f, wd_f, sem_w, wgu_bf, wd_bf, *, nseg, dump0):
    i = pl.program_id(0)
    prev = jnp.maximum(i - 1, 0)
    nxt = jnp.minimum(i + 1, nseg - 1)
    slot = i % 2
    new_expert = (i == 0) | (exp_ref[i] != exp_ref[prev])

    def gather(seg, dst_slot, start):
        for g in range(MOE_GRANS):
            src = pl.multiple_of(rsrc_ref[seg * MOE_GRANS + g], GRAN)
            cp = pltpu.make_async_copy(xloc_ref.at[pl.ds(src, GRAN)], xbuf.at[dst_slot, pl.ds(g * GRAN, GRAN)],
                                       sem_in.at[dst_slot])
            if start:
                cp.start()
            else:
                cp.wait()

    def scatter(seg, src_slot, start):
        for g in range(MOE_GRANS):
            dst = pl.multiple_of(rdst_ref[seg * MOE_GRANS + g], GRAN)
            cp = pltpu.make_async_copy(ybuf.at[src_slot, pl.ds(g * GRAN, GRAN)], yloc_ref.at[pl.ds(dst, GRAN)],
                                       sem_out.at[src_slot])
            if start:
                cp.start()
            else:
                cp.wait()

    @pl.when(i == 0)
    def _():
        gather(i, slot, True)
        ybuf[1 - slot] = jnp.zeros((MOE_BLK, 512), U32)
        for g in range(MOE_GRANS):
            pltpu.make_async_copy(ybuf.at[1 - slot, pl.ds(g * GRAN, GRAN)],
                                  yloc_ref.at[pl.ds((dump0 + (1 - slot) * MOE_GRANS + g) * GRAN, GRAN)],
                                  sem_out.at[1 - slot]).start()

    gather(i, slot, False)

    def fetch(e, slot, start):
        for src, dst in ((wgu_ref, wgu_f), (wd_ref, wd_f)):
            cp = pltpu.make_async_copy(src.at[e], dst.at[slot], sem_w.at[slot])
            if start:
                cp.start(priority=1)
            else:
                cp.wait()

    wslot = wslot_ref[i]

    @pl.when(i == 0)
    def _():
        fetch(exp_ref[i], wslot, True)

    @pl.when(new_expert)
    def _():
        fetch(exp_ref[i], wslot, False)
        wgu_bf[...] = wgu_f[wslot].astype(BF16)
        wd_bf[...] = wd_f[wslot].astype(BF16)

    @pl.when(new_expert & (nexte_ref[i] >= 0))
    def _():
        fetch(nexte_ref[i], 1 - wslot, True)

    kind = kind_ref[i]

    @pl.when(kind == 1)
    def _():
        gather(nxt, 1 - slot, True)
        xw = xbuf[slot]
        xb = _unpack_bf16_pairs(xw[:, 0:512]).astype(BF16)
        gate = pltpu.bitcast(xw[:, 512:SORT_W], F32)
        gu = _dot(xb, wgu_bf[...]) + bgu_ref[...]
        g = jnp.minimum(gu[:, :D_FF], SWIGLU_LIMIT)
        up = jnp.clip(gu[:, D_FF:], -SWIGLU_LIMIT, SWIGLU_LIMIT)
        hid = g * jax.nn.sigmoid(SWIGLU_ALPHA * g) * (up + 1.0)
        y = _dot(hid.astype(BF16), wd_bf[...]) + bd_ref[...]
        y = y * jnp.concatenate([gate] * (D_MODEL // LANES), axis=1)
        ybuf[slot] = _pack_bf16_pairs(y)
        scatter(i, slot, True)
        scatter(prev, 1 - slot, False)

    @pl.when(kind != 1)
    def _():
        gather(nxt, 1 - slot, True)
        ybuf[slot] = jnp.zeros((MOE_BLK, 512), U32)
        scatter(i, slot, True)
        scatter(prev, 1 - slot, False)

    @pl.when(i == nseg - 1)
    def _():
        scatter(i, slot, False)
        gather(nxt, 1 - slot, False)


def _moe(exp, hi, kind, gsrc, xloc, wgu, bgu, wd, bd, *, ncap):
    nseg = exp.shape[0]
    dump0 = ncap
    lane = jnp.arange(nseg * MOE_GRANS, dtype=I32) % MOE_GRANS
    block = jnp.arange(nseg * MOE_GRANS, dtype=I32) // MOE_GRANS
    spare = dump0 + (block % 2) * MOE_GRANS + lane
    rsrc = gsrc * GRAN
    rdst = jnp.where(lane < jnp.repeat(hi, MOE_GRANS), gsrc, spare) * GRAN
    change = jnp.concatenate([jnp.zeros((1,), I32), (exp[1:] != exp[:-1]).astype(I32)])
    wslot = jnp.cumsum(change) % 2
    later = jnp.where(exp[None, :] > exp[:, None], exp[None, :], N_EXPERTS)
    nexte = jnp.min(later, axis=1)
    nexte = jnp.where(nexte >= N_EXPERTS, -1, nexte)
    wmap = lambda i, e, *_: (e[i], 0, 0)
    grid_spec = pltpu.PrefetchScalarGridSpec(
        num_scalar_prefetch=6,
        grid=(nseg,),
        in_specs=[
            pl.BlockSpec(memory_space=pl.ANY),
            pl.BlockSpec(memory_space=pl.ANY),
            pl.BlockSpec((None, 1, 2 * D_FF), wmap),
            pl.BlockSpec(memory_space=pl.ANY),
            pl.BlockSpec((None, 1, D_MODEL), wmap),
        ],
        out_specs=pl.BlockSpec(memory_space=pl.ANY),
        scratch_shapes=[
            pltpu.VMEM((2, MOE_BLK, SORT_W), U32),
            pltpu.VMEM((2, MOE_BLK, 512), U32),
            pltpu.SemaphoreType.DMA((2,)),
            pltpu.SemaphoreType.DMA((2,)),
            pltpu.VMEM((2, D_MODEL, 2 * D_FF), F32),
            pltpu.VMEM((2, D_FF, D_MODEL), F32),
            pltpu.SemaphoreType.DMA((2,)),
            pltpu.VMEM((D_MODEL, 2 * D_FF), BF16),
            pltpu.VMEM((D_FF, D_MODEL), BF16),
        ],
    )
    return pl.pallas_call(
        functools.partial(_moe_kernel, nseg=nseg, dump0=dump0),
        out_shape=jax.ShapeDtypeStruct(((ncap + 2 * MOE_GRANS) * GRAN, 512), U32),
        grid_spec=grid_spec,
        compiler_params=_cparams(("arbitrary",)),
        name="moe",
    )(exp, kind, wslot.astype(I32), nexte.astype(I32), rsrc.astype(I32), rdst.astype(I32),
      xloc, wgu, bgu, wd, bd)


def _combine_kernel(y_ref, lslot_ref, h2_ref, nfin_ref, o_ref, obuf, sem, *, tm, srows, tiles_per_seq, ntile):
    i = pl.program_id(0)
    per = tm // BLK
    slot = i % 2
    slots = lslot_ref[...]
    acc = h2_ref[...]
    for c in range(srows // COMBINE_CHUNK):
        s_id = (lax.broadcasted_iota(I32, (tm, COMBINE_CHUNK), 1) + c * COMBINE_CHUNK).astype(F32)
        take = jnp.zeros((tm, COMBINE_CHUNK), F32)
        for k in range(TOP_K):
            take = take + jnp.where(slots[:, k:k + 1] == s_id, 1.0, 0.0)
        rows = _unpack_bf16_pairs(y_ref[c * COMBINE_CHUNK:(c + 1) * COMBINE_CHUNK, :]).astype(BF16)
        acc = acc + _dot(take.astype(BF16), rows)
    ms = jnp.mean(acc * acc, axis=-1, keepdims=True)
    obuf[slot] = acc * lax.rsqrt(ms + RMS_EPS) * nfin_ref[...]

    def copies(step, start):
        b = step // tiles_per_seq
        j = step % tiles_per_seq
        src_slot = step % 2
        for r in range(per):
            n = j * per + r
            cp = pltpu.make_async_copy(obuf.at[src_slot, pl.ds(r * BLK, BLK)],
                                       o_ref.at[b, pl.ds(pl.multiple_of(jnp.maximum(n - 1, 0) * BLK, BLK), BLK)],
                                       sem.at[src_slot])

            def go():
                if start:
                    cp.start()
                else:
                    cp.wait()

            if r == 0:
                pl.when(j > 0)(go)
            else:
                go()

    copies(i, True)

    @pl.when(i > 0)
    def _():
        copies(i - 1, False)

    @pl.when(i == ntile - 1)
    def _():
        copies(i, False)


def _combine(yloc, lslot, h2, nfin, *, bsz, seq, tm, lp, srows):
    np_ = h2.shape[0]
    tiles_per_seq = lp // tm
    row = lambda i: (i, 0)
    return pl.pallas_call(
        functools.partial(_combine_kernel, tm=tm, srows=srows, tiles_per_seq=tiles_per_seq, ntile=np_ // tm),
        out_shape=jax.ShapeDtypeStruct((bsz, seq, D_MODEL), F32),
        grid=(np_ // tm,),
        in_specs=[
            pl.BlockSpec((srows, 512), row),
            pl.BlockSpec((tm, LANES), row),
            pl.BlockSpec((tm, D_MODEL), row),
            pl.BlockSpec((1, D_MODEL), lambda i: (0, 0)),
        ],
        out_specs=pl.BlockSpec(memory_space=pl.ANY),
        scratch_shapes=[pltpu.VMEM((2, tm, D_MODEL), F32), pltpu.SemaphoreType.DMA((2,))],
        compiler_params=_cparams(("arbitrary",)),
        name="combine",
    )(yloc, lslot, h2, nfin)


def _rope_tables(lp):
    pos = np.arange(lp, dtype=np.float32) - np.float32(PAD)
    inv_freq = np.float32(ROPE_THETA) ** (-np.arange(0, 2 * ROT_HALF, 2, dtype=np.float32) / np.float32(2 * ROT_HALF))
    ang = (pos[:, None] * inv_freq[None, :]).astype(np.float32)
    cos, sin = np.cos(ang), np.sin(ang)
    ones = np.ones((lp, HEAD_DIM - 2 * ROT_HALF), np.float32)
    zeros8 = np.zeros((lp, ROT_HALF), np.float32)
    zrest = np.zeros((lp, HEAD_DIM - 2 * ROT_HALF), np.float32)
    c = np.concatenate([cos, cos, ones], axis=1)
    s1 = np.concatenate([zeros8, sin, zrest], axis=1)
    s2 = np.concatenate([-sin, zeros8, zrest], axis=1)
    tile = lambda t: jnp.asarray(np.concatenate([t, t], axis=1), F32)
    return tile(c), tile(s1), tile(s2)


def _route_tables(grans, gt):
    ntile = grans.shape[0]
    ncap = ntile * gt
    nown = N_EXPERTS + 1
    npos = ncap + nown * MOE_GRANS
    nblocks = npos // MOE_GRANS
    used = jnp.sum(grans, axis=1)
    lstart = jnp.cumsum(grans, axis=1) - grans
    tile0 = jnp.arange(ntile, dtype=I32)[:, None] * gt
    lens = jnp.concatenate([grans.T, (gt - used)[None, :]], axis=0)
    srcs = jnp.concatenate([(tile0 + lstart).T, (tile0[:, 0] + used)[None, :]], axis=0)
    total = jnp.sum(lens, axis=1)
    room = (total + MOE_GRANS - 1) // MOE_GRANS * MOE_GRANS
    oend = jnp.cumsum(room)
    ostart = oend - room
    run_start = (ostart[:, None] + jnp.cumsum(lens, axis=1) - lens).reshape(-1)
    off = srcs.reshape(-1) - run_start
    delta = off - jnp.concatenate([jnp.zeros((1,), I32), off[:-1]])
    pos = jnp.arange(npos, dtype=I32)
    gsrc = pos + jnp.sum(jnp.where(run_start[None, :] <= pos[:, None], delta[None, :], 0), axis=1)

    first = jnp.arange(nblocks, dtype=I32) * MOE_GRANS
    owner = jnp.sum((oend[None, :] <= first[:, None]).astype(I32), axis=1)
    mine = owner[:, None] == jnp.arange(nown, dtype=I32)[None, :]
    valid_end = jnp.sum(jnp.where(mine, (ostart + total)[None, :], 0), axis=1)
    hi = jnp.clip(valid_end - first, 0, MOE_GRANS)
    gsrc = jnp.where((pos % MOE_GRANS) < jnp.repeat(hi, MOE_GRANS), gsrc, 0)
    kind = jnp.where(hi > 0, jnp.where(owner >= N_EXPERTS, 2, 1), 0)
    works = kind == 1
    first_work = jnp.min(jnp.where(works, owner, N_EXPERTS - 1))
    exp = jnp.maximum(lax.cummax(jnp.where(works, owner, -1), axis=0), first_work)
    return gsrc.astype(I32), exp.astype(I32), hi.astype(I32), kind.astype(I32)


def kernel(x, meta_tokens, norm_mix, w_in, ssm_lam_re, ssm_lam_im, ssm_log_dt, ssm_b_re, ssm_b_im,
           ssm_c_re, ssm_c_im, ssm_d, w_glu, b_glu, attn_sinks, w_br_ssm, w_br_attn, w_out,
           norm_ffn, w_router, b_router, w_gate_up, b_gate_up, w_down, b_down, norm_final):
    bsz, seq, width = x.shape
    assert width == D_MODEL and seq % BLK == 0, "expects D_MODEL channels and whole 128-token blocks"
    assert w_in.shape[0] == 1 and w_in.shape[2] == 4352, "one layer with [u | q | k | v | gate_ssm | gate_attn] columns"
    lp = seq + BLK
    nb = lp // BLK
    np_ = bsz * lp
    tm = 640 if lp % 640 == 0 else BLK
    t8 = lp // SSM_STEP
    assert t8 % SSM_SEGS == 0, "sequence must split into SSM_SEGS equal runs of 8-step tiles"
    seg = t8 // SSM_SEGS
    nchunk = 5 if (t8 % 5 == 0 and (t8 // 5) % 16 == 0) else 1

    meta = meta_tokens.astype(F32)
    rc, rs1, rs2 = _rope_tables(lp)
    u, q, kx, vx, sgs, sga = _inproj(x, meta, norm_mix[0][None], w_in[0].astype(BF16), rc, rs1, rs2, tm=tm, lp=lp)

    bc, pb, cc, kc, tab = _ssm_tables(ssm_lam_re[0], ssm_lam_im[0], ssm_log_dt[0], ssm_b_re[0], ssm_b_im[0],
                                      ssm_c_re[0], ssm_c_im[0], ssm_d[0], seg)
    z = _ssm(u, bc, pb, cc, kc, tab, bsz=bsz, lp=lp, nchunk=nchunk)

    attn = _attention(q, kx, vx, attn_sinks[0].astype(F32), bsz=bsz, nb=nb)

    wr = jnp.pad(w_router[0].astype(F32), ((0, 0), (0, LANES - N_EXPERTS)))
    wrh = wr.astype(BF16)
    wrl = (wr - wrh.astype(F32)).astype(BF16)
    br = jnp.concatenate([b_router[0].astype(F32), jnp.full((LANES - N_EXPERTS,), NEG_INF, F32)])[None]
    h2, hnp, lslot, ecol, cnt = _post(
        x, meta, z, attn, sgs, sga, w_glu[0].astype(BF16), b_glu[0][None], w_br_ssm[0].astype(BF16),
        w_br_attn[0].astype(BF16), w_out[0].astype(BF16), norm_ffn[0][None], wrh, wrl, br, tm=tm, lp=lp)

    unit = math.lcm(SORT_CHUNK, COMBINE_CHUNK)
    srows = -(-(tm * TOP_K + N_EXPERTS * (GRAN - 1)) // unit) * unit
    gt = srows // GRAN
    ntile = np_ // tm
    grans = cnt.reshape(ntile, 8, LANES)[:, 0, :N_EXPERTS].astype(I32)
    gsrc, sexp, shi, skind = _route_tables(grans, gt)

    xloc = _sort(hnp, ecol, tm=tm, srows=srows)
    yloc = _moe(sexp, shi, skind, gsrc, xloc, w_gate_up[0], b_gate_up[0][:, None, :],
                w_down[0], b_down[0][:, None, :], ncap=ntile * gt)
    return _combine(yloc, lslot, h2, norm_final[None], bsz=bsz, seq=seq, tm=tm, lp=lp, srows=srows)
```
